```python
import math
import jax, jax.numpy as jnp
from jax import lax
import numpy as np

D_MODEL = 1024
BATCH = 8
SEQ = 2048
DEPTH = 1

BLK = 128
WINDOW = 128
A_HEADS = 8
A_KV_HEADS = 2
A_GROUP = A_HEADS // A_KV_HEADS
A_HEAD_DIM = 64
B_HEADS = 8
B_NOPE_DIM = 64
B_ROPE_DIM = 32
B_V_DIM = 64
Q_RANK = 256
KV_RANK = 128
ROPE_THETA = 10000.0
NUM_BUCKETS = 32
MAX_DISTANCE = 128
N_EXPERTS = 16
EXPERT_FF = 512
CAPACITY_FACTOR = 2
PLE_DIM = 256
EPS = 1e-6

A_Q_COLS = A_HEADS * A_HEAD_DIM
A_K_COLS = A_KV_HEADS * A_HEAD_DIM
A_V_COLS = A_KV_HEADS * A_HEAD_DIM
B_CQ_COLS = Q_RANK
B_CKV_COLS = KV_RANK
B_KR_COLS = B_ROPE_DIM
OFF_AQ = 0
OFF_AK = OFF_AQ + A_Q_COLS
OFF_AV = OFF_AK + A_K_COLS
OFF_BCQ = OFF_AV + A_V_COLS
OFF_BCKV = OFF_BCQ + B_CQ_COLS
OFF_BKR = OFF_BCKV + B_CKV_COLS
IN_COLS = OFF_BKR + B_KR_COLS
A_OUT = A_HEADS * A_HEAD_DIM
B_OUT = B_HEADS * B_V_DIM
MIX_WIDTH = A_OUT + B_OUT

kernel_name = "hymba_swa_mla_ec_moe_encoder"


def rms_norm(x, g):
    xf = x.astype(jnp.float32)
    y = xf * lax.rsqrt(jnp.mean(xf * xf, axis=-1, keepdims=True) + EPS)
    return (y * g.astype(jnp.float32)).astype(x.dtype)


def t5_bucket(rel):
    half = NUM_BUCKETS // 2
    ret = jnp.where(rel > 0, half, 0)
    n = jnp.abs(rel)
    max_exact = half // 2
    is_small = n < max_exact
    nf = jnp.maximum(n, 1).astype(jnp.float32)
    large = max_exact + (jnp.log(nf / max_exact) / math.log(MAX_DISTANCE / max_exact)
                         * (half - max_exact)).astype(jnp.int32)
    large = jnp.minimum(large, half - 1)
    return ret + jnp.where(is_small, n, large)


def window_gqa(q, k, v, sink, rel_bias):
    B, S = q.shape[0], q.shape[1]
    nb = S // BLK
    qb = q.reshape(B, nb, BLK, A_KV_HEADS, A_GROUP, A_HEAD_DIM)
    pad = ((0, 0), (BLK, BLK), (0, 0), (0, 0))

    def band(t):
        tp = jnp.pad(t, pad).reshape(B, nb + 2, BLK, A_KV_HEADS, A_HEAD_DIM)
        return jnp.concatenate([tp[:, :-2], tp[:, 1:-1], tp[:, 2:]], axis=2)

    kb, vb = band(k), band(v)
    qi = jnp.arange(BLK)[:, None]
    kj = jnp.arange(3 * BLK)[None, :]
    rel = kj - BLK - qi
    bias = rel_bias.astype(jnp.float32)[t5_bucket(rel)]
    bias = bias.transpose(2, 0, 1).reshape(A_KV_HEADS, A_GROUP, BLK, 3 * BLK)
    key_idx = jnp.arange(nb)[:, None] * BLK - BLK + jnp.arange(3 * BLK)[None, :]
    in_range = (key_idx >= 0) & (key_idx < S)
    valid = (jnp.abs(rel) <= WINDOW)[None] & in_range[:, None, :]

    scale = A_HEAD_DIM ** -0.5
    s = jnp.einsum('bnqhgd,bnkhd->bnhgqk', qb, kb).astype(jnp.float32) * scale + bias[None, None]
    s = jnp.where(valid[None, :, None, None], s, jnp.float32(-1e30))
    sink_l = sink.astype(jnp.float32).reshape(A_KV_HEADS, A_GROUP)[None, None, :, :, None, None]
    m = jnp.maximum(jnp.max(s, axis=-1, keepdims=True), sink_l)
    e = jnp.exp(s - m)
    denom = jnp.sum(e, axis=-1, keepdims=True) + jnp.exp(sink_l - m)
    pr = (e / denom).astype(v.dtype)
    o = jnp.einsum('bnhgqk,bnkhd->bnqhgd', pr, vb)
    return o.reshape(B, S, A_OUT)


def rope(t, cos, sin):
    half = t.shape[-1] // 2
    t1, t2 = t[..., :half], t[..., half:]
    return jnp.concatenate([t1 * cos - t2 * sin, t1 * sin + t2 * cos], axis=-1)


def mla(c_q, c_kv, k_rope, positions, g_cq, g_ckv, w_uq, w_ukv):
    B, S = c_q.shape[0], c_q.shape[1]
    q = (rms_norm(c_q, g_cq) @ w_uq).reshape(B, S, B_HEADS, B_NOPE_DIM + B_ROPE_DIM)
    kv = (rms_norm(c_kv, g_ckv) @ w_ukv).reshape(B, S, B_HEADS, B_NOPE_DIM + B_V_DIM)
    qn, qr = q[..., :B_NOPE_DIM], q[..., B_NOPE_DIM:]
    kn, v = kv[..., :B_NOPE_DIM], kv[..., B_NOPE_DIM:]

    inv_freq = 1.0 / (ROPE_THETA ** (jnp.arange(0, B_ROPE_DIM, 2, dtype=jnp.float32) / B_ROPE_DIM))
    ang = positions.astype(jnp.float32)[..., None] * inv_freq
    cos, sin = jnp.cos(ang).astype(q.dtype), jnp.sin(ang).astype(q.dtype)
    qr = rope(qr, cos[:, :, None], sin[:, :, None])
    kr = rope(k_rope, cos, sin)

    nb = S // BLK
    qn_b = qn.reshape(B, nb, BLK, B_HEADS, B_NOPE_DIM).transpose(1, 0, 2, 3, 4)
    qr_b = qr.reshape(B, nb, BLK, B_HEADS, B_ROPE_DIM).transpose(1, 0, 2, 3, 4)
    scale = (B_NOPE_DIM + B_ROPE_DIM) ** -0.5

    def attend(blk):
        qn_i, qr_i = blk
        s = (jnp.einsum('bqhd,bkhd->bhqk', qn_i, kn)
             + jnp.einsum('bqhd,bkd->bhqk', qr_i, kr)).astype(jnp.float32) * scale
        pr = jax.nn.softmax(s, axis=-1).astype(v.dtype)
        return jnp.einsum('bhqk,bkhd->bqhd', pr, v)

    o = lax.map(attend, (qn_b, qr_b))
    return o.transpose(1, 0, 2, 3, 4).reshape(B, S, B_OUT)


def ec_moe(h, w_router, w_gate, w_up, w_down):
    B, S, _ = h.shape
    cap = CAPACITY_FACTOR * S // N_EXPERTS
    aff = jax.nn.softmax((h @ w_router).astype(jnp.float32), axis=-1)
    vals, idx = lax.top_k(aff.transpose(0, 2, 1), cap)
    bidx = jnp.arange(B)[:, None, None]
    xg = h[bidx, idx]
    g = jnp.einsum('becd,edf->becf', xg, w_gate)
    u = jnp.einsum('becd,edf->becf', xg, w_up)
    o = jnp.einsum('becf,efd->becd', jax.nn.silu(g) * u, w_down)
    o = o * vals[..., None].astype(h.dtype)
    return jnp.zeros_like(h).at[bidx, idx].add(o)


def setup_inputs(seed: int = 0) -> dict:
    key = jax.random.key(seed)
    ks = jax.random.split(key, 24)
    f32 = jnp.float32

    def nrm(k, shape, fan_in):
        return jax.random.normal(k, shape, f32) * (fan_in ** -0.5)

    def gain(k, shape):
        return 1.0 + 0.05 * jax.random.normal(k, shape, f32)

    x = jax.random.normal(ks[0], (BATCH, SEQ, D_MODEL), f32)
    p = jax.random.normal(ks[1], (DEPTH, BATCH, SEQ, PLE_DIM), f32)
    offs = jax.random.randint(ks[2], (BATCH, 1), 0, 1024, dtype=jnp.int32)
    positions = jnp.arange(SEQ, dtype=jnp.int32)[None, :] + offs
    return {
        "x": x,
        "p": p,
        "positions": positions,
        "rel_bias": 0.5 * jax.random.normal(ks[3], (NUM_BUCKETS, A_HEADS), f32),
        "norm_mix_g": gain(ks[4], (DEPTH, D_MODEL)),
        "w_in": nrm(ks[5], (DEPTH, D_MODEL, IN_COLS), D_MODEL),
        "sink": 0.5 * jax.random.normal(ks[6], (DEPTH, A_HEADS), f32),
        "g_cq": gain(ks[7], (DEPTH, Q_RANK)),
        "g_ckv": gain(ks[8], (DEPTH, KV_RANK)),
        "w_uq": nrm(ks[9], (DEPTH, Q_RANK, B_HEADS * (B_NOPE_DIM + B_ROPE_DIM)), Q_RANK),
        "w_ukv": nrm(ks[10], (DEPTH, KV_RANK, B_HEADS * (B_NOPE_DIM + B_V_DIM)), KV_RANK),
        "g_out_a": gain(ks[11], (DEPTH, A_OUT)),
        "g_out_b": gain(ks[12], (DEPTH, B_OUT)),
        "w_out": nrm(ks[13], (DEPTH, MIX_WIDTH, D_MODEL), MIX_WIDTH),
        "norm_ffn_g": gain(ks[14], (DEPTH, D_MODEL)),
        "w_router": nrm(ks[15], (DEPTH, D_MODEL, N_EXPERTS), D_MODEL),
        "w_e_gate": nrm(ks[16], (DEPTH, N_EXPERTS, D_MODEL, EXPERT_FF), D_MODEL),
        "w_e_up": nrm(ks[17], (DEPTH, N_EXPERTS, D_MODEL, EXPERT_FF), D_MODEL),
        "w_e_down": nrm(ks[18], (DEPTH, N_EXPERTS, EXPERT_FF, D_MODEL), EXPERT_FF),
        "norm_ple_g": gain(ks[19], (DEPTH, D_MODEL)),
        "w_ple_gate": nrm(ks[20], (DEPTH, D_MODEL, D_MODEL), D_MODEL),
        "w_ple_proj": nrm(ks[21], (DEPTH, PLE_DIM, D_MODEL), PLE_DIM),
        "final_norm_g": gain(ks[22], (D_MODEL,)),
    }


def reference(x, p, positions, rel_bias, norm_mix_g, w_in, sink, g_cq, g_ckv, w_uq, w_ukv,
              g_out_a, g_out_b, w_out, norm_ffn_g, w_router, w_e_gate, w_e_up, w_e_down,
              norm_ple_g, w_ple_gate, w_ple_proj, final_norm_g):
    B, S, _ = x.shape
    for i in range(DEPTH):
        h = rms_norm(x, norm_mix_g[i])
        z = h @ w_in[i]
        qa = z[..., OFF_AQ:OFF_AK].reshape(B, S, A_HEADS, A_HEAD_DIM)
        ka = z[..., OFF_AK:OFF_AV].reshape(B, S, A_KV_HEADS, A_HEAD_DIM)
        va = z[..., OFF_AV:OFF_BCQ].reshape(B, S, A_KV_HEADS, A_HEAD_DIM)
        c_q = z[..., OFF_BCQ:OFF_BCKV]
        c_kv = z[..., OFF_BCKV:OFF_BKR]
        k_rope = z[..., OFF_BKR:IN_COLS]
        ya = window_gqa(qa, ka, va, sink[i], rel_bias)
        yb = mla(c_q, c_kv, k_rope, positions, g_cq[i], g_ckv[i], w_uq[i], w_ukv[i])
        y = jnp.concatenate([rms_norm(ya, g_out_a[i]), rms_norm(yb, g_out_b[i])], axis=-1)
        x = x + y @ w_out[i]
        h = rms_norm(x, norm_ffn_g[i])
        x = x + ec_moe(h, w_router[i], w_e_gate[i], w_e_up[i], w_e_down[i])
        gate = jax.nn.sigmoid(rms_norm(x, norm_ple_g[i]) @ w_ple_gate[i])
        x = x + gate * (p[i] @ w_ple_proj[i])
    return rms_norm(x, final_norm_g)
```

```python
import functools

import numpy as np
import jax
import jax.numpy as jnp
from jax import lax
from jax.experimental import pallas as pl
from jax.experimental.pallas import tpu as pltpu

F32 = jnp.float32
BF16 = jnp.bfloat16

D_MODEL = 1024
BLK = 128
WINDOW = 128
A_HEADS = 8
A_KV_HEADS = 2
A_HEAD_DIM = 64
B_HEADS = 8
B_NOPE_DIM = 64
B_ROPE_DIM = 32
B_V_DIM = 64
Q_RANK = 256
KV_RANK = 128
ROPE_THETA = 10000.0
NUM_BUCKETS = 32
MAX_DISTANCE = 128
N_EXPERTS = 16
EXPERT_FF = 512
CAPACITY_FACTOR = 2
PLE_DIM = 256
EPS = 1e-6

LANES = 128
HALF_ROPE = B_ROPE_DIM // 2
MASKED = -1e30
VMEM_LIMIT = 56 * 1024 * 1024
TM = 512
TQ = 256


def _cparams(n_axes):
    return pltpu.CompilerParams(dimension_semantics=("arbitrary",) * n_axes,
                                vmem_limit_bytes=VMEM_LIMIT)


def _rms(x, g):
    return x * lax.rsqrt(jnp.mean(x * x, axis=-1, keepdims=True) + EPS) * g


def _dot(a, b):
    return jnp.dot(a, b, preferred_element_type=F32)


def _dot_nt(a, b):
    return lax.dot_general(a, b, (((1,), (1,)), ((), ())), preferred_element_type=F32)


def _dot_tn(a, b):
    return lax.dot_general(a, b, (((0,), (0,)), ((), ())), preferred_element_type=F32)


def _rope_table_kernel(pos_ref, invf_ref, cos_ref, sin_ref):
    ang = pos_ref[...] * invf_ref[...]
    cos_ref[...] = jnp.cos(ang)
    sin_ref[...] = jnp.sin(ang)


def _rope_tables(positions):
    T = positions.size
    per_row = LANES // HALF_ROPE
    inv_freq = 1.0 / (ROPE_THETA ** (jnp.arange(0, B_ROPE_DIM, 2, dtype=F32) / B_ROPE_DIM))
    pos = jnp.repeat(positions.astype(F32).reshape(T // per_row, per_row), HALF_ROPE, axis=1)
    invf = jnp.tile(inv_freq, per_row).reshape(1, LANES)
    rows = T // per_row
    cos, sin = pl.pallas_call(
        _rope_table_kernel,
        grid=(1,),
        in_specs=[pl.BlockSpec((rows, LANES), lambda i: (0, 0)), pl.BlockSpec((1, LANES), lambda i: (0, 0))],
        out_specs=[pl.BlockSpec((rows, LANES), lambda i: (0, 0))] * 2,
        out_shape=[jax.ShapeDtypeStruct((rows, LANES), F32)] * 2,
        compiler_params=_cparams(1),
        name="rope_tables",
    )(pos, invf)
    return cos.reshape(T, HALF_ROPE), sin.reshape(T, HALF_ROPE)


C_QA = 0
C_KD = C_QA + 512
C_VA = C_KD + 256
C_CQ = C_VA + 512
C_CKV = C_CQ + Q_RANK
C_KR = C_CKV + KV_RANK
C_END = C_KR + LANES


def _in_proj_kernel(x_ref, g_ref, w1_ref, gcq_ref, gckv_ref, wq_ref, wqr_ref, wkv_ref, cos_ref, sin_ref,
                    qa_ref, kd_ref, va_ref, qm_ref, km_ref, vlo_ref, vhi_ref):
    h = _rms(x_ref[...], g_ref[...]).astype(BF16)
    z = _dot(h, w1_ref[...])
    qa_ref[...] = (z[:, C_QA:C_KD] * (A_HEAD_DIM ** -0.5)).astype(BF16)
    kd_ref[...] = z[:, C_KD:C_VA].astype(BF16)
    va_ref[...] = z[:, C_VA:C_CQ].astype(BF16)
    cqn = _rms(z[:, C_CQ:C_CKV], gcq_ref[...]).astype(BF16)
    ckvn = _rms(z[:, C_CKV:C_KR], gckv_ref[...]).astype(BF16)
    krt = z[:, C_KR:C_END]

    lane = lax.broadcasted_iota(jnp.int32, (1, LANES), 1)
    in_rope = (lane >= B_NOPE_DIM) & (lane < B_NOPE_DIM + B_ROPE_DIM)
    cos_t = cos_ref[...]
    sin_t = sin_ref[...]
    c_rope = jnp.where(in_rope, cos_t, 0.0)
    c_full = jnp.where(lane < B_NOPE_DIM, 1.0, c_rope)
    s_rope = jnp.where(in_rope, jnp.where(lane < B_NOPE_DIM + HALF_ROPE, -sin_t, sin_t), 0.0)

    q1 = _dot(cqn, wq_ref[...])
    q2 = _dot(cqn, wqr_ref[...])
    scale = (B_NOPE_DIM + B_ROPE_DIM) ** -0.5
    for hd in range(B_HEADS):
        sl = slice(hd * LANES, (hd + 1) * LANES)
        qm_ref[:, sl] = ((q1[:, sl] * c_full + q2[:, sl] * s_rope) * scale).astype(BF16)

    kv = _dot(ckvn, wkv_ref[...])
    kr_part = pltpu.roll(krt, 64, axis=1) * c_rope + pltpu.roll(krt, 32, axis=1) * s_rope
    for hd in range(B_HEADS):
        sl = slice(hd * LANES, (hd + 1) * LANES)
        km_ref[:, sl] = (kv[:, sl] + kr_part).astype(BF16)
    vlo_ref[...] = kv[:, 1024:1536].astype(BF16)
    vhi_ref[...] = kv[:, 1536:2048].astype(BF16)


def _in_proj(xf, g_mix, w1, g_cq, g_ckv, wq, wqr, wkv, cos_t, sin_t):
    T = xf.shape[0]
    row = lambda n: pl.BlockSpec((TM, n), lambda i: (i, 0))
    full = lambda a: pl.BlockSpec(a.shape, lambda i: (0, 0))
    outs = [512, 256, 512, 1024, 1024, 512, 512]
    return pl.pallas_call(
        _in_proj_kernel,
        grid=(T // TM,),
        in_specs=[row(D_MODEL), full(g_mix), full(w1), full(g_cq), full(g_ckv), full(wq), full(wqr), full(wkv),
                  row(LANES), row(LANES)],
        out_specs=[row(n) for n in outs],
        out_shape=[jax.ShapeDtypeStruct((T, n), BF16) for n in outs],
        compiler_params=_cparams(1),
        name="in_proj",
    )(xf, g_mix, w1, g_cq, g_ckv, wq, wqr, wkv, cos_t, sin_t)


def _bucket_map():
    qi = np.arange(BLK)[:, None]
    kj = np.arange(3 * BLK)[None, :]
    rel = kj - BLK - qi
    n = np.abs(rel)
    half = NUM_BUCKETS // 2
    max_exact = half // 2
    thresholds = [int(np.ceil(max_exact * 2 ** (k / 2) - 1e-9)) for k in range(1, half - max_exact)]
    large = max_exact + sum((n >= t).astype(np.int64) for t in thresholds)
    large = np.minimum(large, half - 1)
    bucket = np.where(rel > 0, half, 0) + np.where(n < max_exact, n, large)
    return np.where(n <= WINDOW, bucket, -1).astype(np.int32)


def _window_attn_kernel(nb, relb_ref, sink_ref, bmap_ref, g_ref, q_ref, kp_ref, kc_ref, kn_ref,
                        vp_ref, vc_ref, vn_ref, o_ref, bias_ref):
    b = pl.program_id(0)
    i = pl.program_id(1)

    @pl.when((b == 0) & (i == 0))
    def _build_bias():
        bmap = bmap_ref[...]
        for hd in range(A_HEADS):
            bias_ref[hd] = jnp.full((BLK, 3 * BLK), MASKED, F32)
        for bk in range(NUM_BUCKETS):
            m = bmap == bk
            for hd in range(A_HEADS):
                bias_ref[hd] = jnp.where(m, relb_ref[bk, hd], bias_ref[hd])

    col = lax.broadcasted_iota(jnp.int32, (1, 3 * BLK), 1)
    key_ok = ((col >= BLK) | (i > 0)) & ((col < 2 * BLK) | (i < nb - 1))
    lane = lax.broadcasted_iota(jnp.int32, (1, LANES), 1)
    lo = lane < A_HEAD_DIM

    probs = []
    for hd in range(A_HEADS):
        g = hd // (A_HEADS // A_KV_HEADS)
        qt = q_ref[:, (hd // 2) * LANES:(hd // 2 + 1) * LANES]
        qm = jnp.where(lo if hd % 2 == 0 else ~lo, qt, jnp.zeros_like(qt))
        ksl = slice(g * LANES, (g + 1) * LANES)
        kcat = jnp.concatenate([kp_ref[:, ksl], kc_ref[:, ksl], kn_ref[:, ksl]], axis=0)
        s = _dot_nt(qm, kcat) + bias_ref[hd]
        s = jnp.where(key_ok, s, MASKED)
        sk = sink_ref[hd]
        m = jnp.maximum(jnp.max(s, axis=-1, keepdims=True), sk)
        e = jnp.exp(s - m)
        denom = jnp.sum(e, axis=-1, keepdims=True) + jnp.exp(sk - m)
        probs.append((e / denom).astype(BF16))

    outs = []
    for j in range(A_HEADS // 2):
        g = (2 * j) // (A_HEADS // A_KV_HEADS)
        sl_e = slice((2 * g) * LANES, (2 * g + 1) * LANES)
        sl_o = slice((2 * g + 1) * LANES, (2 * g + 2) * LANES)
        v_e = jnp.concatenate([vp_ref[:, sl_e], vc_ref[:, sl_e], vn_ref[:, sl_e]], axis=0)
        v_o = jnp.concatenate([vp_ref[:, sl_o], vc_ref[:, sl_o], vn_ref[:, sl_o]], axis=0)
        outs.append(_dot(probs[2 * j], v_e) + _dot(probs[2 * j + 1], v_o))
    ya = jnp.concatenate(outs, axis=1)
    o_ref[...] = _rms(ya, g_ref[...]).astype(BF16)


def _window_attn(qa, kd, va4, rel_bias, sink, g_out_a, B, S):
    nb = S // BLK
    bmap = jnp.asarray(_bucket_map())
    smem = pl.BlockSpec(memory_space=pltpu.SMEM)
    cur = lambda n: pl.BlockSpec((BLK, n), lambda b, i: (b * nb + i, 0))
    prv = lambda n: pl.BlockSpec((BLK, n), lambda b, i: (b * nb + jnp.maximum(i - 1, 0), 0))
    nxt = lambda n: pl.BlockSpec((BLK, n), lambda b, i: (b * nb + jnp.minimum(i + 1, nb - 1), 0))
    return pl.pallas_call(
        functools.partial(_window_attn_kernel, nb),
        grid=(B, nb),
        in_specs=[smem, smem,
                  pl.BlockSpec((BLK, 3 * BLK), lambda b, i: (0, 0)),
                  pl.BlockSpec((1, 512), lambda b, i: (0, 0)),
                  cur(512), prv(256), cur(256), nxt(256), prv(512), cur(512), nxt(512)],
        out_specs=cur(512),
        out_shape=jax.ShapeDtypeStruct((B * S, 512), BF16),
        scratch_shapes=[pltpu.VMEM((A_HEADS, BLK, 3 * BLK), F32)],
        compiler_params=_cparams(2),
        name="window_attn",
    )(rel_bias, sink, bmap, g_out_a, qa, kd, kd, kd, va4, va4, va4)


def _mla_attn_kernel(g_ref, q_ref, k_ref, vlo_ref, vhi_ref, o_ref):
    lane = lax.broadcasted_iota(jnp.int32, (1, LANES), 1)
    lo = lane < B_V_DIM
    outs = []
    for j in range(B_HEADS // 2):
        ps, ls = [], []
        for hd in (2 * j, 2 * j + 1):
            sl = slice(hd * LANES, (hd + 1) * LANES)
            s = _dot_nt(q_ref[:, sl], k_ref[:, sl])
            m = jnp.max(s, axis=-1, keepdims=True)
            e = jnp.exp(s - m)
            ls.append(jnp.sum(e, axis=-1, keepdims=True))
            ps.append(e.astype(BF16))
        sl = slice(j * LANES, (j + 1) * LANES)
        o = _dot(ps[0], vlo_ref[:, sl]) + _dot(ps[1], vhi_ref[:, sl])
        outs.append(o * jnp.where(lo, 1.0 / ls[0], 1.0 / ls[1]))
    yb = jnp.concatenate(outs, axis=1)
    o_ref[...] = _rms(yb, g_ref[...]).astype(BF16)


def _mla_attn(qm, km, vlo, vhi, g_out_b, B, S):
    nq = S // TQ
    qspec = lambda n: pl.BlockSpec((TQ, n), lambda b, i: (b * nq + i, 0))
    kspec = lambda n: pl.BlockSpec((S, n), lambda b, i: (b, 0))
    return pl.pallas_call(
        _mla_attn_kernel,
        grid=(B, nq),
        in_specs=[pl.BlockSpec((1, 512), lambda b, i: (0, 0)), qspec(1024), kspec(1024), kspec(512), kspec(512)],
        out_specs=qspec(512),
        out_shape=jax.ShapeDtypeStruct((B * S, 512), BF16),
        compiler_params=_cparams(2),
        name="mla_attn",
    )(g_out_b, qm, km, vlo, vhi)


def _split_bf16(v):
    hi = v.astype(BF16)
    return hi, (v - hi.astype(F32)).astype(BF16)


def _out_proj_kernel(x_ref, ya_ref, yb_ref, wo_ref, g_ref, wr_ref, x1_ref, h2_ref, aff_ref):
    x1 = x_ref[...] + _dot(ya_ref[...], wo_ref[0:512, :]) + _dot(yb_ref[...], wo_ref[512:1024, :])
    x1_ref[...] = x1
    h2 = _rms(x1, g_ref[...])
    h2_ref[...] = h2.astype(BF16)
    h_hi, h_lo = _split_bf16(h2)
    w_hi, w_lo = _split_bf16(wr_ref[...])
    logits = _dot_nt(w_hi, h_hi) + (_dot_nt(w_hi, h_lo) + _dot_nt(w_lo, h_hi))
    m = jnp.max(logits, axis=0, keepdims=True)
    e = jnp.exp(logits - m)
    aff_ref[0] = e / jnp.sum(e, axis=0, keepdims=True)


def _out_proj(xf, ya, yb, w_out, g_ffn, w_router_t, B, S):
    T = xf.shape[0]
    per_b = S // TM
    row = lambda n: pl.BlockSpec((TM, n), lambda i: (i, 0))
    full = lambda a: pl.BlockSpec(a.shape, lambda i: (0, 0))
    return pl.pallas_call(
        _out_proj_kernel,
        grid=(T // TM,),
        in_specs=[row(D_MODEL), row(512), row(512), full(w_out), full(g_ffn), full(w_router_t)],
        out_specs=[row(D_MODEL), row(D_MODEL),
                   pl.BlockSpec((1, N_EXPERTS, TM), lambda i: (i // per_b, 0, i % per_b))],
        out_shape=[jax.ShapeDtypeStruct((T, D_MODEL), F32), jax.ShapeDtypeStruct((T, D_MODEL), BF16),
                   jax.ShapeDtypeStruct((B, N_EXPERTS, S), F32)],
        compiler_params=_cparams(1),
        name="out_proj",
    )(xf, ya, yb, w_out, g_ffn, w_router_t)


CHUNK = 256


def _prefix_count(flags_f32, tri):
    S = flags_f32.shape[1]
    carry = jnp.zeros((flags_f32.shape[0], 1), F32)
    parts = []
    for c in range(S // CHUNK):
        blk = flags_f32[:, c * CHUNK:(c + 1) * CHUNK]
        parts.append(_dot(blk.astype(BF16), tri) + carry)
        carry = carry + jnp.sum(blk, axis=-1, keepdims=True)
    return jnp.concatenate(parts, axis=1)


def _route_kernel(cap, aff_ref, slot_ref):
    aff = aff_ref[0]
    bits = pltpu.bitcast(aff, jnp.int32)

    def body(it, prefix):
        cand = prefix | jnp.left_shift(jnp.int32(1), 30 - it)
        cnt = jnp.sum(jnp.where(bits >= cand, 1.0, 0.0), axis=-1, keepdims=True)
        return jnp.where(cnt >= cap, cand, prefix)

    thr = lax.fori_loop(0, 31, body, jnp.zeros((aff.shape[0], 1), jnp.int32))
    gt = jnp.where(bits > thr, 1.0, 0.0)
    eq = jnp.where(bits == thr, 1.0, 0.0)
    need = cap - jnp.sum(gt, axis=-1, keepdims=True)
    r = lax.broadcasted_iota(jnp.int32, (CHUNK, CHUNK), 0)
    c = lax.broadcasted_iota(jnp.int32, (CHUNK, CHUNK), 1)
    tri = jnp.where(r < c, 1.0, 0.0).astype(BF16)
    sel = gt + eq * jnp.where(_prefix_count(eq, tri) < need, 1.0, 0.0)
    slot = _prefix_count(sel, tri)
    slot_ref[0] = jnp.where(sel > 0.5, slot, -1.0).astype(jnp.int32)


def _route(aff, cap):
    B, E, S = aff.shape
    spec = pl.BlockSpec((1, E, S), lambda b: (b, 0, 0))
    return pl.pallas_call(
        functools.partial(_route_kernel, cap),
        grid=(B,),
        in_specs=[spec],
        out_specs=spec,
        out_shape=jax.ShapeDtypeStruct((B, E, S), jnp.int32),
        compiler_params=_cparams(1),
        name="route",
    )(aff)


def _moe_kernel(cap, slot_ref, aff_ref, h_ref, wg_ref, wu_ref, wd_ref, o_ref):
    e = pl.program_id(1)

    @pl.when(e == 0)
    def _zero():
        o_ref[...] = jnp.zeros_like(o_ref)

    S = h_ref.shape[0]
    sel = lax.broadcasted_iota(jnp.int32, (cap, S), 0) == slot_ref[0]
    p = jnp.where(sel, 1.0, 0.0).astype(BF16)
    vals = jnp.sum(jnp.where(sel, aff_ref[0], 0.0), axis=-1, keepdims=True)
    xg = _dot(p, h_ref[...]).astype(BF16)
    g = _dot(xg, wg_ref[0])
    u = _dot(xg, wu_ref[0])
    a = (g * (1.0 / (1.0 + jnp.exp(-g))) * u).astype(BF16)
    o = (_dot(a, wd_ref[0]) * vals).astype(BF16)
    o_ref[...] += _dot_tn(p, o)


def _moe(slot, aff, h2, wg, wu, wd, cap):
    B, E, S = slot.shape
    T = h2.shape[0]
    slot3 = slot.reshape(B * E, 1, S)
    aff3 = aff.reshape(B * E, 1, S)
    vec = pl.BlockSpec((1, 1, S), lambda b, e: (b * E + e, 0, 0))
    tok = pl.BlockSpec((S, D_MODEL), lambda b, e: (b, 0))
    wspec = lambda w: pl.BlockSpec((1,) + w.shape[1:], lambda b, e: (e, 0, 0))
    return pl.pallas_call(
        functools.partial(_moe_kernel, cap),
        grid=(B, E),
        in_specs=[vec, vec, tok, wspec(wg), wspec(wu), wspec(wd)],
        out_specs=tok,
        out_shape=jax.ShapeDtypeStruct((T, D_MODEL), F32),
        compiler_params=_cparams(2),
        name="moe",
    )(slot3, aff3, h2, wg, wu, wd)


def _ple_final_kernel(x1_ref, moe_ref, p_ref, gp_ref, wg_ref, wp_ref, gf_ref, o_ref):
    x2 = x1_ref[...] + moe_ref[...]
    z = _dot(_rms(x2, gp_ref[...]).astype(BF16), wg_ref[...])
    gate = 1.0 / (1.0 + jnp.exp(-z))
    x3 = x2 + gate * _dot(p_ref[...].astype(BF16), wp_ref[...])
    o_ref[...] = _rms(x3, gf_ref[...])


def _ple_final(x1, moe, pf, g_ple, w_gate, w_proj, g_final):
    T = x1.shape[0]
    row = lambda n: pl.BlockSpec((TM, n), lambda i: (i, 0))
    full = lambda a: pl.BlockSpec(a.shape, lambda i: (0, 0))
    return pl.pallas_call(
        _ple_final_kernel,
        grid=(T // TM,),
        in_specs=[row(D_MODEL), row(D_MODEL), row(PLE_DIM), full(g_ple), full(w_gate), full(w_proj), full(g_final)],
        out_specs=row(D_MODEL),
        out_shape=jax.ShapeDtypeStruct((T, D_MODEL), F32),
        compiler_params=_cparams(1),
        name="ple_final",
    )(x1, moe, pf, g_ple, w_gate, w_proj, g_final)


def _prep_in_proj_weight(w_in):
    D = w_in.shape[0]
    z64 = jnp.zeros((D, 64), F32)
    k0, k1 = w_in[:, 512:576], w_in[:, 576:640]
    v0, v1 = w_in[:, 640:704], w_in[:, 704:768]
    t1, t2 = w_in[:, 1152:1168], w_in[:, 1168:1184]
    cols = [w_in[:, 0:512], k0, k0, k1, k1, v0, z64, z64, v0, v1, z64, z64, v1,
            w_in[:, 768:1152], t1, t2, t2, t1, z64]
    return jnp.concatenate(cols, axis=1).astype(BF16)


def _prep_mla_weights(w_uq, w_ukv):
    per_q = B_NOPE_DIM + B_ROPE_DIM
    w3 = w_uq.reshape(Q_RANK, B_HEADS, per_q)
    pad = LANES - per_q
    wq = jnp.pad(w3, ((0, 0), (0, 0), (0, pad))).reshape(Q_RANK, B_HEADS * LANES)
    t1 = w3[:, :, B_NOPE_DIM:B_NOPE_DIM + HALF_ROPE]
    t2 = w3[:, :, B_NOPE_DIM + HALF_ROPE:]
    wqr = jnp.concatenate([jnp.zeros((Q_RANK, B_HEADS, B_NOPE_DIM), F32), t2, t1,
                           jnp.zeros((Q_RANK, B_HEADS, pad), F32)], axis=2).reshape(Q_RANK, B_HEADS * LANES)
    w4 = w_ukv.reshape(KV_RANK, B_HEADS, B_NOPE_DIM + B_V_DIM)
    zk = jnp.zeros((KV_RANK, B_HEADS, LANES - B_NOPE_DIM), F32)
    wk = jnp.concatenate([w4[:, :, :B_NOPE_DIM], zk], axis=2).reshape(KV_RANK, B_HEADS * LANES)
    wv = w4[:, :, B_NOPE_DIM:]
    zv = jnp.zeros((KV_RANK, B_HEADS // 2, B_V_DIM), F32)
    wvlo = jnp.concatenate([wv[:, 0::2], zv], axis=2).reshape(KV_RANK, B_HEADS // 2 * LANES)
    wvhi = jnp.concatenate([zv, wv[:, 1::2]], axis=2).reshape(KV_RANK, B_HEADS // 2 * LANES)
    wkv = jnp.concatenate([wk, wvlo, wvhi], axis=1)
    return wq.astype(BF16), wqr.astype(BF16), wkv.astype(BF16)


def kernel(x, p, positions, rel_bias, norm_mix_g, w_in, sink, g_cq, g_ckv, w_uq, w_ukv, g_out_a, g_out_b, w_out,
           norm_ffn_g, w_router, w_e_gate, w_e_up, w_e_down, norm_ple_g, w_ple_gate, w_ple_proj, final_norm_g):
    B, S, D = x.shape
    T = B * S
    cap = CAPACITY_FACTOR * S // N_EXPERTS
    xf = x.reshape(T, D)
    cos16, sin16 = _rope_tables(positions)
    cos_t = jnp.tile(cos16, (1, LANES // HALF_ROPE))
    sin_t = jnp.tile(sin16, (1, LANES // HALF_ROPE))
    assert w_in.shape[0] == 1, "single-layer block: the final norm is fused into the last kernel"
    i = 0
    w1 = _prep_in_proj_weight(w_in[i])
    wq, wqr, wkv = _prep_mla_weights(w_uq[i], w_ukv[i])
    qa, kd, va4, qm, km, vlo, vhi = _in_proj(
        xf, norm_mix_g[i].reshape(1, D), w1, g_cq[i].reshape(1, -1), g_ckv[i].reshape(1, -1),
        wq, wqr, wkv, cos_t, sin_t)
    ya = _window_attn(qa, kd, va4, rel_bias, sink[i], g_out_a[i].reshape(1, -1), B, S)
    yb = _mla_attn(qm, km, vlo, vhi, g_out_b[i].reshape(1, -1), B, S)
    x1, h2, aff = _out_proj(xf, ya, yb, w_out[i].astype(BF16), norm_ffn_g[i].reshape(1, D),
                            w_router[i].T, B, S)
    slot = _route(aff, cap)
    moe = _moe(slot, aff, h2, w_e_gate[i].astype(BF16), w_e_up[i].astype(BF16), w_e_down[i].astype(BF16), cap)
    out = _ple_final(x1, moe, p[i].reshape(T, -1), norm_ple_g[i].reshape(1, D),
                     w_ple_gate[i].astype(BF16), w_ple_proj[i].astype(BF16), final_norm_g.reshape(1, D))
    return out.reshape(B, S, D)
```

```python
import functools

import numpy as np
import jax
import jax.numpy as jnp
from jax import lax
from jax.experimental import pallas as pl
from jax.experimental.pallas import tpu as pltpu

F32 = jnp.float32
BF16 = jnp.bfloat16

D_MODEL = 1024
BLK = 128
WINDOW = 128
A_HEADS = 8
A_KV_HEADS = 2
A_HEAD_DIM = 64
B_HEADS = 8
B_NOPE_DIM = 64
B_ROPE_DIM = 32
B_V_DIM = 64
Q_RANK = 256
KV_RANK = 128
ROPE_THETA = 10000.0
NUM_BUCKETS = 32
MAX_DISTANCE = 128
N_EXPERTS = 16
EXPERT_FF = 512
CAPACITY_FACTOR = 2
PLE_DIM = 256
EPS = 1e-6

LANES = 128
HALF_ROPE = B_ROPE_DIM // 2
MASKED = -1e30
LOG2E = 1.4426950408889634
VMEM_LIMIT = 56 * 1024 * 1024
TM = 512
TQ = 512
MLA_ROWS = 128
WIN_QBLOCKS = 4


def _cparams(n_axes):
    return pltpu.CompilerParams(dimension_semantics=("arbitrary",) * n_axes,
                                vmem_limit_bytes=VMEM_LIMIT)


def _rms(x, g):
    return x * lax.rsqrt(jnp.mean(x * x, axis=-1, keepdims=True) + EPS) * g


def _dot(a, b):
    return jnp.dot(a, b, preferred_element_type=F32)


def _dot_nt(a, b):
    return lax.dot_general(a, b, (((1,), (1,)), ((), ())), preferred_element_type=F32)


def _dot_tn(a, b):
    return lax.dot_general(a, b, (((0,), (0,)), ((), ())), preferred_element_type=F32)


ROPE_GROUPS = LANES // HALF_ROPE
ROPE_ROWS = 256


def _rope_table_kernel(pos_ref, invf_ref, c_ref, s_ref):
    ang = pos_ref[...] * invf_ref[...]
    cd = jnp.cos(ang)
    sd = jnp.sin(ang)
    lane = lax.broadcasted_iota(jnp.int32, (1, LANES), 1)
    in_t1 = (lane >= B_NOPE_DIM) & (lane < B_NOPE_DIM + HALF_ROPE)
    in_t2 = (lane >= B_NOPE_DIM + HALF_ROPE) & (lane < B_NOPE_DIM + B_ROPE_DIM)
    for g in range(ROPE_GROUPS):
        own = (lane >= g * HALF_ROPE) & (lane < (g + 1) * HALF_ROPE)
        xc = jnp.where(own, cd, 0.0)
        xs = jnp.where(own, sd, 0.0)
        shift = HALF_ROPE
        while shift < LANES:
            xc = xc + pltpu.roll(xc, shift, axis=1)
            xs = xs + pltpu.roll(xs, shift, axis=1)
            shift *= 2
        c_ref[g] = jnp.where(lane < B_NOPE_DIM, 1.0, jnp.where(in_t1 | in_t2, xc, 0.0))
        s_ref[g] = jnp.where(in_t1, -xs, jnp.where(in_t2, xs, 0.0))


def _rope_tables(positions):
    T = positions.size
    rows = T // ROPE_GROUPS
    inv_freq = 1.0 / (ROPE_THETA ** (jnp.arange(0, B_ROPE_DIM, 2, dtype=F32) / B_ROPE_DIM))
    pos = jnp.repeat(positions.astype(F32).reshape(ROPE_GROUPS, rows).T, HALF_ROPE, axis=1)
    invf = jnp.tile(inv_freq, ROPE_GROUPS).reshape(1, LANES)
    out_spec = pl.BlockSpec((ROPE_GROUPS, ROPE_ROWS, LANES), lambda i: (0, i, 0))
    c, s = pl.pallas_call(
        _rope_table_kernel,
        grid=(rows // ROPE_ROWS,),
        in_specs=[pl.BlockSpec((ROPE_ROWS, LANES), lambda i: (i, 0)), pl.BlockSpec((1, LANES), lambda i: (0, 0))],
        out_specs=[out_spec] * 2,
        out_shape=[jax.ShapeDtypeStruct((ROPE_GROUPS, rows, LANES), F32)] * 2,
        compiler_params=_cparams(1),
        name="rope_tables",
    )(pos, invf)
    return c.reshape(T, LANES), s.reshape(T, LANES)


C_QA = 0
C_KD = C_QA + 512
C_VA = C_KD + 256
C_CQ = C_VA + 512
C_CKV = C_CQ + Q_RANK
C_KR = C_CKV + KV_RANK
C_END = C_KR + LANES


def _in_proj_kernel(x_ref, g_ref, w1_ref, gcq_ref, gckv_ref, wq_ref, wqr_ref, wkv_ref, c_ref, s_ref,
                    qa_ref, kd_ref, va_ref, qm_ref, km_ref, vlo_ref, vhi_ref):
    h = _rms(x_ref[...], g_ref[...]).astype(BF16)
    z = _dot(h, w1_ref[...])
    qa_ref[...] = (z[:, C_QA:C_KD] * (A_HEAD_DIM ** -0.5 * LOG2E)).astype(BF16)
    kd_ref[...] = z[:, C_KD:C_VA].astype(BF16)
    va_ref[...] = z[:, C_VA:C_CQ].astype(BF16)
    cqn = _rms(z[:, C_CQ:C_CKV], gcq_ref[...]).astype(BF16)
    ckvn = _rms(z[:, C_CKV:C_KR], gckv_ref[...]).astype(BF16)
    krt = z[:, C_KR:C_END]

    lane = lax.broadcasted_iota(jnp.int32, (1, LANES), 1)
    c_full = c_ref[...]
    s_rope = s_ref[...]
    c_rope = jnp.where(lane < B_NOPE_DIM, 0.0, c_full)

    q1 = _dot(cqn, wq_ref[...])
    q2 = _dot(cqn, wqr_ref[...])
    scale = (B_NOPE_DIM + B_ROPE_DIM) ** -0.5 * LOG2E
    for hd in range(B_HEADS):
        sl = slice(hd * LANES, (hd + 1) * LANES)
        qm_ref[:, sl] = ((q1[:, sl] * c_full + q2[:, sl] * s_rope) * scale).astype(BF16)

    kv = _dot(ckvn, wkv_ref[...])
    kr_part = pltpu.roll(krt, 64, axis=1) * c_rope + pltpu.roll(krt, 32, axis=1) * s_rope
    for hd in range(B_HEADS):
        sl = slice(hd * LANES, (hd + 1) * LANES)
        km_ref[:, sl] = (kv[:, sl] + kr_part).astype(BF16)
    vlo_ref[...] = kv[:, 1024:1536].astype(BF16)
    vhi_ref[...] = kv[:, 1536:2048].astype(BF16)


def _in_proj(xf, g_mix, w1, g_cq, g_ckv, wq, wqr, wkv, c_tab, s_tab):
    T = xf.shape[0]
    row = lambda n: pl.BlockSpec((TM, n), lambda i: (i, 0))
    full = lambda a: pl.BlockSpec(a.shape, lambda i: (0, 0))
    outs = [512, 256, 512, 1024, 1024, 512, 512]
    return pl.pallas_call(
        _in_proj_kernel,
        grid=(T // TM,),
        in_specs=[row(D_MODEL), full(g_mix), full(w1), full(g_cq), full(g_ckv), full(wq), full(wqr), full(wkv),
                  row(LANES), row(LANES)],
        out_specs=[row(n) for n in outs],
        out_shape=[jax.ShapeDtypeStruct((T, n), BF16) for n in outs],
        compiler_params=_cparams(1),
        name="in_proj",
    )(xf, g_mix, w1, g_cq, g_ckv, wq, wqr, wkv, c_tab, s_tab)


def _bucket_map():
    qi = np.arange(BLK)[:, None]
    kj = np.arange(3 * BLK)[None, :]
    rel = kj - BLK - qi
    n = np.abs(rel)
    half = NUM_BUCKETS // 2
    max_exact = half // 2
    thresholds = [int(np.ceil(max_exact * 2 ** (k / 2) - 1e-9)) for k in range(1, half - max_exact)]
    large = max_exact + sum((n >= t).astype(np.int64) for t in thresholds)
    large = np.minimum(large, half - 1)
    bucket = np.where(rel > 0, half, 0) + np.where(n < max_exact, n, large)
    return np.where(n <= WINDOW, bucket, -1).astype(np.int32)


def _window_attn_kernel(n_steps, relb_ref, sink_ref, bmap_ref, g_ref, q_ref, kp_ref, kc_ref, kn_ref,
                        vp_ref, vc_ref, vn_ref, o_ref, bias_ref):
    b = pl.program_id(0)
    i = pl.program_id(1)

    @pl.when((b == 0) & (i == 0))
    def _build_bias():
        bmap = bmap_ref[...]
        col = lax.broadcasted_iota(jnp.int32, (1, 3 * BLK), 1)
        for hd in range(A_HEADS):
            bias_ref[1, hd] = jnp.full((BLK, 3 * BLK), MASKED, F32)
        for bk in range(NUM_BUCKETS):
            m = bmap == bk
            for hd in range(A_HEADS):
                bias_ref[1, hd] = jnp.where(m, relb_ref[bk, hd] * LOG2E, bias_ref[1, hd])
        for hd in range(A_HEADS):
            bias_ref[0, hd] = jnp.where(col < BLK, MASKED, bias_ref[1, hd])
            bias_ref[2, hd] = jnp.where(col >= 2 * BLK, MASKED, bias_ref[1, hd])

    lane = lax.broadcasted_iota(jnp.int32, (1, LANES), 1)
    lo = lane < A_HEAD_DIM
    ones_lo = jnp.broadcast_to(jnp.where(lo, 1.0, 0.0).astype(BF16), (3 * BLK, LANES))
    ones_hi = jnp.broadcast_to(jnp.where(lo, 0.0, 1.0).astype(BF16), (3 * BLK, LANES))

    def key_blocks(p_ref, c_ref, n_ref, sl):
        return ([p_ref[:, sl]] + [c_ref[c * BLK:(c + 1) * BLK, sl] for c in range(WIN_QBLOCKS)] + [n_ref[:, sl]])

    for c in range(WIN_QBLOCKS):
        rows = slice(c * BLK, (c + 1) * BLK)
        if c == 0:
            variant = jnp.where(i == 0, 0, 1)
        elif c == WIN_QBLOCKS - 1:
            variant = jnp.where(i == n_steps - 1, 2, 1)
        else:
            variant = 1
        probs, sink_terms = [], []
        for hd in range(A_HEADS):
            g = hd // (A_HEADS // A_KV_HEADS)
            qt = q_ref[rows, (hd // 2) * LANES:(hd // 2 + 1) * LANES]
            qm = jnp.where(lo if hd % 2 == 0 else ~lo, qt, jnp.zeros_like(qt))
            ksl = slice(g * LANES, (g + 1) * LANES)
            kcat = jnp.concatenate(key_blocks(kp_ref, kc_ref, kn_ref, ksl)[c:c + 3], axis=0)
            s = _dot_nt(qm, kcat) + bias_ref[variant, hd]
            sk = sink_ref[hd] * LOG2E
            m = jnp.maximum(jnp.max(s, axis=-1, keepdims=True), sk)
            probs.append(jnp.exp2(s - m).astype(BF16))
            sink_terms.append(jnp.exp2(sk - m))

        outs = []
        for j in range(A_HEADS // 2):
            g = (2 * j) // (A_HEADS // A_KV_HEADS)
            sl_e = slice((2 * g) * LANES, (2 * g + 1) * LANES)
            sl_o = slice((2 * g + 1) * LANES, (2 * g + 2) * LANES)
            v_e = jnp.concatenate(key_blocks(vp_ref, vc_ref, vn_ref, sl_e)[c:c + 3], axis=0)
            v_o = jnp.concatenate(key_blocks(vp_ref, vc_ref, vn_ref, sl_o)[c:c + 3], axis=0)
            o = (_dot(probs[2 * j], jnp.concatenate([v_e, ones_lo], axis=1))
                 + _dot(probs[2 * j + 1], jnp.concatenate([v_o, ones_hi], axis=1)))
            denom = o[:, LANES:] + jnp.where(lo, sink_terms[2 * j], sink_terms[2 * j + 1])
            outs.append(o[:, :LANES] / denom)
        ya = jnp.concatenate(outs, axis=1)
        o_ref[rows, :] = _rms(ya, g_ref[...]).astype(BF16)


def _window_attn(qa, kd, va4, rel_bias, sink, g_out_a, B, S):
    nb = S // BLK
    n_steps = nb // WIN_QBLOCKS
    assert nb >= 2, "first and last query blocks use distinct edge masks"
    bmap = jnp.asarray(_bucket_map())
    smem = pl.BlockSpec(memory_space=pltpu.SMEM)
    cur = lambda n: pl.BlockSpec((WIN_QBLOCKS * BLK, n), lambda b, i: (b * n_steps + i, 0))
    prv = lambda n: pl.BlockSpec((BLK, n), lambda b, i: (b * nb + jnp.maximum(i * WIN_QBLOCKS - 1, 0), 0))
    nxt = lambda n: pl.BlockSpec((BLK, n), lambda b, i: (b * nb + jnp.minimum((i + 1) * WIN_QBLOCKS, nb - 1), 0))
    return pl.pallas_call(
        functools.partial(_window_attn_kernel, n_steps),
        grid=(B, n_steps),
        in_specs=[smem, smem,
                  pl.BlockSpec((BLK, 3 * BLK), lambda b, i: (0, 0)),
                  pl.BlockSpec((1, 512), lambda b, i: (0, 0)),
                  cur(512), prv(256), cur(256), nxt(256), prv(512), cur(512), nxt(512)],
        out_specs=cur(512),
        out_shape=jax.ShapeDtypeStruct((B * S, 512), BF16),
        scratch_shapes=[pltpu.VMEM((3, A_HEADS, BLK, 3 * BLK), F32)],
        compiler_params=_cparams(2),
        name="window_attn",
    )(rel_bias, sink, bmap, g_out_a, qa, kd, kd, kd, va4, va4, va4)


def _mla_attn_kernel(g_ref, q_ref, k_ref, vlo_ref, vhi_ref, o_ref):
    S = k_ref.shape[0]
    lane = lax.broadcasted_iota(jnp.int32, (1, LANES), 1)
    lo = lane < B_V_DIM
    ones_lo = jnp.broadcast_to(jnp.where(lo, 1.0, 0.0).astype(BF16), (S, LANES))
    ones_hi = jnp.broadcast_to(jnp.where(lo, 0.0, 1.0).astype(BF16), (S, LANES))
    for r in range(q_ref.shape[0] // MLA_ROWS):
        rows = slice(r * MLA_ROWS, (r + 1) * MLA_ROWS)
        outs = []
        for j in range(B_HEADS // 2):
            ps = []
            for hd in (2 * j, 2 * j + 1):
                sl = slice(hd * LANES, (hd + 1) * LANES)
                s = _dot_nt(q_ref[rows, sl], k_ref[:, sl])
                m = jnp.max(s, axis=-1, keepdims=True)
                ps.append(jnp.exp2(s - m).astype(BF16))
            sl = slice(j * LANES, (j + 1) * LANES)
            o = (_dot(ps[0], jnp.concatenate([vlo_ref[:, sl], ones_lo], axis=1))
                 + _dot(ps[1], jnp.concatenate([vhi_ref[:, sl], ones_hi], axis=1)))
            outs.append(o[:, :LANES] / o[:, LANES:])
        yb = jnp.concatenate(outs, axis=1)
        o_ref[rows, :] = _rms(yb, g_ref[...]).astype(BF16)


def _mla_attn(qm, km, vlo, vhi, g_out_b, B, S):
    nq = S // TQ
    qspec = lambda n: pl.BlockSpec((TQ, n), lambda b, i: (b * nq + i, 0))
    kspec = lambda n: pl.BlockSpec((S, n), lambda b, i: (b, 0))
    return pl.pallas_call(
        _mla_attn_kernel,
        grid=(B, nq),
        in_specs=[pl.BlockSpec((1, 512), lambda b, i: (0, 0)), qspec(1024), kspec(1024), kspec(512), kspec(512)],
        out_specs=qspec(512),
        out_shape=jax.ShapeDtypeStruct((B * S, 512), BF16),
        compiler_params=_cparams(2),
        name="mla_attn",
    )(g_out_b, qm, km, vlo, vhi)


def _split_bf16(v):
    hi = v.astype(BF16)
    return hi, (v - hi.astype(F32)).astype(BF16)


def _out_proj_kernel(x_ref, ya_ref, yb_ref, wo_ref, g_ref, wr_ref, x1_ref, h2_ref, aff_ref):
    x1 = x_ref[...] + _dot(ya_ref[...], wo_ref[0:512, :]) + _dot(yb_ref[...], wo_ref[512:1024, :])
    x1_ref[...] = x1
    h2 = _rms(x1, g_ref[...])
    h2_ref[...] = h2.astype(BF16)
    h_hi, h_lo = _split_bf16(h2)
    w_hi, w_lo = _split_bf16(wr_ref[...])
    logits = _dot_nt(w_hi, h_hi) + (_dot_nt(w_hi, h_lo) + _dot_nt(w_lo, h_hi))
    m = jnp.max(logits, axis=0, keepdims=True)
    e = jnp.exp(logits - m)
    aff_ref[0] = e / jnp.sum(e, axis=0, keepdims=True)


def _out_proj(xf, ya, yb, w_out, g_ffn, w_router_t, B, S):
    T = xf.shape[0]
    per_b = S // TM
    row = lambda n: pl.BlockSpec((TM, n), lambda i: (i, 0))
    full = lambda a: pl.BlockSpec(a.shape, lambda i: (0, 0))
    return pl.pallas_call(
        _out_proj_kernel,
        grid=(T // TM,),
        in_specs=[row(D_MODEL), row(512), row(512), full(w_out), full(g_ffn), full(w_router_t)],
        out_specs=[row(D_MODEL), row(D_MODEL),
                   pl.BlockSpec((1, N_EXPERTS, TM), lambda i: (i // per_b, 0, i % per_b))],
        out_shape=[jax.ShapeDtypeStruct((T, D_MODEL), F32), jax.ShapeDtypeStruct((T, D_MODEL), BF16),
                   jax.ShapeDtypeStruct((B, N_EXPERTS, S), F32)],
        compiler_params=_cparams(1),
        name="out_proj",
    )(xf, ya, yb, w_out, g_ffn, w_router_t)


CHUNK = 256


def _prefix_count(flags_f32, tri):
    S = flags_f32.shape[1]
    carry = jnp.zeros((flags_f32.shape[0], 1), F32)
    parts = []
    for c in range(S // CHUNK):
        blk = flags_f32[:, c * CHUNK:(c + 1) * CHUNK]
        parts.append(_dot(blk.astype(BF16), tri) + carry)
        carry = carry + jnp.sum(blk, axis=-1, keepdims=True)
    return jnp.concatenate(parts, axis=1)


def _route_kernel(cap, aff_ref, slot_ref):
    aff = aff_ref[0]
    bits = pltpu.bitcast(aff, jnp.int32)

    def body(it, prefix):
        cand = prefix | jnp.left_shift(jnp.int32(1), 30 - it)
        cnt = jnp.sum(jnp.where(bits >= cand, 1.0, 0.0), axis=-1, keepdims=True)
        return jnp.where(cnt >= cap, cand, prefix)

    thr = lax.fori_loop(0, 31, body, jnp.zeros((aff.shape[0], 1), jnp.int32))
    gt = jnp.where(bits > thr, 1.0, 0.0)
    eq = jnp.where(bits == thr, 1.0, 0.0)
    need = cap - jnp.sum(gt, axis=-1, keepdims=True)
    r = lax.broadcasted_iota(jnp.int32, (CHUNK, CHUNK), 0)
    c = lax.broadcasted_iota(jnp.int32, (CHUNK, CHUNK), 1)
    tri = jnp.where(r < c, 1.0, 0.0).astype(BF16)
    sel = gt + eq * jnp.where(_prefix_count(eq, tri) < need, 1.0, 0.0)
    slot = _prefix_count(sel, tri)
    slot_ref[0] = jnp.where(sel > 0.5, slot, -1.0).astype(jnp.int32)


def _route(aff, cap):
    B, E, S = aff.shape
    spec = pl.BlockSpec((1, E, S), lambda b: (b, 0, 0))
    return pl.pallas_call(
        functools.partial(_route_kernel, cap),
        grid=(B,),
        in_specs=[spec],
        out_specs=spec,
        out_shape=jax.ShapeDtypeStruct((B, E, S), jnp.int32),
        compiler_params=_cparams(1),
        name="route",
    )(aff)


def _moe_kernel(cap, slot_ref, aff_ref, h_ref, wg_ref, wu_ref, wd_ref, o_ref):
    e = pl.program_id(1)

    @pl.when(e == 0)
    def _zero():
        o_ref[...] = jnp.zeros_like(o_ref)

    S = h_ref.shape[0]
    sel = lax.broadcasted_iota(jnp.int32, (cap, S), 0) == slot_ref[0]
    p = jnp.where(sel, 1.0, 0.0).astype(BF16)
    vals = jnp.sum(jnp.where(sel, aff_ref[0], 0.0), axis=-1, keepdims=True)
    xg = _dot(p, h_ref[...]).astype(BF16)
    g = _dot(xg, wg_ref[0])
    u = _dot(xg, wu_ref[0])
    a = (g * (1.0 / (1.0 + jnp.exp(-g))) * u).astype(BF16)
    o = (_dot(a, wd_ref[0]) * vals).astype(BF16)
    o_ref[...] += _dot_tn(p, o)


def _moe(slot, aff, h2, wg, wu, wd, cap):
    B, E, S = slot.shape
    T = h2.shape[0]
    slot3 = slot.reshape(B * E, 1, S)
    aff3 = aff.reshape(B * E, 1, S)
    vec = pl.BlockSpec((1, 1, S), lambda b, e: (b * E + e, 0, 0))
    tok = pl.BlockSpec((S, D_MODEL), lambda b, e: (b, 0))
    wspec = lambda w: pl.BlockSpec((1,) + w.shape[1:], lambda b, e: (e, 0, 0))
    return pl.pallas_call(
        functools.partial(_moe_kernel, cap),
        grid=(B, E),
        in_specs=[vec, vec, tok, wspec(wg), wspec(wu), wspec(wd)],
        out_specs=tok,
        out_shape=jax.ShapeDtypeStruct((T, D_MODEL), F32),
        compiler_params=_cparams(2),
        name="moe",
    )(slot3, aff3, h2, wg, wu, wd)


def _ple_final_kernel(x1_ref, moe_ref, p_ref, gp_ref, wg_ref, wp_ref, gf_ref, o_ref):
    x2 = x1_ref[...] + moe_ref[...]
    z = _dot(_rms(x2, gp_ref[...]).astype(BF16), wg_ref[...])
    gate = 1.0 / (1.0 + jnp.exp(-z))
    x3 = x2 + gate * _dot(p_ref[...].astype(BF16), wp_ref[...])
    o_ref[...] = _rms(x3, gf_ref[...])


def _ple_final(x1, moe, pf, g_ple, w_gate, w_proj, g_final):
    T = x1.shape[0]
    row = lambda n: pl.BlockSpec((TM, n), lambda i: (i, 0))
    full = lambda a: pl.BlockSpec(a.shape, lambda i: (0, 0))
    return pl.pallas_call(
        _ple_final_kernel,
        grid=(T // TM,),
        in_specs=[row(D_MODEL), row(D_MODEL), row(PLE_DIM), full(g_ple), full(w_gate), full(w_proj), full(g_final)],
        out_specs=row(D_MODEL),
        out_shape=jax.ShapeDtypeStruct((T, D_MODEL), F32),
        compiler_params=_cparams(1),
        name="ple_final",
    )(x1, moe, pf, g_ple, w_gate, w_proj, g_final)


def _prep_in_proj_weight(w_in):
    D = w_in.shape[0]
    z64 = jnp.zeros((D, 64), F32)
    k0, k1 = w_in[:, 512:576], w_in[:, 576:640]
    v0, v1 = w_in[:, 640:704], w_in[:, 704:768]
    t1, t2 = w_in[:, 1152:1168], w_in[:, 1168:1184]
    cols = [w_in[:, 0:512], k0, k0, k1, k1, v0, z64, z64, v0, v1, z64, z64, v1,
            w_in[:, 768:1152], t1, t2, t2, t1, z64]
    return jnp.concatenate(cols, axis=1).astype(BF16)


def _prep_mla_weights(w_uq, w_ukv):
    per_q = B_NOPE_DIM + B_ROPE_DIM
    w3 = w_uq.reshape(Q_RANK, B_HEADS, per_q)
    pad = LANES - per_q
    wq = jnp.pad(w3, ((0, 0), (0, 0), (0, pad))).reshape(Q_RANK, B_HEADS * LANES)
    t1 = w3[:, :, B_NOPE_DIM:B_NOPE_DIM + HALF_ROPE]
    t2 = w3[:, :, B_NOPE_DIM + HALF_ROPE:]
    wqr = jnp.concatenate([jnp.zeros((Q_RANK, B_HEADS, B_NOPE_DIM), F32), t2, t1,
                           jnp.zeros((Q_RANK, B_HEADS, pad), F32)], axis=2).reshape(Q_RANK, B_HEADS * LANES)
    w4 = w_ukv.reshape(KV_RANK, B_HEADS, B_NOPE_DIM + B_V_DIM)
    zk = jnp.zeros((KV_RANK, B_HEADS, LANES - B_NOPE_DIM), F32)
    wk = jnp.concatenate([w4[:, :, :B_NOPE_DIM], zk], axis=2).reshape(KV_RANK, B_HEADS * LANES)
    wv = w4[:, :, B_NOPE_DIM:]
    zv = jnp.zeros((KV_RANK, B_HEADS // 2, B_V_DIM), F32)
    wvlo = jnp.concatenate([wv[:, 0::2], zv], axis=2).reshape(KV_RANK, B_HEADS // 2 * LANES)
    wvhi = jnp.concatenate([zv, wv[:, 1::2]], axis=2).reshape(KV_RANK, B_HEADS // 2 * LANES)
    wkv = jnp.concatenate([wk, wvlo, wvhi], axis=1)
    return wq.astype(BF16), wqr.astype(BF16), wkv.astype(BF16)


def kernel(x, p, positions, rel_bias, norm_mix_g, w_in, sink, g_cq, g_ckv, w_uq, w_ukv, g_out_a, g_out_b, w_out,
           norm_ffn_g, w_router, w_e_gate, w_e_up, w_e_down, norm_ple_g, w_ple_gate, w_ple_proj, final_norm_g):
    B, S, D = x.shape
    T = B * S
    cap = CAPACITY_FACTOR * S // N_EXPERTS
    xf = x.reshape(T, D)
    c_tab, s_tab = _rope_tables(positions)
    assert w_in.shape[0] == 1, "single-layer block: the final norm is fused into the last kernel"
    i = 0
    w1 = _prep_in_proj_weight(w_in[i])
    wq, wqr, wkv = _prep_mla_weights(w_uq[i], w_ukv[i])
    qa, kd, va4, qm, km, vlo, vhi = _in_proj(
        xf, norm_mix_g[i].reshape(1, D), w1, g_cq[i].reshape(1, -1), g_ckv[i].reshape(1, -1),
        wq, wqr, wkv, c_tab, s_tab)
    ya = _window_attn(qa, kd, va4, rel_bias, sink[i], g_out_a[i].reshape(1, -1), B, S)
    yb = _mla_attn(qm, km, vlo, vhi, g_out_b[i].reshape(1, -1), B, S)
    x1, h2, aff = _out_proj(xf, ya, yb, w_out[i].astype(BF16), norm_ffn_g[i].reshape(1, D),
                            w_router[i].T, B, S)
    slot = _route(aff, cap)
    moe = _moe(slot, aff, h2, w_e_gate[i].astype(BF16), w_e_up[i].astype(BF16), w_e_down[i].astype(BF16), cap)
    out = _ple_final(x1, moe, p[i].reshape(T, -1), norm_ple_g[i].reshape(1, D),
                     w_ple_gate[i].astype(BF16), w_ple_proj[i].astype(BF16), final_norm_g.reshape(1, D))
    return out.reshape(B, S, D)
```

```python
import functools

import numpy as np
import jax
import jax.numpy as jnp
from jax import lax
from jax.experimental import pallas as pl
from jax.experimental.pallas import tpu as pltpu

F32 = jnp.float32
BF16 = jnp.bfloat16

D_MODEL = 1024
BLK = 128
WINDOW = 128
A_HEADS = 8
A_KV_HEADS = 2
A_HEAD_DIM = 64
B_HEADS = 8
B_NOPE_DIM = 64
B_ROPE_DIM = 32
B_V_DIM = 64
Q_RANK = 256
KV_RANK = 128
ROPE_THETA = 10000.0
NUM_BUCKETS = 32
MAX_DISTANCE = 128
N_EXPERTS = 16
EXPERT_FF = 512
CAPACITY_FACTOR = 2
PLE_DIM = 256
EPS = 1e-6

LANES = 128
HALF_ROPE = B_ROPE_DIM // 2
MASKED = -1e30
LOG2E = 1.4426950408889634
VMEM_LIMIT = 56 * 1024 * 1024
TM = 512
TQ = 512
MLA_ROWS = 128
WIN_QBLOCKS = 4


def _cparams(n_axes):
    return pltpu.CompilerParams(dimension_semantics=("arbitrary",) * n_axes,
                                vmem_limit_bytes=VMEM_LIMIT)


def _rms(x, g):
    return x * lax.rsqrt(jnp.mean(x * x, axis=-1, keepdims=True) + EPS) * g


def _dot(a, b):
    return jnp.dot(a, b, preferred_element_type=F32)


def _dot_nt(a, b):
    return lax.dot_general(a, b, (((1,), (1,)), ((), ())), preferred_element_type=F32)


def _dot_tn(a, b):
    return lax.dot_general(a, b, (((0,), (0,)), ((), ())), preferred_element_type=F32)


ROPE_GROUPS = LANES // HALF_ROPE
ROPE_ROWS = 256


def _rope_table_kernel(pos_ref, invf_ref, c_ref, s_ref):
    ang = pos_ref[...] * invf_ref[...]
    cd = jnp.cos(ang)
    sd = jnp.sin(ang)
    lane = lax.broadcasted_iota(jnp.int32, (1, LANES), 1)
    in_t1 = (lane >= B_NOPE_DIM) & (lane < B_NOPE_DIM + HALF_ROPE)
    in_t2 = (lane >= B_NOPE_DIM + HALF_ROPE) & (lane < B_NOPE_DIM + B_ROPE_DIM)
    for g in range(ROPE_GROUPS):
        own = (lane >= g * HALF_ROPE) & (lane < (g + 1) * HALF_ROPE)
        xc = jnp.where(own, cd, 0.0)
        xs = jnp.where(own, sd, 0.0)
        shift = HALF_ROPE
        while shift < LANES:
            xc = xc + pltpu.roll(xc, shift, axis=1)
            xs = xs + pltpu.roll(xs, shift, axis=1)
            shift *= 2
        c_ref[g] = jnp.where(lane < B_NOPE_DIM, 1.0, jnp.where(in_t1 | in_t2, xc, 0.0))
        s_ref[g] = jnp.where(in_t1, -xs, jnp.where(in_t2, xs, 0.0))


def _rope_tables(positions):
    T = positions.size
    rows = T // ROPE_GROUPS
    inv_freq = 1.0 / (ROPE_THETA ** (jnp.arange(0, B_ROPE_DIM, 2, dtype=F32) / B_ROPE_DIM))
    pos = jnp.repeat(positions.astype(F32).reshape(ROPE_GROUPS, rows).T, HALF_ROPE, axis=1)
    invf = jnp.tile(inv_freq, ROPE_GROUPS).reshape(1, LANES)
    out_spec = pl.BlockSpec((ROPE_GROUPS, ROPE_ROWS, LANES), lambda i: (0, i, 0))
    c, s = pl.pallas_call(
        _rope_table_kernel,
        grid=(rows // ROPE_ROWS,),
        in_specs=[pl.BlockSpec((ROPE_ROWS, LANES), lambda i: (i, 0)), pl.BlockSpec((1, LANES), lambda i: (0, 0))],
        out_specs=[out_spec] * 2,
        out_shape=[jax.ShapeDtypeStruct((ROPE_GROUPS, rows, LANES), F32)] * 2,
        compiler_params=_cparams(1),
        name="rope_tables",
    )(pos, invf)
    return c.reshape(T, LANES), s.reshape(T, LANES)


C_QA = 0
C_KD = C_QA + 512
C_VA = C_KD + 256
C_CQ = C_VA + 512
C_CKV = C_CQ + Q_RANK
C_KR = C_CKV + KV_RANK
C_END = C_KR + LANES


def _in_proj_kernel(x_ref, g_ref, w1_ref, gcq_ref, gckv_ref, wq_ref, wqr_ref, wkv_ref, c_ref, s_ref,
                    qa_ref, kd_ref, va_ref, qm_ref, km_ref, vlo_ref, vhi_ref):
    h = _rms(x_ref[...], g_ref[...]).astype(BF16)
    z = _dot(h, w1_ref[...])
    qa_ref[...] = (z[:, C_QA:C_KD] * (A_HEAD_DIM ** -0.5 * LOG2E)).astype(BF16)
    kd_ref[...] = z[:, C_KD:C_VA].astype(BF16)
    va_ref[...] = z[:, C_VA:C_CQ].astype(BF16)
    cqn = _rms(z[:, C_CQ:C_CKV], gcq_ref[...]).astype(BF16)
    ckvn = _rms(z[:, C_CKV:C_KR], gckv_ref[...]).astype(BF16)
    krt = z[:, C_KR:C_END]

    lane = lax.broadcasted_iota(jnp.int32, (1, LANES), 1)
    c_full = c_ref[...]
    s_rope = s_ref[...]
    c_rope = jnp.where(lane < B_NOPE_DIM, 0.0, c_full)

    q1 = _dot(cqn, wq_ref[...])
    q2 = _dot(cqn, wqr_ref[...])
    scale = (B_NOPE_DIM + B_ROPE_DIM) ** -0.5 * LOG2E
    for hd in range(B_HEADS):
        sl = slice(hd * LANES, (hd + 1) * LANES)
        qm_ref[:, sl] = ((q1[:, sl] * c_full + q2[:, sl] * s_rope) * scale).astype(BF16)

    kv = _dot(ckvn, wkv_ref[...])
    kr_part = pltpu.roll(krt, 64, axis=1) * c_rope + pltpu.roll(krt, 32, axis=1) * s_rope
    for hd in range(B_HEADS):
        sl = slice(hd * LANES, (hd + 1) * LANES)
        km_ref[:, sl] = (kv[:, sl] + kr_part).astype(BF16)
    vlo_ref[...] = kv[:, 1024:1536].astype(BF16)
    vhi_ref[...] = kv[:, 1536:2048].astype(BF16)


def _in_proj(xf, g_mix, w1, g_cq, g_ckv, wq, wqr, wkv, c_tab, s_tab):
    T = xf.shape[0]
    row = lambda n: pl.BlockSpec((TM, n), lambda i: (i, 0))
    full = lambda a: pl.BlockSpec(a.shape, lambda i: (0, 0))
    outs = [512, 256, 512, 1024, 1024, 512, 512]
    return pl.pallas_call(
        _in_proj_kernel,
        grid=(T // TM,),
        in_specs=[row(D_MODEL), full(g_mix), full(w1), full(g_cq), full(g_ckv), full(wq), full(wqr), full(wkv),
                  row(LANES), row(LANES)],
        out_specs=[row(n) for n in outs],
        out_shape=[jax.ShapeDtypeStruct((T, n), BF16) for n in outs],
        compiler_params=_cparams(1),
        name="in_proj",
    )(xf, g_mix, w1, g_cq, g_ckv, wq, wqr, wkv, c_tab, s_tab)


def _bucket_map():
    qi = np.arange(BLK)[:, None]
    kj = np.arange(3 * BLK)[None, :]
    rel = kj - BLK - qi
    n = np.abs(rel)
    half = NUM_BUCKETS // 2
    max_exact = half // 2
    thresholds = [int(np.ceil(max_exact * 2 ** (k / 2) - 1e-9)) for k in range(1, half - max_exact)]
    large = max_exact + sum((n >= t).astype(np.int64) for t in thresholds)
    large = np.minimum(large, half - 1)
    bucket = np.where(rel > 0, half, 0) + np.where(n < max_exact, n, large)
    return np.where(n <= WINDOW, bucket, -1).astype(np.int32)


def _window_attn_kernel(n_steps, relb_ref, sink_ref, bmap_ref, g_ref, q_ref, kp_ref, kc_ref, kn_ref,
                        vp_ref, vc_ref, vn_ref, o_ref, bias_ref):
    b = pl.program_id(0)
    i = pl.program_id(1)

    @pl.when((b == 0) & (i == 0))
    def _build_bias():
        bmap = bmap_ref[...]
        col = lax.broadcasted_iota(jnp.int32, (1, 3 * BLK), 1)
        for hd in range(A_HEADS):
            bias_ref[1, hd] = jnp.full((BLK, 3 * BLK), MASKED, F32)
        for bk in range(NUM_BUCKETS):
            m = bmap == bk
            for hd in range(A_HEADS):
                bias_ref[1, hd] = jnp.where(m, relb_ref[bk, hd] * LOG2E, bias_ref[1, hd])
        for hd in range(A_HEADS):
            bias_ref[0, hd] = jnp.where(col < BLK, MASKED, bias_ref[1, hd])
            bias_ref[2, hd] = jnp.where(col >= 2 * BLK, MASKED, bias_ref[1, hd])

    lane = lax.broadcasted_iota(jnp.int32, (1, LANES), 1)
    lo = lane < A_HEAD_DIM
    ones_lo = jnp.broadcast_to(jnp.where(lo, 1.0, 0.0).astype(BF16), (3 * BLK, LANES))
    ones_hi = jnp.broadcast_to(jnp.where(lo, 0.0, 1.0).astype(BF16), (3 * BLK, LANES))

    def key_blocks(p_ref, c_ref, n_ref, sl):
        return ([p_ref[:, sl]] + [c_ref[c * BLK:(c + 1) * BLK, sl] for c in range(WIN_QBLOCKS)] + [n_ref[:, sl]])

    for c in range(WIN_QBLOCKS):
        rows = slice(c * BLK, (c + 1) * BLK)
        if c == 0:
            variant = jnp.where(i == 0, 0, 1)
        elif c == WIN_QBLOCKS - 1:
            variant = jnp.where(i == n_steps - 1, 2, 1)
        else:
            variant = 1
        probs, sink_terms = [], []
        for hd in range(A_HEADS):
            g = hd // (A_HEADS // A_KV_HEADS)
            qt = q_ref[rows, (hd // 2) * LANES:(hd // 2 + 1) * LANES]
            qm = jnp.where(lo if hd % 2 == 0 else ~lo, qt, jnp.zeros_like(qt))
            ksl = slice(g * LANES, (g + 1) * LANES)
            kcat = jnp.concatenate(key_blocks(kp_ref, kc_ref, kn_ref, ksl)[c:c + 3], axis=0)
            s = _dot_nt(qm, kcat) + bias_ref[variant, hd]
            sk = sink_ref[hd] * LOG2E
            m = jnp.maximum(jnp.max(s, axis=-1, keepdims=True), sk)
            probs.append(jnp.exp2(s - m).astype(BF16))
            sink_terms.append(jnp.exp2(sk - m))

        outs = []
        for j in range(A_HEADS // 2):
            g = (2 * j) // (A_HEADS // A_KV_HEADS)
            sl_e = slice((2 * g) * LANES, (2 * g + 1) * LANES)
            sl_o = slice((2 * g + 1) * LANES, (2 * g + 2) * LANES)
            v_e = jnp.concatenate(key_blocks(vp_ref, vc_ref, vn_ref, sl_e)[c:c + 3], axis=0)
            v_o = jnp.concatenate(key_blocks(vp_ref, vc_ref, vn_ref, sl_o)[c:c + 3], axis=0)
            o = (_dot(probs[2 * j], jnp.concatenate([v_e, ones_lo], axis=1))
                 + _dot(probs[2 * j + 1], jnp.concatenate([v_o, ones_hi], axis=1)))
            denom = o[:, LANES:] + jnp.where(lo, sink_terms[2 * j], sink_terms[2 * j + 1])
            outs.append(o[:, :LANES] / denom)
        ya = jnp.concatenate(outs, axis=1)
        o_ref[rows, :] = _rms(ya, g_ref[...]).astype(BF16)


def _window_attn(qa, kd, va4, rel_bias, sink, g_out_a, B, S):
    nb = S // BLK
    n_steps = nb // WIN_QBLOCKS
    assert nb >= 2, "first and last query blocks use distinct edge masks"
    bmap = jnp.asarray(_bucket_map())
    smem = pl.BlockSpec(memory_space=pltpu.SMEM)
    cur = lambda n: pl.BlockSpec((WIN_QBLOCKS * BLK, n), lambda b, i: (b * n_steps + i, 0))
    prv = lambda n: pl.BlockSpec((BLK, n), lambda b, i: (b * nb + jnp.maximum(i * WIN_QBLOCKS - 1, 0), 0))
    nxt = lambda n: pl.BlockSpec((BLK, n), lambda b, i: (b * nb + jnp.minimum((i + 1) * WIN_QBLOCKS, nb - 1), 0))
    return pl.pallas_call(
        functools.partial(_window_attn_kernel, n_steps),
        grid=(B, n_steps),
        in_specs=[smem, smem,
                  pl.BlockSpec((BLK, 3 * BLK), lambda b, i: (0, 0)),
                  pl.BlockSpec((1, 512), lambda b, i: (0, 0)),
                  cur(512), prv(256), cur(256), nxt(256), prv(512), cur(512), nxt(512)],
        out_specs=cur(512),
        out_shape=jax.ShapeDtypeStruct((B * S, 512), BF16),
        scratch_shapes=[pltpu.VMEM((3, A_HEADS, BLK, 3 * BLK), F32)],
        compiler_params=_cparams(2),
        name="window_attn",
    )(rel_bias, sink, bmap, g_out_a, qa, kd, kd, kd, va4, va4, va4)


def _mla_attn_kernel(g_ref, q_ref, k_ref, vlo_ref, vhi_ref, o_ref):
    S = k_ref.shape[0]
    lane = lax.broadcasted_iota(jnp.int32, (1, LANES), 1)
    lo = lane < B_V_DIM
    ones_lo = jnp.broadcast_to(jnp.where(lo, 1.0, 0.0).astype(BF16), (S, LANES))
    ones_hi = jnp.broadcast_to(jnp.where(lo, 0.0, 1.0).astype(BF16), (S, LANES))
    for r in range(q_ref.shape[0] // MLA_ROWS):
        rows = slice(r * MLA_ROWS, (r + 1) * MLA_ROWS)
        outs = []
        for j in range(B_HEADS // 2):
            ps = []
            for hd in (2 * j, 2 * j + 1):
                sl = slice(hd * LANES, (hd + 1) * LANES)
                s = _dot_nt(q_ref[rows, sl], k_ref[:, sl])
                m = jnp.max(s, axis=-1, keepdims=True)
                ps.append(jnp.exp2(s - m).astype(BF16))
            sl = slice(j * LANES, (j + 1) * LANES)
            o = (_dot(ps[0], jnp.concatenate([vlo_ref[:, sl], ones_lo], axis=1))
                 + _dot(ps[1], jnp.concatenate([vhi_ref[:, sl], ones_hi], axis=1)))
            outs.append(o[:, :LANES] / o[:, LANES:])
        yb = jnp.concatenate(outs, axis=1)
        o_ref[rows, :] = _rms(yb, g_ref[...]).astype(BF16)


def _mla_attn(qm, km, vlo, vhi, g_out_b, B, S):
    nq = S // TQ
    qspec = lambda n: pl.BlockSpec((TQ, n), lambda b, i: (b * nq + i, 0))
    kspec = lambda n: pl.BlockSpec((S, n), lambda b, i: (b, 0))
    return pl.pallas_call(
        _mla_attn_kernel,
        grid=(B, nq),
        in_specs=[pl.BlockSpec((1, 512), lambda b, i: (0, 0)), qspec(1024), kspec(1024), kspec(512), kspec(512)],
        out_specs=qspec(512),
        out_shape=jax.ShapeDtypeStruct((B * S, 512), BF16),
        compiler_params=_cparams(2),
        name="mla_attn",
    )(g_out_b, qm, km, vlo, vhi)


def _split_bf16(v):
    hi = v.astype(BF16)
    return hi, (v - hi.astype(F32)).astype(BF16)


def _out_proj_kernel(x_ref, ya_ref, yb_ref, wo_ref, g_ref, wr_ref, x1_ref, h2_ref, aff_ref):
    x1 = x_ref[...] + _dot(ya_ref[...], wo_ref[0:512, :]) + _dot(yb_ref[...], wo_ref[512:1024, :])
    x1_ref[...] = x1
    h2 = _rms(x1, g_ref[...])
    h2_ref[...] = h2.astype(BF16)
    h_hi, h_lo = _split_bf16(h2)
    w_hi, w_lo = _split_bf16(wr_ref[...])
    logits = _dot_nt(w_hi, h_hi) + (_dot_nt(w_hi, h_lo) + _dot_nt(w_lo, h_hi))
    m = jnp.max(logits, axis=0, keepdims=True)
    e = jnp.exp(logits - m)
    aff_ref[0] = e / jnp.sum(e, axis=0, keepdims=True)


def _out_proj(xf, ya, yb, w_out, g_ffn, w_router_t, B, S):
    T = xf.shape[0]
    per_b = S // TM
    row = lambda n: pl.BlockSpec((TM, n), lambda i: (i, 0))
    full = lambda a: pl.BlockSpec(a.shape, lambda i: (0, 0))
    return pl.pallas_call(
        _out_proj_kernel,
        grid=(T // TM,),
        in_specs=[row(D_MODEL), row(512), row(512), full(w_out), full(g_ffn), full(w_router_t)],
        out_specs=[row(D_MODEL), row(D_MODEL),
                   pl.BlockSpec((1, N_EXPERTS, TM), lambda i: (i // per_b, 0, i % per_b))],
        out_shape=[jax.ShapeDtypeStruct((T, D_MODEL), F32), jax.ShapeDtypeStruct((T, D_MODEL), BF16),
                   jax.ShapeDtypeStruct((B, N_EXPERTS, S), F32)],
        compiler_params=_cparams(1),
        name="out_proj",
    )(xf, ya, yb, w_out, g_ffn, w_router_t)


CHUNK = 256


def _prefix_count(flags_f32, tri):
    S = flags_f32.shape[1]
    carry = jnp.zeros((flags_f32.shape[0], 1), F32)
    parts = []
    for c in range(S // CHUNK):
        blk = flags_f32[:, c * CHUNK:(c + 1) * CHUNK]
        parts.append(_dot(blk.astype(BF16), tri) + carry)
        carry = carry + jnp.sum(blk, axis=-1, keepdims=True)
    return jnp.concatenate(parts, axis=1)


def _route_kernel(cap, aff_ref, slot_ref):
    aff = aff_ref[0]
    bits = pltpu.bitcast(aff, jnp.int32)

    def body(it, prefix):
        cand = prefix | jnp.left_shift(jnp.int32(1), 30 - it)
        cnt = jnp.sum(jnp.where(bits >= cand, 1.0, 0.0), axis=-1, keepdims=True)
        return jnp.where(cnt >= cap, cand, prefix)

    thr = lax.fori_loop(0, 31, body, jnp.zeros((aff.shape[0], 1), jnp.int32))
    gt = jnp.where(bits > thr, 1.0, 0.0)
    eq = jnp.where(bits == thr, 1.0, 0.0)
    need = cap - jnp.sum(gt, axis=-1, keepdims=True)
    r = lax.broadcasted_iota(jnp.int32, (CHUNK, CHUNK), 0)
    c = lax.broadcasted_iota(jnp.int32, (CHUNK, CHUNK), 1)
    tri = jnp.where(r < c, 1.0, 0.0).astype(BF16)
    sel = gt + eq * jnp.where(_prefix_count(eq, tri) < need, 1.0, 0.0)
    slot = _prefix_count(sel, tri)
    slot_ref[0] = jnp.where(sel > 0.5, slot, -1.0).astype(jnp.int32)


def _route(aff, cap):
    B, E, S = aff.shape
    spec = pl.BlockSpec((1, E, S), lambda b: (b, 0, 0))
    return pl.pallas_call(
        functools.partial(_route_kernel, cap),
        grid=(B,),
        in_specs=[spec],
        out_specs=spec,
        out_shape=jax.ShapeDtypeStruct((B, E, S), jnp.int32),
        compiler_params=_cparams(1),
        name="route",
    )(aff)


def _moe_kernel(cap, slot0_ref, aff0_ref, slot1_ref, aff1_ref, slot2_ref, aff2_ref, h_ref, wg_ref, wu_ref, wd_ref,
                o_ref, pa_ref, pb_ref, va_ref, vb_ref):
    first = (pl.program_id(0) == 0) & (pl.program_id(1) == 0)
    S = h_ref.shape[0]

    def build(slot_row, aff_row, p_ref, v_ref):
        sel = lax.broadcasted_iota(jnp.int32, (cap, S), 0) == slot_row
        p_ref[...] = jnp.where(sel, 1.0, 0.0).astype(BF16)
        v_ref[...] = jnp.sum(jnp.where(sel, aff_row, 0.0), axis=-1, keepdims=True)

    def expert(p_ref, v_ref, i):
        p = p_ref[...]
        xg = _dot(p, h_ref[...]).astype(BF16)
        g = _dot(xg, wg_ref[i])
        u = _dot(xg, wu_ref[i])
        a = (g * (1.0 / (1.0 + jnp.exp(-g))) * u).astype(BF16)
        o = (_dot(a, wd_ref[i]) * v_ref[...]).astype(BF16)
        return _dot_tn(p, o)

    @pl.when(first)
    def _seed():
        build(slot0_ref[0], aff0_ref[0], pa_ref, va_ref)

    @pl.when(pl.program_id(1) == 0)
    def _zero():
        o_ref[...] = jnp.zeros_like(o_ref)

    build(slot1_ref[0], aff1_ref[0], pb_ref, vb_ref)
    o_ref[...] += expert(pa_ref, va_ref, 0)
    build(slot2_ref[0], aff2_ref[0], pa_ref, va_ref)
    o_ref[...] += expert(pb_ref, vb_ref, 1)


def _moe(slot, aff, h2, wg, wu, wd, cap):
    B, E, S = slot.shape
    T = h2.shape[0]
    assert E % 2 == 0
    last = B * E - 1
    slot3 = slot.reshape(B * E, 1, S)
    aff3 = aff.reshape(B * E, 1, S)
    row = lambda f: pl.BlockSpec((1, 1, S), lambda b, k: (f(b * E + 2 * k), 0, 0))
    seed, odd, nxt = row(lambda n: 0), row(lambda n: n + 1), row(lambda n: jnp.minimum(n + 2, last))
    tok = pl.BlockSpec((S, D_MODEL), lambda b, k: (b, 0))
    wspec = lambda w: pl.BlockSpec((2,) + w.shape[1:], lambda b, k: (k, 0, 0))
    return pl.pallas_call(
        functools.partial(_moe_kernel, cap),
        grid=(B, E // 2),
        in_specs=[seed, seed, odd, odd, nxt, nxt, tok, wspec(wg), wspec(wu), wspec(wd)],
        out_specs=tok,
        out_shape=jax.ShapeDtypeStruct((T, D_MODEL), F32),
        scratch_shapes=[pltpu.VMEM((cap, S), BF16), pltpu.VMEM((cap, S), BF16),
                        pltpu.VMEM((cap, 1), F32), pltpu.VMEM((cap, 1), F32)],
        compiler_params=_cparams(2),
        name="moe",
    )(slot3, aff3, slot3, aff3, slot3, aff3, h2, wg, wu, wd)


def _ple_final_kernel(x1_ref, moe_ref, p_ref, gp_ref, wg_ref, wp_ref, gf_ref, o_ref):
    x2 = x1_ref[...] + moe_ref[...]
    z = _dot(_rms(x2, gp_ref[...]).astype(BF16), wg_ref[...])
    gate = 1.0 / (1.0 + jnp.exp(-z))
    x3 = x2 + gate * _dot(p_ref[...].astype(BF16), wp_ref[...])
    o_ref[...] = _rms(x3, gf_ref[...])


def _ple_final(x1, moe, pf, g_ple, w_gate, w_proj, g_final):
    T = x1.shape[0]
    row = lambda n: pl.BlockSpec((TM, n), lambda i: (i, 0))
    full = lambda a: pl.BlockSpec(a.shape, lambda i: (0, 0))
    return pl.pallas_call(
        _ple_final_kernel,
        grid=(T // TM,),
        in_specs=[row(D_MODEL), row(D_MODEL), row(PLE_DIM), full(g_ple), full(w_gate), full(w_proj), full(g_final)],
        out_specs=row(D_MODEL),
        out_shape=jax.ShapeDtypeStruct((T, D_MODEL), F32),
        compiler_params=_cparams(1),
        name="ple_final",
    )(x1, moe, pf, g_ple, w_gate, w_proj, g_final)


def _prep_in_proj_weight(w_in):
    D = w_in.shape[0]
    z64 = jnp.zeros((D, 64), F32)
    k0, k1 = w_in[:, 512:576], w_in[:, 576:640]
    v0, v1 = w_in[:, 640:704], w_in[:, 704:768]
    t1, t2 = w_in[:, 1152:1168], w_in[:, 1168:1184]
    cols = [w_in[:, 0:512], k0, k0, k1, k1, v0, z64, z64, v0, v1, z64, z64, v1,
            w_in[:, 768:1152], t1, t2, t2, t1, z64]
    return jnp.concatenate(cols, axis=1).astype(BF16)


def _prep_mla_weights(w_uq, w_ukv):
    per_q = B_NOPE_DIM + B_ROPE_DIM
    w3 = w_uq.reshape(Q_RANK, B_HEADS, per_q)
    pad = LANES - per_q
    wq = jnp.pad(w3, ((0, 0), (0, 0), (0, pad))).reshape(Q_RANK, B_HEADS * LANES)
    t1 = w3[:, :, B_NOPE_DIM:B_NOPE_DIM + HALF_ROPE]
    t2 = w3[:, :, B_NOPE_DIM + HALF_ROPE:]
    wqr = jnp.concatenate([jnp.zeros((Q_RANK, B_HEADS, B_NOPE_DIM), F32), t2, t1,
                           jnp.zeros((Q_RANK, B_HEADS, pad), F32)], axis=2).reshape(Q_RANK, B_HEADS * LANES)
    w4 = w_ukv.reshape(KV_RANK, B_HEADS, B_NOPE_DIM + B_V_DIM)
    zk = jnp.zeros((KV_RANK, B_HEADS, LANES - B_NOPE_DIM), F32)
    wk = jnp.concatenate([w4[:, :, :B_NOPE_DIM], zk], axis=2).reshape(KV_RANK, B_HEADS * LANES)
    wv = w4[:, :, B_NOPE_DIM:]
    zv = jnp.zeros((KV_RANK, B_HEADS // 2, B_V_DIM), F32)
    wvlo = jnp.concatenate([wv[:, 0::2], zv], axis=2).reshape(KV_RANK, B_HEADS // 2 * LANES)
    wvhi = jnp.concatenate([zv, wv[:, 1::2]], axis=2).reshape(KV_RANK, B_HEADS // 2 * LANES)
    wkv = jnp.concatenate([wk, wvlo, wvhi], axis=1)
    return wq.astype(BF16), wqr.astype(BF16), wkv.astype(BF16)


def kernel(x, p, positions, rel_bias, norm_mix_g, w_in, sink, g_cq, g_ckv, w_uq, w_ukv, g_out_a, g_out_b, w_out,
           norm_ffn_g, w_router, w_e_gate, w_e_up, w_e_down, norm_ple_g, w_ple_gate, w_ple_proj, final_norm_g):
    B, S, D = x.shape
    T = B * S
    cap = CAPACITY_FACTOR * S // N_EXPERTS
    xf = x.reshape(T, D)
    c_tab, s_tab = _rope_tables(positions)
    assert w_in.shape[0] == 1, "single-layer block: the final norm is fused into the last kernel"
    i = 0
    w1 = _prep_in_proj_weight(w_in[i])
    wq, wqr, wkv = _prep_mla_weights(w_uq[i], w_ukv[i])
    qa, kd, va4, qm, km, vlo, vhi = _in_proj(
        xf, norm_mix_g[i].reshape(1, D), w1, g_cq[i].reshape(1, -1), g_ckv[i].reshape(1, -1),
        wq, wqr, wkv, c_tab, s_tab)
    ya = _window_attn(qa, kd, va4, rel_bias, sink[i], g_out_a[i].reshape(1, -1), B, S)
    yb = _mla_attn(qm, km, vlo, vhi, g_out_b[i].reshape(1, -1), B, S)
    x1, h2, aff = _out_proj(xf, ya, yb, w_out[i].astype(BF16), norm_ffn_g[i].reshape(1, D),
                            w_router[i].T, B, S)
    slot = _route(aff, cap)
    moe = _moe(slot, aff, h2, w_e_gate[i].astype(BF16), w_e_up[i].astype(BF16), w_e_down[i].astype(BF16), cap)
    out = _ple_final(x1, moe, p[i].reshape(T, -1), norm_ple_g[i].reshape(1, D),
                     w_ple_gate[i].astype(BF16), w_ple_proj[i].astype(BF16), final_norm_g.reshape(1, D))
    return out.reshape(B, S, D)
```

```python
import functools

import numpy as np
import jax
import jax.numpy as jnp
from jax import lax
from jax.experimental import pallas as pl
from jax.experimental.pallas import tpu as pltpu

F32 = jnp.float32
BF16 = jnp.bfloat16

D_MODEL = 1024
BLK = 128
WINDOW = 128
A_HEADS = 8
A_KV_HEADS = 2
A_HEAD_DIM = 64
B_HEADS = 8
B_NOPE_DIM = 64
B_ROPE_DIM = 32
B_V_DIM = 64
Q_RANK = 256
KV_RANK = 128
ROPE_THETA = 10000.0
NUM_BUCKETS = 32
MAX_DISTANCE = 128
N_EXPERTS = 16
EXPERT_FF = 512
CAPACITY_FACTOR = 2
PLE_DIM = 256
EPS = 1e-6

LANES = 128
HALF_ROPE = B_ROPE_DIM // 2
MASKED = -1e30
LOG2E = 1.4426950408889634
VMEM_LIMIT = 56 * 1024 * 1024
TM = 512
TQ = 512
MLA_ROWS = 128
WIN_QBLOCKS = 4


def _cparams(n_axes):
    return pltpu.CompilerParams(dimension_semantics=("arbitrary",) * n_axes,
                                vmem_limit_bytes=VMEM_LIMIT)


def _rms(x, g):
    return x * lax.rsqrt(jnp.mean(x * x, axis=-1, keepdims=True) + EPS) * g


def _dot(a, b):
    return jnp.dot(a, b, preferred_element_type=F32)


def _dot_nt(a, b):
    return lax.dot_general(a, b, (((1,), (1,)), ((), ())), preferred_element_type=F32)


def _dot_tn(a, b):
    return lax.dot_general(a, b, (((0,), (0,)), ((), ())), preferred_element_type=F32)


ROPE_GROUPS = LANES // HALF_ROPE
ROPE_ROWS = 256


def _rope_table_kernel(pos_ref, invf_ref, c_ref, s_ref):
    ang = pos_ref[...] * invf_ref[...]
    cd = jnp.cos(ang)
    sd = jnp.sin(ang)
    lane = lax.broadcasted_iota(jnp.int32, (1, LANES), 1)
    in_t1 = (lane >= B_NOPE_DIM) & (lane < B_NOPE_DIM + HALF_ROPE)
    in_t2 = (lane >= B_NOPE_DIM + HALF_ROPE) & (lane < B_NOPE_DIM + B_ROPE_DIM)
    for g in range(ROPE_GROUPS):
        own = (lane >= g * HALF_ROPE) & (lane < (g + 1) * HALF_ROPE)
        xc = jnp.where(own, cd, 0.0)
        xs = jnp.where(own, sd, 0.0)
        shift = HALF_ROPE
        while shift < LANES:
            xc = xc + pltpu.roll(xc, shift, axis=1)
            xs = xs + pltpu.roll(xs, shift, axis=1)
            shift *= 2
        c_ref[g] = jnp.where(lane < B_NOPE_DIM, 1.0, jnp.where(in_t1 | in_t2, xc, 0.0))
        s_ref[g] = jnp.where(in_t1, -xs, jnp.where(in_t2, xs, 0.0))


def _rope_tables(positions):
    T = positions.size
    rows = T // ROPE_GROUPS
    inv_freq = 1.0 / (ROPE_THETA ** (jnp.arange(0, B_ROPE_DIM, 2, dtype=F32) / B_ROPE_DIM))
    pos = jnp.repeat(positions.astype(F32).reshape(ROPE_GROUPS, rows).T, HALF_ROPE, axis=1)
    invf = jnp.tile(inv_freq, ROPE_GROUPS).reshape(1, LANES)
    out_spec = pl.BlockSpec((ROPE_GROUPS, ROPE_ROWS, LANES), lambda i: (0, i, 0))
    c, s = pl.pallas_call(
        _rope_table_kernel,
        grid=(rows // ROPE_ROWS,),
        in_specs=[pl.BlockSpec((ROPE_ROWS, LANES), lambda i: (i, 0)), pl.BlockSpec((1, LANES), lambda i: (0, 0))],
        out_specs=[out_spec] * 2,
        out_shape=[jax.ShapeDtypeStruct((ROPE_GROUPS, rows, LANES), F32)] * 2,
        compiler_params=_cparams(1),
        name="rope_tables",
    )(pos, invf)
    return c.reshape(T, LANES), s.reshape(T, LANES)


C_QA = 0
C_KD = C_QA + 512
C_VA = C_KD + 256
C_CQ = C_VA + 512
C_CKV = C_CQ + Q_RANK
C_KR = C_CKV + KV_RANK
C_END = C_KR + LANES


def _in_proj_kernel(x_ref, g_ref, w1_ref, gcq_ref, gckv_ref, wq_ref, wqr_ref, wkv_ref, c_ref, s_ref,
                    qa_ref, kd_ref, va_ref, qm_ref, km_ref, vlo_ref, vhi_ref):
    h = _rms(x_ref[...], g_ref[...]).astype(BF16)
    z = _dot(h, w1_ref[...])
    qa_ref[...] = (z[:, C_QA:C_KD] * (A_HEAD_DIM ** -0.5 * LOG2E)).astype(BF16)
    kd_ref[...] = z[:, C_KD:C_VA].astype(BF16)
    va_ref[...] = z[:, C_VA:C_CQ].astype(BF16)
    cqn = _rms(z[:, C_CQ:C_CKV], gcq_ref[...]).astype(BF16)
    ckvn = _rms(z[:, C_CKV:C_KR], gckv_ref[...]).astype(BF16)
    krt = z[:, C_KR:C_END]

    lane = lax.broadcasted_iota(jnp.int32, (1, LANES), 1)
    c_full = c_ref[...]
    s_rope = s_ref[...]
    c_rope = jnp.where(lane < B_NOPE_DIM, 0.0, c_full)

    q1 = _dot(cqn, wq_ref[...])
    q2 = _dot(cqn, wqr_ref[...])
    scale = (B_NOPE_DIM + B_ROPE_DIM) ** -0.5 * LOG2E
    for hd in range(B_HEADS):
        sl = slice(hd * LANES, (hd + 1) * LANES)
        qm_ref[:, sl] = ((q1[:, sl] * c_full + q2[:, sl] * s_rope) * scale).astype(BF16)

    kv = _dot(ckvn, wkv_ref[...])
    kr_part = pltpu.roll(krt, 64, axis=1) * c_rope + pltpu.roll(krt, 32, axis=1) * s_rope
    for hd in range(B_HEADS):
        sl = slice(hd * LANES, (hd + 1) * LANES)
        km_ref[:, sl] = (kv[:, sl] + kr_part).astype(BF16)
    vlo_ref[...] = kv[:, 1024:1536].astype(BF16)
    vhi_ref[...] = kv[:, 1536:2048].astype(BF16)


def _in_proj(xf, g_mix, w1, g_cq, g_ckv, wq, wqr, wkv, c_tab, s_tab):
    T = xf.shape[0]
    row = lambda n: pl.BlockSpec((TM, n), lambda i: (i, 0))
    full = lambda a: pl.BlockSpec(a.shape, lambda i: (0, 0))
    outs = [512, 256, 512, 1024, 1024, 512, 512]
    return pl.pallas_call(
        _in_proj_kernel,
        grid=(T // TM,),
        in_specs=[row(D_MODEL), full(g_mix), full(w1), full(g_cq), full(g_ckv), full(wq), full(wqr), full(wkv),
                  row(LANES), row(LANES)],
        out_specs=[row(n) for n in outs],
        out_shape=[jax.ShapeDtypeStruct((T, n), BF16) for n in outs],
        compiler_params=_cparams(1),
        name="in_proj",
    )(xf, g_mix, w1, g_cq, g_ckv, wq, wqr, wkv, c_tab, s_tab)


def _bucket_map():
    qi = np.arange(BLK)[:, None]
    kj = np.arange(3 * BLK)[None, :]
    rel = kj - BLK - qi
    n = np.abs(rel)
    half = NUM_BUCKETS // 2
    max_exact = half // 2
    thresholds = [int(np.ceil(max_exact * 2 ** (k / 2) - 1e-9)) for k in range(1, half - max_exact)]
    large = max_exact + sum((n >= t).astype(np.int64) for t in thresholds)
    large = np.minimum(large, half - 1)
    bucket = np.where(rel > 0, half, 0) + np.where(n < max_exact, n, large)
    return np.where(n <= WINDOW, bucket, -1).astype(np.int32)


def _window_attn_kernel(n_steps, relb_ref, sink_ref, bmap_ref, g_ref, q_ref, kp_ref, kc_ref, kn_ref,
                        vp_ref, vc_ref, vn_ref, o_ref, bias_ref):
    b = pl.program_id(0)
    i = pl.program_id(1)

    @pl.when((b == 0) & (i == 0))
    def _build_bias():
        bmap = bmap_ref[...]
        col = lax.broadcasted_iota(jnp.int32, (1, 3 * BLK), 1)
        for hd in range(A_HEADS):
            bias_ref[1, hd] = jnp.full((BLK, 3 * BLK), MASKED, F32)
        for bk in range(NUM_BUCKETS):
            m = bmap == bk
            for hd in range(A_HEADS):
                bias_ref[1, hd] = jnp.where(m, relb_ref[bk, hd] * LOG2E, bias_ref[1, hd])
        for hd in range(A_HEADS):
            bias_ref[0, hd] = jnp.where(col < BLK, MASKED, bias_ref[1, hd])
            bias_ref[2, hd] = jnp.where(col >= 2 * BLK, MASKED, bias_ref[1, hd])

    lane = lax.broadcasted_iota(jnp.int32, (1, LANES), 1)
    lo = lane < A_HEAD_DIM
    ones_lo = jnp.broadcast_to(jnp.where(lo, 1.0, 0.0).astype(BF16), (3 * BLK, LANES))
    ones_hi = jnp.broadcast_to(jnp.where(lo, 0.0, 1.0).astype(BF16), (3 * BLK, LANES))

    def key_blocks(p_ref, c_ref, n_ref, sl):
        return ([p_ref[:, sl]] + [c_ref[c * BLK:(c + 1) * BLK, sl] for c in range(WIN_QBLOCKS)] + [n_ref[:, sl]])

    for c in range(WIN_QBLOCKS):
        rows = slice(c * BLK, (c + 1) * BLK)
        if c == 0:
            variant = jnp.where(i == 0, 0, 1)
        elif c == WIN_QBLOCKS - 1:
            variant = jnp.where(i == n_steps - 1, 2, 1)
        else:
            variant = 1
        probs, sink_terms = [], []
        for hd in range(A_HEADS):
            g = hd // (A_HEADS // A_KV_HEADS)
            qt = q_ref[rows, (hd // 2) * LANES:(hd // 2 + 1) * LANES]
            qm = jnp.where(lo if hd % 2 == 0 else ~lo, qt, jnp.zeros_like(qt))
            ksl = slice(g * LANES, (g + 1) * LANES)
            kcat = jnp.concatenate(key_blocks(kp_ref, kc_ref, kn_ref, ksl)[c:c + 3], axis=0)
            s = _dot_nt(qm, kcat) + bias_ref[variant, hd]
            sk = sink_ref[hd] * LOG2E
            m = jnp.maximum(jnp.max(s, axis=-1, keepdims=True), sk)
            probs.append(jnp.exp2(s - m).astype(BF16))
            sink_terms.append(jnp.exp2(sk - m))

        outs = []
        for j in range(A_HEADS // 2):
            g = (2 * j) // (A_HEADS // A_KV_HEADS)
            sl_e = slice((2 * g) * LANES, (2 * g + 1) * LANES)
            sl_o = slice((2 * g + 1) * LANES, (2 * g + 2) * LANES)
            v_e = jnp.concatenate(key_blocks(vp_ref, vc_ref, vn_ref, sl_e)[c:c + 3], axis=0)
            v_o = jnp.concatenate(key_blocks(vp_ref, vc_ref, vn_ref, sl_o)[c:c + 3], axis=0)
            o = (_dot(probs[2 * j], jnp.concatenate([v_e, ones_lo], axis=1))
                 + _dot(probs[2 * j + 1], jnp.concatenate([v_o, ones_hi], axis=1)))
            denom = o[:, LANES:] + jnp.where(lo, sink_terms[2 * j], sink_terms[2 * j + 1])
            outs.append(o[:, :LANES] / denom)
        ya = jnp.concatenate(outs, axis=1)
        o_ref[rows, :] = _rms(ya, g_ref[...]).astype(BF16)


def _window_attn(qa, kd, va4, rel_bias, sink, g_out_a, B, S):
    nb = S // BLK
    n_steps = nb // WIN_QBLOCKS
    assert nb >= 2, "first and last query blocks use distinct edge masks"
    bmap = jnp.asarray(_bucket_map())
    smem = pl.BlockSpec(memory_space=pltpu.SMEM)
    cur = lambda n: pl.BlockSpec((WIN_QBLOCKS * BLK, n), lambda b, i: (b * n_steps + i, 0))
    prv = lambda n: pl.BlockSpec((BLK, n), lambda b, i: (b * nb + jnp.maximum(i * WIN_QBLOCKS - 1, 0), 0))
    nxt = lambda n: pl.BlockSpec((BLK, n), lambda b, i: (b * nb + jnp.minimum((i + 1) * WIN_QBLOCKS, nb - 1), 0))
    return pl.pallas_call(
        functools.partial(_window_attn_kernel, n_steps),
        grid=(B, n_steps),
        in_specs=[smem, smem,
                  pl.BlockSpec((BLK, 3 * BLK), lambda b, i: (0, 0)),
                  pl.BlockSpec((1, 512), lambda b, i: (0, 0)),
                  cur(512), prv(256), cur(256), nxt(256), prv(512), cur(512), nxt(512)],
        out_specs=cur(512),
        out_shape=jax.ShapeDtypeStruct((B * S, 512), BF16),
        scratch_shapes=[pltpu.VMEM((3, A_HEADS, BLK, 3 * BLK), F32)],
        compiler_params=_cparams(2),
        name="window_attn",
    )(rel_bias, sink, bmap, g_out_a, qa, kd, kd, kd, va4, va4, va4)


def _mla_attn_kernel(g_ref, q_ref, k_ref, vlo_ref, vhi_ref, o_ref):
    S = k_ref.shape[0]
    lane = lax.broadcasted_iota(jnp.int32, (1, LANES), 1)
    lo = lane < B_V_DIM
    ones_lo = jnp.broadcast_to(jnp.where(lo, 1.0, 0.0).astype(BF16), (S, LANES))
    ones_hi = jnp.broadcast_to(jnp.where(lo, 0.0, 1.0).astype(BF16), (S, LANES))
    for r in range(q_ref.shape[0] // MLA_ROWS):
        rows = slice(r * MLA_ROWS, (r + 1) * MLA_ROWS)
        outs = []
        for j in range(B_HEADS // 2):
            ps = []
            for hd in (2 * j, 2 * j + 1):
                sl = slice(hd * LANES, (hd + 1) * LANES)
                s = _dot_nt(q_ref[rows, sl], k_ref[:, sl])
                m = jnp.max(s, axis=-1, keepdims=True)
                ps.append(jnp.exp2(s - m).astype(BF16))
            sl = slice(j * LANES, (j + 1) * LANES)
            o = (_dot(ps[0], jnp.concatenate([vlo_ref[:, sl], ones_lo], axis=1))
                 + _dot(ps[1], jnp.concatenate([vhi_ref[:, sl], ones_hi], axis=1)))
            outs.append(o[:, :LANES] / o[:, LANES:])
        yb = jnp.concatenate(outs, axis=1)
        o_ref[rows, :] = _rms(yb, g_ref[...]).astype(BF16)


def _mla_attn(qm, km, vlo, vhi, g_out_b, B, S):
    nq = S // TQ
    qspec = lambda n: pl.BlockSpec((TQ, n), lambda b, i: (b * nq + i, 0))
    kspec = lambda n: pl.BlockSpec((S, n), lambda b, i: (b, 0))
    return pl.pallas_call(
        _mla_attn_kernel,
        grid=(B, nq),
        in_specs=[pl.BlockSpec((1, 512), lambda b, i: (0, 0)), qspec(1024), kspec(1024), kspec(512), kspec(512)],
        out_specs=qspec(512),
        out_shape=jax.ShapeDtypeStruct((B * S, 512), BF16),
        compiler_params=_cparams(2),
        name="mla_attn",
    )(g_out_b, qm, km, vlo, vhi)


def _split_bf16(v):
    hi = v.astype(BF16)
    return hi, (v - hi.astype(F32)).astype(BF16)


def _out_proj_kernel(x_ref, ya_ref, yb_ref, wo_ref, g_ref, wr_ref, x1_ref, h2_ref, aff_ref):
    x1 = x_ref[...] + _dot(ya_ref[...], wo_ref[0:512, :]) + _dot(yb_ref[...], wo_ref[512:1024, :])
    x1_ref[...] = x1
    h2 = _rms(x1, g_ref[...])
    h2_ref[...] = h2.astype(BF16)
    h_hi, h_lo = _split_bf16(h2)
    w_hi, w_lo = _split_bf16(wr_ref[...])
    logits = _dot_nt(w_hi, h_hi) + (_dot_nt(w_hi, h_lo) + _dot_nt(w_lo, h_hi))
    m = jnp.max(logits, axis=0, keepdims=True)
    e = jnp.exp(logits - m)
    aff_ref[0] = e / jnp.sum(e, axis=0, keepdims=True)


def _out_proj(xf, ya, yb, w_out, g_ffn, w_router_t, B, S):
    T = xf.shape[0]
    per_b = S // TM
    row = lambda n: pl.BlockSpec((TM, n), lambda i: (i, 0))
    full = lambda a: pl.BlockSpec(a.shape, lambda i: (0, 0))
    return pl.pallas_call(
        _out_proj_kernel,
        grid=(T // TM,),
        in_specs=[row(D_MODEL), row(512), row(512), full(w_out), full(g_ffn), full(w_router_t)],
        out_specs=[row(D_MODEL), row(D_MODEL),
                   pl.BlockSpec((1, N_EXPERTS, TM), lambda i: (i // per_b, 0, i % per_b))],
        out_shape=[jax.ShapeDtypeStruct((T, D_MODEL), F32), jax.ShapeDtypeStruct((T, D_MODEL), BF16),
                   jax.ShapeDtypeStruct((B, N_EXPERTS, S), F32)],
        compiler_params=_cparams(1),
        name="out_proj",
    )(xf, ya, yb, w_out, g_ffn, w_router_t)


CHUNK = 256
ROUTE_LOG_STEPS = 12
ROUTE_LIN_STEPS = 32


def _prefix_count(flags_f32, tri):
    S = flags_f32.shape[1]
    carry = jnp.zeros((flags_f32.shape[0], 1), F32)
    parts = []
    for c in range(S // CHUNK):
        blk = flags_f32[:, c * CHUNK:(c + 1) * CHUNK]
        parts.append(_dot(blk.astype(BF16), tri) + carry)
        carry = carry + jnp.sum(blk, axis=-1, keepdims=True)
    return jnp.concatenate(parts, axis=1)


def _route_kernel(cap, aff_ref, slot_ref):
    aff = aff_ref[...]
    rows = aff.shape[0]

    def enough(pivot):
        return jnp.sum(jnp.where(aff >= pivot, 1.0, 0.0), axis=-1, keepdims=True) >= cap

    def log_body(_, st):
        lo, hi, elo, ehi = st
        mid = 0.5 * (elo + ehi)
        pivot = jnp.exp2(mid)
        ok = enough(pivot)
        return (jnp.where(ok, pivot, lo), jnp.where(ok, hi, pivot), jnp.where(ok, mid, elo), jnp.where(ok, ehi, mid))

    def lin_body(_, st):
        lo, hi = st
        pivot = 0.5 * (lo + hi)
        ok = enough(pivot)
        return jnp.where(ok, pivot, lo), jnp.where(ok, hi, pivot)

    col = lambda v: jnp.full((rows, 1), v, F32)
    lo, hi, _, _ = lax.fori_loop(0, ROUTE_LOG_STEPS, log_body, (col(0.0), col(2.0), col(-152.0), col(1.0)))
    lo, hi = lax.fori_loop(0, ROUTE_LIN_STEPS, lin_body, (lo, hi))

    gt = jnp.where(aff >= hi, 1.0, 0.0)
    eq = jnp.where(aff >= lo, 1.0, 0.0) - gt
    need = cap - jnp.sum(gt, axis=-1, keepdims=True)
    r = lax.broadcasted_iota(jnp.int32, (CHUNK, CHUNK), 0)
    c = lax.broadcasted_iota(jnp.int32, (CHUNK, CHUNK), 1)
    tri = jnp.where(r < c, 1.0, 0.0).astype(BF16)
    sel = gt + eq * jnp.where(_prefix_count(eq, tri) < need, 1.0, 0.0)
    slot = _prefix_count(sel, tri)
    slot_ref[...] = jnp.where(sel > 0.5, slot, -1.0)


def _route(aff, cap):
    B, E, S = aff.shape
    spec = pl.BlockSpec((B * E, S), lambda i: (0, 0))
    return pl.pallas_call(
        functools.partial(_route_kernel, cap),
        grid=(1,),
        in_specs=[spec],
        out_specs=spec,
        out_shape=jax.ShapeDtypeStruct((B * E, S), F32),
        compiler_params=_cparams(1),
        name="route",
    )(aff.reshape(B * E, S))


def _moe_kernel(cap, slot0_ref, aff0_ref, slot1_ref, aff1_ref, slot2_ref, aff2_ref, h_ref, wg_ref, wu_ref, wd_ref,
                o_ref, pa_ref, pb_ref, va_ref, vb_ref):
    first = (pl.program_id(0) == 0) & (pl.program_id(1) == 0)
    S = h_ref.shape[0]

    slot_ids = lax.broadcasted_iota(jnp.int32, (cap, 1), 0).astype(F32)

    def build(slot_row, aff_row, p_ref, v_ref):
        sel = slot_ids == slot_row
        p_ref[...] = jnp.where(sel, 1.0, 0.0).astype(BF16)
        v_ref[...] = jnp.sum(jnp.where(sel, aff_row, 0.0), axis=-1, keepdims=True)

    def expert(p_ref, v_ref, i):
        p = p_ref[...]
        xg = _dot(p, h_ref[...]).astype(BF16)
        g = _dot(xg, wg_ref[i])
        u = _dot(xg, wu_ref[i])
        a = (g * (1.0 / (1.0 + jnp.exp(-g))) * u).astype(BF16)
        o = (_dot(a, wd_ref[i]) * v_ref[...]).astype(BF16)
        return _dot_tn(p, o)

    @pl.when(first)
    def _seed():
        build(slot0_ref[0], aff0_ref[0], pa_ref, va_ref)

    @pl.when(pl.program_id(1) == 0)
    def _zero():
        o_ref[...] = jnp.zeros_like(o_ref)

    build(slot1_ref[0], aff1_ref[0], pb_ref, vb_ref)
    o_ref[...] += expert(pa_ref, va_ref, 0)
    build(slot2_ref[0], aff2_ref[0], pa_ref, va_ref)
    o_ref[...] += expert(pb_ref, vb_ref, 1)


def _moe(slot, aff, h2, wg, wu, wd, cap):
    B, E, S = aff.shape
    T = h2.shape[0]
    assert E % 2 == 0
    last = B * E - 1
    slot3 = slot.reshape(B * E, 1, S)
    aff3 = aff.reshape(B * E, 1, S)
    row = lambda f: pl.BlockSpec((1, 1, S), lambda b, k: (f(b * E + 2 * k), 0, 0))
    seed, odd, nxt = row(lambda n: 0), row(lambda n: n + 1), row(lambda n: jnp.minimum(n + 2, last))
    tok = pl.BlockSpec((S, D_MODEL), lambda b, k: (b, 0))
    wspec = lambda w: pl.BlockSpec((2,) + w.shape[1:], lambda b, k: (k, 0, 0))
    return pl.pallas_call(
        functools.partial(_moe_kernel, cap),
        grid=(B, E // 2),
        in_specs=[seed, seed, odd, odd, nxt, nxt, tok, wspec(wg), wspec(wu), wspec(wd)],
        out_specs=tok,
        out_shape=jax.ShapeDtypeStruct((T, D_MODEL), F32),
        scratch_shapes=[pltpu.VMEM((cap, S), BF16), pltpu.VMEM((cap, S), BF16),
                        pltpu.VMEM((cap, 1), F32), pltpu.VMEM((cap, 1), F32)],
        compiler_params=_cparams(2),
        name="moe",
    )(slot3, aff3, slot3, aff3, slot3, aff3, h2, wg, wu, wd)


def _ple_final_kernel(x1_ref, moe_ref, p_ref, gp_ref, wg_ref, wp_ref, gf_ref, o_ref):
    x2 = x1_ref[...] + moe_ref[...]
    z = _dot(_rms(x2, gp_ref[...]).astype(BF16), wg_ref[...])
    gate = 1.0 / (1.0 + jnp.exp(-z))
    x3 = x2 + gate * _dot(p_ref[...].astype(BF16), wp_ref[...])
    o_ref[...] = _rms(x3, gf_ref[...])


def _ple_final(x1, moe, pf, g_ple, w_gate, w_proj, g_final):
    T = x1.shape[0]
    row = lambda n: pl.BlockSpec((TM, n), lambda i: (i, 0))
    full = lambda a: pl.BlockSpec(a.shape, lambda i: (0, 0))
    return pl.pallas_call(
        _ple_final_kernel,
        grid=(T // TM,),
        in_specs=[row(D_MODEL), row(D_MODEL), row(PLE_DIM), full(g_ple), full(w_gate), full(w_proj), full(g_final)],
        out_specs=row(D_MODEL),
        out_shape=jax.ShapeDtypeStruct((T, D_MODEL), F32),
        compiler_params=_cparams(1),
        name="ple_final",
    )(x1, moe, pf, g_ple, w_gate, w_proj, g_final)


def _prep_in_proj_weight(w_in):
    D = w_in.shape[0]
    z64 = jnp.zeros((D, 64), F32)
    k0, k1 = w_in[:, 512:576], w_in[:, 576:640]
    v0, v1 = w_in[:, 640:704], w_in[:, 704:768]
    t1, t2 = w_in[:, 1152:1168], w_in[:, 1168:1184]
    cols = [w_in[:, 0:512], k0, k0, k1, k1, v0, z64, z64, v0, v1, z64, z64, v1,
            w_in[:, 768:1152], t1, t2, t2, t1, z64]
    return jnp.concatenate(cols, axis=1).astype(BF16)


def _prep_mla_weights(w_uq, w_ukv):
    per_q = B_NOPE_DIM + B_ROPE_DIM
    w3 = w_uq.reshape(Q_RANK, B_HEADS, per_q)
    pad = LANES - per_q
    wq = jnp.pad(w3, ((0, 0), (0, 0), (0, pad))).reshape(Q_RANK, B_HEADS * LANES)
    t1 = w3[:, :, B_NOPE_DIM:B_NOPE_DIM + HALF_ROPE]
    t2 = w3[:, :, B_NOPE_DIM + HALF_ROPE:]
    wqr = jnp.concatenate([jnp.zeros((Q_RANK, B_HEADS, B_NOPE_DIM), F32), t2, t1,
                           jnp.zeros((Q_RANK, B_HEADS, pad), F32)], axis=2).reshape(Q_RANK, B_HEADS * LANES)
    w4 = w_ukv.reshape(KV_RANK, B_HEADS, B_NOPE_DIM + B_V_DIM)
    zk = jnp.zeros((KV_RANK, B_HEADS, LANES - B_NOPE_DIM), F32)
    wk = jnp.concatenate([w4[:, :, :B_NOPE_DIM], zk], axis=2).reshape(KV_RANK, B_HEADS * LANES)
    wv = w4[:, :, B_NOPE_DIM:]
    zv = jnp.zeros((KV_RANK, B_HEADS // 2, B_V_DIM), F32)
    wvlo = jnp.concatenate([wv[:, 0::2], zv], axis=2).reshape(KV_RANK, B_HEADS // 2 * LANES)
    wvhi = jnp.concatenate([zv, wv[:, 1::2]], axis=2).reshape(KV_RANK, B_HEADS // 2 * LANES)
    wkv = jnp.concatenate([wk, wvlo, wvhi], axis=1)
    return wq.astype(BF16), wqr.astype(BF16), wkv.astype(BF16)


def kernel(x, p, positions, rel_bias, norm_mix_g, w_in, sink, g_cq, g_ckv, w_uq, w_ukv, g_out_a, g_out_b, w_out,
           norm_ffn_g, w_router, w_e_gate, w_e_up, w_e_down, norm_ple_g, w_ple_gate, w_ple_proj, final_norm_g):
    B, S, D = x.shape
    T = B * S
    cap = CAPACITY_FACTOR * S // N_EXPERTS
    xf = x.reshape(T, D)
    c_tab, s_tab = _rope_tables(positions)
    assert w_in.shape[0] == 1, "single-layer block: the final norm is fused into the last kernel"
    i = 0
    w1 = _prep_in_proj_weight(w_in[i])
    wq, wqr, wkv = _prep_mla_weights(w_uq[i], w_ukv[i])
    qa, kd, va4, qm, km, vlo, vhi = _in_proj(
        xf, norm_mix_g[i].reshape(1, D), w1, g_cq[i].reshape(1, -1), g_ckv[i].reshape(1, -1),
        wq, wqr, wkv, c_tab, s_tab)
    ya = _window_attn(qa, kd, va4, rel_bias, sink[i], g_out_a[i].reshape(1, -1), B, S)
    yb = _mla_attn(qm, km, vlo, vhi, g_out_b[i].reshape(1, -1), B, S)
    x1, h2, aff = _out_proj(xf, ya, yb, w_out[i].astype(BF16), norm_ffn_g[i].reshape(1, D),
                            w_router[i].T, B, S)
    slot = _route(aff, cap)
    moe = _moe(slot, aff, h2, w_e_gate[i].astype(BF16), w_e_up[i].astype(BF16), w_e_down[i].astype(BF16), cap)
    out = _ple_final(x1, moe, p[i].reshape(T, -1), norm_ple_g[i].reshape(1, D),
                     w_ple_gate[i].astype(BF16), w_ple_proj[i].astype(BF16), final_norm_g.reshape(1, D))
    return out.reshape(B, S, D)
```

```python
import functools

import numpy as np
import jax
import jax.numpy as jnp
from jax import lax
from jax.experimental import pallas as pl
from jax.experimental.pallas import tpu as pltpu

F32 = jnp.float32
BF16 = jnp.bfloat16

D_MODEL = 1024
BLK = 128
WINDOW = 128
A_HEADS = 8
A_KV_HEADS = 2
A_HEAD_DIM = 64
B_HEADS = 8
B_NOPE_DIM = 64
B_ROPE_DIM = 32
B_V_DIM = 64
Q_RANK = 256
KV_RANK = 128
ROPE_THETA = 10000.0
NUM_BUCKETS = 32
MAX_DISTANCE = 128
N_EXPERTS = 16
EXPERT_FF = 512
CAPACITY_FACTOR = 2
PLE_DIM = 256
EPS = 1e-6

LANES = 128
HALF_ROPE = B_ROPE_DIM // 2
MASKED = -1e30
LOG2E = 1.4426950408889634
VMEM_LIMIT = 56 * 1024 * 1024
TM = 512
TQ = 512
MLA_ROWS = 128
WIN_QBLOCKS = 4


def _cparams(n_axes):
    return pltpu.CompilerParams(dimension_semantics=("arbitrary",) * n_axes,
                                vmem_limit_bytes=VMEM_LIMIT)


def _rms(x, g):
    return x * lax.rsqrt(jnp.mean(x * x, axis=-1, keepdims=True) + EPS) * g


def _dot(a, b):
    return jnp.dot(a, b, preferred_element_type=F32)


def _dot_nt(a, b):
    return lax.dot_general(a, b, (((1,), (1,)), ((), ())), preferred_element_type=F32)


def _dot_tn(a, b):
    return lax.dot_general(a, b, (((0,), (0,)), ((), ())), preferred_element_type=F32)


ROPE_GROUPS = LANES // HALF_ROPE
ROPE_ROWS = 256


def _rope_table_kernel(pos_ref, invf_ref, c_ref, s_ref):
    ang = pos_ref[...] * invf_ref[...]
    cd = jnp.cos(ang)
    sd = jnp.sin(ang)
    lane = lax.broadcasted_iota(jnp.int32, (1, LANES), 1)
    in_t1 = (lane >= B_NOPE_DIM) & (lane < B_NOPE_DIM + HALF_ROPE)
    in_t2 = (lane >= B_NOPE_DIM + HALF_ROPE) & (lane < B_NOPE_DIM + B_ROPE_DIM)
    for g in range(ROPE_GROUPS):
        own = (lane >= g * HALF_ROPE) & (lane < (g + 1) * HALF_ROPE)
        xc = jnp.where(own, cd, 0.0)
        xs = jnp.where(own, sd, 0.0)
        shift = HALF_ROPE
        while shift < LANES:
            xc = xc + pltpu.roll(xc, shift, axis=1)
            xs = xs + pltpu.roll(xs, shift, axis=1)
            shift *= 2
        c_ref[g] = jnp.where(lane < B_NOPE_DIM, 1.0, jnp.where(in_t1 | in_t2, xc, 0.0))
        s_ref[g] = jnp.where(in_t1, -xs, jnp.where(in_t2, xs, 0.0))


def _rope_tables(positions):
    T = positions.size
    rows = T // ROPE_GROUPS
    inv_freq = 1.0 / (ROPE_THETA ** (jnp.arange(0, B_ROPE_DIM, 2, dtype=F32) / B_ROPE_DIM))
    pos = jnp.repeat(positions.astype(F32).reshape(ROPE_GROUPS, rows).T, HALF_ROPE, axis=1)
    invf = jnp.tile(inv_freq, ROPE_GROUPS).reshape(1, LANES)
    out_spec = pl.BlockSpec((ROPE_GROUPS, ROPE_ROWS, LANES), lambda i: (0, i, 0))
    c, s = pl.pallas_call(
        _rope_table_kernel,
        grid=(rows // ROPE_ROWS,),
        in_specs=[pl.BlockSpec((ROPE_ROWS, LANES), lambda i: (i, 0)), pl.BlockSpec((1, LANES), lambda i: (0, 0))],
        out_specs=[out_spec] * 2,
        out_shape=[jax.ShapeDtypeStruct((ROPE_GROUPS, rows, LANES), F32)] * 2,
        compiler_params=_cparams(1),
        name="rope_tables",
    )(pos, invf)
    return c.reshape(T, LANES), s.reshape(T, LANES)


C_QA = 0
C_KA = C_QA + 512
C_VA = C_KA + LANES
C_CQ = C_VA + LANES
C_CKV = C_CQ + Q_RANK
C_KR = C_CKV + KV_RANK
C_END = C_KR + LANES
MLA_K_COLS = B_HEADS * LANES


def _in_proj_kernel(x_ref, g_ref, w1_ref, gcq_ref, gckv_ref, wq_ref, wqr_ref, wkv_ref, c_ref, s_ref,
                    qa_ref, ka_ref, va_ref, qm_ref, km_ref, vm_ref):
    h = _rms(x_ref[...], g_ref[...]).astype(BF16)
    z = _dot(h, w1_ref[...])
    qa_ref[...] = (z[:, C_QA:C_KA] * (A_HEAD_DIM ** -0.5 * LOG2E)).astype(BF16)
    ka_ref[...] = z[:, C_KA:C_VA].astype(BF16)
    va_ref[...] = z[:, C_VA:C_CQ].astype(BF16)
    cqn = _rms(z[:, C_CQ:C_CKV], gcq_ref[...]).astype(BF16)
    ckvn = _rms(z[:, C_CKV:C_KR], gckv_ref[...]).astype(BF16)
    krt = z[:, C_KR:C_END]

    lane = lax.broadcasted_iota(jnp.int32, (1, LANES), 1)
    c_full = c_ref[...]
    s_rope = s_ref[...]
    c_rope = jnp.where(lane < B_NOPE_DIM, 0.0, c_full)

    q1 = _dot(cqn, wq_ref[...])
    q2 = _dot(cqn, wqr_ref[...])
    scale = (B_NOPE_DIM + B_ROPE_DIM) ** -0.5 * LOG2E
    for hd in range(B_HEADS):
        sl = slice(hd * LANES, (hd + 1) * LANES)
        qm_ref[:, sl] = ((q1[:, sl] * c_full + q2[:, sl] * s_rope) * scale).astype(BF16)

    kv = _dot(ckvn, wkv_ref[...])
    kr_part = pltpu.roll(krt, 64, axis=1) * c_rope + pltpu.roll(krt, 32, axis=1) * s_rope
    for hd in range(B_HEADS):
        sl = slice(hd * LANES, (hd + 1) * LANES)
        km_ref[:, sl] = (kv[:, sl] + kr_part).astype(BF16)
    vm_ref[...] = kv[:, MLA_K_COLS:].astype(BF16)


def _in_proj(xf, g_mix, w1, g_cq, g_ckv, wq, wqr, wkv, c_tab, s_tab):
    T = xf.shape[0]
    row = lambda n: pl.BlockSpec((TM, n), lambda i: (i, 0))
    full = lambda a: pl.BlockSpec(a.shape, lambda i: (0, 0))
    outs = [512, LANES, LANES, MLA_K_COLS, MLA_K_COLS, B_HEADS * B_V_DIM]
    return pl.pallas_call(
        _in_proj_kernel,
        grid=(T // TM,),
        in_specs=[row(D_MODEL), full(g_mix), full(w1), full(g_cq), full(g_ckv), full(wq), full(wqr), full(wkv),
                  row(LANES), row(LANES)],
        out_specs=[row(n) for n in outs],
        out_shape=[jax.ShapeDtypeStruct((T, n), BF16) for n in outs],
        compiler_params=_cparams(1),
        name="in_proj",
    )(xf, g_mix, w1, g_cq, g_ckv, wq, wqr, wkv, c_tab, s_tab)


def _bucket_map():
    qi = np.arange(BLK)[:, None]
    kj = np.arange(3 * BLK)[None, :]
    rel = kj - BLK - qi
    n = np.abs(rel)
    half = NUM_BUCKETS // 2
    max_exact = half // 2
    thresholds = [int(np.ceil(max_exact * 2 ** (k / 2) - 1e-9)) for k in range(1, half - max_exact)]
    large = max_exact + sum((n >= t).astype(np.int64) for t in thresholds)
    large = np.minimum(large, half - 1)
    bucket = np.where(rel > 0, half, 0) + np.where(n < max_exact, n, large)
    return np.where(n <= WINDOW, bucket, -1).astype(np.int32)


def _window_attn_kernel(n_steps, relb_ref, sink_ref, bmap_ref, g_ref, q_ref, kp_ref, kc_ref, kn_ref,
                        vp_ref, vc_ref, vn_ref, o_ref, bias_ref):
    b = pl.program_id(0)
    i = pl.program_id(1)

    @pl.when((b == 0) & (i == 0))
    def _build_bias():
        bmap = bmap_ref[...]
        col = lax.broadcasted_iota(jnp.int32, (1, 3 * BLK), 1)
        for hd in range(A_HEADS):
            bias_ref[1, hd] = jnp.full((BLK, 3 * BLK), MASKED, F32)
        for bk in range(NUM_BUCKETS):
            m = bmap == bk
            for hd in range(A_HEADS):
                bias_ref[1, hd] = jnp.where(m, relb_ref[bk, hd] * LOG2E, bias_ref[1, hd])
        for hd in range(A_HEADS):
            bias_ref[0, hd] = jnp.where(col < BLK, MASKED, bias_ref[1, hd])
            bias_ref[2, hd] = jnp.where(col >= 2 * BLK, MASKED, bias_ref[1, hd])

    lane = lax.broadcasted_iota(jnp.int32, (1, LANES), 1)
    lo = lane < A_HEAD_DIM
    ones_lo = jnp.broadcast_to(jnp.where(lo, 1.0, 0.0).astype(BF16), (3 * BLK, LANES))
    ones_hi = jnp.broadcast_to(jnp.where(lo, 0.0, 1.0).astype(BF16), (3 * BLK, LANES))

    def key_blocks(p_ref, c_ref, n_ref):
        return [p_ref[...]] + [c_ref[c * BLK:(c + 1) * BLK, :] for c in range(WIN_QBLOCKS)] + [n_ref[...]]

    def lane_swap(t):
        return pltpu.bitcast(pltpu.roll(pltpu.bitcast(t, jnp.int32), LANES // 2, axis=1), BF16)

    kdup = [[], []]
    v_even = [[], []]
    v_odd = [[], []]
    for kt, vt in zip(key_blocks(kp_ref, kc_ref, kn_ref), key_blocks(vp_ref, vc_ref, vn_ref)):
        ks, vs, zero = lane_swap(kt), lane_swap(vt), jnp.zeros_like(vt)
        kdup[0].append(jnp.where(lo, kt, ks))
        kdup[1].append(jnp.where(lo, ks, kt))
        v_even[0].append(jnp.where(lo, vt, zero))
        v_odd[0].append(jnp.where(lo, zero, vs))
        v_even[1].append(jnp.where(lo, vs, zero))
        v_odd[1].append(jnp.where(lo, zero, vt))

    for c in range(WIN_QBLOCKS):
        rows = slice(c * BLK, (c + 1) * BLK)
        if c == 0:
            variant = jnp.where(i == 0, 0, 1)
        elif c == WIN_QBLOCKS - 1:
            variant = jnp.where(i == n_steps - 1, 2, 1)
        else:
            variant = 1
        probs, sink_terms = [], []
        for hd in range(A_HEADS):
            g = hd // (A_HEADS // A_KV_HEADS)
            qt = q_ref[rows, (hd // 2) * LANES:(hd // 2 + 1) * LANES]
            qm = jnp.where(lo if hd % 2 == 0 else ~lo, qt, jnp.zeros_like(qt))
            kcat = jnp.concatenate(kdup[g][c:c + 3], axis=0)
            s = _dot_nt(qm, kcat) + bias_ref[variant, hd]
            sk = sink_ref[hd] * LOG2E
            m = jnp.maximum(jnp.max(s, axis=-1, keepdims=True), sk)
            probs.append(jnp.exp2(s - m).astype(BF16))
            sink_terms.append(jnp.exp2(sk - m))

        outs = []
        for j in range(A_HEADS // 2):
            g = (2 * j) // (A_HEADS // A_KV_HEADS)
            v_e = jnp.concatenate(v_even[g][c:c + 3], axis=0)
            v_o = jnp.concatenate(v_odd[g][c:c + 3], axis=0)
            o = (_dot(probs[2 * j], jnp.concatenate([v_e, ones_lo], axis=1))
                 + _dot(probs[2 * j + 1], jnp.concatenate([v_o, ones_hi], axis=1)))
            denom = o[:, LANES:] + jnp.where(lo, sink_terms[2 * j], sink_terms[2 * j + 1])
            outs.append(o[:, :LANES] / denom)
        ya = jnp.concatenate(outs, axis=1)
        o_ref[rows, :] = _rms(ya, g_ref[...]).astype(BF16)


def _window_attn(qa, ka, va, rel_bias, sink, g_out_a, B, S):
    nb = S // BLK
    n_steps = nb // WIN_QBLOCKS
    assert nb >= 2, "first and last query blocks use distinct edge masks"
    bmap = jnp.asarray(_bucket_map())
    smem = pl.BlockSpec(memory_space=pltpu.SMEM)
    cur = lambda n: pl.BlockSpec((WIN_QBLOCKS * BLK, n), lambda b, i: (b * n_steps + i, 0))
    prv = lambda n: pl.BlockSpec((BLK, n), lambda b, i: (b * nb + jnp.maximum(i * WIN_QBLOCKS - 1, 0), 0))
    nxt = lambda n: pl.BlockSpec((BLK, n), lambda b, i: (b * nb + jnp.minimum((i + 1) * WIN_QBLOCKS, nb - 1), 0))
    return pl.pallas_call(
        functools.partial(_window_attn_kernel, n_steps),
        grid=(B, n_steps),
        in_specs=[smem, smem,
                  pl.BlockSpec((BLK, 3 * BLK), lambda b, i: (0, 0)),
                  pl.BlockSpec((1, 512), lambda b, i: (0, 0)),
                  cur(512), prv(LANES), cur(LANES), nxt(LANES), prv(LANES), cur(LANES), nxt(LANES)],
        out_specs=cur(512),
        out_shape=jax.ShapeDtypeStruct((B * S, 512), BF16),
        scratch_shapes=[pltpu.VMEM((3, A_HEADS, BLK, 3 * BLK), F32)],
        compiler_params=_cparams(2),
        name="window_attn",
    )(rel_bias, sink, bmap, g_out_a, qa, ka, ka, ka, va, va, va)


def _mla_attn_kernel(n_cast, g_ref, q_ref, k_ref, v_ref, *rest):
    o_ref = rest[n_cast]
    for src, dst in zip(rest[:n_cast], rest[n_cast + 1:]):
        dst[...] = src[...].astype(BF16)
    S = k_ref.shape[0]
    lane = lax.broadcasted_iota(jnp.int32, (1, LANES), 1)
    lo = lane < B_V_DIM
    ones_lo = jnp.broadcast_to(jnp.where(lo, 1.0, 0.0).astype(BF16), (S, LANES))
    ones_hi = jnp.broadcast_to(jnp.where(lo, 0.0, 1.0).astype(BF16), (S, LANES))
    v_pairs = []
    for j in range(B_HEADS // 2):
        vt = v_ref[:, j * LANES:(j + 1) * LANES]
        zero = jnp.zeros_like(vt)
        v_pairs.append((jnp.concatenate([jnp.where(lo, vt, zero), ones_lo], axis=1),
                        jnp.concatenate([jnp.where(lo, zero, vt), ones_hi], axis=1)))
    for r in range(q_ref.shape[0] // MLA_ROWS):
        rows = slice(r * MLA_ROWS, (r + 1) * MLA_ROWS)
        outs = []
        for j in range(B_HEADS // 2):
            ps = []
            for hd in (2 * j, 2 * j + 1):
                sl = slice(hd * LANES, (hd + 1) * LANES)
                s = _dot_nt(q_ref[rows, sl], k_ref[:, sl])
                m = jnp.max(s, axis=-1, keepdims=True)
                ps.append(jnp.exp2(s - m).astype(BF16))
            o = _dot(ps[0], v_pairs[j][0]) + _dot(ps[1], v_pairs[j][1])
            outs.append(o[:, :LANES] / o[:, LANES:])
        yb = jnp.concatenate(outs, axis=1)
        o_ref[rows, :] = _rms(yb, g_ref[...]).astype(BF16)


def _mla_attn(qm, km, vm, g_out_b, B, S, cast_weights):
    nq = S // TQ
    steps = B * nq
    qspec = lambda n: pl.BlockSpec((TQ, n), lambda b, i: (b * nq + i, 0))
    kspec = lambda n: pl.BlockSpec((S, n), lambda b, i: (b, 0))
    flat = [w.reshape(-1, w.shape[-1]) for w in cast_weights]
    slab = lambda w: pl.BlockSpec((w.shape[0] // steps, w.shape[1]), lambda b, i: (b * nq + i, 0))
    outs = pl.pallas_call(
        functools.partial(_mla_attn_kernel, len(flat)),
        grid=(B, nq),
        in_specs=[pl.BlockSpec((1, 512), lambda b, i: (0, 0)), qspec(1024), kspec(1024), kspec(512)]
                 + [slab(w) for w in flat],
        out_specs=[qspec(512)] + [slab(w) for w in flat],
        out_shape=[jax.ShapeDtypeStruct((B * S, 512), BF16)]
                  + [jax.ShapeDtypeStruct(w.shape, BF16) for w in flat],
        compiler_params=_cparams(2),
        name="mla_attn",
    )(g_out_b, qm, km, vm, *flat)
    return outs[0], [o.reshape(w.shape) for o, w in zip(outs[1:], cast_weights)]


def _split_bf16(v):
    hi = v.astype(BF16)
    return hi, (v - hi.astype(F32)).astype(BF16)


def _out_proj_kernel(x_ref, ya_ref, yb_ref, wo_ref, g_ref, wr_ref, x1_ref, h2_ref, aff_ref):
    x1 = x_ref[...] + _dot(ya_ref[...], wo_ref[0:512, :]) + _dot(yb_ref[...], wo_ref[512:1024, :])
    x1_ref[...] = x1
    h2 = _rms(x1, g_ref[...])
    h2_ref[...] = h2.astype(BF16)
    h_hi, h_lo = _split_bf16(h2)
    w_hi, w_lo = _split_bf16(wr_ref[...])
    logits = _dot_nt(w_hi, h_hi) + (_dot_nt(w_hi, h_lo) + _dot_nt(w_lo, h_hi))
    m = jnp.max(logits, axis=0, keepdims=True)
    e = jnp.exp(logits - m)
    aff_ref[0] = e / jnp.sum(e, axis=0, keepdims=True)


def _out_proj(xf, ya, yb, w_out, g_ffn, w_router_t, B, S):
    T = xf.shape[0]
    per_b = S // TM
    row = lambda n: pl.BlockSpec((TM, n), lambda i: (i, 0))
    full = lambda a: pl.BlockSpec(a.shape, lambda i: (0, 0))
    return pl.pallas_call(
        _out_proj_kernel,
        grid=(T // TM,),
        in_specs=[row(D_MODEL), row(512), row(512), full(w_out), full(g_ffn), full(w_router_t)],
        out_specs=[row(D_MODEL), row(D_MODEL),
                   pl.BlockSpec((1, N_EXPERTS, TM), lambda i: (i // per_b, 0, i % per_b))],
        out_shape=[jax.ShapeDtypeStruct((T, D_MODEL), F32), jax.ShapeDtypeStruct((T, D_MODEL), BF16),
                   jax.ShapeDtypeStruct((B, N_EXPERTS, S), F32)],
        compiler_params=_cparams(1),
        name="out_proj",
    )(xf, ya, yb, w_out, g_ffn, w_router_t)


CHUNK = 256
ROUTE_LOG_STEPS = 12
ROUTE_LIN_STEPS = 32


def _prefix_count(flags_f32, tri):
    S = flags_f32.shape[1]
    carry = jnp.zeros((flags_f32.shape[0], 1), F32)
    parts = []
    for c in range(S // CHUNK):
        blk = flags_f32[:, c * CHUNK:(c + 1) * CHUNK]
        parts.append(_dot(blk.astype(BF16), tri) + carry)
        carry = carry + jnp.sum(blk, axis=-1, keepdims=True)
    return jnp.concatenate(parts, axis=1)


def _route_kernel(cap, aff_ref, slot_ref):
    aff = aff_ref[...]
    rows = aff.shape[0]

    def enough(pivot):
        return jnp.sum(jnp.where(aff >= pivot, 1.0, 0.0), axis=-1, keepdims=True) >= cap

    def log_body(_, st):
        lo, hi, elo, ehi = st
        mid = 0.5 * (elo + ehi)
        pivot = jnp.exp2(mid)
        ok = enough(pivot)
        return (jnp.where(ok, pivot, lo), jnp.where(ok, hi, pivot), jnp.where(ok, mid, elo), jnp.where(ok, ehi, mid))

    def lin_body(_, st):
        lo, hi = st
        pivot = 0.5 * (lo + hi)
        ok = enough(pivot)
        return jnp.where(ok, pivot, lo), jnp.where(ok, hi, pivot)

    col = lambda v: jnp.full((rows, 1), v, F32)
    lo, hi, _, _ = lax.fori_loop(0, ROUTE_LOG_STEPS, log_body, (col(0.0), col(2.0), col(-152.0), col(1.0)))
    lo, hi = lax.fori_loop(0, ROUTE_LIN_STEPS, lin_body, (lo, hi))

    gt = jnp.where(aff >= hi, 1.0, 0.0)
    eq = jnp.where(aff >= lo, 1.0, 0.0) - gt
    need = cap - jnp.sum(gt, axis=-1, keepdims=True)
    r = lax.broadcasted_iota(jnp.int32, (CHUNK, CHUNK), 0)
    c = lax.broadcasted_iota(jnp.int32, (CHUNK, CHUNK), 1)
    tri = jnp.where(r < c, 1.0, 0.0).astype(BF16)
    sel = gt + eq * jnp.where(_prefix_count(eq, tri) < need, 1.0, 0.0)
    slot = _prefix_count(sel, tri)
    slot_ref[...] = jnp.where(sel > 0.5, slot, -1.0)


def _route(aff, cap):
    B, E, S = aff.shape
    spec = pl.BlockSpec((B * E, S), lambda i: (0, 0))
    return pl.pallas_call(
        functools.partial(_route_kernel, cap),
        grid=(1,),
        in_specs=[spec],
        out_specs=spec,
        out_shape=jax.ShapeDtypeStruct((B * E, S), F32),
        compiler_params=_cparams(1),
        name="route",
    )(aff.reshape(B * E, S))


def _moe_kernel(cap, slot0_ref, aff0_ref, slot1_ref, aff1_ref, slot2_ref, aff2_ref, h_ref, wg_ref, wu_ref, wd_ref,
                o_ref, pa_ref, pb_ref, va_ref, vb_ref):
    first = (pl.program_id(0) == 0) & (pl.program_id(1) == 0)
    S = h_ref.shape[0]

    slot_ids = lax.broadcasted_iota(jnp.int32, (cap, 1), 0).astype(F32)

    def build(slot_row, aff_row, p_ref, v_ref):
        sel = slot_ids == slot_row
        p_ref[...] = jnp.where(sel, 1.0, 0.0).astype(BF16)
        v_ref[...] = jnp.sum(jnp.where(sel, aff_row, 0.0), axis=-1, keepdims=True)

    def expert(p_ref, v_ref, i):
        p = p_ref[...]
        xg = _dot(p, h_ref[...]).astype(BF16)
        g = _dot(xg, wg_ref[i])
        u = _dot(xg, wu_ref[i])
        a = (g * (1.0 / (1.0 + jnp.exp(-g))) * u).astype(BF16)
        o = (_dot(a, wd_ref[i]) * v_ref[...]).astype(BF16)
        return _dot_tn(p, o)

    @pl.when(first)
    def _seed():
        build(slot0_ref[0], aff0_ref[0], pa_ref, va_ref)

    @pl.when(pl.program_id(1) == 0)
    def _zero():
        o_ref[...] = jnp.zeros_like(o_ref)

    build(slot1_ref[0], aff1_ref[0], pb_ref, vb_ref)
    o_ref[...] += expert(pa_ref, va_ref, 0)
    build(slot2_ref[0], aff2_ref[0], pa_ref, va_ref)
    o_ref[...] += expert(pb_ref, vb_ref, 1)


def _moe(slot, aff, h2, wg, wu, wd, cap):
    B, E, S = aff.shape
    T = h2.shape[0]
    assert E % 2 == 0
    last = B * E - 1
    slot3 = slot.reshape(B * E, 1, S)
    aff3 = aff.reshape(B * E, 1, S)
    row = lambda f: pl.BlockSpec((1, 1, S), lambda b, k: (f(b * E + 2 * k), 0, 0))
    seed, odd, nxt = row(lambda n: 0), row(lambda n: n + 1), row(lambda n: jnp.minimum(n + 2, last))
    tok = pl.BlockSpec((S, D_MODEL), lambda b, k: (b, 0))
    wspec = lambda w: pl.BlockSpec((2,) + w.shape[1:], lambda b, k: (k, 0, 0))
    return pl.pallas_call(
        functools.partial(_moe_kernel, cap),
        grid=(B, E // 2),
        in_specs=[seed, seed, odd, odd, nxt, nxt, tok, wspec(wg), wspec(wu), wspec(wd)],
        out_specs=tok,
        out_shape=jax.ShapeDtypeStruct((T, D_MODEL), F32),
        scratch_shapes=[pltpu.VMEM((cap, S), BF16), pltpu.VMEM((cap, S), BF16),
                        pltpu.VMEM((cap, 1), F32), pltpu.VMEM((cap, 1), F32)],
        compiler_params=_cparams(2),
        name="moe",
    )(slot3, aff3, slot3, aff3, slot3, aff3, h2, wg, wu, wd)


def _ple_final_kernel(x1_ref, moe_ref, p_ref, gp_ref, wg_ref, wp_ref, gf_ref, o_ref):
    x2 = x1_ref[...] + moe_ref[...]
    z = _dot(_rms(x2, gp_ref[...]).astype(BF16), wg_ref[...])
    gate = 1.0 / (1.0 + jnp.exp(-z))
    x3 = x2 + gate * _dot(p_ref[...].astype(BF16), wp_ref[...])
    o_ref[...] = _rms(x3, gf_ref[...])


def _ple_final(x1, moe, pf, g_ple, w_gate, w_proj, g_final):
    T = x1.shape[0]
    row = lambda n: pl.BlockSpec((TM, n), lambda i: (i, 0))
    full = lambda a: pl.BlockSpec(a.shape, lambda i: (0, 0))
    return pl.pallas_call(
        _ple_final_kernel,
        grid=(T // TM,),
        in_specs=[row(D_MODEL), row(D_MODEL), row(PLE_DIM), full(g_ple), full(w_gate), full(w_proj), full(g_final)],
        out_specs=row(D_MODEL),
        out_shape=jax.ShapeDtypeStruct((T, D_MODEL), F32),
        compiler_params=_cparams(1),
        name="ple_final",
    )(x1, moe, pf, g_ple, w_gate, w_proj, g_final)


def _prep_in_proj_weight(w_in):
    D = w_in.shape[0]
    t1, t2 = w_in[:, 1152:1168], w_in[:, 1168:1184]
    cols = [w_in[:, 0:1152], t1, t2, t2, t1, jnp.zeros((D, 64), F32)]
    return jnp.concatenate(cols, axis=1).astype(BF16)


def _prep_mla_weights(w_uq, w_ukv):
    per_q = B_NOPE_DIM + B_ROPE_DIM
    w3 = w_uq.reshape(Q_RANK, B_HEADS, per_q)
    pad = LANES - per_q
    wq = jnp.pad(w3, ((0, 0), (0, 0), (0, pad))).reshape(Q_RANK, B_HEADS * LANES)
    t1 = w3[:, :, B_NOPE_DIM:B_NOPE_DIM + HALF_ROPE]
    t2 = w3[:, :, B_NOPE_DIM + HALF_ROPE:]
    wqr = jnp.concatenate([jnp.zeros((Q_RANK, B_HEADS, B_NOPE_DIM), F32), t2, t1,
                           jnp.zeros((Q_RANK, B_HEADS, pad), F32)], axis=2).reshape(Q_RANK, B_HEADS * LANES)
    w4 = w_ukv.reshape(KV_RANK, B_HEADS, B_NOPE_DIM + B_V_DIM)
    zk = jnp.zeros((KV_RANK, B_HEADS, LANES - B_NOPE_DIM), F32)
    wk = jnp.concatenate([w4[:, :, :B_NOPE_DIM], zk], axis=2).reshape(KV_RANK, B_HEADS * LANES)
    wv = w4[:, :, B_NOPE_DIM:].reshape(KV_RANK, B_HEADS * B_V_DIM)
    wkv = jnp.concatenate([wk, wv], axis=1)
    return wq.astype(BF16), wqr.astype(BF16), wkv.astype(BF16)


def kernel(x, p, positions, rel_bias, norm_mix_g, w_in, sink, g_cq, g_ckv, w_uq, w_ukv, g_out_a, g_out_b, w_out,
           norm_ffn_g, w_router, w_e_gate, w_e_up, w_e_down, norm_ple_g, w_ple_gate, w_ple_proj, final_norm_g):
    B, S, D = x.shape
    T = B * S
    cap = CAPACITY_FACTOR * S // N_EXPERTS
    xf = x.reshape(T, D)
    c_tab, s_tab = _rope_tables(positions)
    assert w_in.shape[0] == 1, "single-layer block: the final norm is fused into the last kernel"
    i = 0
    w1 = _prep_in_proj_weight(w_in[i])
    wq, wqr, wkv = _prep_mla_weights(w_uq[i], w_ukv[i])
    qa, ka, va, qm, km, vm = _in_proj(
        xf, norm_mix_g[i].reshape(1, D), w1, g_cq[i].reshape(1, -1), g_ckv[i].reshape(1, -1),
        wq, wqr, wkv, c_tab, s_tab)
    ya = _window_attn(qa, ka, va, rel_bias, sink[i], g_out_a[i].reshape(1, -1), B, S)
    yb, (wg, wu, wd) = _mla_attn(qm, km, vm, g_out_b[i].reshape(1, -1), B, S,
                                 (w_e_gate[i], w_e_up[i], w_e_down[i]))
    x1, h2, aff = _out_proj(xf, ya, yb, w_out[i].astype(BF16), norm_ffn_g[i].reshape(1, D),
                            w_router[i].T, B, S)
    slot = _route(aff, cap)
    moe = _moe(slot, aff, h2, wg, wu, wd, cap)
    out = _ple_final(x1, moe, p[i].reshape(T, -1), norm_ple_g[i].reshape(1, D),
                     w_ple_gate[i].astype(BF16), w_ple_proj[i].astype(BF16), final_norm_g.reshape(1, D))
    return out.reshape(B, S, D)
```

```python
import functools

import numpy as np
import jax
import jax.numpy as jnp
from jax import lax
from jax.experimental import pallas as pl
from jax.experimental.pallas import tpu as pltpu

F32 = jnp.float32
BF16 = jnp.bfloat16

D_MODEL = 1024
BLK = 128
WINDOW = 128
A_HEADS = 8
A_KV_HEADS = 2
A_HEAD_DIM = 64
B_HEADS = 8
B_NOPE_DIM = 64
B_ROPE_DIM = 32
B_V_DIM = 64
Q_RANK = 256
KV_RANK = 128
ROPE_THETA = 10000.0
NUM_BUCKETS = 32
MAX_DISTANCE = 128
N_EXPERTS = 16
EXPERT_FF = 512
CAPACITY_FACTOR = 2
PLE_DIM = 256
EPS = 1e-6

LANES = 128
HALF_ROPE = B_ROPE_DIM // 2
MASKED = -1e30
LOG2E = 1.4426950408889634
VMEM_LIMIT = 56 * 1024 * 1024
TM = 512
TQ = 512
MLA_ROWS = 128
WIN_QBLOCKS = 4
GATHER_GROUP = 8


def _cparams(n_axes):
    return pltpu.CompilerParams(dimension_semantics=("arbitrary",) * n_axes,
                                vmem_limit_bytes=VMEM_LIMIT)


def _rms(x, g):
    return x * lax.rsqrt(jnp.mean(x * x, axis=-1, keepdims=True) + EPS) * g


def _dot(a, b):
    return jnp.dot(a, b, preferred_element_type=F32)


def _dot_nt(a, b):
    return lax.dot_general(a, b, (((1,), (1,)), ((), ())), preferred_element_type=F32)


def _dot_tn(a, b):
    return lax.dot_general(a, b, (((0,), (0,)), ((), ())), preferred_element_type=F32)


ROPE_GROUPS = LANES // HALF_ROPE
ROPE_ROWS = 256


def _rope_table_kernel(pos_ref, invf_ref, c_ref, s_ref):
    ang = pos_ref[...] * invf_ref[...]
    cd = jnp.cos(ang)
    sd = jnp.sin(ang)
    lane = lax.broadcasted_iota(jnp.int32, (1, LANES), 1)
    in_t1 = (lane >= B_NOPE_DIM) & (lane < B_NOPE_DIM + HALF_ROPE)
    in_t2 = (lane >= B_NOPE_DIM + HALF_ROPE) & (lane < B_NOPE_DIM + B_ROPE_DIM)
    for g in range(ROPE_GROUPS):
        own = (lane >= g * HALF_ROPE) & (lane < (g + 1) * HALF_ROPE)
        xc = jnp.where(own, cd, 0.0)
        xs = jnp.where(own, sd, 0.0)
        shift = HALF_ROPE
        while shift < LANES:
            xc = xc + pltpu.roll(xc, shift, axis=1)
            xs = xs + pltpu.roll(xs, shift, axis=1)
            shift *= 2
        c_ref[g] = jnp.where(lane < B_NOPE_DIM, 1.0, jnp.where(in_t1 | in_t2, xc, 0.0))
        s_ref[g] = jnp.where(in_t1, -xs, jnp.where(in_t2, xs, 0.0))


def _rope_tables(positions):
    T = positions.size
    rows = T // ROPE_GROUPS
    inv_freq = 1.0 / (ROPE_THETA ** (jnp.arange(0, B_ROPE_DIM, 2, dtype=F32) / B_ROPE_DIM))
    pos = jnp.repeat(positions.astype(F32).reshape(ROPE_GROUPS, rows).T, HALF_ROPE, axis=1)
    invf = jnp.tile(inv_freq, ROPE_GROUPS).reshape(1, LANES)
    out_spec = pl.BlockSpec((ROPE_GROUPS, ROPE_ROWS, LANES), lambda i: (0, i, 0))
    c, s = pl.pallas_call(
        _rope_table_kernel,
        grid=(rows // ROPE_ROWS,),
        in_specs=[pl.BlockSpec((ROPE_ROWS, LANES), lambda i: (i, 0)), pl.BlockSpec((1, LANES), lambda i: (0, 0))],
        out_specs=[out_spec] * 2,
        out_shape=[jax.ShapeDtypeStruct((ROPE_GROUPS, rows, LANES), F32)] * 2,
        compiler_params=_cparams(1),
        name="rope_tables",
    )(pos, invf)
    return c.reshape(T, LANES), s.reshape(T, LANES)


C_QA = 0
C_KA = C_QA + 512
C_VA = C_KA + LANES
C_CQ = C_VA + LANES
C_CKV = C_CQ + Q_RANK
C_KR = C_CKV + KV_RANK
C_END = C_KR + LANES
MLA_K_COLS = B_HEADS * LANES


def _in_proj_kernel(x_ref, g_ref, w1_ref, gcq_ref, gckv_ref, wq_ref, wqr_ref, wkv_ref, c_ref, s_ref,
                    qa_ref, ka_ref, va_ref, qm_ref, km_ref, vm_ref):
    h = _rms(x_ref[...], g_ref[...]).astype(BF16)
    z = _dot(h, w1_ref[...])
    qa_ref[...] = (z[:, C_QA:C_KA] * (A_HEAD_DIM ** -0.5 * LOG2E)).astype(BF16)
    ka_ref[...] = z[:, C_KA:C_VA].astype(BF16)
    va_ref[...] = z[:, C_VA:C_CQ].astype(BF16)
    cqn = _rms(z[:, C_CQ:C_CKV], gcq_ref[...]).astype(BF16)
    ckvn = _rms(z[:, C_CKV:C_KR], gckv_ref[...]).astype(BF16)
    krt = z[:, C_KR:C_END]

    lane = lax.broadcasted_iota(jnp.int32, (1, LANES), 1)
    c_full = c_ref[...]
    s_rope = s_ref[...]
    c_rope = jnp.where(lane < B_NOPE_DIM, 0.0, c_full)

    q1 = _dot(cqn, wq_ref[...])
    q2 = _dot(cqn, wqr_ref[...])
    scale = (B_NOPE_DIM + B_ROPE_DIM) ** -0.5 * LOG2E
    for hd in range(B_HEADS):
        sl = slice(hd * LANES, (hd + 1) * LANES)
        qm_ref[:, sl] = ((q1[:, sl] * c_full + q2[:, sl] * s_rope) * scale).astype(BF16)

    kv = _dot(ckvn, wkv_ref[...])
    kr_part = pltpu.roll(krt, 64, axis=1) * c_rope + pltpu.roll(krt, 32, axis=1) * s_rope
    for hd in range(B_HEADS):
        sl = slice(hd * LANES, (hd + 1) * LANES)
        km_ref[:, sl] = (kv[:, sl] + kr_part).astype(BF16)
    vm_ref[...] = kv[:, MLA_K_COLS:].astype(BF16)


def _in_proj(xf, g_mix, w1, g_cq, g_ckv, wq, wqr, wkv, c_tab, s_tab):
    T = xf.shape[0]
    row = lambda n: pl.BlockSpec((TM, n), lambda i: (i, 0))
    full = lambda a: pl.BlockSpec(a.shape, lambda i: (0, 0))
    outs = [512, LANES, LANES, MLA_K_COLS, MLA_K_COLS, B_HEADS * B_V_DIM]
    return pl.pallas_call(
        _in_proj_kernel,
        grid=(T // TM,),
        in_specs=[row(D_MODEL), full(g_mix), full(w1), full(g_cq), full(g_ckv), full(wq), full(wqr), full(wkv),
                  row(LANES), row(LANES)],
        out_specs=[row(n) for n in outs],
        out_shape=[jax.ShapeDtypeStruct((T, n), BF16) for n in outs],
        compiler_params=_cparams(1),
        name="in_proj",
    )(xf, g_mix, w1, g_cq, g_ckv, wq, wqr, wkv, c_tab, s_tab)


def _bucket_map():
    qi = np.arange(BLK)[:, None]
    kj = np.arange(3 * BLK)[None, :]
    rel = kj - BLK - qi
    n = np.abs(rel)
    half = NUM_BUCKETS // 2
    max_exact = half // 2
    thresholds = [int(np.ceil(max_exact * 2 ** (k / 2) - 1e-9)) for k in range(1, half - max_exact)]
    large = max_exact + sum((n >= t).astype(np.int64) for t in thresholds)
    large = np.minimum(large, half - 1)
    bucket = np.where(rel > 0, half, 0) + np.where(n < max_exact, n, large)
    return np.where(n <= WINDOW, bucket, -1).astype(np.int32)


def _window_attn_kernel(n_steps, relb_ref, sink_ref, bmap_ref, g_ref, q_ref, kp_ref, kc_ref, kn_ref,
                        vp_ref, vc_ref, vn_ref, o_ref, bias_ref):
    b = pl.program_id(0)
    i = pl.program_id(1)

    @pl.when((b == 0) & (i == 0))
    def _build_bias():
        bmap = bmap_ref[...]
        col = lax.broadcasted_iota(jnp.int32, (1, 3 * BLK), 1)
        for hd in range(A_HEADS):
            bias_ref[1, hd] = jnp.full((BLK, 3 * BLK), MASKED, F32)
        for bk in range(NUM_BUCKETS):
            m = bmap == bk
            for hd in range(A_HEADS):
                bias_ref[1, hd] = jnp.where(m, relb_ref[bk, hd] * LOG2E, bias_ref[1, hd])
        for hd in range(A_HEADS):
            bias_ref[0, hd] = jnp.where(col < BLK, MASKED, bias_ref[1, hd])
            bias_ref[2, hd] = jnp.where(col >= 2 * BLK, MASKED, bias_ref[1, hd])

    lane = lax.broadcasted_iota(jnp.int32, (1, LANES), 1)
    lo = lane < A_HEAD_DIM
    ones_lo = jnp.broadcast_to(jnp.where(lo, 1.0, 0.0).astype(BF16), (3 * BLK, LANES))
    ones_hi = jnp.broadcast_to(jnp.where(lo, 0.0, 1.0).astype(BF16), (3 * BLK, LANES))

    def key_blocks(p_ref, c_ref, n_ref):
        return [p_ref[...]] + [c_ref[c * BLK:(c + 1) * BLK, :] for c in range(WIN_QBLOCKS)] + [n_ref[...]]

    def lane_swap(t):
        return pltpu.bitcast(pltpu.roll(pltpu.bitcast(t, jnp.int32), LANES // 2, axis=1), BF16)

    kdup = [[], []]
    v_even = [[], []]
    v_odd = [[], []]
    for kt, vt in zip(key_blocks(kp_ref, kc_ref, kn_ref), key_blocks(vp_ref, vc_ref, vn_ref)):
        ks, vs, zero = lane_swap(kt), lane_swap(vt), jnp.zeros_like(vt)
        kdup[0].append(jnp.where(lo, kt, ks))
        kdup[1].append(jnp.where(lo, ks, kt))
        v_even[0].append(jnp.where(lo, vt, zero))
        v_odd[0].append(jnp.where(lo, zero, vs))
        v_even[1].append(jnp.where(lo, vs, zero))
        v_odd[1].append(jnp.where(lo, zero, vt))

    for c in range(WIN_QBLOCKS):
        rows = slice(c * BLK, (c + 1) * BLK)
        if c == 0:
            variant = jnp.where(i == 0, 0, 1)
        elif c == WIN_QBLOCKS - 1:
            variant = jnp.where(i == n_steps - 1, 2, 1)
        else:
            variant = 1
        probs, sink_terms = [], []
        for hd in range(A_HEADS):
            g = hd // (A_HEADS // A_KV_HEADS)
            qt = q_ref[rows, (hd // 2) * LANES:(hd // 2 + 1) * LANES]
            qm = jnp.where(lo if hd % 2 == 0 else ~lo, qt, jnp.zeros_like(qt))
            kcat = jnp.concatenate(kdup[g][c:c + 3], axis=0)
            s = _dot_nt(qm, kcat) + bias_ref[variant, hd]
            sk = sink_ref[hd] * LOG2E
            m = jnp.maximum(jnp.max(s, axis=-1, keepdims=True), sk)
            probs.append(jnp.exp2(s - m).astype(BF16))
            sink_terms.append(jnp.exp2(sk - m))

        outs = []
        for j in range(A_HEADS // 2):
            g = (2 * j) // (A_HEADS // A_KV_HEADS)
            v_e = jnp.concatenate(v_even[g][c:c + 3], axis=0)
            v_o = jnp.concatenate(v_odd[g][c:c + 3], axis=0)
            o = (_dot(probs[2 * j], jnp.concatenate([v_e, ones_lo], axis=1))
                 + _dot(probs[2 * j + 1], jnp.concatenate([v_o, ones_hi], axis=1)))
            denom = o[:, LANES:] + jnp.where(lo, sink_terms[2 * j], sink_terms[2 * j + 1])
            outs.append(o[:, :LANES] / denom)
        ya = jnp.concatenate(outs, axis=1)
        o_ref[rows, :] = _rms(ya, g_ref[...]).astype(BF16)


def _window_attn(qa, ka, va, rel_bias, sink, g_out_a, B, S):
    nb = S // BLK
    n_steps = nb // WIN_QBLOCKS
    assert nb >= 2, "first and last query blocks use distinct edge masks"
    bmap = jnp.asarray(_bucket_map())
    smem = pl.BlockSpec(memory_space=pltpu.SMEM)
    cur = lambda n: pl.BlockSpec((WIN_QBLOCKS * BLK, n), lambda b, i: (b * n_steps + i, 0))
    prv = lambda n: pl.BlockSpec((BLK, n), lambda b, i: (b * nb + jnp.maximum(i * WIN_QBLOCKS - 1, 0), 0))
    nxt = lambda n: pl.BlockSpec((BLK, n), lambda b, i: (b * nb + jnp.minimum((i + 1) * WIN_QBLOCKS, nb - 1), 0))
    return pl.pallas_call(
        functools.partial(_window_attn_kernel, n_steps),
        grid=(B, n_steps),
        in_specs=[smem, smem,
                  pl.BlockSpec((BLK, 3 * BLK), lambda b, i: (0, 0)),
                  pl.BlockSpec((1, 512), lambda b, i: (0, 0)),
                  cur(512), prv(LANES), cur(LANES), nxt(LANES), prv(LANES), cur(LANES), nxt(LANES)],
        out_specs=cur(512),
        out_shape=jax.ShapeDtypeStruct((B * S, 512), BF16),
        scratch_shapes=[pltpu.VMEM((3, A_HEADS, BLK, 3 * BLK), F32)],
        compiler_params=_cparams(2),
        name="window_attn",
    )(rel_bias, sink, bmap, g_out_a, qa, ka, ka, ka, va, va, va)


def _mla_attn_kernel(n_cast, g_ref, q_ref, k_ref, v_ref, *rest):
    o_ref = rest[n_cast]
    for src, dst in zip(rest[:n_cast], rest[n_cast + 1:]):
        dst[...] = src[...].astype(BF16)
    S = k_ref.shape[0]
    lane = lax.broadcasted_iota(jnp.int32, (1, LANES), 1)
    lo = lane < B_V_DIM
    ones_lo = jnp.broadcast_to(jnp.where(lo, 1.0, 0.0).astype(BF16), (S, LANES))
    ones_hi = jnp.broadcast_to(jnp.where(lo, 0.0, 1.0).astype(BF16), (S, LANES))
    v_pairs = []
    for j in range(B_HEADS // 2):
        vt = v_ref[:, j * LANES:(j + 1) * LANES]
        zero = jnp.zeros_like(vt)
        v_pairs.append((jnp.concatenate([jnp.where(lo, vt, zero), ones_lo], axis=1),
                        jnp.concatenate([jnp.where(lo, zero, vt), ones_hi], axis=1)))
    for r in range(q_ref.shape[0] // MLA_ROWS):
        rows = slice(r * MLA_ROWS, (r + 1) * MLA_ROWS)
        outs = []
        for j in range(B_HEADS // 2):
            ps = []
            for hd in (2 * j, 2 * j + 1):
                sl = slice(hd * LANES, (hd + 1) * LANES)
                s = _dot_nt(q_ref[rows, sl], k_ref[:, sl])
                m = jnp.max(s, axis=-1, keepdims=True)
                ps.append(jnp.exp2(s - m).astype(BF16))
            o = _dot(ps[0], v_pairs[j][0]) + _dot(ps[1], v_pairs[j][1])
            outs.append(o[:, :LANES] / o[:, LANES:])
        yb = jnp.concatenate(outs, axis=1)
        o_ref[rows, :] = _rms(yb, g_ref[...]).astype(BF16)


def _mla_attn(qm, km, vm, g_out_b, B, S, cast_weights):
    nq = S // TQ
    steps = B * nq
    qspec = lambda n: pl.BlockSpec((TQ, n), lambda b, i: (b * nq + i, 0))
    kspec = lambda n: pl.BlockSpec((S, n), lambda b, i: (b, 0))
    flat = [w.reshape(-1, w.shape[-1]) for w in cast_weights]
    slab = lambda w: pl.BlockSpec((w.shape[0] // steps, w.shape[1]), lambda b, i: (b * nq + i, 0))
    outs = pl.pallas_call(
        functools.partial(_mla_attn_kernel, len(flat)),
        grid=(B, nq),
        in_specs=[pl.BlockSpec((1, 512), lambda b, i: (0, 0)), qspec(1024), kspec(1024), kspec(512)]
                 + [slab(w) for w in flat],
        out_specs=[qspec(512)] + [slab(w) for w in flat],
        out_shape=[jax.ShapeDtypeStruct((B * S, 512), BF16)]
                  + [jax.ShapeDtypeStruct(w.shape, BF16) for w in flat],
        compiler_params=_cparams(2),
        name="mla_attn",
    )(g_out_b, qm, km, vm, *flat)
    return outs[0], [o.reshape(w.shape) for o, w in zip(outs[1:], cast_weights)]


def _split_bf16(v):
    hi = v.astype(BF16)
    return hi, (v - hi.astype(F32)).astype(BF16)


def _out_proj_kernel(x_ref, ya_ref, yb_ref, wo_ref, g_ref, wr_ref, x1_ref, h2_ref, aff_ref):
    x1 = x_ref[...] + _dot(ya_ref[...], wo_ref[0:512, :]) + _dot(yb_ref[...], wo_ref[512:1024, :])
    x1_ref[...] = x1
    h2 = _rms(x1, g_ref[...])
    h2_ref[...] = h2.astype(BF16).astype(F32)
    h_hi, h_lo = _split_bf16(h2)
    w_hi, w_lo = _split_bf16(wr_ref[...])
    logits = _dot_nt(w_hi, h_hi) + (_dot_nt(w_hi, h_lo) + _dot_nt(w_lo, h_hi))
    m = jnp.max(logits, axis=0, keepdims=True)
    e = jnp.exp(logits - m)
    aff_ref[0] = e / jnp.sum(e, axis=0, keepdims=True)


def _out_proj(xf, ya, yb, w_out, g_ffn, w_router_t, B, S):
    T = xf.shape[0]
    per_b = S // TM
    row = lambda n: pl.BlockSpec((TM, n), lambda i: (i, 0))
    full = lambda a: pl.BlockSpec(a.shape, lambda i: (0, 0))
    return pl.pallas_call(
        _out_proj_kernel,
        grid=(T // TM,),
        in_specs=[row(D_MODEL), row(512), row(512), full(w_out), full(g_ffn), full(w_router_t)],
        out_specs=[row(D_MODEL), row(D_MODEL),
                   pl.BlockSpec((1, N_EXPERTS, TM), lambda i: (i // per_b, 0, i % per_b))],
        out_shape=[jax.ShapeDtypeStruct((T, D_MODEL), F32), jax.ShapeDtypeStruct((T, D_MODEL), F32),
                   jax.ShapeDtypeStruct((B, N_EXPERTS, S), F32)],
        compiler_params=_cparams(1),
        name="out_proj",
    )(xf, ya, yb, w_out, g_ffn, w_router_t)


CHUNK = 256
ROUTE_LOG_STEPS = 12
ROUTE_LIN_STEPS = 32


def _prefix_count(flags_f32, tri):
    S = flags_f32.shape[1]
    carry = jnp.zeros((flags_f32.shape[0], 1), F32)
    parts = []
    for c in range(S // CHUNK):
        blk = flags_f32[:, c * CHUNK:(c + 1) * CHUNK]
        parts.append(_dot(blk.astype(BF16), tri) + carry)
        carry = carry + jnp.sum(blk, axis=-1, keepdims=True)
    return jnp.concatenate(parts, axis=1)


def _route_kernel(cap, aff_ref, slot_ref):
    aff = aff_ref[...]
    rows = aff.shape[0]

    def enough(pivot):
        return jnp.sum(jnp.where(aff >= pivot, 1.0, 0.0), axis=-1, keepdims=True) >= cap

    def log_body(_, st):
        lo, hi, elo, ehi = st
        mid = 0.5 * (elo + ehi)
        pivot = jnp.exp2(mid)
        ok = enough(pivot)
        return (jnp.where(ok, pivot, lo), jnp.where(ok, hi, pivot), jnp.where(ok, mid, elo), jnp.where(ok, ehi, mid))

    def lin_body(_, st):
        lo, hi = st
        pivot = 0.5 * (lo + hi)
        ok = enough(pivot)
        return jnp.where(ok, pivot, lo), jnp.where(ok, hi, pivot)

    col = lambda v: jnp.full((rows, 1), v, F32)
    lo, hi, _, _ = lax.fori_loop(0, ROUTE_LOG_STEPS, log_body, (col(0.0), col(2.0), col(-152.0), col(1.0)))
    lo, hi = lax.fori_loop(0, ROUTE_LIN_STEPS, lin_body, (lo, hi))

    gt = jnp.where(aff >= hi, 1.0, 0.0)
    eq = jnp.where(aff >= lo, 1.0, 0.0) - gt
    need = cap - jnp.sum(gt, axis=-1, keepdims=True)
    r = lax.broadcasted_iota(jnp.int32, (CHUNK, CHUNK), 0)
    c = lax.broadcasted_iota(jnp.int32, (CHUNK, CHUNK), 1)
    tri = jnp.where(r < c, 1.0, 0.0).astype(BF16)
    sel = gt + eq * jnp.where(_prefix_count(eq, tri) < need, 1.0, 0.0)
    slot = _prefix_count(sel, tri)
    slot_ref[...] = jnp.where(sel > 0.5, slot, -1.0)


def _route(aff, cap):
    B, E, S = aff.shape
    spec = pl.BlockSpec((B * E, S), lambda i: (0, 0))
    return pl.pallas_call(
        functools.partial(_route_kernel, cap),
        grid=(1,),
        in_specs=[spec],
        out_specs=spec,
        out_shape=jax.ShapeDtypeStruct((B * E, S), F32),
        compiler_params=_cparams(1),
        name="route",
    )(aff.reshape(B * E, S))


def _moe_kernel(cap, slot0_ref, aff0_ref, slot1_ref, aff1_ref, slot2_ref, aff2_ref, h_ref, wg_ref, wu_ref, wd_ref,
                o_ref, pa_ref, pb_ref, va_ref, vb_ref, xa_ref, xb_ref):
    S = h_ref.shape[0]
    slot_ids = lax.broadcasted_iota(jnp.int32, (cap, 1), 0).astype(F32)
    tok_ids = lax.broadcasted_iota(jnp.int32, (1, S), 1).astype(F32)

    def prepare(slot_row, aff_row, p_ref, v_ref, x_ref):
        sel = slot_ids == slot_row
        p_ref[...] = jnp.where(sel, 1.0, 0.0).astype(BF16)
        v_ref[...] = jnp.sum(jnp.where(sel, aff_row, 0.0), axis=-1, keepdims=True)
        tok = jnp.sum(jnp.where(sel, tok_ids, 0.0), axis=-1, keepdims=True).astype(jnp.int32)
        tok = jnp.clip(tok, 0, S - 1)
        done = []
        for grp in range(cap // GATHER_GROUP):
            ids = tok[grp * GATHER_GROUP:(grp + 1) * GATHER_GROUP, :]
            if grp >= 2:
                ids = jnp.minimum(ids, done[grp - 2] + S)
            for j in range(GATHER_GROUP):
                t = ids[j, 0]
                x_ref[pl.ds(grp * GATHER_GROUP + j, 1), :] = h_ref[pl.ds(t, 1), :]
            done.append(t)

    def expert(p_ref, v_ref, x_ref, i):
        xg = x_ref[...].astype(BF16)
        g = _dot(xg, wg_ref[i])
        u = _dot(xg, wu_ref[i])
        a = (g * (1.0 / (1.0 + jnp.exp(-g))) * u).astype(BF16)
        o = (_dot(a, wd_ref[i]) * v_ref[...]).astype(BF16)
        return _dot_tn(p_ref[...], o)

    @pl.when(pl.program_id(1) == 0)
    def _first_of_sequence():
        prepare(slot0_ref[0], aff0_ref[0], pa_ref, va_ref, xa_ref)
        o_ref[...] = jnp.zeros_like(o_ref)

    prepare(slot1_ref[0], aff1_ref[0], pb_ref, vb_ref, xb_ref)
    o_ref[...] += expert(pa_ref, va_ref, xa_ref, 0)
    prepare(slot2_ref[0], aff2_ref[0], pa_ref, va_ref, xa_ref)
    o_ref[...] += expert(pb_ref, vb_ref, xb_ref, 1)


def _moe(slot, aff, h2, wg, wu, wd, cap):
    B, E, S = aff.shape
    T = h2.shape[0]
    assert E % 2 == 0
    last = B * E - 1
    slot3 = slot.reshape(B * E, 1, S)
    aff3 = aff.reshape(B * E, 1, S)
    row = lambda f: pl.BlockSpec((1, 1, S), lambda b, k: (f(b * E + 2 * k), 0, 0))
    cur, odd, nxt = row(lambda n: n), row(lambda n: n + 1), row(lambda n: jnp.minimum(n + 2, last))
    tok = pl.BlockSpec((S, D_MODEL), lambda b, k: (b, 0))
    wspec = lambda w: pl.BlockSpec((2,) + w.shape[1:], lambda b, k: (k, 0, 0))
    return pl.pallas_call(
        functools.partial(_moe_kernel, cap),
        grid=(B, E // 2),
        in_specs=[cur, cur, odd, odd, nxt, nxt, tok, wspec(wg), wspec(wu), wspec(wd)],
        out_specs=tok,
        out_shape=jax.ShapeDtypeStruct((T, D_MODEL), F32),
        scratch_shapes=[pltpu.VMEM((cap, S), BF16), pltpu.VMEM((cap, S), BF16),
                        pltpu.VMEM((cap, 1), F32), pltpu.VMEM((cap, 1), F32),
                        pltpu.VMEM((cap, D_MODEL), F32), pltpu.VMEM((cap, D_MODEL), F32)],
        compiler_params=_cparams(2),
        name="moe",
    )(slot3, aff3, slot3, aff3, slot3, aff3, h2, wg, wu, wd)


def _ple_final_kernel(x1_ref, moe_ref, p_ref, gp_ref, wg_ref, wp_ref, gf_ref, o_ref):
    x2 = x1_ref[...] + moe_ref[...]
    z = _dot(_rms(x2, gp_ref[...]).astype(BF16), wg_ref[...])
    gate = 1.0 / (1.0 + jnp.exp(-z))
    x3 = x2 + gate * _dot(p_ref[...].astype(BF16), wp_ref[...])
    o_ref[...] = _rms(x3, gf_ref[...])


def _ple_final(x1, moe, pf, g_ple, w_gate, w_proj, g_final):
    T = x1.shape[0]
    row = lambda n: pl.BlockSpec((TM, n), lambda i: (i, 0))
    full = lambda a: pl.BlockSpec(a.shape, lambda i: (0, 0))
    return pl.pallas_call(
        _ple_final_kernel,
        grid=(T // TM,),
        in_specs=[row(D_MODEL), row(D_MODEL), row(PLE_DIM), full(g_ple), full(w_gate), full(w_proj), full(g_final)],
        out_specs=row(D_MODEL),
        out_shape=jax.ShapeDtypeStruct((T, D_MODEL), F32),
        compiler_params=_cparams(1),
        name="ple_final",
    )(x1, moe, pf, g_ple, w_gate, w_proj, g_final)


def _prep_in_proj_weight(w_in):
    D = w_in.shape[0]
    t1, t2 = w_in[:, 1152:1168], w_in[:, 1168:1184]
    cols = [w_in[:, 0:1152], t1, t2, t2, t1, jnp.zeros((D, 64), F32)]
    return jnp.concatenate(cols, axis=1).astype(BF16)


def _prep_mla_weights(w_uq, w_ukv):
    per_q = B_NOPE_DIM + B_ROPE_DIM
    w3 = w_uq.reshape(Q_RANK, B_HEADS, per_q)
    pad = LANES - per_q
    wq = jnp.pad(w3, ((0, 0), (0, 0), (0, pad))).reshape(Q_RANK, B_HEADS * LANES)
    t1 = w3[:, :, B_NOPE_DIM:B_NOPE_DIM + HALF_ROPE]
    t2 = w3[:, :, B_NOPE_DIM + HALF_ROPE:]
    wqr = jnp.concatenate([jnp.zeros((Q_RANK, B_HEADS, B_NOPE_DIM), F32), t2, t1,
                           jnp.zeros((Q_RANK, B_HEADS, pad), F32)], axis=2).reshape(Q_RANK, B_HEADS * LANES)
    w4 = w_ukv.reshape(KV_RANK, B_HEADS, B_NOPE_DIM + B_V_DIM)
    zk = jnp.zeros((KV_RANK, B_HEADS, LANES - B_NOPE_DIM), F32)
    wk = jnp.concatenate([w4[:, :, :B_NOPE_DIM], zk], axis=2).reshape(KV_RANK, B_HEADS * LANES)
    wv = w4[:, :, B_NOPE_DIM:].reshape(KV_RANK, B_HEADS * B_V_DIM)
    wkv = jnp.concatenate([wk, wv], axis=1)
    return wq.astype(BF16), wqr.astype(BF16), wkv.astype(BF16)


def kernel(x, p, positions, rel_bias, norm_mix_g, w_in, sink, g_cq, g_ckv, w_uq, w_ukv, g_out_a, g_out_b, w_out,
           norm_ffn_g, w_router, w_e_gate, w_e_up, w_e_down, norm_ple_g, w_ple_gate, w_ple_proj, final_norm_g):
    B, S, D = x.shape
    T = B * S
    cap = CAPACITY_FACTOR * S // N_EXPERTS
    xf = x.reshape(T, D)
    c_tab, s_tab = _rope_tables(positions)
    assert w_in.shape[0] == 1, "single-layer block: the final norm is fused into the last kernel"
    i = 0
    w1 = _prep_in_proj_weight(w_in[i])
    wq, wqr, wkv = _prep_mla_weights(w_uq[i], w_ukv[i])
    qa, ka, va, qm, km, vm = _in_proj(
        xf, norm_mix_g[i].reshape(1, D), w1, g_cq[i].reshape(1, -1), g_ckv[i].reshape(1, -1),
        wq, wqr, wkv, c_tab, s_tab)
    ya = _window_attn(qa, ka, va, rel_bias, sink[i], g_out_a[i].reshape(1, -1), B, S)
    yb, (wg, wu, wd) = _mla_attn(qm, km, vm, g_out_b[i].reshape(1, -1), B, S,
                                 (w_e_gate[i], w_e_up[i], w_e_down[i]))
    x1, h2, aff = _out_proj(xf, ya, yb, w_out[i].astype(BF16), norm_ffn_g[i].reshape(1, D),
                            w_router[i].T, B, S)
    slot = _route(aff, cap)
    moe = _moe(slot, aff, h2, wg, wu, wd, cap)
    out = _ple_final(x1, moe, p[i].reshape(T, -1), norm_ple_g[i].reshape(1, D),
                     w_ple_gate[i].astype(BF16), w_ple_proj[i].astype(BF16), final_norm_g.reshape(1, D))
    return out.reshape(B, S, D)
```

```python
import functools

import numpy as np
import jax
import jax.numpy as jnp
from jax import lax
from jax.experimental import pallas as pl
from jax.experimental.pallas import tpu as pltpu

F32 = jnp.float32
BF16 = jnp.bfloat16

D_MODEL = 1024
BLK = 128
WINDOW = 128
A_HEADS = 8
A_KV_HEADS = 2
A_HEAD_DIM = 64
B_HEADS = 8
B_NOPE_DIM = 64
B_ROPE_DIM = 32
B_V_DIM = 64
Q_RANK = 256
KV_RANK = 128
ROPE_THETA = 10000.0
NUM_BUCKETS = 32
MAX_DISTANCE = 128
N_EXPERTS = 16
EXPERT_FF = 512
CAPACITY_FACTOR = 2
PLE_DIM = 256
EPS = 1e-6

LANES = 128
HALF_ROPE = B_ROPE_DIM // 2
MASKED = -1e30
LOG2E = 1.4426950408889634
VMEM_LIMIT = 56 * 1024 * 1024
TM = 1024
TQ = 512
MLA_ROWS = 128
WIN_QBLOCKS = 8
GATHER_GROUP = 8


def _cparams(n_axes):
    return pltpu.CompilerParams(dimension_semantics=("arbitrary",) * n_axes,
                                vmem_limit_bytes=VMEM_LIMIT)


def _rms(x, g):
    return x * lax.rsqrt(jnp.mean(x * x, axis=-1, keepdims=True) + EPS) * g


def _dot(a, b):
    return jnp.dot(a, b, preferred_element_type=F32)


def _dot_nt(a, b):
    return lax.dot_general(a, b, (((1,), (1,)), ((), ())), preferred_element_type=F32)


def _dot_tn(a, b):
    return lax.dot_general(a, b, (((0,), (0,)), ((), ())), preferred_element_type=F32)


ROPE_GROUPS = LANES // HALF_ROPE
ROPE_ROWS = 256


def _rope_table_kernel(pos_ref, invf_ref, c_ref, s_ref):
    ang = pos_ref[...] * invf_ref[...]
    cd = jnp.cos(ang)
    sd = jnp.sin(ang)
    lane = lax.broadcasted_iota(jnp.int32, (1, LANES), 1)
    in_t1 = (lane >= B_NOPE_DIM) & (lane < B_NOPE_DIM + HALF_ROPE)
    in_t2 = (lane >= B_NOPE_DIM + HALF_ROPE) & (lane < B_NOPE_DIM + B_ROPE_DIM)
    for g in range(ROPE_GROUPS):
        own = (lane >= g * HALF_ROPE) & (lane < (g + 1) * HALF_ROPE)
        xc = jnp.where(own, cd, 0.0)
        xs = jnp.where(own, sd, 0.0)
        shift = HALF_ROPE
        while shift < LANES:
            xc = xc + pltpu.roll(xc, shift, axis=1)
            xs = xs + pltpu.roll(xs, shift, axis=1)
            shift *= 2
        c_ref[g] = jnp.where(lane < B_NOPE_DIM, 1.0, jnp.where(in_t1 | in_t2, xc, 0.0))
        s_ref[g] = jnp.where(in_t1, -xs, jnp.where(in_t2, xs, 0.0))


def _rope_tables(positions):
    T = positions.size
    rows = T // ROPE_GROUPS
    inv_freq = 1.0 / (ROPE_THETA ** (jnp.arange(0, B_ROPE_DIM, 2, dtype=F32) / B_ROPE_DIM))
    pos = jnp.repeat(positions.astype(F32).reshape(ROPE_GROUPS, rows).T, HALF_ROPE, axis=1)
    invf = jnp.tile(inv_freq, ROPE_GROUPS).reshape(1, LANES)
    out_spec = pl.BlockSpec((ROPE_GROUPS, ROPE_ROWS, LANES), lambda i: (0, i, 0))
    c, s = pl.pallas_call(
        _rope_table_kernel,
        grid=(rows // ROPE_ROWS,),
        in_specs=[pl.BlockSpec((ROPE_ROWS, LANES), lambda i: (i, 0)), pl.BlockSpec((1, LANES), lambda i: (0, 0))],
        out_specs=[out_spec] * 2,
        out_shape=[jax.ShapeDtypeStruct((ROPE_GROUPS, rows, LANES), F32)] * 2,
        compiler_params=_cparams(1),
        name="rope_tables",
    )(pos, invf)
    return c.reshape(T, LANES), s.reshape(T, LANES)


C_QA = 0
C_KA = C_QA + 512
C_VA = C_KA + LANES
C_CQ = C_VA + LANES
C_CKV = C_CQ + Q_RANK
C_KR = C_CKV + KV_RANK
C_END = C_KR + LANES
MLA_K_COLS = B_HEADS * LANES


def _in_proj_kernel(x_ref, g_ref, w1_ref, gcq_ref, gckv_ref, wq_ref, wqr_ref, wkv_ref, c_ref, s_ref,
                    qa_ref, ka_ref, va_ref, qm_ref, km_ref, vm_ref):
    h = _rms(x_ref[...], g_ref[...]).astype(BF16)
    z = _dot(h, w1_ref[...])
    qa_ref[...] = (z[:, C_QA:C_KA] * (A_HEAD_DIM ** -0.5 * LOG2E)).astype(BF16)
    ka_ref[...] = z[:, C_KA:C_VA].astype(BF16)
    va_ref[...] = z[:, C_VA:C_CQ].astype(BF16)
    cqn = _rms(z[:, C_CQ:C_CKV], gcq_ref[...]).astype(BF16)
    ckvn = _rms(z[:, C_CKV:C_KR], gckv_ref[...]).astype(BF16)
    krt = z[:, C_KR:C_END]

    lane = lax.broadcasted_iota(jnp.int32, (1, LANES), 1)
    c_full = c_ref[...]
    s_rope = s_ref[...]
    c_rope = jnp.where(lane < B_NOPE_DIM, 0.0, c_full)

    q1 = _dot(cqn, wq_ref[...])
    q2 = _dot(cqn, wqr_ref[...])
    scale = (B_NOPE_DIM + B_ROPE_DIM) ** -0.5 * LOG2E
    for hd in range(B_HEADS):
        sl = slice(hd * LANES, (hd + 1) * LANES)
        qm_ref[:, sl] = ((q1[:, sl] * c_full + q2[:, sl] * s_rope) * scale).astype(BF16)

    kv = _dot(ckvn, wkv_ref[...])
    kr_part = pltpu.roll(krt, 64, axis=1) * c_rope + pltpu.roll(krt, 32, axis=1) * s_rope
    for hd in range(B_HEADS):
        sl = slice(hd * LANES, (hd + 1) * LANES)
        km_ref[:, sl] = (kv[:, sl] + kr_part).astype(BF16)
    vm_ref[...] = kv[:, MLA_K_COLS:].astype(BF16)


def _in_proj(xf, g_mix, w1, g_cq, g_ckv, wq, wqr, wkv, c_tab, s_tab):
    T = xf.shape[0]
    row = lambda n: pl.BlockSpec((TM, n), lambda i: (i, 0))
    full = lambda a: pl.BlockSpec(a.shape, lambda i: (0, 0))
    outs = [512, LANES, LANES, MLA_K_COLS, MLA_K_COLS, B_HEADS * B_V_DIM]
    return pl.pallas_call(
        _in_proj_kernel,
        grid=(T // TM,),
        in_specs=[row(D_MODEL), full(g_mix), full(w1), full(g_cq), full(g_ckv), full(wq), full(wqr), full(wkv),
                  row(LANES), row(LANES)],
        out_specs=[row(n) for n in outs],
        out_shape=[jax.ShapeDtypeStruct((T, n), BF16) for n in outs],
        compiler_params=_cparams(1),
        name="in_proj",
    )(xf, g_mix, w1, g_cq, g_ckv, wq, wqr, wkv, c_tab, s_tab)


def _bucket_map():
    qi = np.arange(BLK)[:, None]
    kj = np.arange(3 * BLK)[None, :]
    rel = kj - BLK - qi
    n = np.abs(rel)
    half = NUM_BUCKETS // 2
    max_exact = half // 2
    thresholds = [int(np.ceil(max_exact * 2 ** (k / 2) - 1e-9)) for k in range(1, half - max_exact)]
    large = max_exact + sum((n >= t).astype(np.int64) for t in thresholds)
    large = np.minimum(large, half - 1)
    bucket = np.where(rel > 0, half, 0) + np.where(n < max_exact, n, large)
    return np.where(n <= WINDOW, bucket, -1).astype(np.int32)


def _window_attn_kernel(n_steps, relb_ref, sink_ref, bmap_ref, g_ref, q_ref, kp_ref, kc_ref, kn_ref,
                        vp_ref, vc_ref, vn_ref, o_ref, bias_ref):
    b = pl.program_id(0)
    i = pl.program_id(1)

    @pl.when((b == 0) & (i == 0))
    def _build_bias():
        bmap = bmap_ref[...]
        col = lax.broadcasted_iota(jnp.int32, (1, 3 * BLK), 1)
        for hd in range(A_HEADS):
            bias_ref[1, hd] = jnp.full((BLK, 3 * BLK), MASKED, F32)
        for bk in range(NUM_BUCKETS):
            m = bmap == bk
            for hd in range(A_HEADS):
                bias_ref[1, hd] = jnp.where(m, relb_ref[bk, hd] * LOG2E, bias_ref[1, hd])
        for hd in range(A_HEADS):
            bias_ref[0, hd] = jnp.where(col < BLK, MASKED, bias_ref[1, hd])
            bias_ref[2, hd] = jnp.where(col >= 2 * BLK, MASKED, bias_ref[1, hd])

    lane = lax.broadcasted_iota(jnp.int32, (1, LANES), 1)
    lo = lane < A_HEAD_DIM
    ones_lo = jnp.broadcast_to(jnp.where(lo, 1.0, 0.0).astype(BF16), (3 * BLK, LANES))
    ones_hi = jnp.broadcast_to(jnp.where(lo, 0.0, 1.0).astype(BF16), (3 * BLK, LANES))

    def key_blocks(p_ref, c_ref, n_ref):
        return [p_ref[...]] + [c_ref[c * BLK:(c + 1) * BLK, :] for c in range(WIN_QBLOCKS)] + [n_ref[...]]

    def lane_swap(t):
        return pltpu.bitcast(pltpu.roll(pltpu.bitcast(t, jnp.int32), LANES // 2, axis=1), BF16)

    kdup = [[], []]
    v_even = [[], []]
    v_odd = [[], []]
    for kt, vt in zip(key_blocks(kp_ref, kc_ref, kn_ref), key_blocks(vp_ref, vc_ref, vn_ref)):
        ks, vs, zero = lane_swap(kt), lane_swap(vt), jnp.zeros_like(vt)
        kdup[0].append(jnp.where(lo, kt, ks))
        kdup[1].append(jnp.where(lo, ks, kt))
        v_even[0].append(jnp.where(lo, vt, zero))
        v_odd[0].append(jnp.where(lo, zero, vs))
        v_even[1].append(jnp.where(lo, vs, zero))
        v_odd[1].append(jnp.where(lo, zero, vt))

    for c in range(WIN_QBLOCKS):
        rows = slice(c * BLK, (c + 1) * BLK)
        if c == 0:
            variant = jnp.where(i == 0, 0, 1)
        elif c == WIN_QBLOCKS - 1:
            variant = jnp.where(i == n_steps - 1, 2, 1)
        else:
            variant = 1
        probs, sink_terms = [], []
        for hd in range(A_HEADS):
            g = hd // (A_HEADS // A_KV_HEADS)
            qt = q_ref[rows, (hd // 2) * LANES:(hd // 2 + 1) * LANES]
            qm = jnp.where(lo if hd % 2 == 0 else ~lo, qt, jnp.zeros_like(qt))
            kcat = jnp.concatenate(kdup[g][c:c + 3], axis=0)
            s = _dot_nt(qm, kcat) + bias_ref[variant, hd]
            sk = sink_ref[hd] * LOG2E
            m = jnp.maximum(jnp.max(s, axis=-1, keepdims=True), sk)
            probs.append(jnp.exp2(s - m).astype(BF16))
            sink_terms.append(jnp.exp2(sk - m))

        outs = []
        for j in range(A_HEADS // 2):
            g = (2 * j) // (A_HEADS // A_KV_HEADS)
            v_e = jnp.concatenate(v_even[g][c:c + 3], axis=0)
            v_o = jnp.concatenate(v_odd[g][c:c + 3], axis=0)
            o = (_dot(probs[2 * j], jnp.concatenate([v_e, ones_lo], axis=1))
                 + _dot(probs[2 * j + 1], jnp.concatenate([v_o, ones_hi], axis=1)))
            denom = o[:, LANES:] + jnp.where(lo, sink_terms[2 * j], sink_terms[2 * j + 1])
            outs.append(o[:, :LANES] / denom)
        ya = jnp.concatenate(outs, axis=1)
        o_ref[rows, :] = _rms(ya, g_ref[...]).astype(BF16)


def _window_attn(qa, ka, va, rel_bias, sink, g_out_a, B, S):
    nb = S // BLK
    n_steps = nb // WIN_QBLOCKS
    assert nb >= 2, "first and last query blocks use distinct edge masks"
    bmap = jnp.asarray(_bucket_map())
    smem = pl.BlockSpec(memory_space=pltpu.SMEM)
    cur = lambda n: pl.BlockSpec((WIN_QBLOCKS * BLK, n), lambda b, i: (b * n_steps + i, 0))
    prv = lambda n: pl.BlockSpec((BLK, n), lambda b, i: (b * nb + jnp.maximum(i * WIN_QBLOCKS - 1, 0), 0))
    nxt = lambda n: pl.BlockSpec((BLK, n), lambda b, i: (b * nb + jnp.minimum((i + 1) * WIN_QBLOCKS, nb - 1), 0))
    return pl.pallas_call(
        functools.partial(_window_attn_kernel, n_steps),
        grid=(B, n_steps),
        in_specs=[smem, smem,
                  pl.BlockSpec((BLK, 3 * BLK), lambda b, i: (0, 0)),
                  pl.BlockSpec((1, 512), lambda b, i: (0, 0)),
                  cur(512), prv(LANES), cur(LANES), nxt(LANES), prv(LANES), cur(LANES), nxt(LANES)],
        out_specs=cur(512),
        out_shape=jax.ShapeDtypeStruct((B * S, 512), BF16),
        scratch_shapes=[pltpu.VMEM((3, A_HEADS, BLK, 3 * BLK), F32)],
        compiler_params=_cparams(2),
        name="window_attn",
    )(rel_bias, sink, bmap, g_out_a, qa, ka, ka, ka, va, va, va)


def _mla_attn_kernel(n_cast, g_ref, q_ref, k_ref, v_ref, *rest):
    o_ref = rest[n_cast]
    for src, dst in zip(rest[:n_cast], rest[n_cast + 1:]):
        dst[...] = src[...].astype(BF16)
    S = k_ref.shape[0]
    lane = lax.broadcasted_iota(jnp.int32, (1, LANES), 1)
    lo = lane < B_V_DIM
    ones_lo = jnp.broadcast_to(jnp.where(lo, 1.0, 0.0).astype(BF16), (S, LANES))
    ones_hi = jnp.broadcast_to(jnp.where(lo, 0.0, 1.0).astype(BF16), (S, LANES))
    v_pairs = []
    for j in range(B_HEADS // 2):
        vt = v_ref[:, j * LANES:(j + 1) * LANES]
        zero = jnp.zeros_like(vt)
        v_pairs.append((jnp.concatenate([jnp.where(lo, vt, zero), ones_lo], axis=1),
                        jnp.concatenate([jnp.where(lo, zero, vt), ones_hi], axis=1)))
    for r in range(q_ref.shape[0] // MLA_ROWS):
        rows = slice(r * MLA_ROWS, (r + 1) * MLA_ROWS)
        outs = []
        for j in range(B_HEADS // 2):
            ps = []
            for hd in (2 * j, 2 * j + 1):
                sl = slice(hd * LANES, (hd + 1) * LANES)
                s = _dot_nt(q_ref[rows, sl], k_ref[:, sl])
                m = jnp.max(s, axis=-1, keepdims=True)
                ps.append(jnp.exp2(s - m).astype(BF16))
            o = _dot(ps[0], v_pairs[j][0]) + _dot(ps[1], v_pairs[j][1])
            outs.append(o[:, :LANES] / o[:, LANES:])
        yb = jnp.concatenate(outs, axis=1)
        o_ref[rows, :] = _rms(yb, g_ref[...]).astype(BF16)


def _mla_attn(qm, km, vm, g_out_b, B, S, cast_weights):
    nq = S // TQ
    steps = B * nq
    qspec = lambda n: pl.BlockSpec((TQ, n), lambda b, i: (b * nq + i, 0))
    kspec = lambda n: pl.BlockSpec((S, n), lambda b, i: (b, 0))
    flat = [w.reshape(-1, w.shape[-1]) for w in cast_weights]
    slab = lambda w: pl.BlockSpec((w.shape[0] // steps, w.shape[1]), lambda b, i: (b * nq + i, 0))
    outs = pl.pallas_call(
        functools.partial(_mla_attn_kernel, len(flat)),
        grid=(B, nq),
        in_specs=[pl.BlockSpec((1, 512), lambda b, i: (0, 0)), qspec(1024), kspec(1024), kspec(512)]
                 + [slab(w) for w in flat],
        out_specs=[qspec(512)] + [slab(w) for w in flat],
        out_shape=[jax.ShapeDtypeStruct((B * S, 512), BF16)]
                  + [jax.ShapeDtypeStruct(w.shape, BF16) for w in flat],
        compiler_params=_cparams(2),
        name="mla_attn",
    )(g_out_b, qm, km, vm, *flat)
    return outs[0], [o.reshape(w.shape) for o, w in zip(outs[1:], cast_weights)]


def _split_bf16(v):
    hi = v.astype(BF16)
    return hi, (v - hi.astype(F32)).astype(BF16)


def _out_proj_kernel(x_ref, ya_ref, yb_ref, wo_ref, g_ref, wr_ref, x1_ref, h2_ref, aff_ref):
    x1 = x_ref[...] + _dot(ya_ref[...], wo_ref[0:512, :]) + _dot(yb_ref[...], wo_ref[512:1024, :])
    x1_ref[...] = x1
    h2 = _rms(x1, g_ref[...])
    h2_ref[...] = h2.astype(BF16).astype(F32)
    h_hi, h_lo = _split_bf16(h2)
    w_hi, w_lo = _split_bf16(wr_ref[...])
    logits = _dot_nt(w_hi, h_hi) + (_dot_nt(w_hi, h_lo) + _dot_nt(w_lo, h_hi))
    m = jnp.max(logits, axis=0, keepdims=True)
    e = jnp.exp(logits - m)
    aff_ref[0] = e / jnp.sum(e, axis=0, keepdims=True)


def _out_proj(xf, ya, yb, w_out, g_ffn, w_router_t, B, S):
    T = xf.shape[0]
    per_b = S // TM
    row = lambda n: pl.BlockSpec((TM, n), lambda i: (i, 0))
    full = lambda a: pl.BlockSpec(a.shape, lambda i: (0, 0))
    return pl.pallas_call(
        _out_proj_kernel,
        grid=(T // TM,),
        in_specs=[row(D_MODEL), row(512), row(512), full(w_out), full(g_ffn), full(w_router_t)],
        out_specs=[row(D_MODEL), row(D_MODEL),
                   pl.BlockSpec((1, N_EXPERTS, TM), lambda i: (i // per_b, 0, i % per_b))],
        out_shape=[jax.ShapeDtypeStruct((T, D_MODEL), F32), jax.ShapeDtypeStruct((T, D_MODEL), F32),
                   jax.ShapeDtypeStruct((B, N_EXPERTS, S), F32)],
        compiler_params=_cparams(1),
        name="out_proj",
    )(xf, ya, yb, w_out, g_ffn, w_router_t)


CHUNK = 256
ROUTE_LOG_STEPS = 12
ROUTE_LIN_STEPS = 32


def _prefix_count(flags_f32, tri):
    S = flags_f32.shape[1]
    carry = jnp.zeros((flags_f32.shape[0], 1), F32)
    parts = []
    for c in range(S // CHUNK):
        blk = flags_f32[:, c * CHUNK:(c + 1) * CHUNK]
        parts.append(_dot(blk.astype(BF16), tri) + carry)
        carry = carry + jnp.sum(blk, axis=-1, keepdims=True)
    return jnp.concatenate(parts, axis=1)


def _route_kernel(cap, aff_ref, slot_ref):
    aff = aff_ref[...]
    rows = aff.shape[0]

    def enough(pivot):
        return jnp.sum(jnp.where(aff >= pivot, 1.0, 0.0), axis=-1, keepdims=True) >= cap

    def log_body(_, st):
        lo, hi, elo, ehi = st
        mid = 0.5 * (elo + ehi)
        pivot = jnp.exp2(mid)
        ok = enough(pivot)
        return (jnp.where(ok, pivot, lo), jnp.where(ok, hi, pivot), jnp.where(ok, mid, elo), jnp.where(ok, ehi, mid))

    def lin_body(_, st):
        lo, hi = st
        pivot = 0.5 * (lo + hi)
        ok = enough(pivot)
        return jnp.where(ok, pivot, lo), jnp.where(ok, hi, pivot)

    col = lambda v: jnp.full((rows, 1), v, F32)
    lo, hi, _, _ = lax.fori_loop(0, ROUTE_LOG_STEPS, log_body, (col(0.0), col(2.0), col(-152.0), col(1.0)))
    lo, hi = lax.fori_loop(0, ROUTE_LIN_STEPS, lin_body, (lo, hi))

    gt = jnp.where(aff >= hi, 1.0, 0.0)
    eq = jnp.where(aff >= lo, 1.0, 0.0) - gt
    need = cap - jnp.sum(gt, axis=-1, keepdims=True)
    r = lax.broadcasted_iota(jnp.int32, (CHUNK, CHUNK), 0)
    c = lax.broadcasted_iota(jnp.int32, (CHUNK, CHUNK), 1)
    tri = jnp.where(r < c, 1.0, 0.0).astype(BF16)
    sel = gt + eq * jnp.where(_prefix_count(eq, tri) < need, 1.0, 0.0)
    slot = _prefix_count(sel, tri)
    slot_ref[...] = jnp.where(sel > 0.5, slot, -1.0)


def _route(aff, cap):
    B, E, S = aff.shape
    spec = pl.BlockSpec((B * E, S), lambda i: (0, 0))
    return pl.pallas_call(
        functools.partial(_route_kernel, cap),
        grid=(1,),
        in_specs=[spec],
        out_specs=spec,
        out_shape=jax.ShapeDtypeStruct((B * E, S), F32),
        compiler_params=_cparams(1),
        name="route",
    )(aff.reshape(B * E, S))


def _moe_kernel(cap, slot0_ref, aff0_ref, slot1_ref, aff1_ref, slot2_ref, aff2_ref, h_ref, wg_ref, wu_ref, wd_ref,
                o_ref, pa_ref, pb_ref, va_ref, vb_ref, xa_ref, xb_ref):
    S = h_ref.shape[0]
    slot_ids = lax.broadcasted_iota(jnp.int32, (cap, 1), 0).astype(F32)
    tok_ids = lax.broadcasted_iota(jnp.int32, (1, S), 1).astype(F32)

    def prepare(slot_row, aff_row, p_ref, v_ref, x_ref):
        sel = slot_ids == slot_row
        p_ref[...] = jnp.where(sel, 1.0, 0.0).astype(BF16)
        v_ref[...] = jnp.sum(jnp.where(sel, aff_row, 0.0), axis=-1, keepdims=True)
        tok = jnp.sum(jnp.where(sel, tok_ids, 0.0), axis=-1, keepdims=True).astype(jnp.int32)
        tok = jnp.clip(tok, 0, S - 1)
        done = []
        for grp in range(cap // GATHER_GROUP):
            ids = tok[grp * GATHER_GROUP:(grp + 1) * GATHER_GROUP, :]
            if grp >= 2:
                ids = jnp.minimum(ids, done[grp - 2] + S)
            for j in range(GATHER_GROUP):
                t = ids[j, 0]
                x_ref[pl.ds(grp * GATHER_GROUP + j, 1), :] = h_ref[pl.ds(t, 1), :]
            done.append(t)

    def expert(p_ref, v_ref, x_ref, i):
        xg = x_ref[...].astype(BF16)
        g = _dot(xg, wg_ref[i])
        u = _dot(xg, wu_ref[i])
        a = (g * (1.0 / (1.0 + jnp.exp(-g))) * u).astype(BF16)
        o = (_dot(a, wd_ref[i]) * v_ref[...]).astype(BF16)
        return _dot_tn(p_ref[...], o)

    @pl.when(pl.program_id(1) == 0)
    def _first_of_sequence():
        prepare(slot0_ref[0], aff0_ref[0], pa_ref, va_ref, xa_ref)
        o_ref[...] = jnp.zeros_like(o_ref)

    prepare(slot1_ref[0], aff1_ref[0], pb_ref, vb_ref, xb_ref)
    o_ref[...] += expert(pa_ref, va_ref, xa_ref, 0)
    prepare(slot2_ref[0], aff2_ref[0], pa_ref, va_ref, xa_ref)
    o_ref[...] += expert(pb_ref, vb_ref, xb_ref, 1)


def _moe(slot, aff, h2, wg, wu, wd, cap):
    B, E, S = aff.shape
    T = h2.shape[0]
    assert E % 2 == 0
    last = B * E - 1
    slot3 = slot.reshape(B * E, 1, S)
    aff3 = aff.reshape(B * E, 1, S)
    row = lambda f: pl.BlockSpec((1, 1, S), lambda b, k: (f(b * E + 2 * k), 0, 0))
    cur, odd, nxt = row(lambda n: n), row(lambda n: n + 1), row(lambda n: jnp.minimum(n + 2, last))
    tok = pl.BlockSpec((S, D_MODEL), lambda b, k: (b, 0))
    wspec = lambda w: pl.BlockSpec((2,) + w.shape[1:], lambda b, k: (k, 0, 0))
    return pl.pallas_call(
        functools.partial(_moe_kernel, cap),
        grid=(B, E // 2),
        in_specs=[cur, cur, odd, odd, nxt, nxt, tok, wspec(wg), wspec(wu), wspec(wd)],
        out_specs=tok,
        out_shape=jax.ShapeDtypeStruct((T, D_MODEL), F32),
        scratch_shapes=[pltpu.VMEM((cap, S), BF16), pltpu.VMEM((cap, S), BF16),
                        pltpu.VMEM((cap, 1), F32), pltpu.VMEM((cap, 1), F32),
                        pltpu.VMEM((cap, D_MODEL), F32), pltpu.VMEM((cap, D_MODEL), F32)],
        compiler_params=_cparams(2),
        name="moe",
    )(slot3, aff3, slot3, aff3, slot3, aff3, h2, wg, wu, wd)


def _ple_final_kernel(x1_ref, moe_ref, p_ref, gp_ref, wg_ref, wp_ref, gf_ref, o_ref):
    x2 = x1_ref[...] + moe_ref[...]
    z = _dot(_rms(x2, gp_ref[...]).astype(BF16), wg_ref[...])
    gate = 1.0 / (1.0 + jnp.exp(-z))
    x3 = x2 + gate * _dot(p_ref[...].astype(BF16), wp_ref[...])
    o_ref[...] = _rms(x3, gf_ref[...])


def _ple_final(x1, moe, pf, g_ple, w_gate, w_proj, g_final):
    T = x1.shape[0]
    row = lambda n: pl.BlockSpec((TM, n), lambda i: (i, 0))
    full = lambda a: pl.BlockSpec(a.shape, lambda i: (0, 0))
    return pl.pallas_call(
        _ple_final_kernel,
        grid=(T // TM,),
        in_specs=[row(D_MODEL), row(D_MODEL), row(PLE_DIM), full(g_ple), full(w_gate), full(w_proj), full(g_final)],
        out_specs=row(D_MODEL),
        out_shape=jax.ShapeDtypeStruct((T, D_MODEL), F32),
        compiler_params=_cparams(1),
        name="ple_final",
    )(x1, moe, pf, g_ple, w_gate, w_proj, g_final)


def _prep_in_proj_weight(w_in):
    D = w_in.shape[0]
    t1, t2 = w_in[:, 1152:1168], w_in[:, 1168:1184]
    cols = [w_in[:, 0:1152], t1, t2, t2, t1, jnp.zeros((D, 64), F32)]
    return jnp.concatenate(cols, axis=1).astype(BF16)


def _prep_mla_weights(w_uq, w_ukv):
    per_q = B_NOPE_DIM + B_ROPE_DIM
    w3 = w_uq.reshape(Q_RANK, B_HEADS, per_q)
    pad = LANES - per_q
    wq = jnp.pad(w3, ((0, 0), (0, 0), (0, pad))).reshape(Q_RANK, B_HEADS * LANES)
    t1 = w3[:, :, B_NOPE_DIM:B_NOPE_DIM + HALF_ROPE]
    t2 = w3[:, :, B_NOPE_DIM + HALF_ROPE:]
    wqr = jnp.concatenate([jnp.zeros((Q_RANK, B_HEADS, B_NOPE_DIM), F32), t2, t1,
                           jnp.zeros((Q_RANK, B_HEADS, pad), F32)], axis=2).reshape(Q_RANK, B_HEADS * LANES)
    w4 = w_ukv.reshape(KV_RANK, B_HEADS, B_NOPE_DIM + B_V_DIM)
    zk = jnp.zeros((KV_RANK, B_HEADS, LANES - B_NOPE_DIM), F32)
    wk = jnp.concatenate([w4[:, :, :B_NOPE_DIM], zk], axis=2).reshape(KV_RANK, B_HEADS * LANES)
    wv = w4[:, :, B_NOPE_DIM:].reshape(KV_RANK, B_HEADS * B_V_DIM)
    wkv = jnp.concatenate([wk, wv], axis=1)
    return wq.astype(BF16), wqr.astype(BF16), wkv.astype(BF16)


def kernel(x, p, positions, rel_bias, norm_mix_g, w_in, sink, g_cq, g_ckv, w_uq, w_ukv, g_out_a, g_out_b, w_out,
           norm_ffn_g, w_router, w_e_gate, w_e_up, w_e_down, norm_ple_g, w_ple_gate, w_ple_proj, final_norm_g):
    B, S, D = x.shape
    T = B * S
    cap = CAPACITY_FACTOR * S // N_EXPERTS
    xf = x.reshape(T, D)
    c_tab, s_tab = _rope_tables(positions)
    assert w_in.shape[0] == 1, "single-layer block: the final norm is fused into the last kernel"
    i = 0
    w1 = _prep_in_proj_weight(w_in[i])
    wq, wqr, wkv = _prep_mla_weights(w_uq[i], w_ukv[i])
    qa, ka, va, qm, km, vm = _in_proj(
        xf, norm_mix_g[i].reshape(1, D), w1, g_cq[i].reshape(1, -1), g_ckv[i].reshape(1, -1),
        wq, wqr, wkv, c_tab, s_tab)
    ya = _window_attn(qa, ka, va, rel_bias, sink[i], g_out_a[i].reshape(1, -1), B, S)
    yb, (wg, wu, wd) = _mla_attn(qm, km, vm, g_out_b[i].reshape(1, -1), B, S,
                                 (w_e_gate[i], w_e_up[i], w_e_down[i]))
    x1, h2, aff = _out_proj(xf, ya, yb, w_out[i].astype(BF16), norm_ffn_g[i].reshape(1, D),
                            w_router[i].T, B, S)
    slot = _route(aff, cap)
    moe = _moe(slot, aff, h2, wg, wu, wd, cap)
    out = _ple_final(x1, moe, p[i].reshape(T, -1), norm_ple_g[i].reshape(1, D),
                     w_ple_gate[i].astype(BF16), w_ple_proj[i].astype(BF16), final_norm_g.reshape(1, D))
    return out.reshape(B, S, D)
```

```python
import functools

import numpy as np
import jax
import jax.numpy as jnp
from jax import lax
from jax.experimental import pallas as pl
from jax.experimental.pallas import tpu as pltpu

F32 = jnp.float32
BF16 = jnp.bfloat16

D_MODEL = 1024
BLK = 128
WINDOW = 128
A_HEADS = 8
A_KV_HEADS = 2
A_HEAD_DIM = 64
B_HEADS = 8
B_NOPE_DIM = 64
B_ROPE_DIM = 32
B_V_DIM = 64
Q_RANK = 256
KV_RANK = 128
ROPE_THETA = 10000.0
NUM_BUCKETS = 32
MAX_DISTANCE = 128
N_EXPERTS = 16
EXPERT_FF = 512
CAPACITY_FACTOR = 2
PLE_DIM = 256
EPS = 1e-6

LANES = 128
HALF_ROPE = B_ROPE_DIM // 2
MASKED = -1e30
LOG2E = 1.4426950408889634
VMEM_LIMIT = 56 * 1024 * 1024
TM = 1024
TQ = 512
MLA_ROWS = 128
WIN_QBLOCKS = 8
GATHER_GROUP = 8


def _cparams(n_axes):
    return pltpu.CompilerParams(dimension_semantics=("arbitrary",) * n_axes,
                                vmem_limit_bytes=VMEM_LIMIT)


def _rms(x, g):
    return x * lax.rsqrt(jnp.mean(x * x, axis=-1, keepdims=True) + EPS) * g


def _dot(a, b):
    return jnp.dot(a, b, preferred_element_type=F32)


def _dot_nt(a, b):
    return lax.dot_general(a, b, (((1,), (1,)), ((), ())), preferred_element_type=F32)


def _dot_tn(a, b):
    return lax.dot_general(a, b, (((0,), (0,)), ((), ())), preferred_element_type=F32)


ROPE_GROUPS = LANES // HALF_ROPE


def _rope_table_kernel(pos_ref, invf_ref, cos_ref, sin_ref):
    ang = pos_ref[...] * invf_ref[...]
    cos_ref[...] = jnp.cos(ang)
    sin_ref[...] = jnp.sin(ang)


def _rope_tables(positions):
    T = positions.size
    rows = T // ROPE_GROUPS
    inv_freq = 1.0 / (ROPE_THETA ** (jnp.arange(0, B_ROPE_DIM, 2, dtype=F32) / B_ROPE_DIM))
    pos = jnp.repeat(positions.astype(F32).reshape(ROPE_GROUPS, rows).T, HALF_ROPE, axis=1)
    invf = jnp.tile(inv_freq, ROPE_GROUPS).reshape(1, LANES)
    spec = pl.BlockSpec((rows, LANES), lambda i: (0, 0))
    return pl.pallas_call(
        _rope_table_kernel,
        grid=(1,),
        in_specs=[spec, pl.BlockSpec((1, LANES), lambda i: (0, 0))],
        out_specs=[spec] * 2,
        out_shape=[jax.ShapeDtypeStruct((rows, LANES), F32)] * 2,
        compiler_params=_cparams(1),
        name="rope_tables",
    )(pos, invf)


def _expand_rope(cd, sd, group):
    lane = lax.broadcasted_iota(jnp.int32, (1, LANES), 1)
    own = (lane >= group * HALF_ROPE) & (lane < (group + 1) * HALF_ROPE)
    xc = jnp.where(own, cd, 0.0)
    xs = jnp.where(own, sd, 0.0)
    shift = HALF_ROPE
    while shift < LANES:
        xc = xc + pltpu.roll(xc, shift, axis=1)
        xs = xs + pltpu.roll(xs, shift, axis=1)
        shift *= 2
    in_t1 = (lane >= B_NOPE_DIM) & (lane < B_NOPE_DIM + HALF_ROPE)
    in_t2 = (lane >= B_NOPE_DIM + HALF_ROPE) & (lane < B_NOPE_DIM + B_ROPE_DIM)
    c = jnp.where(lane < B_NOPE_DIM, 1.0, jnp.where(in_t1 | in_t2, xc, 0.0))
    s = jnp.where(in_t1, -xs, jnp.where(in_t2, xs, 0.0))
    return c, s


C_QA = 0
C_KA = C_QA + 512
C_VA = C_KA + LANES
C_CQ = C_VA + LANES
C_CKV = C_CQ + Q_RANK
C_KR = C_CKV + KV_RANK
C_END = C_KR + LANES
MLA_K_COLS = B_HEADS * LANES


def _in_proj_kernel(tiles_per_group, x_ref, g_ref, w1_ref, gcq_ref, gckv_ref, wq_ref, wqr_ref, wkv_ref,
                    cos_ref, sin_ref, qa_ref, ka_ref, va_ref, qm_ref, km_ref, vm_ref):
    h = _rms(x_ref[...], g_ref[...]).astype(BF16)
    z = _dot(h, w1_ref[...])
    qa_ref[...] = (z[:, C_QA:C_KA] * (A_HEAD_DIM ** -0.5 * LOG2E)).astype(BF16)
    ka_ref[...] = z[:, C_KA:C_VA].astype(BF16)
    va_ref[...] = z[:, C_VA:C_CQ].astype(BF16)
    cqn = _rms(z[:, C_CQ:C_CKV], gcq_ref[...]).astype(BF16)
    ckvn = _rms(z[:, C_CKV:C_KR], gckv_ref[...]).astype(BF16)
    krt = z[:, C_KR:C_END]

    lane = lax.broadcasted_iota(jnp.int32, (1, LANES), 1)
    c_full, s_rope = _expand_rope(cos_ref[...], sin_ref[...], pl.program_id(0) // tiles_per_group)
    c_rope = jnp.where(lane < B_NOPE_DIM, 0.0, c_full)

    q1 = _dot(cqn, wq_ref[...])
    q2 = _dot(cqn, wqr_ref[...])
    scale = (B_NOPE_DIM + B_ROPE_DIM) ** -0.5 * LOG2E
    for hd in range(B_HEADS):
        sl = slice(hd * LANES, (hd + 1) * LANES)
        qm_ref[:, sl] = ((q1[:, sl] * c_full + q2[:, sl] * s_rope) * scale).astype(BF16)

    kv = _dot(ckvn, wkv_ref[...])
    kr_part = pltpu.roll(krt, 64, axis=1) * c_rope + pltpu.roll(krt, 32, axis=1) * s_rope
    for hd in range(B_HEADS):
        sl = slice(hd * LANES, (hd + 1) * LANES)
        km_ref[:, sl] = (kv[:, sl] + kr_part).astype(BF16)
    vm_ref[...] = kv[:, MLA_K_COLS:].astype(BF16)


def _in_proj(xf, g_mix, w1, g_cq, g_ckv, wq, wqr, wkv, cos_d, sin_d):
    T = xf.shape[0]
    tiles_per_group = cos_d.shape[0] // TM
    assert tiles_per_group * TM == cos_d.shape[0]
    row = lambda n: pl.BlockSpec((TM, n), lambda i: (i, 0))
    full = lambda a: pl.BlockSpec(a.shape, lambda i: (0, 0))
    dense = pl.BlockSpec((TM, LANES), lambda i: (i % tiles_per_group, 0))
    outs = [512, LANES, LANES, MLA_K_COLS, MLA_K_COLS, B_HEADS * B_V_DIM]
    return pl.pallas_call(
        functools.partial(_in_proj_kernel, tiles_per_group),
        grid=(T // TM,),
        in_specs=[row(D_MODEL), full(g_mix), full(w1), full(g_cq), full(g_ckv), full(wq), full(wqr), full(wkv),
                  dense, dense],
        out_specs=[row(n) for n in outs],
        out_shape=[jax.ShapeDtypeStruct((T, n), BF16) for n in outs],
        compiler_params=_cparams(1),
        name="in_proj",
    )(xf, g_mix, w1, g_cq, g_ckv, wq, wqr, wkv, cos_d, sin_d)


def _bucket_map():
    qi = np.arange(BLK)[:, None]
    kj = np.arange(3 * BLK)[None, :]
    rel = kj - BLK - qi
    n = np.abs(rel)
    half = NUM_BUCKETS // 2
    max_exact = half // 2
    thresholds = [int(np.ceil(max_exact * 2 ** (k / 2) - 1e-9)) for k in range(1, half - max_exact)]
    large = max_exact + sum((n >= t).astype(np.int64) for t in thresholds)
    large = np.minimum(large, half - 1)
    bucket = np.where(rel > 0, half, 0) + np.where(n < max_exact, n, large)
    return np.where(n <= WINDOW, bucket, -1).astype(np.int32)


def _window_attn_kernel(n_steps, relb_ref, sink_ref, bmap_ref, g_ref, q_ref, kp_ref, kc_ref, kn_ref,
                        vp_ref, vc_ref, vn_ref, o_ref, bias_ref):
    b = pl.program_id(0)
    i = pl.program_id(1)

    @pl.when((b == 0) & (i == 0))
    def _build_bias():
        bmap = bmap_ref[...]
        col = lax.broadcasted_iota(jnp.int32, (1, 3 * BLK), 1)
        for hd in range(A_HEADS):
            bias_ref[1, hd] = jnp.full((BLK, 3 * BLK), MASKED, F32)
        for bk in range(NUM_BUCKETS):
            m = bmap == bk
            for hd in range(A_HEADS):
                bias_ref[1, hd] = jnp.where(m, relb_ref[bk, hd] * LOG2E, bias_ref[1, hd])
        for hd in range(A_HEADS):
            bias_ref[0, hd] = jnp.where(col < BLK, MASKED, bias_ref[1, hd])
            bias_ref[2, hd] = jnp.where(col >= 2 * BLK, MASKED, bias_ref[1, hd])

    lane = lax.broadcasted_iota(jnp.int32, (1, LANES), 1)
    lo = lane < A_HEAD_DIM
    ones_lo = jnp.broadcast_to(jnp.where(lo, 1.0, 0.0).astype(BF16), (3 * BLK, LANES))
    ones_hi = jnp.broadcast_to(jnp.where(lo, 0.0, 1.0).astype(BF16), (3 * BLK, LANES))

    def key_blocks(p_ref, c_ref, n_ref):
        return [p_ref[...]] + [c_ref[c * BLK:(c + 1) * BLK, :] for c in range(WIN_QBLOCKS)] + [n_ref[...]]

    def lane_swap(t):
        return pltpu.bitcast(pltpu.roll(pltpu.bitcast(t, jnp.int32), LANES // 2, axis=1), BF16)

    kdup = [[], []]
    v_even = [[], []]
    v_odd = [[], []]
    for kt, vt in zip(key_blocks(kp_ref, kc_ref, kn_ref), key_blocks(vp_ref, vc_ref, vn_ref)):
        ks, vs, zero = lane_swap(kt), lane_swap(vt), jnp.zeros_like(vt)
        kdup[0].append(jnp.where(lo, kt, ks))
        kdup[1].append(jnp.where(lo, ks, kt))
        v_even[0].append(jnp.where(lo, vt, zero))
        v_odd[0].append(jnp.where(lo, zero, vs))
        v_even[1].append(jnp.where(lo, vs, zero))
        v_odd[1].append(jnp.where(lo, zero, vt))

    for c in range(WIN_QBLOCKS):
        rows = slice(c * BLK, (c + 1) * BLK)
        if c == 0:
            variant = jnp.where(i == 0, 0, 1)
        elif c == WIN_QBLOCKS - 1:
            variant = jnp.where(i == n_steps - 1, 2, 1)
        else:
            variant = 1
        probs, sink_terms = [], []
        for hd in range(A_HEADS):
            g = hd // (A_HEADS // A_KV_HEADS)
            qt = q_ref[rows, (hd // 2) * LANES:(hd // 2 + 1) * LANES]
            qm = jnp.where(lo if hd % 2 == 0 else ~lo, qt, jnp.zeros_like(qt))
            kcat = jnp.concatenate(kdup[g][c:c + 3], axis=0)
            s = _dot_nt(qm, kcat) + bias_ref[variant, hd]
            sk = sink_ref[hd] * LOG2E
            m = jnp.maximum(jnp.max(s, axis=-1, keepdims=True), sk)
            probs.append(jnp.exp2(s - m).astype(BF16))
            sink_terms.append(jnp.exp2(sk - m))

        outs = []
        for j in range(A_HEADS // 2):
            g = (2 * j) // (A_HEADS // A_KV_HEADS)
            v_e = jnp.concatenate(v_even[g][c:c + 3], axis=0)
            v_o = jnp.concatenate(v_odd[g][c:c + 3], axis=0)
            o = (_dot(probs[2 * j], jnp.concatenate([v_e, ones_lo], axis=1))
                 + _dot(probs[2 * j + 1], jnp.concatenate([v_o, ones_hi], axis=1)))
            denom = o[:, LANES:] + jnp.where(lo, sink_terms[2 * j], sink_terms[2 * j + 1])
            outs.append(o[:, :LANES] / denom)
        ya = jnp.concatenate(outs, axis=1)
        o_ref[rows, :] = _rms(ya, g_ref[...]).astype(BF16)


def _window_attn(qa, ka, va, rel_bias, sink, g_out_a, B, S):
    nb = S // BLK
    n_steps = nb // WIN_QBLOCKS
    assert nb >= 2, "first and last query blocks use distinct edge masks"
    bmap = jnp.asarray(_bucket_map())
    smem = pl.BlockSpec(memory_space=pltpu.SMEM)
    cur = lambda n: pl.BlockSpec((WIN_QBLOCKS * BLK, n), lambda b, i: (b * n_steps + i, 0))
    prv = lambda n: pl.BlockSpec((BLK, n), lambda b, i: (b * nb + jnp.maximum(i * WIN_QBLOCKS - 1, 0), 0))
    nxt = lambda n: pl.BlockSpec((BLK, n), lambda b, i: (b * nb + jnp.minimum((i + 1) * WIN_QBLOCKS, nb - 1), 0))
    return pl.pallas_call(
        functools.partial(_window_attn_kernel, n_steps),
        grid=(B, n_steps),
        in_specs=[smem, smem,
                  pl.BlockSpec((BLK, 3 * BLK), lambda b, i: (0, 0)),
                  pl.BlockSpec((1, 512), lambda b, i: (0, 0)),
                  cur(512), prv(LANES), cur(LANES), nxt(LANES), prv(LANES), cur(LANES), nxt(LANES)],
        out_specs=cur(512),
        out_shape=jax.ShapeDtypeStruct((B * S, 512), BF16),
        scratch_shapes=[pltpu.VMEM((3, A_HEADS, BLK, 3 * BLK), F32)],
        compiler_params=_cparams(2),
        name="window_attn",
    )(rel_bias, sink, bmap, g_out_a, qa, ka, ka, ka, va, va, va)


def _mla_attn_kernel(n_cast, g_ref, q_ref, k_ref, v_ref, *rest):
    o_ref = rest[n_cast]
    for src, dst in zip(rest[:n_cast], rest[n_cast + 1:]):
        dst[...] = src[...].astype(BF16)
    S = k_ref.shape[0]
    lane = lax.broadcasted_iota(jnp.int32, (1, LANES), 1)
    lo = lane < B_V_DIM
    ones_lo = jnp.broadcast_to(jnp.where(lo, 1.0, 0.0).astype(BF16), (S, LANES))
    ones_hi = jnp.broadcast_to(jnp.where(lo, 0.0, 1.0).astype(BF16), (S, LANES))
    v_pairs = []
    for j in range(B_HEADS // 2):
        vt = v_ref[:, j * LANES:(j + 1) * LANES]
        zero = jnp.zeros_like(vt)
        v_pairs.append((jnp.concatenate([jnp.where(lo, vt, zero), ones_lo], axis=1),
                        jnp.concatenate([jnp.where(lo, zero, vt), ones_hi], axis=1)))
    for r in range(q_ref.shape[0] // MLA_ROWS):
        rows = slice(r * MLA_ROWS, (r + 1) * MLA_ROWS)
        outs = []
        for j in range(B_HEADS // 2):
            ps = []
            for hd in (2 * j, 2 * j + 1):
                sl = slice(hd * LANES, (hd + 1) * LANES)
                s = _dot_nt(q_ref[rows, sl], k_ref[:, sl])
                m = jnp.max(s, axis=-1, keepdims=True)
                ps.append(jnp.exp2(s - m).astype(BF16))
            o = _dot(ps[0], v_pairs[j][0]) + _dot(ps[1], v_pairs[j][1])
            outs.append(o[:, :LANES] / o[:, LANES:])
        yb = jnp.concatenate(outs, axis=1)
        o_ref[rows, :] = _rms(yb, g_ref[...]).astype(BF16)


def _mla_attn(qm, km, vm, g_out_b, B, S, cast_weights):
    nq = S // TQ
    steps = B * nq
    qspec = lambda n: pl.BlockSpec((TQ, n), lambda b, i: (b * nq + i, 0))
    kspec = lambda n: pl.BlockSpec((S, n), lambda b, i: (b, 0))
    flat = [w.reshape(-1, w.shape[-1]) for w in cast_weights]
    slab = lambda w: pl.BlockSpec((w.shape[0] // steps, w.shape[1]), lambda b, i: (b * nq + i, 0))
    outs = pl.pallas_call(
        functools.partial(_mla_attn_kernel, len(flat)),
        grid=(B, nq),
        in_specs=[pl.BlockSpec((1, 512), lambda b, i: (0, 0)), qspec(1024), kspec(1024), kspec(512)]
                 + [slab(w) for w in flat],
        out_specs=[qspec(512)] + [slab(w) for w in flat],
        out_shape=[jax.ShapeDtypeStruct((B * S, 512), BF16)]
                  + [jax.ShapeDtypeStruct(w.shape, BF16) for w in flat],
        compiler_params=_cparams(2),
        name="mla_attn",
    )(g_out_b, qm, km, vm, *flat)
    return outs[0], [o.reshape(w.shape) for o, w in zip(outs[1:], cast_weights)]


def _split_bf16(v):
    hi = v.astype(BF16)
    return hi, (v - hi.astype(F32)).astype(BF16)


def _out_proj_kernel(x_ref, ya_ref, yb_ref, wo_ref, g_ref, wr_ref, x1_ref, h2_ref, aff_ref):
    x1 = x_ref[...] + _dot(ya_ref[...], wo_ref[0:512, :]) + _dot(yb_ref[...], wo_ref[512:1024, :])
    x1_ref[...] = x1
    h2 = _rms(x1, g_ref[...])
    h2_ref[...] = h2.astype(BF16).astype(F32)
    h_hi, h_lo = _split_bf16(h2)
    w_hi, w_lo = _split_bf16(wr_ref[...])
    logits = _dot_nt(w_hi, h_hi) + (_dot_nt(w_hi, h_lo) + _dot_nt(w_lo, h_hi))
    m = jnp.max(logits, axis=0, keepdims=True)
    e = jnp.exp(logits - m)
    aff_ref[0] = e / jnp.sum(e, axis=0, keepdims=True)


def _out_proj(xf, ya, yb, w_out, g_ffn, w_router_t, B, S):
    T = xf.shape[0]
    per_b = S // TM
    row = lambda n: pl.BlockSpec((TM, n), lambda i: (i, 0))
    full = lambda a: pl.BlockSpec(a.shape, lambda i: (0, 0))
    return pl.pallas_call(
        _out_proj_kernel,
        grid=(T // TM,),
        in_specs=[row(D_MODEL), row(512), row(512), full(w_out), full(g_ffn), full(w_router_t)],
        out_specs=[row(D_MODEL), row(D_MODEL),
                   pl.BlockSpec((1, N_EXPERTS, TM), lambda i: (i // per_b, 0, i % per_b))],
        out_shape=[jax.ShapeDtypeStruct((T, D_MODEL), F32), jax.ShapeDtypeStruct((T, D_MODEL), F32),
                   jax.ShapeDtypeStruct((B, N_EXPERTS, S), F32)],
        compiler_params=_cparams(1),
        name="out_proj",
    )(xf, ya, yb, w_out, g_ffn, w_router_t)


CHUNK = 256
ROUTE_LOG_STEPS = 12
ROUTE_LIN_STEPS = 32


def _prefix_count(flags_f32, tri):
    S = flags_f32.shape[1]
    carry = jnp.zeros((flags_f32.shape[0], 1), F32)
    parts = []
    for c in range(S // CHUNK):
        blk = flags_f32[:, c * CHUNK:(c + 1) * CHUNK]
        parts.append(_dot(blk.astype(BF16), tri) + carry)
        carry = carry + jnp.sum(blk, axis=-1, keepdims=True)
    return jnp.concatenate(parts, axis=1)


def _route_kernel(cap, aff_ref, slot_ref):
    aff = aff_ref[...]
    rows = aff.shape[0]

    def enough(pivot):
        return jnp.sum(jnp.where(aff >= pivot, 1.0, 0.0), axis=-1, keepdims=True) >= cap

    def log_body(_, st):
        lo, hi, elo, ehi = st
        mid = 0.5 * (elo + ehi)
        pivot = jnp.exp2(mid)
        ok = enough(pivot)
        return (jnp.where(ok, pivot, lo), jnp.where(ok, hi, pivot), jnp.where(ok, mid, elo), jnp.where(ok, ehi, mid))

    def lin_body(_, st):
        lo, hi = st
        pivot = 0.5 * (lo + hi)
        ok = enough(pivot)
        return jnp.where(ok, pivot, lo), jnp.where(ok, hi, pivot)

    col = lambda v: jnp.full((rows, 1), v, F32)
    lo, hi, _, _ = lax.fori_loop(0, ROUTE_LOG_STEPS, log_body, (col(0.0), col(2.0), col(-152.0), col(1.0)))
    lo, hi = lax.fori_loop(0, ROUTE_LIN_STEPS, lin_body, (lo, hi))

    gt = jnp.where(aff >= hi, 1.0, 0.0)
    eq = jnp.where(aff >= lo, 1.0, 0.0) - gt
    need = cap - jnp.sum(gt, axis=-1, keepdims=True)
    r = lax.broadcasted_iota(jnp.int32, (CHUNK, CHUNK), 0)
    c = lax.broadcasted_iota(jnp.int32, (CHUNK, CHUNK), 1)
    tri = jnp.where(r < c, 1.0, 0.0).astype(BF16)
    sel = gt + eq * jnp.where(_prefix_count(eq, tri) < need, 1.0, 0.0)
    slot = _prefix_count(sel, tri)
    slot_ref[...] = jnp.where(sel > 0.5, slot, -1.0)


def _route(aff, cap):
    B, E, S = aff.shape
    spec = pl.BlockSpec((B * E, S), lambda i: (0, 0))
    return pl.pallas_call(
        functools.partial(_route_kernel, cap),
        grid=(1,),
        in_specs=[spec],
        out_specs=spec,
        out_shape=jax.ShapeDtypeStruct((B * E, S), F32),
        compiler_params=_cparams(1),
        name="route",
    )(aff.reshape(B * E, S))


def _moe_kernel(cap, slot_a_ref, aff_a_ref, slot_b_ref, aff_b_ref, next_a_ref, next_b_ref, h_ref,
                wg_ref, wu_ref, wd_ref, o_ref, pa_ref, pb_ref, va_ref, vb_ref, xa_ref, xb_ref):
    S = h_ref.shape[0]
    slot_ids = lax.broadcasted_iota(jnp.int32, (cap, 1), 0).astype(F32)
    tok_ids = lax.broadcasted_iota(jnp.int32, (1, S), 1).astype(F32)

    def build(slot_row, aff_row, p_ref, v_ref):
        sel = slot_ids == slot_row
        p_ref[...] = jnp.where(sel, 1.0, 0.0).astype(BF16)
        v_ref[...] = jnp.sum(jnp.where(sel, aff_row, 0.0), axis=-1, keepdims=True)

    def gather(slot_row, x_ref, done):
        sel = slot_ids == slot_row
        tok = jnp.sum(jnp.where(sel, tok_ids, 0.0), axis=-1, keepdims=True).astype(jnp.int32)
        tok = jnp.clip(tok, 0, S - 1)
        for grp in range(cap // GATHER_GROUP):
            ids = tok[grp * GATHER_GROUP:(grp + 1) * GATHER_GROUP, :]
            if len(done) >= 2:
                ids = jnp.minimum(ids, done[-2] + S)
            for j in range(GATHER_GROUP):
                t = ids[j, 0]
                x_ref[pl.ds(grp * GATHER_GROUP + j, 1), :] = h_ref[pl.ds(t, 1), :]
            done.append(t)

    def gate_up(x_ref, i):
        xg = x_ref[...].astype(BF16)
        g = _dot(xg, wg_ref[i])
        u = _dot(xg, wu_ref[i])
        return (g * (1.0 / (1.0 + jnp.exp(-g))) * u).astype(BF16)

    def down_scatter(act, p_ref, v_ref, i):
        o = (_dot(act, wd_ref[i]) * v_ref[...]).astype(BF16)
        return _dot_tn(p_ref[...], o)

    @pl.when((pl.program_id(0) == 0) & (pl.program_id(1) == 0))
    def _first_step():
        moved = []
        gather(slot_a_ref[0], xa_ref, moved)
        gather(slot_b_ref[0], xb_ref, moved)

    @pl.when(pl.program_id(1) == 0)
    def _first_of_sequence():
        o_ref[...] = jnp.zeros_like(o_ref)

    build(slot_a_ref[0], aff_a_ref[0], pa_ref, va_ref)
    build(slot_b_ref[0], aff_b_ref[0], pb_ref, vb_ref)
    moved = []
    act_a = gate_up(xa_ref, 0)
    gather(next_a_ref[0], xa_ref, moved)
    act_b = gate_up(xb_ref, 1)
    gather(next_b_ref[0], xb_ref, moved)
    o_ref[...] += down_scatter(act_a, pa_ref, va_ref, 0)
    o_ref[...] += down_scatter(act_b, pb_ref, vb_ref, 1)


def _moe(slot, aff, h2, wg, wu, wd, cap):
    B, E, S = aff.shape
    T = h2.shape[0]
    assert E % 2 == 0
    n_steps = B * E // 2
    last = B * E - 1
    slot3 = slot.reshape(B * E, 1, S)
    aff3 = aff.reshape(B * E, 1, S)
    row = lambda d: pl.BlockSpec((1, 1, S), lambda b, k: (jnp.minimum(b * E + 2 * k + d, last), 0, 0))
    cur_a, cur_b, nxt_a, nxt_b = row(0), row(1), row(2), row(3)
    tok = pl.BlockSpec((S, D_MODEL), lambda b, k: (b, 0))
    ahead = pl.BlockSpec((S, D_MODEL),
                         lambda b, k: (jnp.minimum(b * (E // 2) + k + 1, n_steps - 1) // (E // 2), 0))
    wspec = lambda w: pl.BlockSpec((2,) + w.shape[1:], lambda b, k: (k, 0, 0))
    return pl.pallas_call(
        functools.partial(_moe_kernel, cap),
        grid=(B, E // 2),
        in_specs=[cur_a, cur_a, cur_b, cur_b, nxt_a, nxt_b, ahead, wspec(wg), wspec(wu), wspec(wd)],
        out_specs=tok,
        out_shape=jax.ShapeDtypeStruct((T, D_MODEL), F32),
        scratch_shapes=[pltpu.VMEM((cap, S), BF16), pltpu.VMEM((cap, S), BF16),
                        pltpu.VMEM((cap, 1), F32), pltpu.VMEM((cap, 1), F32),
                        pltpu.VMEM((cap, D_MODEL), F32), pltpu.VMEM((cap, D_MODEL), F32)],
        compiler_params=_cparams(2),
        name="moe",
    )(slot3, aff3, slot3, aff3, slot3, slot3, h2, wg, wu, wd)


def _ple_final_kernel(x1_ref, moe_ref, p_ref, gp_ref, wg_ref, wp_ref, gf_ref, o_ref):
    x2 = x1_ref[...] + moe_ref[...]
    z = _dot(_rms(x2, gp_ref[...]).astype(BF16), wg_ref[...])
    gate = 1.0 / (1.0 + jnp.exp(-z))
    x3 = x2 + gate * _dot(p_ref[...].astype(BF16), wp_ref[...])
    o_ref[...] = _rms(x3, gf_ref[...])


def _ple_final(x1, moe, pf, g_ple, w_gate, w_proj, g_final):
    T = x1.shape[0]
    row = lambda n: pl.BlockSpec((TM, n), lambda i: (i, 0))
    full = lambda a: pl.BlockSpec(a.shape, lambda i: (0, 0))
    return pl.pallas_call(
        _ple_final_kernel,
        grid=(T // TM,),
        in_specs=[row(D_MODEL), row(D_MODEL), row(PLE_DIM), full(g_ple), full(w_gate), full(w_proj), full(g_final)],
        out_specs=row(D_MODEL),
        out_shape=jax.ShapeDtypeStruct((T, D_MODEL), F32),
        compiler_params=_cparams(1),
        name="ple_final",
    )(x1, moe, pf, g_ple, w_gate, w_proj, g_final)


def _prep_in_proj_weight(w_in):
    D = w_in.shape[0]
    t1, t2 = w_in[:, 1152:1168], w_in[:, 1168:1184]
    cols = [w_in[:, 0:1152], t1, t2, t2, t1, jnp.zeros((D, 64), F32)]
    return jnp.concatenate(cols, axis=1).astype(BF16)


def _prep_mla_weights(w_uq, w_ukv):
    per_q = B_NOPE_DIM + B_ROPE_DIM
    w3 = w_uq.reshape(Q_RANK, B_HEADS, per_q)
    pad = LANES - per_q
    wq = jnp.pad(w3, ((0, 0), (0, 0), (0, pad))).reshape(Q_RANK, B_HEADS * LANES)
    t1 = w3[:, :, B_NOPE_DIM:B_NOPE_DIM + HALF_ROPE]
    t2 = w3[:, :, B_NOPE_DIM + HALF_ROPE:]
    wqr = jnp.concatenate([jnp.zeros((Q_RANK, B_HEADS, B_NOPE_DIM), F32), t2, t1,
                           jnp.zeros((Q_RANK, B_HEADS, pad), F32)], axis=2).reshape(Q_RANK, B_HEADS * LANES)
    w4 = w_ukv.reshape(KV_RANK, B_HEADS, B_NOPE_DIM + B_V_DIM)
    zk = jnp.zeros((KV_RANK, B_HEADS, LANES - B_NOPE_DIM), F32)
    wk = jnp.concatenate([w4[:, :, :B_NOPE_DIM], zk], axis=2).reshape(KV_RANK, B_HEADS * LANES)
    wv = w4[:, :, B_NOPE_DIM:].reshape(KV_RANK, B_HEADS * B_V_DIM)
    wkv = jnp.concatenate([wk, wv], axis=1)
    return wq.astype(BF16), wqr.astype(BF16), wkv.astype(BF16)


def kernel(x, p, positions, rel_bias, norm_mix_g, w_in, sink, g_cq, g_ckv, w_uq, w_ukv, g_out_a, g_out_b, w_out,
           norm_ffn_g, w_router, w_e_gate, w_e_up, w_e_down, norm_ple_g, w_ple_gate, w_ple_proj, final_norm_g):
    B, S, D = x.shape
    T = B * S
    cap = CAPACITY_FACTOR * S // N_EXPERTS
    xf = x.reshape(T, D)
    cos_d, sin_d = _rope_tables(positions)
    assert w_in.shape[0] == 1, "single-layer block: the final norm is fused into the last kernel"
    i = 0
    w1 = _prep_in_proj_weight(w_in[i])
    wq, wqr, wkv = _prep_mla_weights(w_uq[i], w_ukv[i])
    qa, ka, va, qm, km, vm = _in_proj(
        xf, norm_mix_g[i].reshape(1, D), w1, g_cq[i].reshape(1, -1), g_ckv[i].reshape(1, -1),
        wq, wqr, wkv, cos_d, sin_d)
    ya = _window_attn(qa, ka, va, rel_bias, sink[i], g_out_a[i].reshape(1, -1), B, S)
    yb, (wg, wu, wd) = _mla_attn(qm, km, vm, g_out_b[i].reshape(1, -1), B, S,
                                 (w_e_gate[i], w_e_up[i], w_e_down[i]))
    x1, h2, aff = _out_proj(xf, ya, yb, w_out[i].astype(BF16), norm_ffn_g[i].reshape(1, D),
                            w_router[i].T, B, S)
    slot = _route(aff, cap)
    moe = _moe(slot, aff, h2, wg, wu, wd, cap)
    out = _ple_final(x1, moe, p[i].reshape(T, -1), norm_ple_g[i].reshape(1, D),
                     w_ple_gate[i].astype(BF16), w_ple_proj[i].astype(BF16), final_norm_g.reshape(1, D))
    return out.reshape(B, S, D)
```

```python
import functools

import numpy as np
import jax
import jax.numpy as jnp
from jax import lax
from jax.experimental import pallas as pl
from jax.experimental.pallas import tpu as pltpu

F32 = jnp.float32
BF16 = jnp.bfloat16

D_MODEL = 1024
BLK = 128
WINDOW = 128
A_HEADS = 8
A_KV_HEADS = 2
A_HEAD_DIM = 64
B_HEADS = 8
B_NOPE_DIM = 64
B_ROPE_DIM = 32
B_V_DIM = 64
Q_RANK = 256
KV_RANK = 128
ROPE_THETA = 10000.0
NUM_BUCKETS = 32
MAX_DISTANCE = 128
N_EXPERTS = 16
EXPERT_FF = 512
CAPACITY_FACTOR = 2
PLE_DIM = 256
EPS = 1e-6

LANES = 128
HALF_ROPE = B_ROPE_DIM // 2
MASKED = -1e30
LOG2E = 1.4426950408889634
VMEM_LIMIT = 56 * 1024 * 1024
TM = 1024
TQ = 512
MLA_ROWS = 128
WIN_QBLOCKS = 8
GATHER_GROUP = 8
SCATTER_GROUP = 8


def _cparams(n_axes):
    return pltpu.CompilerParams(dimension_semantics=("arbitrary",) * n_axes,
                                vmem_limit_bytes=VMEM_LIMIT)


def _rms(x, g):
    return x * lax.rsqrt(jnp.mean(x * x, axis=-1, keepdims=True) + EPS) * g


def _dot(a, b):
    return jnp.dot(a, b, preferred_element_type=F32)


def _dot_nt(a, b):
    return lax.dot_general(a, b, (((1,), (1,)), ((), ())), preferred_element_type=F32)


def _dot_tn(a, b):
    return lax.dot_general(a, b, (((0,), (0,)), ((), ())), preferred_element_type=F32)


ROPE_GROUPS = LANES // HALF_ROPE


def _rope_table_kernel(pos_ref, invf_ref, cos_ref, sin_ref):
    ang = pos_ref[...] * invf_ref[...]
    cos_ref[...] = jnp.cos(ang)
    sin_ref[...] = jnp.sin(ang)


def _rope_tables(positions):
    T = positions.size
    rows = T // ROPE_GROUPS
    inv_freq = 1.0 / (ROPE_THETA ** (jnp.arange(0, B_ROPE_DIM, 2, dtype=F32) / B_ROPE_DIM))
    pos = jnp.repeat(positions.astype(F32).reshape(ROPE_GROUPS, rows).T, HALF_ROPE, axis=1)
    invf = jnp.tile(inv_freq, ROPE_GROUPS).reshape(1, LANES)
    spec = pl.BlockSpec((rows, LANES), lambda i: (0, 0))
    return pl.pallas_call(
        _rope_table_kernel,
        grid=(1,),
        in_specs=[spec, pl.BlockSpec((1, LANES), lambda i: (0, 0))],
        out_specs=[spec] * 2,
        out_shape=[jax.ShapeDtypeStruct((rows, LANES), F32)] * 2,
        compiler_params=_cparams(1),
        name="rope_tables",
    )(pos, invf)


def _expand_rope(cd, sd, group):
    lane = lax.broadcasted_iota(jnp.int32, (1, LANES), 1)
    own = (lane >= group * HALF_ROPE) & (lane < (group + 1) * HALF_ROPE)
    xc = jnp.where(own, cd, 0.0)
    xs = jnp.where(own, sd, 0.0)
    shift = HALF_ROPE
    while shift < LANES:
        xc = xc + pltpu.roll(xc, shift, axis=1)
        xs = xs + pltpu.roll(xs, shift, axis=1)
        shift *= 2
    in_t1 = (lane >= B_NOPE_DIM) & (lane < B_NOPE_DIM + HALF_ROPE)
    in_t2 = (lane >= B_NOPE_DIM + HALF_ROPE) & (lane < B_NOPE_DIM + B_ROPE_DIM)
    c = jnp.where(lane < B_NOPE_DIM, 1.0, jnp.where(in_t1 | in_t2, xc, 0.0))
    s = jnp.where(in_t1, -xs, jnp.where(in_t2, xs, 0.0))
    return c, s


C_QA = 0
C_KA = C_QA + 512
C_VA = C_KA + LANES
C_CQ = C_VA + LANES
C_CKV = C_CQ + Q_RANK
C_KR = C_CKV + KV_RANK
C_END = C_KR + LANES
MLA_K_COLS = B_HEADS * LANES


def _in_proj_kernel(tiles_per_group, x_ref, g_ref, w1_ref, gcq_ref, gckv_ref, wq_ref, wqr_ref, wkv_ref,
                    cos_ref, sin_ref, qa_ref, ka_ref, va_ref, qm_ref, km_ref, vm_ref):
    h = _rms(x_ref[...], g_ref[...]).astype(BF16)
    z = _dot(h, w1_ref[...])
    qa_ref[...] = (z[:, C_QA:C_KA] * (A_HEAD_DIM ** -0.5 * LOG2E)).astype(BF16)
    ka_ref[...] = z[:, C_KA:C_VA].astype(BF16)
    va_ref[...] = z[:, C_VA:C_CQ].astype(BF16)
    cqn = _rms(z[:, C_CQ:C_CKV], gcq_ref[...]).astype(BF16)
    ckvn = _rms(z[:, C_CKV:C_KR], gckv_ref[...]).astype(BF16)
    krt = z[:, C_KR:C_END]

    lane = lax.broadcasted_iota(jnp.int32, (1, LANES), 1)
    c_full, s_rope = _expand_rope(cos_ref[...], sin_ref[...], pl.program_id(0) // tiles_per_group)
    c_rope = jnp.where(lane < B_NOPE_DIM, 0.0, c_full)

    q1 = _dot(cqn, wq_ref[...])
    q2 = _dot(cqn, wqr_ref[...])
    scale = (B_NOPE_DIM + B_ROPE_DIM) ** -0.5 * LOG2E
    for hd in range(B_HEADS):
        sl = slice(hd * LANES, (hd + 1) * LANES)
        qm_ref[:, sl] = ((q1[:, sl] * c_full + q2[:, sl] * s_rope) * scale).astype(BF16)

    kv = _dot(ckvn, wkv_ref[...])
    kr_part = pltpu.roll(krt, 64, axis=1) * c_rope + pltpu.roll(krt, 32, axis=1) * s_rope
    for hd in range(B_HEADS):
        sl = slice(hd * LANES, (hd + 1) * LANES)
        km_ref[:, sl] = (kv[:, sl] + kr_part).astype(BF16)
    vm_ref[...] = kv[:, MLA_K_COLS:].astype(BF16)


def _in_proj(xf, g_mix, w1, g_cq, g_ckv, wq, wqr, wkv, cos_d, sin_d):
    T = xf.shape[0]
    tiles_per_group = cos_d.shape[0] // TM
    assert tiles_per_group * TM == cos_d.shape[0]
    row = lambda n: pl.BlockSpec((TM, n), lambda i: (i, 0))
    full = lambda a: pl.BlockSpec(a.shape, lambda i: (0, 0))
    dense = pl.BlockSpec((TM, LANES), lambda i: (i % tiles_per_group, 0))
    outs = [512, LANES, LANES, MLA_K_COLS, MLA_K_COLS, B_HEADS * B_V_DIM]
    return pl.pallas_call(
        functools.partial(_in_proj_kernel, tiles_per_group),
        grid=(T // TM,),
        in_specs=[row(D_MODEL), full(g_mix), full(w1), full(g_cq), full(g_ckv), full(wq), full(wqr), full(wkv),
                  dense, dense],
        out_specs=[row(n) for n in outs],
        out_shape=[jax.ShapeDtypeStruct((T, n), BF16) for n in outs],
        compiler_params=_cparams(1),
        name="in_proj",
    )(xf, g_mix, w1, g_cq, g_ckv, wq, wqr, wkv, cos_d, sin_d)


def _bucket_map():
    qi = np.arange(BLK)[:, None]
    kj = np.arange(3 * BLK)[None, :]
    rel = kj - BLK - qi
    n = np.abs(rel)
    half = NUM_BUCKETS // 2
    max_exact = half // 2
    thresholds = [int(np.ceil(max_exact * 2 ** (k / 2) - 1e-9)) for k in range(1, half - max_exact)]
    large = max_exact + sum((n >= t).astype(np.int64) for t in thresholds)
    large = np.minimum(large, half - 1)
    bucket = np.where(rel > 0, half, 0) + np.where(n < max_exact, n, large)
    return np.where(n <= WINDOW, bucket, -1).astype(np.int32)


def _window_attn_kernel(n_steps, relb_ref, sink_ref, bmap_ref, g_ref, q_ref, kp_ref, kc_ref, kn_ref,
                        vp_ref, vc_ref, vn_ref, o_ref, bias_ref):
    b = pl.program_id(0)
    i = pl.program_id(1)

    @pl.when((b == 0) & (i == 0))
    def _build_bias():
        bmap = bmap_ref[...]
        col = lax.broadcasted_iota(jnp.int32, (1, 3 * BLK), 1)
        for hd in range(A_HEADS):
            bias_ref[1, hd] = jnp.full((BLK, 3 * BLK), MASKED, F32)
        for bk in range(NUM_BUCKETS):
            m = bmap == bk
            for hd in range(A_HEADS):
                bias_ref[1, hd] = jnp.where(m, relb_ref[bk, hd] * LOG2E, bias_ref[1, hd])
        for hd in range(A_HEADS):
            bias_ref[0, hd] = jnp.where(col < BLK, MASKED, bias_ref[1, hd])
            bias_ref[2, hd] = jnp.where(col >= 2 * BLK, MASKED, bias_ref[1, hd])

    lane = lax.broadcasted_iota(jnp.int32, (1, LANES), 1)
    lo = lane < A_HEAD_DIM
    ones_lo = jnp.broadcast_to(jnp.where(lo, 1.0, 0.0).astype(BF16), (3 * BLK, LANES))
    ones_hi = jnp.broadcast_to(jnp.where(lo, 0.0, 1.0).astype(BF16), (3 * BLK, LANES))

    def key_blocks(p_ref, c_ref, n_ref):
        return [p_ref[...]] + [c_ref[c * BLK:(c + 1) * BLK, :] for c in range(WIN_QBLOCKS)] + [n_ref[...]]

    def lane_swap(t):
        return pltpu.bitcast(pltpu.roll(pltpu.bitcast(t, jnp.int32), LANES // 2, axis=1), BF16)

    kdup = [[], []]
    v_even = [[], []]
    v_odd = [[], []]
    for kt, vt in zip(key_blocks(kp_ref, kc_ref, kn_ref), key_blocks(vp_ref, vc_ref, vn_ref)):
        ks, vs, zero = lane_swap(kt), lane_swap(vt), jnp.zeros_like(vt)
        kdup[0].append(jnp.where(lo, kt, ks))
        kdup[1].append(jnp.where(lo, ks, kt))
        v_even[0].append(jnp.where(lo, vt, zero))
        v_odd[0].append(jnp.where(lo, zero, vs))
        v_even[1].append(jnp.where(lo, vs, zero))
        v_odd[1].append(jnp.where(lo, zero, vt))

    for c in range(WIN_QBLOCKS):
        rows = slice(c * BLK, (c + 1) * BLK)
        if c == 0:
            variant = jnp.where(i == 0, 0, 1)
        elif c == WIN_QBLOCKS - 1:
            variant = jnp.where(i == n_steps - 1, 2, 1)
        else:
            variant = 1
        probs, sink_terms = [], []
        for hd in range(A_HEADS):
            g = hd // (A_HEADS // A_KV_HEADS)
            qt = q_ref[rows, (hd // 2) * LANES:(hd // 2 + 1) * LANES]
            qm = jnp.where(lo if hd % 2 == 0 else ~lo, qt, jnp.zeros_like(qt))
            kcat = jnp.concatenate(kdup[g][c:c + 3], axis=0)
            s = _dot_nt(qm, kcat) + bias_ref[variant, hd]
            sk = sink_ref[hd] * LOG2E
            m = jnp.maximum(jnp.max(s, axis=-1, keepdims=True), sk)
            probs.append(jnp.exp2(s - m).astype(BF16))
            sink_terms.append(jnp.exp2(sk - m))

        outs = []
        for j in range(A_HEADS // 2):
            g = (2 * j) // (A_HEADS // A_KV_HEADS)
            v_e = jnp.concatenate(v_even[g][c:c + 3], axis=0)
            v_o = jnp.concatenate(v_odd[g][c:c + 3], axis=0)
            o = (_dot(probs[2 * j], jnp.concatenate([v_e, ones_lo], axis=1))
                 + _dot(probs[2 * j + 1], jnp.concatenate([v_o, ones_hi], axis=1)))
            denom = o[:, LANES:] + jnp.where(lo, sink_terms[2 * j], sink_terms[2 * j + 1])
            outs.append(o[:, :LANES] / denom)
        ya = jnp.concatenate(outs, axis=1)
        o_ref[rows, :] = _rms(ya, g_ref[...]).astype(BF16)


def _window_attn(qa, ka, va, rel_bias, sink, g_out_a, B, S):
    nb = S // BLK
    n_steps = nb // WIN_QBLOCKS
    assert nb >= 2, "first and last query blocks use distinct edge masks"
    bmap = jnp.asarray(_bucket_map())
    smem = pl.BlockSpec(memory_space=pltpu.SMEM)
    cur = lambda n: pl.BlockSpec((WIN_QBLOCKS * BLK, n), lambda b, i: (b * n_steps + i, 0))
    prv = lambda n: pl.BlockSpec((BLK, n), lambda b, i: (b * nb + jnp.maximum(i * WIN_QBLOCKS - 1, 0), 0))
    nxt = lambda n: pl.BlockSpec((BLK, n), lambda b, i: (b * nb + jnp.minimum((i + 1) * WIN_QBLOCKS, nb - 1), 0))
    return pl.pallas_call(
        functools.partial(_window_attn_kernel, n_steps),
        grid=(B, n_steps),
        in_specs=[smem, smem,
                  pl.BlockSpec((BLK, 3 * BLK), lambda b, i: (0, 0)),
                  pl.BlockSpec((1, 512), lambda b, i: (0, 0)),
                  cur(512), prv(LANES), cur(LANES), nxt(LANES), prv(LANES), cur(LANES), nxt(LANES)],
        out_specs=cur(512),
        out_shape=jax.ShapeDtypeStruct((B * S, 512), BF16),
        scratch_shapes=[pltpu.VMEM((3, A_HEADS, BLK, 3 * BLK), F32)],
        compiler_params=_cparams(2),
        name="window_attn",
    )(rel_bias, sink, bmap, g_out_a, qa, ka, ka, ka, va, va, va)


def _mla_attn_kernel(n_cast, g_ref, q_ref, k_ref, v_ref, *rest):
    o_ref = rest[n_cast]
    for src, dst in zip(rest[:n_cast], rest[n_cast + 1:]):
        dst[...] = src[...].astype(BF16)
    S = k_ref.shape[0]
    lane = lax.broadcasted_iota(jnp.int32, (1, LANES), 1)
    lo = lane < B_V_DIM
    ones_lo = jnp.broadcast_to(jnp.where(lo, 1.0, 0.0).astype(BF16), (S, LANES))
    ones_hi = jnp.broadcast_to(jnp.where(lo, 0.0, 1.0).astype(BF16), (S, LANES))
    v_pairs = []
    for j in range(B_HEADS // 2):
        vt = v_ref[:, j * LANES:(j + 1) * LANES]
        zero = jnp.zeros_like(vt)
        v_pairs.append((jnp.concatenate([jnp.where(lo, vt, zero), ones_lo], axis=1),
                        jnp.concatenate([jnp.where(lo, zero, vt), ones_hi], axis=1)))
    for r in range(q_ref.shape[0] // MLA_ROWS):
        rows = slice(r * MLA_ROWS, (r + 1) * MLA_ROWS)
        outs = []
        for j in range(B_HEADS // 2):
            ps = []
            for hd in (2 * j, 2 * j + 1):
                sl = slice(hd * LANES, (hd + 1) * LANES)
                s = _dot_nt(q_ref[rows, sl], k_ref[:, sl])
                m = jnp.max(s, axis=-1, keepdims=True)
                ps.append(jnp.exp2(s - m).astype(BF16))
            o = _dot(ps[0], v_pairs[j][0]) + _dot(ps[1], v_pairs[j][1])
            outs.append(o[:, :LANES] / o[:, LANES:])
        yb = jnp.concatenate(outs, axis=1)
        o_ref[rows, :] = _rms(yb, g_ref[...]).astype(BF16)


def _mla_attn(qm, km, vm, g_out_b, B, S, cast_weights):
    nq = S // TQ
    steps = B * nq
    qspec = lambda n: pl.BlockSpec((TQ, n), lambda b, i: (b * nq + i, 0))
    kspec = lambda n: pl.BlockSpec((S, n), lambda b, i: (b, 0))
    flat = [w.reshape(-1, w.shape[-1]) for w in cast_weights]
    slab = lambda w: pl.BlockSpec((w.shape[0] // steps, w.shape[1]), lambda b, i: (b * nq + i, 0))
    outs = pl.pallas_call(
        functools.partial(_mla_attn_kernel, len(flat)),
        grid=(B, nq),
        in_specs=[pl.BlockSpec((1, 512), lambda b, i: (0, 0)), qspec(1024), kspec(1024), kspec(512)]
                 + [slab(w) for w in flat],
        out_specs=[qspec(512)] + [slab(w) for w in flat],
        out_shape=[jax.ShapeDtypeStruct((B * S, 512), BF16)]
                  + [jax.ShapeDtypeStruct(w.shape, BF16) for w in flat],
        compiler_params=_cparams(2),
        name="mla_attn",
    )(g_out_b, qm, km, vm, *flat)
    return outs[0], [o.reshape(w.shape) for o, w in zip(outs[1:], cast_weights)]


def _split_bf16(v):
    hi = v.astype(BF16)
    return hi, (v - hi.astype(F32)).astype(BF16)


def _out_proj_kernel(x_ref, ya_ref, yb_ref, wo_ref, g_ref, wr_ref, x1_ref, h2_ref, aff_ref):
    x1 = x_ref[...] + _dot(ya_ref[...], wo_ref[0:512, :]) + _dot(yb_ref[...], wo_ref[512:1024, :])
    x1_ref[...] = x1
    h2 = _rms(x1, g_ref[...])
    h2_ref[...] = h2.astype(BF16).astype(F32)
    h_hi, h_lo = _split_bf16(h2)
    w_hi, w_lo = _split_bf16(wr_ref[...])
    logits = _dot_nt(w_hi, h_hi) + (_dot_nt(w_hi, h_lo) + _dot_nt(w_lo, h_hi))
    m = jnp.max(logits, axis=0, keepdims=True)
    e = jnp.exp(logits - m)
    aff_ref[0] = e / jnp.sum(e, axis=0, keepdims=True)


def _out_proj(xf, ya, yb, w_out, g_ffn, w_router_t, B, S):
    T = xf.shape[0]
    per_b = S // TM
    row = lambda n: pl.BlockSpec((TM, n), lambda i: (i, 0))
    full = lambda a: pl.BlockSpec(a.shape, lambda i: (0, 0))
    return pl.pallas_call(
        _out_proj_kernel,
        grid=(T // TM,),
        in_specs=[row(D_MODEL), row(512), row(512), full(w_out), full(g_ffn), full(w_router_t)],
        out_specs=[row(D_MODEL), row(D_MODEL),
                   pl.BlockSpec((1, N_EXPERTS, TM), lambda i: (i // per_b, 0, i % per_b))],
        out_shape=[jax.ShapeDtypeStruct((T, D_MODEL), F32), jax.ShapeDtypeStruct((T, D_MODEL), F32),
                   jax.ShapeDtypeStruct((B, N_EXPERTS, S), F32)],
        compiler_params=_cparams(1),
        name="out_proj",
    )(xf, ya, yb, w_out, g_ffn, w_router_t)


CHUNK = 256
ROUTE_LOG_STEPS = 12
ROUTE_LIN_STEPS = 32


def _prefix_count(flags_f32, tri):
    S = flags_f32.shape[1]
    carry = jnp.zeros((flags_f32.shape[0], 1), F32)
    parts = []
    for c in range(S // CHUNK):
        blk = flags_f32[:, c * CHUNK:(c + 1) * CHUNK]
        parts.append(_dot(blk.astype(BF16), tri) + carry)
        carry = carry + jnp.sum(blk, axis=-1, keepdims=True)
    return jnp.concatenate(parts, axis=1)


def _route_kernel(cap, aff_ref, slot_ref):
    aff = aff_ref[...]
    rows = aff.shape[0]

    def enough(pivot):
        return jnp.sum(jnp.where(aff >= pivot, 1.0, 0.0), axis=-1, keepdims=True) >= cap

    def log_body(_, st):
        lo, hi, elo, ehi = st
        mid = 0.5 * (elo + ehi)
        pivot = jnp.exp2(mid)
        ok = enough(pivot)
        return (jnp.where(ok, pivot, lo), jnp.where(ok, hi, pivot), jnp.where(ok, mid, elo), jnp.where(ok, ehi, mid))

    def lin_body(_, st):
        lo, hi = st
        pivot = 0.5 * (lo + hi)
        ok = enough(pivot)
        return jnp.where(ok, pivot, lo), jnp.where(ok, hi, pivot)

    col = lambda v: jnp.full((rows, 1), v, F32)
    lo, hi, _, _ = lax.fori_loop(0, ROUTE_LOG_STEPS, log_body, (col(0.0), col(2.0), col(-152.0), col(1.0)))
    lo, hi = lax.fori_loop(0, ROUTE_LIN_STEPS, lin_body, (lo, hi))

    gt = jnp.where(aff >= hi, 1.0, 0.0)
    eq = jnp.where(aff >= lo, 1.0, 0.0) - gt
    need = cap - jnp.sum(gt, axis=-1, keepdims=True)
    r = lax.broadcasted_iota(jnp.int32, (CHUNK, CHUNK), 0)
    c = lax.broadcasted_iota(jnp.int32, (CHUNK, CHUNK), 1)
    tri = jnp.where(r < c, 1.0, 0.0).astype(BF16)
    sel = gt + eq * jnp.where(_prefix_count(eq, tri) < need, 1.0, 0.0)
    slot = _prefix_count(sel, tri)
    slot_ref[...] = jnp.where(sel > 0.5, slot, -1.0)


def _route(aff, cap):
    B, E, S = aff.shape
    spec = pl.BlockSpec((B * E, S), lambda i: (0, 0))
    return pl.pallas_call(
        functools.partial(_route_kernel, cap),
        grid=(1,),
        in_specs=[spec],
        out_specs=spec,
        out_shape=jax.ShapeDtypeStruct((B * E, S), F32),
        compiler_params=_cparams(1),
        name="route",
    )(aff.reshape(B * E, S))


def _moe_kernel(cap, slot_a_ref, aff_a_ref, slot_b_ref, aff_b_ref, next_a_ref, next_b_ref, h_ref,
                wg_ref, wu_ref, wd_ref, o_ref, xa_ref, xb_ref, ya_ref, yb_ref, tok_ref):
    S = h_ref.shape[0]
    parity = pl.program_id(1) % 2
    slot_ids = lax.broadcasted_iota(jnp.int32, (cap, 1), 0).astype(F32)
    tok_ids = lax.broadcasted_iota(jnp.int32, (1, S), 1).astype(F32)

    def slot_affinities(slot_row, aff_row):
        return jnp.sum(jnp.where(slot_ids == slot_row, aff_row, 0.0), axis=-1, keepdims=True)

    def gather(slot_row, x_ref, which, par, done):
        sel = slot_ids == slot_row
        tok = jnp.sum(jnp.where(sel, tok_ids, 0.0), axis=-1, keepdims=True).astype(jnp.int32)
        tok = jnp.clip(tok, 0, S - 1)
        for grp in range(cap // GATHER_GROUP):
            ids = tok[grp * GATHER_GROUP:(grp + 1) * GATHER_GROUP, :]
            if len(done) >= 2:
                ids = jnp.minimum(ids, done[-2] + S)
            for j in range(GATHER_GROUP):
                c = grp * GATHER_GROUP + j
                t = ids[j, 0]
                tok_ref[which, par, c] = t
                x_ref[pl.ds(c, 1), :] = h_ref[pl.ds(t, 1), :]
            done.append(t)

    def gate_up(x_ref, i):
        xg = x_ref[...].astype(BF16)
        g = _dot(xg, wg_ref[i])
        u = _dot(xg, wu_ref[i])
        return (g * (1.0 / (1.0 + jnp.exp(-g))) * u).astype(BF16)

    def scatter_add(y_ref, which):
        for c0 in range(0, cap, SCATTER_GROUP):
            toks = [tok_ref[which, parity, c0 + j] for j in range(SCATTER_GROUP)]
            rows = [o_ref[pl.ds(toks[j], 1), :] + y_ref[pl.ds(c0 + j, 1), :] for j in range(SCATTER_GROUP)]
            for j in range(SCATTER_GROUP):
                o_ref[pl.ds(toks[j], 1), :] = rows[j]

    @pl.when((pl.program_id(0) == 0) & (pl.program_id(1) == 0))
    def _first_step():
        moved = []
        gather(slot_a_ref[0], xa_ref, 0, 0, moved)
        gather(slot_b_ref[0], xb_ref, 1, 0, moved)

    @pl.when(pl.program_id(1) == 0)
    def _first_of_sequence():
        o_ref[...] = jnp.zeros_like(o_ref)

    moved = []
    act_a = gate_up(xa_ref, 0)
    gather(next_a_ref[0], xa_ref, 0, 1 - parity, moved)
    act_b = gate_up(xb_ref, 1)
    gather(next_b_ref[0], xb_ref, 1, 1 - parity, moved)
    ya_ref[...] = _dot(act_a, wd_ref[0]) * slot_affinities(slot_a_ref[0], aff_a_ref[0])
    yb_ref[...] = _dot(act_b, wd_ref[1]) * slot_affinities(slot_b_ref[0], aff_b_ref[0])
    scatter_add(ya_ref, 0)
    scatter_add(yb_ref, 1)


def _moe(slot, aff, h2, wg, wu, wd, cap):
    B, E, S = aff.shape
    T = h2.shape[0]
    assert E % 2 == 0
    n_steps = B * E // 2
    last = B * E - 1
    slot3 = slot.reshape(B * E, 1, S)
    aff3 = aff.reshape(B * E, 1, S)
    row = lambda d: pl.BlockSpec((1, 1, S), lambda b, k: (jnp.minimum(b * E + 2 * k + d, last), 0, 0))
    cur_a, cur_b, nxt_a, nxt_b = row(0), row(1), row(2), row(3)
    tok = pl.BlockSpec((S, D_MODEL), lambda b, k: (b, 0))
    ahead = pl.BlockSpec((S, D_MODEL),
                         lambda b, k: (jnp.minimum(b * (E // 2) + k + 1, n_steps - 1) // (E // 2), 0))
    wspec = lambda w: pl.BlockSpec((2,) + w.shape[1:], lambda b, k: (k, 0, 0))
    return pl.pallas_call(
        functools.partial(_moe_kernel, cap),
        grid=(B, E // 2),
        in_specs=[cur_a, cur_a, cur_b, cur_b, nxt_a, nxt_b, ahead, wspec(wg), wspec(wu), wspec(wd)],
        out_specs=tok,
        out_shape=jax.ShapeDtypeStruct((T, D_MODEL), F32),
        scratch_shapes=[pltpu.VMEM((cap, D_MODEL), F32), pltpu.VMEM((cap, D_MODEL), F32),
                        pltpu.VMEM((cap, D_MODEL), F32), pltpu.VMEM((cap, D_MODEL), F32),
                        pltpu.SMEM((2, 2, cap), jnp.int32)],
        compiler_params=_cparams(2),
        name="moe",
    )(slot3, aff3, slot3, aff3, slot3, slot3, h2, wg, wu, wd)


def _ple_final_kernel(x1_ref, moe_ref, p_ref, gp_ref, wg_ref, wp_ref, gf_ref, o_ref):
    x2 = x1_ref[...] + moe_ref[...]
    z = _dot(_rms(x2, gp_ref[...]).astype(BF16), wg_ref[...])
    gate = 1.0 / (1.0 + jnp.exp(-z))
    x3 = x2 + gate * _dot(p_ref[...].astype(BF16), wp_ref[...])
    o_ref[...] = _rms(x3, gf_ref[...])


def _ple_final(x1, moe, pf, g_ple, w_gate, w_proj, g_final):
    T = x1.shape[0]
    row = lambda n: pl.BlockSpec((TM, n), lambda i: (i, 0))
    full = lambda a: pl.BlockSpec(a.shape, lambda i: (0, 0))
    return pl.pallas_call(
        _ple_final_kernel,
        grid=(T // TM,),
        in_specs=[row(D_MODEL), row(D_MODEL), row(PLE_DIM), full(g_ple), full(w_gate), full(w_proj), full(g_final)],
        out_specs=row(D_MODEL),
        out_shape=jax.ShapeDtypeStruct((T, D_MODEL), F32),
        compiler_params=_cparams(1),
        name="ple_final",
    )(x1, moe, pf, g_ple, w_gate, w_proj, g_final)


def _prep_in_proj_weight(w_in):
    D = w_in.shape[0]
    t1, t2 = w_in[:, 1152:1168], w_in[:, 1168:1184]
    cols = [w_in[:, 0:1152], t1, t2, t2, t1, jnp.zeros((D, 64), F32)]
    return jnp.concatenate(cols, axis=1).astype(BF16)


def _prep_mla_weights(w_uq, w_ukv):
    per_q = B_NOPE_DIM + B_ROPE_DIM
    w3 = w_uq.reshape(Q_RANK, B_HEADS, per_q)
    pad = LANES - per_q
    wq = jnp.pad(w3, ((0, 0), (0, 0), (0, pad))).reshape(Q_RANK, B_HEADS * LANES)
    t1 = w3[:, :, B_NOPE_DIM:B_NOPE_DIM + HALF_ROPE]
    t2 = w3[:, :, B_NOPE_DIM + HALF_ROPE:]
    wqr = jnp.concatenate([jnp.zeros((Q_RANK, B_HEADS, B_NOPE_DIM), F32), t2, t1,
                           jnp.zeros((Q_RANK, B_HEADS, pad), F32)], axis=2).reshape(Q_RANK, B_HEADS * LANES)
    w4 = w_ukv.reshape(KV_RANK, B_HEADS, B_NOPE_DIM + B_V_DIM)
    zk = jnp.zeros((KV_RANK, B_HEADS, LANES - B_NOPE_DIM), F32)
    wk = jnp.concatenate([w4[:, :, :B_NOPE_DIM], zk], axis=2).reshape(KV_RANK, B_HEADS * LANES)
    wv = w4[:, :, B_NOPE_DIM:].reshape(KV_RANK, B_HEADS * B_V_DIM)
    wkv = jnp.concatenate([wk, wv], axis=1)
    return wq.astype(BF16), wqr.astype(BF16), wkv.astype(BF16)


def kernel(x, p, positions, rel_bias, norm_mix_g, w_in, sink, g_cq, g_ckv, w_uq, w_ukv, g_out_a, g_out_b, w_out,
           norm_ffn_g, w_router, w_e_gate, w_e_up, w_e_down, norm_ple_g, w_ple_gate, w_ple_proj, final_norm_g):
    B, S, D = x.shape
    T = B * S
    cap = CAPACITY_FACTOR * S // N_EXPERTS
    xf = x.reshape(T, D)
    cos_d, sin_d = _rope_tables(positions)
    assert w_in.shape[0] == 1, "single-layer block: the final norm is fused into the last kernel"
    i = 0
    w1 = _prep_in_proj_weight(w_in[i])
    wq, wqr, wkv = _prep_mla_weights(w_uq[i], w_ukv[i])
    qa, ka, va, qm, km, vm = _in_proj(
        xf, norm_mix_g[i].reshape(1, D), w1, g_cq[i].reshape(1, -1), g_ckv[i].reshape(1, -1),
        wq, wqr, wkv, cos_d, sin_d)
    ya = _window_attn(qa, ka, va, rel_bias, sink[i], g_out_a[i].reshape(1, -1), B, S)
    yb, (wg, wu, wd) = _mla_attn(qm, km, vm, g_out_b[i].reshape(1, -1), B, S,
                                 (w_e_gate[i], w_e_up[i], w_e_down[i]))
    x1, h2, aff = _out_proj(xf, ya, yb, w_out[i].astype(BF16), norm_ffn_g[i].reshape(1, D),
                            w_router[i].T, B, S)
    slot = _route(aff, cap)
    moe = _moe(slot, aff, h2, wg, wu, wd, cap)
    out = _ple_final(x1, moe, p[i].reshape(T, -1), norm_ple_g[i].reshape(1, D),
                     w_ple_gate[i].astype(BF16), w_ple_proj[i].astype(BF16), final_norm_g.reshape(1, D))
    return out.reshape(B, S, D)
```

```python
import functools

import numpy as np
import jax
import jax.numpy as jnp
from jax import lax
from jax.experimental import pallas as pl
from jax.experimental.pallas import tpu as pltpu

F32 = jnp.float32
BF16 = jnp.bfloat16

D_MODEL = 1024
BLK = 128
WINDOW = 128
A_HEADS = 8
A_KV_HEADS = 2
A_HEAD_DIM = 64
B_HEADS = 8
B_NOPE_DIM = 64
B_ROPE_DIM = 32
B_V_DIM = 64
Q_RANK = 256
KV_RANK = 128
ROPE_THETA = 10000.0
NUM_BUCKETS = 32
MAX_DISTANCE = 128
N_EXPERTS = 16
EXPERT_FF = 512
CAPACITY_FACTOR = 2
PLE_DIM = 256
EPS = 1e-6

LANES = 128
HALF_ROPE = B_ROPE_DIM // 2
MASKED = -1e30
LOG2E = 1.4426950408889634
VMEM_LIMIT = 56 * 1024 * 1024
TM = 1024
TQ = 512
MLA_ROWS = 128
WIN_QBLOCKS = 8
GATHER_GROUP = 8
SCATTER_GROUP = 8
SLAB_ROWS = D_MODEL // LANES


def _cparams(n_axes):
    return pltpu.CompilerParams(dimension_semantics=("arbitrary",) * n_axes,
                                vmem_limit_bytes=VMEM_LIMIT)


def _rms(x, g):
    return x * lax.rsqrt(jnp.mean(x * x, axis=-1, keepdims=True) + EPS) * g


def _dot(a, b):
    return jnp.dot(a, b, preferred_element_type=F32)


def _dot_nt(a, b):
    return lax.dot_general(a, b, (((1,), (1,)), ((), ())), preferred_element_type=F32)


def _dot_tn(a, b):
    return lax.dot_general(a, b, (((0,), (0,)), ((), ())), preferred_element_type=F32)


ROPE_GROUPS = LANES // HALF_ROPE


def _rope_table_kernel(pos_ref, invf_ref, cos_ref, sin_ref):
    ang = pos_ref[...] * invf_ref[...]
    cos_ref[...] = jnp.cos(ang)
    sin_ref[...] = jnp.sin(ang)


def _rope_tables(positions):
    T = positions.size
    rows = T // ROPE_GROUPS
    inv_freq = 1.0 / (ROPE_THETA ** (jnp.arange(0, B_ROPE_DIM, 2, dtype=F32) / B_ROPE_DIM))
    pos = jnp.repeat(positions.astype(F32).reshape(ROPE_GROUPS, rows).T, HALF_ROPE, axis=1)
    invf = jnp.tile(inv_freq, ROPE_GROUPS).reshape(1, LANES)
    spec = pl.BlockSpec((rows, LANES), lambda i: (0, 0))
    return pl.pallas_call(
        _rope_table_kernel,
        grid=(1,),
        in_specs=[spec, pl.BlockSpec((1, LANES), lambda i: (0, 0))],
        out_specs=[spec] * 2,
        out_shape=[jax.ShapeDtypeStruct((rows, LANES), F32)] * 2,
        compiler_params=_cparams(1),
        name="rope_tables",
    )(pos, invf)


def _expand_rope(cd, sd, group):
    lane = lax.broadcasted_iota(jnp.int32, (1, LANES), 1)
    own = (lane >= group * HALF_ROPE) & (lane < (group + 1) * HALF_ROPE)
    xc = jnp.where(own, cd, 0.0)
    xs = jnp.where(own, sd, 0.0)
    shift = HALF_ROPE
    while shift < LANES:
        xc = xc + pltpu.roll(xc, shift, axis=1)
        xs = xs + pltpu.roll(xs, shift, axis=1)
        shift *= 2
    in_t1 = (lane >= B_NOPE_DIM) & (lane < B_NOPE_DIM + HALF_ROPE)
    in_t2 = (lane >= B_NOPE_DIM + HALF_ROPE) & (lane < B_NOPE_DIM + B_ROPE_DIM)
    c = jnp.where(lane < B_NOPE_DIM, 1.0, jnp.where(in_t1 | in_t2, xc, 0.0))
    s = jnp.where(in_t1, -xs, jnp.where(in_t2, xs, 0.0))
    return c, s


C_QA = 0
C_KA = C_QA + 512
C_VA = C_KA + LANES
C_CQ = C_VA + LANES
C_CKV = C_CQ + Q_RANK
C_KR = C_CKV + KV_RANK
C_END = C_KR + LANES
MLA_K_COLS = B_HEADS * LANES


def _in_proj_kernel(tiles_per_group, x_ref, g_ref, w1_ref, gcq_ref, gckv_ref, wq_ref, wqr_ref, wkv_ref,
                    cos_ref, sin_ref, qa_ref, ka_ref, va_ref, qm_ref, km_ref, vm_ref):
    h = _rms(x_ref[...], g_ref[...]).astype(BF16)
    z = _dot(h, w1_ref[...])
    qa_ref[...] = (z[:, C_QA:C_KA] * (A_HEAD_DIM ** -0.5 * LOG2E)).astype(BF16)
    ka_ref[...] = z[:, C_KA:C_VA].astype(BF16)
    va_ref[...] = z[:, C_VA:C_CQ].astype(BF16)
    cqn = _rms(z[:, C_CQ:C_CKV], gcq_ref[...]).astype(BF16)
    ckvn = _rms(z[:, C_CKV:C_KR], gckv_ref[...]).astype(BF16)
    krt = z[:, C_KR:C_END]

    lane = lax.broadcasted_iota(jnp.int32, (1, LANES), 1)
    c_full, s_rope = _expand_rope(cos_ref[...], sin_ref[...], pl.program_id(0) // tiles_per_group)
    c_rope = jnp.where(lane < B_NOPE_DIM, 0.0, c_full)

    q1 = _dot(cqn, wq_ref[...])
    q2 = _dot(cqn, wqr_ref[...])
    scale = (B_NOPE_DIM + B_ROPE_DIM) ** -0.5 * LOG2E
    for hd in range(B_HEADS):
        sl = slice(hd * LANES, (hd + 1) * LANES)
        qm_ref[:, sl] = ((q1[:, sl] * c_full + q2[:, sl] * s_rope) * scale).astype(BF16)

    kv = _dot(ckvn, wkv_ref[...])
    kr_part = pltpu.roll(krt, 64, axis=1) * c_rope + pltpu.roll(krt, 32, axis=1) * s_rope
    for hd in range(B_HEADS):
        sl = slice(hd * LANES, (hd + 1) * LANES)
        km_ref[:, sl] = (kv[:, sl] + kr_part).astype(BF16)
    vm_ref[...] = kv[:, MLA_K_COLS:].astype(BF16)


def _in_proj(xf, g_mix, w1, g_cq, g_ckv, wq, wqr, wkv, cos_d, sin_d):
    T = xf.shape[0]
    tiles_per_group = cos_d.shape[0] // TM
    assert tiles_per_group * TM == cos_d.shape[0]
    row = lambda n: pl.BlockSpec((TM, n), lambda i: (i, 0))
    full = lambda a: pl.BlockSpec(a.shape, lambda i: (0, 0))
    dense = pl.BlockSpec((TM, LANES), lambda i: (i % tiles_per_group, 0))
    outs = [512, LANES, LANES, MLA_K_COLS, MLA_K_COLS, B_HEADS * B_V_DIM]
    return pl.pallas_call(
        functools.partial(_in_proj_kernel, tiles_per_group),
        grid=(T // TM,),
        in_specs=[row(D_MODEL), full(g_mix), full(w1), full(g_cq), full(g_ckv), full(wq), full(wqr), full(wkv),
                  dense, dense],
        out_specs=[row(n) for n in outs],
        out_shape=[jax.ShapeDtypeStruct((T, n), BF16) for n in outs],
        compiler_params=_cparams(1),
        name="in_proj",
    )(xf, g_mix, w1, g_cq, g_ckv, wq, wqr, wkv, cos_d, sin_d)


def _bucket_map():
    qi = np.arange(BLK)[:, None]
    kj = np.arange(3 * BLK)[None, :]
    rel = kj - BLK - qi
    n = np.abs(rel)
    half = NUM_BUCKETS // 2
    max_exact = half // 2
    thresholds = [int(np.ceil(max_exact * 2 ** (k / 2) - 1e-9)) for k in range(1, half - max_exact)]
    large = max_exact + sum((n >= t).astype(np.int64) for t in thresholds)
    large = np.minimum(large, half - 1)
    bucket = np.where(rel > 0, half, 0) + np.where(n < max_exact, n, large)
    return np.where(n <= WINDOW, bucket, -1).astype(np.int32)


def _window_attn_kernel(n_steps, relb_ref, sink_ref, bmap_ref, g_ref, q_ref, kp_ref, kc_ref, kn_ref,
                        vp_ref, vc_ref, vn_ref, o_ref, bias_ref):
    b = pl.program_id(0)
    i = pl.program_id(1)

    @pl.when((b == 0) & (i == 0))
    def _build_bias():
        bmap = bmap_ref[...]
        col = lax.broadcasted_iota(jnp.int32, (1, 3 * BLK), 1)
        for hd in range(A_HEADS):
            bias_ref[1, hd] = jnp.full((BLK, 3 * BLK), MASKED, F32)
        for bk in range(NUM_BUCKETS):
            m = bmap == bk
            for hd in range(A_HEADS):
                bias_ref[1, hd] = jnp.where(m, relb_ref[bk, hd] * LOG2E, bias_ref[1, hd])
        for hd in range(A_HEADS):
            bias_ref[0, hd] = jnp.where(col < BLK, MASKED, bias_ref[1, hd])
            bias_ref[2, hd] = jnp.where(col >= 2 * BLK, MASKED, bias_ref[1, hd])

    lane = lax.broadcasted_iota(jnp.int32, (1, LANES), 1)
    lo = lane < A_HEAD_DIM
    ones_lo = jnp.broadcast_to(jnp.where(lo, 1.0, 0.0).astype(BF16), (3 * BLK, LANES))
    ones_hi = jnp.broadcast_to(jnp.where(lo, 0.0, 1.0).astype(BF16), (3 * BLK, LANES))

    def key_blocks(p_ref, c_ref, n_ref):
        return [p_ref[...]] + [c_ref[c * BLK:(c + 1) * BLK, :] for c in range(WIN_QBLOCKS)] + [n_ref[...]]

    def lane_swap(t):
        return pltpu.bitcast(pltpu.roll(pltpu.bitcast(t, jnp.int32), LANES // 2, axis=1), BF16)

    kdup = [[], []]
    v_even = [[], []]
    v_odd = [[], []]
    for kt, vt in zip(key_blocks(kp_ref, kc_ref, kn_ref), key_blocks(vp_ref, vc_ref, vn_ref)):
        ks, vs, zero = lane_swap(kt), lane_swap(vt), jnp.zeros_like(vt)
        kdup[0].append(jnp.where(lo, kt, ks))
        kdup[1].append(jnp.where(lo, ks, kt))
        v_even[0].append(jnp.where(lo, vt, zero))
        v_odd[0].append(jnp.where(lo, zero, vs))
        v_even[1].append(jnp.where(lo, vs, zero))
        v_odd[1].append(jnp.where(lo, zero, vt))

    for c in range(WIN_QBLOCKS):
        rows = slice(c * BLK, (c + 1) * BLK)
        if c == 0:
            variant = jnp.where(i == 0, 0, 1)
        elif c == WIN_QBLOCKS - 1:
            variant = jnp.where(i == n_steps - 1, 2, 1)
        else:
            variant = 1
        probs, sink_terms = [], []
        for hd in range(A_HEADS):
            g = hd // (A_HEADS // A_KV_HEADS)
            qt = q_ref[rows, (hd // 2) * LANES:(hd // 2 + 1) * LANES]
            qm = jnp.where(lo if hd % 2 == 0 else ~lo, qt, jnp.zeros_like(qt))
            kcat = jnp.concatenate(kdup[g][c:c + 3], axis=0)
            s = _dot_nt(qm, kcat) + bias_ref[variant, hd]
            sk = sink_ref[hd] * LOG2E
            m = jnp.maximum(jnp.max(s, axis=-1, keepdims=True), sk)
            probs.append(jnp.exp2(s - m).astype(BF16))
            sink_terms.append(jnp.exp2(sk - m))

        outs = []
        for j in range(A_HEADS // 2):
            g = (2 * j) // (A_HEADS // A_KV_HEADS)
            v_e = jnp.concatenate(v_even[g][c:c + 3], axis=0)
            v_o = jnp.concatenate(v_odd[g][c:c + 3], axis=0)
            o = (_dot(probs[2 * j], jnp.concatenate([v_e, ones_lo], axis=1))
                 + _dot(probs[2 * j + 1], jnp.concatenate([v_o, ones_hi], axis=1)))
            denom = o[:, LANES:] + jnp.where(lo, sink_terms[2 * j], sink_terms[2 * j + 1])
            outs.append(o[:, :LANES] / denom)
        ya = jnp.concatenate(outs, axis=1)
        o_ref[rows, :] = _rms(ya, g_ref[...]).astype(BF16)


def _window_attn(qa, ka, va, rel_bias, sink, g_out_a, B, S):
    nb = S // BLK
    n_steps = nb // WIN_QBLOCKS
    assert nb >= 2, "first and last query blocks use distinct edge masks"
    bmap = jnp.asarray(_bucket_map())
    smem = pl.BlockSpec(memory_space=pltpu.SMEM)
    cur = lambda n: pl.BlockSpec((WIN_QBLOCKS * BLK, n), lambda b, i: (b * n_steps + i, 0))
    prv = lambda n: pl.BlockSpec((BLK, n), lambda b, i: (b * nb + jnp.maximum(i * WIN_QBLOCKS - 1, 0), 0))
    nxt = lambda n: pl.BlockSpec((BLK, n), lambda b, i: (b * nb + jnp.minimum((i + 1) * WIN_QBLOCKS, nb - 1), 0))
    return pl.pallas_call(
        functools.partial(_window_attn_kernel, n_steps),
        grid=(B, n_steps),
        in_specs=[smem, smem,
                  pl.BlockSpec((BLK, 3 * BLK), lambda b, i: (0, 0)),
                  pl.BlockSpec((1, 512), lambda b, i: (0, 0)),
                  cur(512), prv(LANES), cur(LANES), nxt(LANES), prv(LANES), cur(LANES), nxt(LANES)],
        out_specs=cur(512),
        out_shape=jax.ShapeDtypeStruct((B * S, 512), BF16),
        scratch_shapes=[pltpu.VMEM((3, A_HEADS, BLK, 3 * BLK), F32)],
        compiler_params=_cparams(2),
        name="window_attn",
    )(rel_bias, sink, bmap, g_out_a, qa, ka, ka, ka, va, va, va)


def _mla_attn_kernel(n_cast, g_ref, q_ref, k_ref, v_ref, *rest):
    o_ref = rest[n_cast]
    for src, dst in zip(rest[:n_cast], rest[n_cast + 1:]):
        dst[...] = src[...].astype(BF16)
    S = k_ref.shape[0]
    lane = lax.broadcasted_iota(jnp.int32, (1, LANES), 1)
    lo = lane < B_V_DIM
    ones_lo = jnp.broadcast_to(jnp.where(lo, 1.0, 0.0).astype(BF16), (S, LANES))
    ones_hi = jnp.broadcast_to(jnp.where(lo, 0.0, 1.0).astype(BF16), (S, LANES))
    v_pairs = []
    for j in range(B_HEADS // 2):
        vt = v_ref[:, j * LANES:(j + 1) * LANES]
        zero = jnp.zeros_like(vt)
        v_pairs.append((jnp.concatenate([jnp.where(lo, vt, zero), ones_lo], axis=1),
                        jnp.concatenate([jnp.where(lo, zero, vt), ones_hi], axis=1)))
    for r in range(q_ref.shape[0] // MLA_ROWS):
        rows = slice(r * MLA_ROWS, (r + 1) * MLA_ROWS)
        outs = []
        for j in range(B_HEADS // 2):
            ps = []
            for hd in (2 * j, 2 * j + 1):
                sl = slice(hd * LANES, (hd + 1) * LANES)
                s = _dot_nt(q_ref[rows, sl], k_ref[:, sl])
                m = jnp.max(s, axis=-1, keepdims=True)
                ps.append(jnp.exp2(s - m).astype(BF16))
            o = _dot(ps[0], v_pairs[j][0]) + _dot(ps[1], v_pairs[j][1])
            outs.append(o[:, :LANES] / o[:, LANES:])
        yb = jnp.concatenate(outs, axis=1)
        o_ref[rows, :] = _rms(yb, g_ref[...]).astype(BF16)


def _mla_attn(qm, km, vm, g_out_b, B, S, cast_weights):
    nq = S // TQ
    steps = B * nq
    qspec = lambda n: pl.BlockSpec((TQ, n), lambda b, i: (b * nq + i, 0))
    kspec = lambda n: pl.BlockSpec((S, n), lambda b, i: (b, 0))
    flat = [w.reshape(-1, w.shape[-1]) for w in cast_weights]
    slab = lambda w: pl.BlockSpec((w.shape[0] // steps, w.shape[1]), lambda b, i: (b * nq + i, 0))
    outs = pl.pallas_call(
        functools.partial(_mla_attn_kernel, len(flat)),
        grid=(B, nq),
        in_specs=[pl.BlockSpec((1, 512), lambda b, i: (0, 0)), qspec(1024), kspec(1024), kspec(512)]
                 + [slab(w) for w in flat],
        out_specs=[qspec(512)] + [slab(w) for w in flat],
        out_shape=[jax.ShapeDtypeStruct((B * S, 512), BF16)]
                  + [jax.ShapeDtypeStruct(w.shape, BF16) for w in flat],
        compiler_params=_cparams(2),
        name="mla_attn",
    )(g_out_b, qm, km, vm, *flat)
    return outs[0], [o.reshape(w.shape) for o, w in zip(outs[1:], cast_weights)]


def _split_bf16(v):
    hi = v.astype(BF16)
    return hi, (v - hi.astype(F32)).astype(BF16)


def _out_proj_kernel(x_ref, ya_ref, yb_ref, wo_ref, g_ref, wr_ref, x1_ref, h2_ref, aff_ref):
    x1 = x_ref[...] + _dot(ya_ref[...], wo_ref[0:512, :]) + _dot(yb_ref[...], wo_ref[512:1024, :])
    x1_ref[...] = x1
    h2 = _rms(x1, g_ref[...])
    h2r = h2.astype(BF16).astype(F32)
    for j in range(SLAB_ROWS):
        h2_ref[pl.ds(j, h2r.shape[0], stride=SLAB_ROWS), :] = h2r[:, j * LANES:(j + 1) * LANES]
    h_hi, h_lo = _split_bf16(h2)
    w_hi, w_lo = _split_bf16(wr_ref[...])
    logits = _dot_nt(w_hi, h_hi) + (_dot_nt(w_hi, h_lo) + _dot_nt(w_lo, h_hi))
    m = jnp.max(logits, axis=0, keepdims=True)
    e = jnp.exp(logits - m)
    aff_ref[0] = e / jnp.sum(e, axis=0, keepdims=True)


def _out_proj(xf, ya, yb, w_out, g_ffn, w_router_t, B, S):
    T = xf.shape[0]
    per_b = S // TM
    row = lambda n: pl.BlockSpec((TM, n), lambda i: (i, 0))
    full = lambda a: pl.BlockSpec(a.shape, lambda i: (0, 0))
    return pl.pallas_call(
        _out_proj_kernel,
        grid=(T // TM,),
        in_specs=[row(D_MODEL), row(512), row(512), full(w_out), full(g_ffn), full(w_router_t)],
        out_specs=[row(D_MODEL), pl.BlockSpec((TM * SLAB_ROWS, LANES), lambda i: (i, 0)),
                   pl.BlockSpec((1, N_EXPERTS, TM), lambda i: (i // per_b, 0, i % per_b))],
        out_shape=[jax.ShapeDtypeStruct((T, D_MODEL), F32), jax.ShapeDtypeStruct((T * SLAB_ROWS, LANES), F32),
                   jax.ShapeDtypeStruct((B, N_EXPERTS, S), F32)],
        compiler_params=_cparams(1),
        name="out_proj",
    )(xf, ya, yb, w_out, g_ffn, w_router_t)


CHUNK = 256
ROUTE_LOG_STEPS = 12
ROUTE_LIN_STEPS = 32


def _prefix_count(flags_f32, tri):
    S = flags_f32.shape[1]
    carry = jnp.zeros((flags_f32.shape[0], 1), F32)
    parts = []
    for c in range(S // CHUNK):
        blk = flags_f32[:, c * CHUNK:(c + 1) * CHUNK]
        parts.append(_dot(blk.astype(BF16), tri) + carry)
        carry = carry + jnp.sum(blk, axis=-1, keepdims=True)
    return jnp.concatenate(parts, axis=1)


def _route_kernel(cap, aff_ref, slot_ref):
    aff = aff_ref[...]
    rows = aff.shape[0]

    def enough(pivot):
        return jnp.sum(jnp.where(aff >= pivot, 1.0, 0.0), axis=-1, keepdims=True) >= cap

    def log_body(_, st):
        lo, hi, elo, ehi = st
        mid = 0.5 * (elo + ehi)
        pivot = jnp.exp2(mid)
        ok = enough(pivot)
        return (jnp.where(ok, pivot, lo), jnp.where(ok, hi, pivot), jnp.where(ok, mid, elo), jnp.where(ok, ehi, mid))

    def lin_body(_, st):
        lo, hi = st
        pivot = 0.5 * (lo + hi)
        ok = enough(pivot)
        return jnp.where(ok, pivot, lo), jnp.where(ok, hi, pivot)

    col = lambda v: jnp.full((rows, 1), v, F32)
    lo, hi, _, _ = lax.fori_loop(0, ROUTE_LOG_STEPS, log_body, (col(0.0), col(2.0), col(-152.0), col(1.0)))
    lo, hi = lax.fori_loop(0, ROUTE_LIN_STEPS, lin_body, (lo, hi))

    gt = jnp.where(aff >= hi, 1.0, 0.0)
    eq = jnp.where(aff >= lo, 1.0, 0.0) - gt
    need = cap - jnp.sum(gt, axis=-1, keepdims=True)
    r = lax.broadcasted_iota(jnp.int32, (CHUNK, CHUNK), 0)
    c = lax.broadcasted_iota(jnp.int32, (CHUNK, CHUNK), 1)
    tri = jnp.where(r < c, 1.0, 0.0).astype(BF16)
    sel = gt + eq * jnp.where(_prefix_count(eq, tri) < need, 1.0, 0.0)
    slot = _prefix_count(sel, tri)
    slot_ref[...] = jnp.where(sel > 0.5, slot, -1.0)


def _route(aff, cap):
    B, E, S = aff.shape
    spec = pl.BlockSpec((B * E, S), lambda i: (0, 0))
    return pl.pallas_call(
        functools.partial(_route_kernel, cap),
        grid=(1,),
        in_specs=[spec],
        out_specs=spec,
        out_shape=jax.ShapeDtypeStruct((B * E, S), F32),
        compiler_params=_cparams(1),
        name="route",
    )(aff.reshape(B * E, S))


def _moe_kernel(cap, slot_a_ref, aff_a_ref, slot_b_ref, aff_b_ref, next_a_ref, next_b_ref, h_ref,
                wg_ref, wu_ref, wd_ref, o_ref, xa_ref, xb_ref, ya_ref, yb_ref, tok_ref):
    S = h_ref.shape[0] // SLAB_ROWS
    stride = cap + SLAB_ROWS
    parity = pl.program_id(1) % 2
    slot_ids = lax.broadcasted_iota(jnp.int32, (cap, 1), 0).astype(F32)
    tok_ids = lax.broadcasted_iota(jnp.int32, (1, S), 1).astype(F32)

    def slot_affinities(slot_row, aff_row):
        return jnp.sum(jnp.where(slot_ids == slot_row, aff_row, 0.0), axis=-1, keepdims=True)

    def gather(slot_row, x_ref, which, par, done):
        sel = slot_ids == slot_row
        tok = jnp.sum(jnp.where(sel, tok_ids, 0.0), axis=-1, keepdims=True).astype(jnp.int32)
        tok = jnp.clip(tok, 0, S - 1)
        for grp in range(cap // GATHER_GROUP):
            ids = tok[grp * GATHER_GROUP:(grp + 1) * GATHER_GROUP, :]
            if len(done) >= 2:
                ids = jnp.minimum(ids, done[-2] + S)
            for j in range(GATHER_GROUP):
                c = grp * GATHER_GROUP + j
                t = ids[j, 0]
                row = pl.multiple_of(t * SLAB_ROWS, SLAB_ROWS)
                tok_ref[which, par, c] = row
                x_ref[pl.ds(c, SLAB_ROWS, stride=stride), :] = h_ref[pl.ds(row, SLAB_ROWS), :]
            done.append(t)

    def chunks_to_rows(tile_ref):
        return jnp.concatenate([tile_ref[j * stride:j * stride + cap, :] for j in range(SLAB_ROWS)], axis=1)

    def gate_up(x_ref, i):
        xg = chunks_to_rows(x_ref).astype(BF16)
        g = _dot(xg, wg_ref[i])
        u = _dot(xg, wu_ref[i])
        return (g * (1.0 / (1.0 + jnp.exp(-g))) * u).astype(BF16)

    def down(act, slot_row, aff_row, y_ref, i):
        y = _dot(act, wd_ref[i]) * slot_affinities(slot_row, aff_row)
        for j in range(SLAB_ROWS):
            y_ref[j * stride:j * stride + cap, :] = y[:, j * LANES:(j + 1) * LANES]

    def scatter_add(y_ref, which):
        for c0 in range(0, cap, SCATTER_GROUP):
            rows = [tok_ref[which, parity, c0 + j] for j in range(SCATTER_GROUP)]
            rows = [pl.multiple_of(r, SLAB_ROWS) for r in rows]
            new = [o_ref[pl.ds(rows[j], SLAB_ROWS), :] + y_ref[pl.ds(c0 + j, SLAB_ROWS, stride=stride), :]
                   for j in range(SCATTER_GROUP)]
            for j in range(SCATTER_GROUP):
                o_ref[pl.ds(rows[j], SLAB_ROWS), :] = new[j]

    @pl.when((pl.program_id(0) == 0) & (pl.program_id(1) == 0))
    def _first_step():
        moved = []
        gather(slot_a_ref[0], xa_ref, 0, 0, moved)
        gather(slot_b_ref[0], xb_ref, 1, 0, moved)

    @pl.when(pl.program_id(1) == 0)
    def _first_of_sequence():
        o_ref[...] = jnp.zeros_like(o_ref)

    moved = []
    act_a = gate_up(xa_ref, 0)
    gather(next_a_ref[0], xa_ref, 0, 1 - parity, moved)
    act_b = gate_up(xb_ref, 1)
    gather(next_b_ref[0], xb_ref, 1, 1 - parity, moved)
    down(act_a, slot_a_ref[0], aff_a_ref[0], ya_ref, 0)
    down(act_b, slot_b_ref[0], aff_b_ref[0], yb_ref, 1)
    scatter_add(ya_ref, 0)
    scatter_add(yb_ref, 1)


def _moe(slot, aff, h2, wg, wu, wd, cap):
    B, E, S = aff.shape
    assert E % 2 == 0 and (E // 2) % 2 == 0 and h2.shape == (B * S * SLAB_ROWS, LANES)
    n_steps = B * E // 2
    last = B * E - 1
    slot3 = slot.reshape(B * E, 1, S)
    aff3 = aff.reshape(B * E, 1, S)
    row = lambda d: pl.BlockSpec((1, 1, S), lambda b, k: (jnp.minimum(b * E + 2 * k + d, last), 0, 0))
    cur_a, cur_b, nxt_a, nxt_b = row(0), row(1), row(2), row(3)
    tok = pl.BlockSpec((S * SLAB_ROWS, LANES), lambda b, k: (b, 0))
    ahead = pl.BlockSpec((S * SLAB_ROWS, LANES),
                         lambda b, k: (jnp.minimum(b * (E // 2) + k + 1, n_steps - 1) // (E // 2), 0))
    tile = pltpu.VMEM((SLAB_ROWS * (cap + SLAB_ROWS), LANES), F32)
    wspec = lambda w: pl.BlockSpec((2,) + w.shape[1:], lambda b, k: (k, 0, 0))
    return pl.pallas_call(
        functools.partial(_moe_kernel, cap),
        grid=(B, E // 2),
        in_specs=[cur_a, cur_a, cur_b, cur_b, nxt_a, nxt_b, ahead, wspec(wg), wspec(wu), wspec(wd)],
        out_specs=tok,
        out_shape=jax.ShapeDtypeStruct(h2.shape, F32),
        scratch_shapes=[tile, tile, tile, tile, pltpu.SMEM((2, 2, cap), jnp.int32)],
        compiler_params=_cparams(2),
        name="moe",
    )(slot3, aff3, slot3, aff3, slot3, slot3, h2, wg, wu, wd)


def _ple_final_kernel(x1_ref, moe_ref, p_ref, gp_ref, wg_ref, wp_ref, gf_ref, o_ref):
    rows = x1_ref.shape[0]
    moe = jnp.concatenate([moe_ref[pl.ds(j, rows, stride=SLAB_ROWS), :] for j in range(SLAB_ROWS)], axis=1)
    x2 = x1_ref[...] + moe
    z = _dot(_rms(x2, gp_ref[...]).astype(BF16), wg_ref[...])
    gate = 1.0 / (1.0 + jnp.exp(-z))
    x3 = x2 + gate * _dot(p_ref[...].astype(BF16), wp_ref[...])
    o_ref[...] = _rms(x3, gf_ref[...])


def _ple_final(x1, moe, pf, g_ple, w_gate, w_proj, g_final):
    T = x1.shape[0]
    row = lambda n: pl.BlockSpec((TM, n), lambda i: (i, 0))
    full = lambda a: pl.BlockSpec(a.shape, lambda i: (0, 0))
    return pl.pallas_call(
        _ple_final_kernel,
        grid=(T // TM,),
        in_specs=[row(D_MODEL), pl.BlockSpec((TM * SLAB_ROWS, LANES), lambda i: (i, 0)), row(PLE_DIM),
                  full(g_ple), full(w_gate), full(w_proj), full(g_final)],
        out_specs=row(D_MODEL),
        out_shape=jax.ShapeDtypeStruct((T, D_MODEL), F32),
        compiler_params=_cparams(1),
        name="ple_final",
    )(x1, moe, pf, g_ple, w_gate, w_proj, g_final)


def _prep_in_proj_weight(w_in):
    D = w_in.shape[0]
    t1, t2 = w_in[:, 1152:1168], w_in[:, 1168:1184]
    cols = [w_in[:, 0:1152], t1, t2, t2, t1, jnp.zeros((D, 64), F32)]
    return jnp.concatenate(cols, axis=1).astype(BF16)


def _prep_mla_weights(w_uq, w_ukv):
    per_q = B_NOPE_DIM + B_ROPE_DIM
    w3 = w_uq.reshape(Q_RANK, B_HEADS, per_q)
    pad = LANES - per_q
    wq = jnp.pad(w3, ((0, 0), (0, 0), (0, pad))).reshape(Q_RANK, B_HEADS * LANES)
    t1 = w3[:, :, B_NOPE_DIM:B_NOPE_DIM + HALF_ROPE]
    t2 = w3[:, :, B_NOPE_DIM + HALF_ROPE:]
    wqr = jnp.concatenate([jnp.zeros((Q_RANK, B_HEADS, B_NOPE_DIM), F32), t2, t1,
                           jnp.zeros((Q_RANK, B_HEADS, pad), F32)], axis=2).reshape(Q_RANK, B_HEADS * LANES)
    w4 = w_ukv.reshape(KV_RANK, B_HEADS, B_NOPE_DIM + B_V_DIM)
    zk = jnp.zeros((KV_RANK, B_HEADS, LANES - B_NOPE_DIM), F32)
    wk = jnp.concatenate([w4[:, :, :B_NOPE_DIM], zk], axis=2).reshape(KV_RANK, B_HEADS * LANES)
    wv = w4[:, :, B_NOPE_DIM:].reshape(KV_RANK, B_HEADS * B_V_DIM)
    wkv = jnp.concatenate([wk, wv], axis=1)
    return wq.astype(BF16), wqr.astype(BF16), wkv.astype(BF16)


def kernel(x, p, positions, rel_bias, norm_mix_g, w_in, sink, g_cq, g_ckv, w_uq, w_ukv, g_out_a, g_out_b, w_out,
           norm_ffn_g, w_router, w_e_gate, w_e_up, w_e_down, norm_ple_g, w_ple_gate, w_ple_proj, final_norm_g):
    B, S, D = x.shape
    T = B * S
    cap = CAPACITY_FACTOR * S // N_EXPERTS
    xf = x.reshape(T, D)
    cos_d, sin_d = _rope_tables(positions)
    assert w_in.shape[0] == 1, "single-layer block: the final norm is fused into the last kernel"
    i = 0
    w1 = _prep_in_proj_weight(w_in[i])
    wq, wqr, wkv = _prep_mla_weights(w_uq[i], w_ukv[i])
    qa, ka, va, qm, km, vm = _in_proj(
        xf, norm_mix_g[i].reshape(1, D), w1, g_cq[i].reshape(1, -1), g_ckv[i].reshape(1, -1),
        wq, wqr, wkv, cos_d, sin_d)
    ya = _window_attn(qa, ka, va, rel_bias, sink[i], g_out_a[i].reshape(1, -1), B, S)
    yb, (wg, wu, wd) = _mla_attn(qm, km, vm, g_out_b[i].reshape(1, -1), B, S,
                                 (w_e_gate[i], w_e_up[i], w_e_down[i]))
    x1, h2, aff = _out_proj(xf, ya, yb, w_out[i].astype(BF16), norm_ffn_g[i].reshape(1, D),
                            w_router[i].T, B, S)
    slot = _route(aff, cap)
    moe = _moe(slot, aff, h2, wg, wu, wd, cap)
    out = _ple_final(x1, moe, p[i].reshape(T, -1), norm_ple_g[i].reshape(1, D),
                     w_ple_gate[i].astype(BF16), w_ple_proj[i].astype(BF16), final_norm_g.reshape(1, D))
    return out.reshape(B, S, D)
```

```python
import functools

import numpy as np
import jax
import jax.numpy as jnp
from jax import lax
from jax.experimental import pallas as pl
from jax.experimental.pallas import tpu as pltpu

F32 = jnp.float32
BF16 = jnp.bfloat16

D_MODEL = 1024
BLK = 128
WINDOW = 128
A_HEADS = 8
A_KV_HEADS = 2
A_HEAD_DIM = 64
B_HEADS = 8
B_NOPE_DIM = 64
B_ROPE_DIM = 32
B_V_DIM = 64
Q_RANK = 256
KV_RANK = 128
ROPE_THETA = 10000.0
NUM_BUCKETS = 32
MAX_DISTANCE = 128
N_EXPERTS = 16
EXPERT_FF = 512
CAPACITY_FACTOR = 2
PLE_DIM = 256
EPS = 1e-6

LANES = 128
HALF_ROPE = B_ROPE_DIM // 2
MASKED = -1e30
LOG2E = 1.4426950408889634
VMEM_LIMIT = 56 * 1024 * 1024
TM = 1024
TQ = 512
MLA_ROWS = 128
WIN_QBLOCKS = 8
GATHER_GROUP = 8
SCATTER_GROUP = 8
SLAB_ROWS = D_MODEL // LANES


def _cparams(n_axes):
    return pltpu.CompilerParams(dimension_semantics=("arbitrary",) * n_axes,
                                vmem_limit_bytes=VMEM_LIMIT)


def _rms(x, g):
    return x * lax.rsqrt(jnp.mean(x * x, axis=-1, keepdims=True) + EPS) * g


def _dot(a, b):
    return jnp.dot(a, b, preferred_element_type=F32)


def _dot_nt(a, b):
    return lax.dot_general(a, b, (((1,), (1,)), ((), ())), preferred_element_type=F32)


def _dot_tn(a, b):
    return lax.dot_general(a, b, (((0,), (0,)), ((), ())), preferred_element_type=F32)


ROPE_GROUPS = LANES // HALF_ROPE


def _rope_table_kernel(pos_ref, invf_ref, cos_ref, sin_ref):
    ang = pos_ref[...] * invf_ref[...]
    cos_ref[...] = jnp.cos(ang)
    sin_ref[...] = jnp.sin(ang)


def _rope_tables(positions):
    T = positions.size
    rows = T // ROPE_GROUPS
    inv_freq = 1.0 / (ROPE_THETA ** (jnp.arange(0, B_ROPE_DIM, 2, dtype=F32) / B_ROPE_DIM))
    pos = jnp.repeat(positions.astype(F32).reshape(ROPE_GROUPS, rows).T, HALF_ROPE, axis=1)
    invf = jnp.tile(inv_freq, ROPE_GROUPS).reshape(1, LANES)
    spec = pl.BlockSpec((rows, LANES), lambda i: (0, 0))
    return pl.pallas_call(
        _rope_table_kernel,
        grid=(1,),
        in_specs=[spec, pl.BlockSpec((1, LANES), lambda i: (0, 0))],
        out_specs=[spec] * 2,
        out_shape=[jax.ShapeDtypeStruct((rows, LANES), F32)] * 2,
        compiler_params=_cparams(1),
        name="rope_tables",
    )(pos, invf)


def _expand_rope(cd, sd, group):
    lane = lax.broadcasted_iota(jnp.int32, (1, LANES), 1)
    own = (lane >= group * HALF_ROPE) & (lane < (group + 1) * HALF_ROPE)
    xc = jnp.where(own, cd, 0.0)
    xs = jnp.where(own, sd, 0.0)
    shift = HALF_ROPE
    while shift < LANES:
        xc = xc + pltpu.roll(xc, shift, axis=1)
        xs = xs + pltpu.roll(xs, shift, axis=1)
        shift *= 2
    in_t1 = (lane >= B_NOPE_DIM) & (lane < B_NOPE_DIM + HALF_ROPE)
    in_t2 = (lane >= B_NOPE_DIM + HALF_ROPE) & (lane < B_NOPE_DIM + B_ROPE_DIM)
    c = jnp.where(lane < B_NOPE_DIM, 1.0, jnp.where(in_t1 | in_t2, xc, 0.0))
    s = jnp.where(in_t1, -xs, jnp.where(in_t2, xs, 0.0))
    return c, s


C_QA = 0
C_KA = C_QA + 512
C_VA = C_KA + LANES
C_CQ = C_VA + LANES
C_CKV = C_CQ + Q_RANK
C_KR = C_CKV + KV_RANK
C_END = C_KR + LANES
MLA_K_COLS = B_HEADS * LANES


def _in_proj_kernel(tiles_per_group, x_ref, g_ref, w1_ref, gcq_ref, gckv_ref, wq_ref, wqr_ref, wkv_ref,
                    cos_ref, sin_ref, qa_ref, ka_ref, va_ref, qm_ref, km_ref, vm_ref):
    h = _rms(x_ref[...], g_ref[...]).astype(BF16)
    z = _dot(h, w1_ref[...])
    qa_ref[...] = (z[:, C_QA:C_KA] * (A_HEAD_DIM ** -0.5 * LOG2E)).astype(BF16)
    ka_ref[...] = z[:, C_KA:C_VA].astype(BF16)
    va_ref[...] = z[:, C_VA:C_CQ].astype(BF16)
    cqn = _rms(z[:, C_CQ:C_CKV], gcq_ref[...]).astype(BF16)
    ckvn = _rms(z[:, C_CKV:C_KR], gckv_ref[...]).astype(BF16)
    krt = z[:, C_KR:C_END]

    lane = lax.broadcasted_iota(jnp.int32, (1, LANES), 1)
    c_full, s_rope = _expand_rope(cos_ref[...], sin_ref[...], pl.program_id(0) // tiles_per_group)
    c_rope = jnp.where(lane < B_NOPE_DIM, 0.0, c_full)

    q1 = _dot(cqn, wq_ref[...])
    q2 = _dot(cqn, wqr_ref[...])
    scale = (B_NOPE_DIM + B_ROPE_DIM) ** -0.5 * LOG2E
    for hd in range(B_HEADS):
        sl = slice(hd * LANES, (hd + 1) * LANES)
        qm_ref[:, sl] = ((q1[:, sl] * c_full + q2[:, sl] * s_rope) * scale).astype(BF16)

    kv = _dot(ckvn, wkv_ref[...])
    kr_part = pltpu.roll(krt, 64, axis=1) * c_rope + pltpu.roll(krt, 32, axis=1) * s_rope
    for hd in range(B_HEADS):
        sl = slice(hd * LANES, (hd + 1) * LANES)
        km_ref[:, sl] = (kv[:, sl] + kr_part).astype(BF16)
    vm_ref[...] = kv[:, MLA_K_COLS:].astype(BF16)


def _in_proj(xf, g_mix, w1, g_cq, g_ckv, wq, wqr, wkv, cos_d, sin_d):
    T = xf.shape[0]
    tiles_per_group = cos_d.shape[0] // TM
    assert tiles_per_group * TM == cos_d.shape[0]
    row = lambda n: pl.BlockSpec((TM, n), lambda i: (i, 0))
    full = lambda a: pl.BlockSpec(a.shape, lambda i: (0, 0))
    dense = pl.BlockSpec((TM, LANES), lambda i: (i % tiles_per_group, 0))
    outs = [512, LANES, LANES, MLA_K_COLS, MLA_K_COLS, B_HEADS * B_V_DIM]
    return pl.pallas_call(
        functools.partial(_in_proj_kernel, tiles_per_group),
        grid=(T // TM,),
        in_specs=[row(D_MODEL), full(g_mix), full(w1), full(g_cq), full(g_ckv), full(wq), full(wqr), full(wkv),
                  dense, dense],
        out_specs=[row(n) for n in outs],
        out_shape=[jax.ShapeDtypeStruct((T, n), BF16) for n in outs],
        compiler_params=_cparams(1),
        name="in_proj",
    )(xf, g_mix, w1, g_cq, g_ckv, wq, wqr, wkv, cos_d, sin_d)


def _bucket_map():
    qi = np.arange(BLK)[:, None]
    kj = np.arange(3 * BLK)[None, :]
    rel = kj - BLK - qi
    n = np.abs(rel)
    half = NUM_BUCKETS // 2
    max_exact = half // 2
    thresholds = [int(np.ceil(max_exact * 2 ** (k / 2) - 1e-9)) for k in range(1, half - max_exact)]
    large = max_exact + sum((n >= t).astype(np.int64) for t in thresholds)
    large = np.minimum(large, half - 1)
    bucket = np.where(rel > 0, half, 0) + np.where(n < max_exact, n, large)
    return np.where(n <= WINDOW, bucket, -1).astype(np.int32)


def _window_attn_kernel(n_steps, relb_ref, sink_ref, bmap_ref, g_ref, q_ref, kp_ref, kc_ref, kn_ref,
                        vp_ref, vc_ref, vn_ref, o_ref, bias_ref):
    b = pl.program_id(0)
    i = pl.program_id(1)

    @pl.when((b == 0) & (i == 0))
    def _build_bias():
        bmap = bmap_ref[...]
        col = lax.broadcasted_iota(jnp.int32, (1, 3 * BLK), 1)
        for hd in range(A_HEADS):
            bias_ref[1, hd] = jnp.full((BLK, 3 * BLK), MASKED, F32)
        for bk in range(NUM_BUCKETS):
            m = bmap == bk
            for hd in range(A_HEADS):
                bias_ref[1, hd] = jnp.where(m, relb_ref[bk, hd] * LOG2E, bias_ref[1, hd])
        for hd in range(A_HEADS):
            bias_ref[0, hd] = jnp.where(col < BLK, MASKED, bias_ref[1, hd])
            bias_ref[2, hd] = jnp.where(col >= 2 * BLK, MASKED, bias_ref[1, hd])

    lane = lax.broadcasted_iota(jnp.int32, (1, LANES), 1)
    lo = lane < A_HEAD_DIM
    ones_lo = jnp.broadcast_to(jnp.where(lo, 1.0, 0.0).astype(BF16), (3 * BLK, LANES))
    ones_hi = jnp.broadcast_to(jnp.where(lo, 0.0, 1.0).astype(BF16), (3 * BLK, LANES))

    def key_blocks(p_ref, c_ref, n_ref):
        return [p_ref[...]] + [c_ref[c * BLK:(c + 1) * BLK, :] for c in range(WIN_QBLOCKS)] + [n_ref[...]]

    def lane_swap(t):
        return pltpu.bitcast(pltpu.roll(pltpu.bitcast(t, jnp.int32), LANES // 2, axis=1), BF16)

    kdup = [[], []]
    v_even = [[], []]
    v_odd = [[], []]
    for kt, vt in zip(key_blocks(kp_ref, kc_ref, kn_ref), key_blocks(vp_ref, vc_ref, vn_ref)):
        ks, vs, zero = lane_swap(kt), lane_swap(vt), jnp.zeros_like(vt)
        kdup[0].append(jnp.where(lo, kt, ks))
        kdup[1].append(jnp.where(lo, ks, kt))
        v_even[0].append(jnp.where(lo, vt, zero))
        v_odd[0].append(jnp.where(lo, zero, vs))
        v_even[1].append(jnp.where(lo, vs, zero))
        v_odd[1].append(jnp.where(lo, zero, vt))

    for c in range(WIN_QBLOCKS):
        rows = slice(c * BLK, (c + 1) * BLK)
        if c == 0:
            variant = jnp.where(i == 0, 0, 1)
        elif c == WIN_QBLOCKS - 1:
            variant = jnp.where(i == n_steps - 1, 2, 1)
        else:
            variant = 1
        probs, sink_terms = [], []
        for hd in range(A_HEADS):
            g = hd // (A_HEADS // A_KV_HEADS)
            qt = q_ref[rows, (hd // 2) * LANES:(hd // 2 + 1) * LANES]
            qm = jnp.where(lo if hd % 2 == 0 else ~lo, qt, jnp.zeros_like(qt))
            kcat = jnp.concatenate(kdup[g][c:c + 3], axis=0)
            s = _dot_nt(qm, kcat) + bias_ref[variant, hd]
            sk = sink_ref[hd] * LOG2E
            m = jnp.maximum(jnp.max(s, axis=-1, keepdims=True), sk)
            probs.append(jnp.exp2(s - m).astype(BF16))
            sink_terms.append(jnp.exp2(sk - m))

        outs = []
        for j in range(A_HEADS // 2):
            g = (2 * j) // (A_HEADS // A_KV_HEADS)
            v_e = jnp.concatenate(v_even[g][c:c + 3], axis=0)
            v_o = jnp.concatenate(v_odd[g][c:c + 3], axis=0)
            o = (_dot(probs[2 * j], jnp.concatenate([v_e, ones_lo], axis=1))
                 + _dot(probs[2 * j + 1], jnp.concatenate([v_o, ones_hi], axis=1)))
            denom = o[:, LANES:] + jnp.where(lo, sink_terms[2 * j], sink_terms[2 * j + 1])
            outs.append(o[:, :LANES] / denom)
        ya = jnp.concatenate(outs, axis=1)
        o_ref[rows, :] = _rms(ya, g_ref[...]).astype(BF16)


def _window_attn(qa, ka, va, rel_bias, sink, g_out_a, B, S):
    nb = S // BLK
    n_steps = nb // WIN_QBLOCKS
    assert nb >= 2, "first and last query blocks use distinct edge masks"
    bmap = jnp.asarray(_bucket_map())
    smem = pl.BlockSpec(memory_space=pltpu.SMEM)
    cur = lambda n: pl.BlockSpec((WIN_QBLOCKS * BLK, n), lambda b, i: (b * n_steps + i, 0))
    prv = lambda n: pl.BlockSpec((BLK, n), lambda b, i: (b * nb + jnp.maximum(i * WIN_QBLOCKS - 1, 0), 0))
    nxt = lambda n: pl.BlockSpec((BLK, n), lambda b, i: (b * nb + jnp.minimum((i + 1) * WIN_QBLOCKS, nb - 1), 0))
    return pl.pallas_call(
        functools.partial(_window_attn_kernel, n_steps),
        grid=(B, n_steps),
        in_specs=[smem, smem,
                  pl.BlockSpec((BLK, 3 * BLK), lambda b, i: (0, 0)),
                  pl.BlockSpec((1, 512), lambda b, i: (0, 0)),
                  cur(512), prv(LANES), cur(LANES), nxt(LANES), prv(LANES), cur(LANES), nxt(LANES)],
        out_specs=cur(512),
        out_shape=jax.ShapeDtypeStruct((B * S, 512), BF16),
        scratch_shapes=[pltpu.VMEM((3, A_HEADS, BLK, 3 * BLK), F32)],
        compiler_params=_cparams(2),
        name="window_attn",
    )(rel_bias, sink, bmap, g_out_a, qa, ka, ka, ka, va, va, va)


def _mla_attn_kernel(n_cast, g_ref, q_ref, k_ref, v_ref, *rest):
    o_ref = rest[n_cast]
    for src, dst in zip(rest[:n_cast], rest[n_cast + 1:]):
        dst[...] = src[...].astype(BF16)
    S = k_ref.shape[0]
    lane = lax.broadcasted_iota(jnp.int32, (1, LANES), 1)
    lo = lane < B_V_DIM
    ones_lo = jnp.broadcast_to(jnp.where(lo, 1.0, 0.0).astype(BF16), (S, LANES))
    ones_hi = jnp.broadcast_to(jnp.where(lo, 0.0, 1.0).astype(BF16), (S, LANES))
    v_pairs = []
    for j in range(B_HEADS // 2):
        vt = v_ref[:, j * LANES:(j + 1) * LANES]
        zero = jnp.zeros_like(vt)
        v_pairs.append((jnp.concatenate([jnp.where(lo, vt, zero), ones_lo], axis=1),
                        jnp.concatenate([jnp.where(lo, zero, vt), ones_hi], axis=1)))
    for r in range(q_ref.shape[0] // MLA_ROWS):
        rows = slice(r * MLA_ROWS, (r + 1) * MLA_ROWS)
        outs = []
        for j in range(B_HEADS // 2):
            ps = []
            for hd in (2 * j, 2 * j + 1):
                sl = slice(hd * LANES, (hd + 1) * LANES)
                s = _dot_nt(q_ref[rows, sl], k_ref[:, sl])
                m = jnp.max(s, axis=-1, keepdims=True)
                ps.append(jnp.exp2(s - m).astype(BF16))
            o = _dot(ps[0], v_pairs[j][0]) + _dot(ps[1], v_pairs[j][1])
            outs.append(o[:, :LANES] / o[:, LANES:])
        yb = jnp.concatenate(outs, axis=1)
        o_ref[rows, :] = _rms(yb, g_ref[...]).astype(BF16)


def _mla_attn(qm, km, vm, g_out_b, B, S, cast_weights):
    nq = S // TQ
    steps = B * nq
    qspec = lambda n: pl.BlockSpec((TQ, n), lambda b, i: (b * nq + i, 0))
    kspec = lambda n: pl.BlockSpec((S, n), lambda b, i: (b, 0))
    flat = [w.reshape(-1, w.shape[-1]) for w in cast_weights]
    slab = lambda w: pl.BlockSpec((w.shape[0] // steps, w.shape[1]), lambda b, i: (b * nq + i, 0))
    outs = pl.pallas_call(
        functools.partial(_mla_attn_kernel, len(flat)),
        grid=(B, nq),
        in_specs=[pl.BlockSpec((1, 512), lambda b, i: (0, 0)), qspec(1024), kspec(1024), kspec(512)]
                 + [slab(w) for w in flat],
        out_specs=[qspec(512)] + [slab(w) for w in flat],
        out_shape=[jax.ShapeDtypeStruct((B * S, 512), BF16)]
                  + [jax.ShapeDtypeStruct(w.shape, BF16) for w in flat],
        compiler_params=_cparams(2),
        name="mla_attn",
    )(g_out_b, qm, km, vm, *flat)
    return outs[0], [o.reshape(w.shape) for o, w in zip(outs[1:], cast_weights)]


def _out_proj_kernel(x_ref, ya_ref, yb_ref, wo_ref, g_ref, wr_ref, x1_ref, h2_ref, aff_ref):
    x1 = x_ref[...] + _dot(ya_ref[...], wo_ref[0:512, :]) + _dot(yb_ref[...], wo_ref[512:1024, :])
    x1_ref[...] = x1
    h2 = _rms(x1, g_ref[...])
    h2b = h2.astype(BF16)
    h2r = h2b.astype(F32)
    for j in range(SLAB_ROWS):
        h2_ref[pl.ds(j, h2r.shape[0], stride=SLAB_ROWS), :] = h2r[:, j * LANES:(j + 1) * LANES]
    logits = _dot_nt(wr_ref[...].astype(BF16), h2b)
    m = jnp.max(logits, axis=0, keepdims=True)
    e = jnp.exp(logits - m)
    aff_ref[0] = e / jnp.sum(e, axis=0, keepdims=True)


def _out_proj(xf, ya, yb, w_out, g_ffn, w_router_t, B, S):
    T = xf.shape[0]
    per_b = S // TM
    row = lambda n: pl.BlockSpec((TM, n), lambda i: (i, 0))
    full = lambda a: pl.BlockSpec(a.shape, lambda i: (0, 0))
    return pl.pallas_call(
        _out_proj_kernel,
        grid=(T // TM,),
        in_specs=[row(D_MODEL), row(512), row(512), full(w_out), full(g_ffn), full(w_router_t)],
        out_specs=[row(D_MODEL), pl.BlockSpec((TM * SLAB_ROWS, LANES), lambda i: (i, 0)),
                   pl.BlockSpec((1, N_EXPERTS, TM), lambda i: (i // per_b, 0, i % per_b))],
        out_shape=[jax.ShapeDtypeStruct((T, D_MODEL), F32), jax.ShapeDtypeStruct((T * SLAB_ROWS, LANES), F32),
                   jax.ShapeDtypeStruct((B, N_EXPERTS, S), F32)],
        compiler_params=_cparams(1),
        name="out_proj",
    )(xf, ya, yb, w_out, g_ffn, w_router_t)


CHUNK = 256
ROUTE_LOG_STEPS = 12
ROUTE_LIN_STEPS = 32


def _prefix_count(flags_f32, tri):
    S = flags_f32.shape[1]
    carry = jnp.zeros((flags_f32.shape[0], 1), F32)
    parts = []
    for c in range(S // CHUNK):
        blk = flags_f32[:, c * CHUNK:(c + 1) * CHUNK]
        parts.append(_dot(blk.astype(BF16), tri) + carry)
        carry = carry + jnp.sum(blk, axis=-1, keepdims=True)
    return jnp.concatenate(parts, axis=1)


def _route_kernel(cap, aff_ref, slot_ref):
    aff = aff_ref[...]
    rows = aff.shape[0]

    def enough(pivot):
        return jnp.sum(jnp.where(aff >= pivot, 1.0, 0.0), axis=-1, keepdims=True) >= cap

    def log_body(_, st):
        lo, hi, elo, ehi = st
        mid = 0.5 * (elo + ehi)
        pivot = jnp.exp2(mid)
        ok = enough(pivot)
        return (jnp.where(ok, pivot, lo), jnp.where(ok, hi, pivot), jnp.where(ok, mid, elo), jnp.where(ok, ehi, mid))

    def lin_body(_, st):
        lo, hi = st
        pivot = 0.5 * (lo + hi)
        ok = enough(pivot)
        return jnp.where(ok, pivot, lo), jnp.where(ok, hi, pivot)

    col = lambda v: jnp.full((rows, 1), v, F32)
    lo, hi, _, _ = lax.fori_loop(0, ROUTE_LOG_STEPS, log_body, (col(0.0), col(2.0), col(-152.0), col(1.0)))
    lo, hi = lax.fori_loop(0, ROUTE_LIN_STEPS, lin_body, (lo, hi))

    gt = jnp.where(aff >= hi, 1.0, 0.0)
    eq = jnp.where(aff >= lo, 1.0, 0.0) - gt
    need = cap - jnp.sum(gt, axis=-1, keepdims=True)
    r = lax.broadcasted_iota(jnp.int32, (CHUNK, CHUNK), 0)
    c = lax.broadcasted_iota(jnp.int32, (CHUNK, CHUNK), 1)
    tri = jnp.where(r < c, 1.0, 0.0).astype(BF16)
    sel = gt + eq * jnp.where(_prefix_count(eq, tri) < need, 1.0, 0.0)
    slot = _prefix_count(sel, tri)
    slot_ref[...] = jnp.where(sel > 0.5, slot, -1.0)


def _route(aff, cap):
    B, E, S = aff.shape
    spec = pl.BlockSpec((B * E, S), lambda i: (0, 0))
    return pl.pallas_call(
        functools.partial(_route_kernel, cap),
        grid=(1,),
        in_specs=[spec],
        out_specs=spec,
        out_shape=jax.ShapeDtypeStruct((B * E, S), F32),
        compiler_params=_cparams(1),
        name="route",
    )(aff.reshape(B * E, S))


def _moe_kernel(cap, slot_a_ref, aff_a_ref, slot_b_ref, aff_b_ref, next_a_ref, next_b_ref, h_ref,
                wg_ref, wu_ref, wd_ref, o_ref, xa_ref, xb_ref, ya_ref, yb_ref, tok_ref):
    S = h_ref.shape[0] // SLAB_ROWS
    stride = cap + SLAB_ROWS
    parity = pl.program_id(1) % 2
    slot_ids = lax.broadcasted_iota(jnp.int32, (cap, 1), 0).astype(F32)
    tok_ids = lax.broadcasted_iota(jnp.int32, (1, S), 1).astype(F32)

    def slot_affinities(slot_row, aff_row):
        return jnp.sum(jnp.where(slot_ids == slot_row, aff_row, 0.0), axis=-1, keepdims=True)

    def gather(slot_row, x_ref, which, par, done):
        sel = slot_ids == slot_row
        tok = jnp.sum(jnp.where(sel, tok_ids, 0.0), axis=-1, keepdims=True).astype(jnp.int32)
        tok = jnp.clip(tok, 0, S - 1)
        for grp in range(cap // GATHER_GROUP):
            ids = tok[grp * GATHER_GROUP:(grp + 1) * GATHER_GROUP, :]
            if len(done) >= 2:
                ids = jnp.minimum(ids, done[-2] + S)
            for j in range(GATHER_GROUP):
                c = grp * GATHER_GROUP + j
                t = ids[j, 0]
                row = pl.multiple_of(t * SLAB_ROWS, SLAB_ROWS)
                tok_ref[which, par, c] = row
                x_ref[pl.ds(c, SLAB_ROWS, stride=stride), :] = h_ref[pl.ds(row, SLAB_ROWS), :]
            done.append(t)

    def chunks_to_rows(tile_ref):
        return jnp.concatenate([tile_ref[j * stride:j * stride + cap, :] for j in range(SLAB_ROWS)], axis=1)

    def gate_up(x_ref, i):
        xg = chunks_to_rows(x_ref).astype(BF16)
        g = _dot(xg, wg_ref[i])
        u = _dot(xg, wu_ref[i])
        return (g * (1.0 / (1.0 + jnp.exp(-g))) * u).astype(BF16)

    def down(act, slot_row, aff_row, y_ref, i):
        y = _dot(act, wd_ref[i]) * slot_affinities(slot_row, aff_row)
        for j in range(SLAB_ROWS):
            y_ref[j * stride:j * stride + cap, :] = y[:, j * LANES:(j + 1) * LANES]

    def scatter_add(y_ref, which):
        for c0 in range(0, cap, SCATTER_GROUP):
            rows = [tok_ref[which, parity, c0 + j] for j in range(SCATTER_GROUP)]
            rows = [pl.multiple_of(r, SLAB_ROWS) for r in rows]
            new = [o_ref[pl.ds(rows[j], SLAB_ROWS), :] + y_ref[pl.ds(c0 + j, SLAB_ROWS, stride=stride), :]
                   for j in range(SCATTER_GROUP)]
            for j in range(SCATTER_GROUP):
                o_ref[pl.ds(rows[j], SLAB_ROWS), :] = new[j]

    @pl.when((pl.program_id(0) == 0) & (pl.program_id(1) == 0))
    def _first_step():
        moved = []
        gather(slot_a_ref[0], xa_ref, 0, 0, moved)
        gather(slot_b_ref[0], xb_ref, 1, 0, moved)

    @pl.when(pl.program_id(1) == 0)
    def _first_of_sequence():
        o_ref[...] = jnp.zeros_like(o_ref)

    moved = []
    act_a = gate_up(xa_ref, 0)
    gather(next_a_ref[0], xa_ref, 0, 1 - parity, moved)
    act_b = gate_up(xb_ref, 1)
    gather(next_b_ref[0], xb_ref, 1, 1 - parity, moved)
    down(act_a, slot_a_ref[0], aff_a_ref[0], ya_ref, 0)
    down(act_b, slot_b_ref[0], aff_b_ref[0], yb_ref, 1)
    scatter_add(ya_ref, 0)
    scatter_add(yb_ref, 1)


def _moe(slot, aff, h2, wg, wu, wd, cap):
    B, E, S = aff.shape
    assert E % 2 == 0 and (E // 2) % 2 == 0 and h2.shape == (B * S * SLAB_ROWS, LANES)
    n_steps = B * E // 2
    last = B * E - 1
    slot3 = slot.reshape(B * E, 1, S)
    aff3 = aff.reshape(B * E, 1, S)
    row = lambda d: pl.BlockSpec((1, 1, S), lambda b, k: (jnp.minimum(b * E + 2 * k + d, last), 0, 0))
    cur_a, cur_b, nxt_a, nxt_b = row(0), row(1), row(2), row(3)
    tok = pl.BlockSpec((S * SLAB_ROWS, LANES), lambda b, k: (b, 0))
    ahead = pl.BlockSpec((S * SLAB_ROWS, LANES),
                         lambda b, k: (jnp.minimum(b * (E // 2) + k + 1, n_steps - 1) // (E // 2), 0))
    tile = pltpu.VMEM((SLAB_ROWS * (cap + SLAB_ROWS), LANES), F32)
    wspec = lambda w: pl.BlockSpec((2,) + w.shape[1:], lambda b, k: (k, 0, 0))
    return pl.pallas_call(
        functools.partial(_moe_kernel, cap),
        grid=(B, E // 2),
        in_specs=[cur_a, cur_a, cur_b, cur_b, nxt_a, nxt_b, ahead, wspec(wg), wspec(wu), wspec(wd)],
        out_specs=tok,
        out_shape=jax.ShapeDtypeStruct(h2.shape, F32),
        scratch_shapes=[tile, tile, tile, tile, pltpu.SMEM((2, 2, cap), jnp.int32)],
        compiler_params=_cparams(2),
        name="moe",
    )(slot3, aff3, slot3, aff3, slot3, slot3, h2, wg, wu, wd)


def _ple_final_kernel(x1_ref, moe_ref, p_ref, gp_ref, wg_ref, wp_ref, gf_ref, o_ref):
    rows = x1_ref.shape[0]
    moe = jnp.concatenate([moe_ref[pl.ds(j, rows, stride=SLAB_ROWS), :] for j in range(SLAB_ROWS)], axis=1)
    x2 = x1_ref[...] + moe
    z = _dot(_rms(x2, gp_ref[...]).astype(BF16), wg_ref[...])
    gate = 1.0 / (1.0 + jnp.exp(-z))
    x3 = x2 + gate * _dot(p_ref[...].astype(BF16), wp_ref[...])
    o_ref[...] = _rms(x3, gf_ref[...])


def _ple_final(x1, moe, pf, g_ple, w_gate, w_proj, g_final):
    T = x1.shape[0]
    row = lambda n: pl.BlockSpec((TM, n), lambda i: (i, 0))
    full = lambda a: pl.BlockSpec(a.shape, lambda i: (0, 0))
    return pl.pallas_call(
        _ple_final_kernel,
        grid=(T // TM,),
        in_specs=[row(D_MODEL), pl.BlockSpec((TM * SLAB_ROWS, LANES), lambda i: (i, 0)), row(PLE_DIM),
                  full(g_ple), full(w_gate), full(w_proj), full(g_final)],
        out_specs=row(D_MODEL),
        out_shape=jax.ShapeDtypeStruct((T, D_MODEL), F32),
        compiler_params=_cparams(1),
        name="ple_final",
    )(x1, moe, pf, g_ple, w_gate, w_proj, g_final)


def _prep_in_proj_weight(w_in):
    D = w_in.shape[0]
    t1, t2 = w_in[:, 1152:1168], w_in[:, 1168:1184]
    cols = [w_in[:, 0:1152], t1, t2, t2, t1, jnp.zeros((D, 64), F32)]
    return jnp.concatenate(cols, axis=1).astype(BF16)


def _prep_mla_weights(w_uq, w_ukv):
    per_q = B_NOPE_DIM + B_ROPE_DIM
    w3 = w_uq.reshape(Q_RANK, B_HEADS, per_q)
    pad = LANES - per_q
    wq = jnp.pad(w3, ((0, 0), (0, 0), (0, pad))).reshape(Q_RANK, B_HEADS * LANES)
    t1 = w3[:, :, B_NOPE_DIM:B_NOPE_DIM + HALF_ROPE]
    t2 = w3[:, :, B_NOPE_DIM + HALF_ROPE:]
    wqr = jnp.concatenate([jnp.zeros((Q_RANK, B_HEADS, B_NOPE_DIM), F32), t2, t1,
                           jnp.zeros((Q_RANK, B_HEADS, pad), F32)], axis=2).reshape(Q_RANK, B_HEADS * LANES)
    w4 = w_ukv.reshape(KV_RANK, B_HEADS, B_NOPE_DIM + B_V_DIM)
    zk = jnp.zeros((KV_RANK, B_HEADS, LANES - B_NOPE_DIM), F32)
    wk = jnp.concatenate([w4[:, :, :B_NOPE_DIM], zk], axis=2).reshape(KV_RANK, B_HEADS * LANES)
    wv = w4[:, :, B_NOPE_DIM:].reshape(KV_RANK, B_HEADS * B_V_DIM)
    wkv = jnp.concatenate([wk, wv], axis=1)
    return wq.astype(BF16), wqr.astype(BF16), wkv.astype(BF16)


def kernel(x, p, positions, rel_bias, norm_mix_g, w_in, sink, g_cq, g_ckv, w_uq, w_ukv, g_out_a, g_out_b, w_out,
           norm_ffn_g, w_router, w_e_gate, w_e_up, w_e_down, norm_ple_g, w_ple_gate, w_ple_proj, final_norm_g):
    B, S, D = x.shape
    T = B * S
    cap = CAPACITY_FACTOR * S // N_EXPERTS
    xf = x.reshape(T, D)
    cos_d, sin_d = _rope_tables(positions)
    assert w_in.shape[0] == 1, "single-layer block: the final norm is fused into the last kernel"
    i = 0
    w1 = _prep_in_proj_weight(w_in[i])
    wq, wqr, wkv = _prep_mla_weights(w_uq[i], w_ukv[i])
    qa, ka, va, qm, km, vm = _in_proj(
        xf, norm_mix_g[i].reshape(1, D), w1, g_cq[i].reshape(1, -1), g_ckv[i].reshape(1, -1),
        wq, wqr, wkv, cos_d, sin_d)
    ya = _window_attn(qa, ka, va, rel_bias, sink[i], g_out_a[i].reshape(1, -1), B, S)
    yb, (wg, wu, wd) = _mla_attn(qm, km, vm, g_out_b[i].reshape(1, -1), B, S,
                                 (w_e_gate[i], w_e_up[i], w_e_down[i]))
    x1, h2, aff = _out_proj(xf, ya, yb, w_out[i].astype(BF16), norm_ffn_g[i].reshape(1, D),
                            w_router[i].T, B, S)
    slot = _route(aff, cap)
    moe = _moe(slot, aff, h2, wg, wu, wd, cap)
    out = _ple_final(x1, moe, p[i].reshape(T, -1), norm_ple_g[i].reshape(1, D),
                     w_ple_gate[i].astype(BF16), w_ple_proj[i].astype(BF16), final_norm_g.reshape(1, D))
    return out.reshape(B, S, D)
```

```python
import functools

import numpy as np
import jax
import jax.numpy as jnp
from jax import lax
from jax.experimental import pallas as pl
from jax.experimental.pallas import tpu as pltpu

F32 = jnp.float32
BF16 = jnp.bfloat16

D_MODEL = 1024
BLK = 128
WINDOW = 128
A_HEADS = 8
A_KV_HEADS = 2
A_HEAD_DIM = 64
B_HEADS = 8
B_NOPE_DIM = 64
B_ROPE_DIM = 32
B_V_DIM = 64
Q_RANK = 256
KV_RANK = 128
ROPE_THETA = 10000.0
NUM_BUCKETS = 32
MAX_DISTANCE = 128
N_EXPERTS = 16
EXPERT_FF = 512
CAPACITY_FACTOR = 2
PLE_DIM = 256
EPS = 1e-6

LANES = 128
HALF_ROPE = B_ROPE_DIM // 2
MASKED = -1e30
LOG2E = 1.4426950408889634
VMEM_LIMIT = 56 * 1024 * 1024
TM = 1024
TQ = 1024
MLA_ROWS = 128
WIN_QBLOCKS = 8
GATHER_GROUP = 8
SCATTER_GROUP = 8
SLAB_ROWS = D_MODEL // LANES


def _cparams(n_axes):
    return pltpu.CompilerParams(dimension_semantics=("arbitrary",) * n_axes,
                                vmem_limit_bytes=VMEM_LIMIT)


def _rms(x, g):
    return x * lax.rsqrt(jnp.mean(x * x, axis=-1, keepdims=True) + EPS) * g


def _dot(a, b):
    return jnp.dot(a, b, preferred_element_type=F32)


def _dot_nt(a, b):
    return lax.dot_general(a, b, (((1,), (1,)), ((), ())), preferred_element_type=F32)


def _dot_tn(a, b):
    return lax.dot_general(a, b, (((0,), (0,)), ((), ())), preferred_element_type=F32)


ROPE_GROUPS = LANES // HALF_ROPE


def _rope_table_kernel(pos_ref, invf_ref, cos_ref, sin_ref):
    ang = pos_ref[...] * invf_ref[...]
    cos_ref[...] = jnp.cos(ang)
    sin_ref[...] = jnp.sin(ang)


def _rope_tables(positions):
    T = positions.size
    rows = T // ROPE_GROUPS
    inv_freq = 1.0 / (ROPE_THETA ** (jnp.arange(0, B_ROPE_DIM, 2, dtype=F32) / B_ROPE_DIM))
    pos = jnp.repeat(positions.astype(F32).reshape(ROPE_GROUPS, rows).T, HALF_ROPE, axis=1)
    invf = jnp.tile(inv_freq, ROPE_GROUPS).reshape(1, LANES)
    spec = pl.BlockSpec((rows, LANES), lambda i: (0, 0))
    return pl.pallas_call(
        _rope_table_kernel,
        grid=(1,),
        in_specs=[spec, pl.BlockSpec((1, LANES), lambda i: (0, 0))],
        out_specs=[spec] * 2,
        out_shape=[jax.ShapeDtypeStruct((rows, LANES), F32)] * 2,
        compiler_params=_cparams(1),
        name="rope_tables",
    )(pos, invf)


def _expand_rope(cd, sd, group):
    lane = lax.broadcasted_iota(jnp.int32, (1, LANES), 1)
    own = (lane >= group * HALF_ROPE) & (lane < (group + 1) * HALF_ROPE)
    xc = jnp.where(own, cd, 0.0)
    xs = jnp.where(own, sd, 0.0)
    shift = HALF_ROPE
    while shift < LANES:
        xc = xc + pltpu.roll(xc, shift, axis=1)
        xs = xs + pltpu.roll(xs, shift, axis=1)
        shift *= 2
    in_t1 = (lane >= B_NOPE_DIM) & (lane < B_NOPE_DIM + HALF_ROPE)
    in_t2 = (lane >= B_NOPE_DIM + HALF_ROPE) & (lane < B_NOPE_DIM + B_ROPE_DIM)
    c = jnp.where(lane < B_NOPE_DIM, 1.0, jnp.where(in_t1 | in_t2, xc, 0.0))
    s = jnp.where(in_t1, -xs, jnp.where(in_t2, xs, 0.0))
    return c, s


C_QA = 0
C_KA = C_QA + 512
C_VA = C_KA + LANES
C_CQ = C_VA + LANES
C_CKV = C_CQ + Q_RANK
C_KR = C_CKV + KV_RANK
C_END = C_KR + LANES
MLA_K_COLS = B_HEADS * LANES


def _in_proj_kernel(tiles_per_group, x_ref, g_ref, w1_ref, gcq_ref, gckv_ref, wq_ref, wqr_ref, wkv_ref,
                    cos_ref, sin_ref, qa_ref, ka_ref, va_ref, qm_ref, km_ref, vm_ref):
    h = _rms(x_ref[...], g_ref[...]).astype(BF16)
    z = _dot(h, w1_ref[...])
    qa_ref[...] = (z[:, C_QA:C_KA] * (A_HEAD_DIM ** -0.5 * LOG2E)).astype(BF16)
    ka_ref[...] = z[:, C_KA:C_VA].astype(BF16)
    va_ref[...] = z[:, C_VA:C_CQ].astype(BF16)
    cqn = _rms(z[:, C_CQ:C_CKV], gcq_ref[...]).astype(BF16)
    ckvn = _rms(z[:, C_CKV:C_KR], gckv_ref[...]).astype(BF16)
    krt = z[:, C_KR:C_END]

    lane = lax.broadcasted_iota(jnp.int32, (1, LANES), 1)
    c_full, s_rope = _expand_rope(cos_ref[...], sin_ref[...], pl.program_id(0) // tiles_per_group)
    c_rope = jnp.where(lane < B_NOPE_DIM, 0.0, c_full)

    q1 = _dot(cqn, wq_ref[...])
    q2 = _dot(cqn, wqr_ref[...])
    scale = (B_NOPE_DIM + B_ROPE_DIM) ** -0.5 * LOG2E
    for hd in range(B_HEADS):
        sl = slice(hd * LANES, (hd + 1) * LANES)
        qm_ref[:, sl] = ((q1[:, sl] * c_full + q2[:, sl] * s_rope) * scale).astype(BF16)

    kv = _dot(ckvn, wkv_ref[...])
    kr_part = pltpu.roll(krt, 64, axis=1) * c_rope + pltpu.roll(krt, 32, axis=1) * s_rope
    for hd in range(B_HEADS):
        sl = slice(hd * LANES, (hd + 1) * LANES)
        km_ref[:, sl] = (kv[:, sl] + kr_part).astype(BF16)
    vm_ref[...] = kv[:, MLA_K_COLS:].astype(BF16)


def _in_proj(xf, g_mix, w1, g_cq, g_ckv, wq, wqr, wkv, cos_d, sin_d):
    T = xf.shape[0]
    tiles_per_group = cos_d.shape[0] // TM
    assert tiles_per_group * TM == cos_d.shape[0]
    row = lambda n: pl.BlockSpec((TM, n), lambda i: (i, 0))
    full = lambda a: pl.BlockSpec(a.shape, lambda i: (0, 0))
    dense = pl.BlockSpec((TM, LANES), lambda i: (i % tiles_per_group, 0))
    outs = [512, LANES, LANES, MLA_K_COLS, MLA_K_COLS, B_HEADS * B_V_DIM]
    return pl.pallas_call(
        functools.partial(_in_proj_kernel, tiles_per_group),
        grid=(T // TM,),
        in_specs=[row(D_MODEL), full(g_mix), full(w1), full(g_cq), full(g_ckv), full(wq), full(wqr), full(wkv),
                  dense, dense],
        out_specs=[row(n) for n in outs],
        out_shape=[jax.ShapeDtypeStruct((T, n), BF16) for n in outs],
        compiler_params=_cparams(1),
        name="in_proj",
    )(xf, g_mix, w1, g_cq, g_ckv, wq, wqr, wkv, cos_d, sin_d)


def _bucket_map():
    qi = np.arange(BLK)[:, None]
    kj = np.arange(3 * BLK)[None, :]
    rel = kj - BLK - qi
    n = np.abs(rel)
    half = NUM_BUCKETS // 2
    max_exact = half // 2
    thresholds = [int(np.ceil(max_exact * 2 ** (k / 2) - 1e-9)) for k in range(1, half - max_exact)]
    large = max_exact + sum((n >= t).astype(np.int64) for t in thresholds)
    large = np.minimum(large, half - 1)
    bucket = np.where(rel > 0, half, 0) + np.where(n < max_exact, n, large)
    return np.where(n <= WINDOW, bucket, -1).astype(np.int32)


def _window_attn_kernel(n_steps, relb_ref, sink_ref, bmap_ref, g_ref, q_ref, kp_ref, kc_ref, kn_ref,
                        vp_ref, vc_ref, vn_ref, o_ref, bias_ref):
    b = pl.program_id(0)
    i = pl.program_id(1)

    @pl.when((b == 0) & (i == 0))
    def _build_bias():
        bmap = bmap_ref[...]
        col = lax.broadcasted_iota(jnp.int32, (1, 3 * BLK), 1)
        for hd in range(A_HEADS):
            bias_ref[1, hd] = jnp.full((BLK, 3 * BLK), MASKED, F32)
        for bk in range(NUM_BUCKETS):
            m = bmap == bk
            for hd in range(A_HEADS):
                bias_ref[1, hd] = jnp.where(m, relb_ref[bk, hd] * LOG2E, bias_ref[1, hd])
        for hd in range(A_HEADS):
            bias_ref[0, hd] = jnp.where(col < BLK, MASKED, bias_ref[1, hd])
            bias_ref[2, hd] = jnp.where(col >= 2 * BLK, MASKED, bias_ref[1, hd])

    lane = lax.broadcasted_iota(jnp.int32, (1, LANES), 1)
    lo = lane < A_HEAD_DIM
    ones_lo = jnp.broadcast_to(jnp.where(lo, 1.0, 0.0).astype(BF16), (3 * BLK, LANES))
    ones_hi = jnp.broadcast_to(jnp.where(lo, 0.0, 1.0).astype(BF16), (3 * BLK, LANES))

    def key_blocks(p_ref, c_ref, n_ref):
        return [p_ref[...]] + [c_ref[c * BLK:(c + 1) * BLK, :] for c in range(WIN_QBLOCKS)] + [n_ref[...]]

    def lane_swap(t):
        return pltpu.bitcast(pltpu.roll(pltpu.bitcast(t, jnp.int32), LANES // 2, axis=1), BF16)

    kdup = [[], []]
    v_even = [[], []]
    v_odd = [[], []]
    for kt, vt in zip(key_blocks(kp_ref, kc_ref, kn_ref), key_blocks(vp_ref, vc_ref, vn_ref)):
        ks, vs, zero = lane_swap(kt), lane_swap(vt), jnp.zeros_like(vt)
        kdup[0].append(jnp.where(lo, kt, ks))
        kdup[1].append(jnp.where(lo, ks, kt))
        v_even[0].append(jnp.where(lo, vt, zero))
        v_odd[0].append(jnp.where(lo, zero, vs))
        v_even[1].append(jnp.where(lo, vs, zero))
        v_odd[1].append(jnp.where(lo, zero, vt))

    for c in range(WIN_QBLOCKS):
        rows = slice(c * BLK, (c + 1) * BLK)
        if c == 0:
            variant = jnp.where(i == 0, 0, 1)
        elif c == WIN_QBLOCKS - 1:
            variant = jnp.where(i == n_steps - 1, 2, 1)
        else:
            variant = 1
        probs, sink_terms = [], []
        for hd in range(A_HEADS):
            g = hd // (A_HEADS // A_KV_HEADS)
            qt = q_ref[rows, (hd // 2) * LANES:(hd // 2 + 1) * LANES]
            qm = jnp.where(lo if hd % 2 == 0 else ~lo, qt, jnp.zeros_like(qt))
            kcat = jnp.concatenate(kdup[g][c:c + 3], axis=0)
            s = _dot_nt(qm, kcat) + bias_ref[variant, hd]
            sk = sink_ref[hd] * LOG2E
            m = jnp.maximum(jnp.max(s, axis=-1, keepdims=True), sk)
            probs.append(jnp.exp2(s - m).astype(BF16))
            sink_terms.append(jnp.exp2(sk - m))

        outs = []
        for j in range(A_HEADS // 2):
            g = (2 * j) // (A_HEADS // A_KV_HEADS)
            v_e = jnp.concatenate(v_even[g][c:c + 3], axis=0)
            v_o = jnp.concatenate(v_odd[g][c:c + 3], axis=0)
            o = (_dot(probs[2 * j], jnp.concatenate([v_e, ones_lo], axis=1))
                 + _dot(probs[2 * j + 1], jnp.concatenate([v_o, ones_hi], axis=1)))
            denom = o[:, LANES:] + jnp.where(lo, sink_terms[2 * j], sink_terms[2 * j + 1])
            outs.append(o[:, :LANES] / denom)
        ya = jnp.concatenate(outs, axis=1)
        o_ref[rows, :] = _rms(ya, g_ref[...]).astype(BF16)


def _window_attn(qa, ka, va, rel_bias, sink, g_out_a, B, S):
    nb = S // BLK
    n_steps = nb // WIN_QBLOCKS
    assert nb >= 2, "first and last query blocks use distinct edge masks"
    bmap = jnp.asarray(_bucket_map())
    smem = pl.BlockSpec(memory_space=pltpu.SMEM)
    cur = lambda n: pl.BlockSpec((WIN_QBLOCKS * BLK, n), lambda b, i: (b * n_steps + i, 0))
    prv = lambda n: pl.BlockSpec((BLK, n), lambda b, i: (b * nb + jnp.maximum(i * WIN_QBLOCKS - 1, 0), 0))
    nxt = lambda n: pl.BlockSpec((BLK, n), lambda b, i: (b * nb + jnp.minimum((i + 1) * WIN_QBLOCKS, nb - 1), 0))
    return pl.pallas_call(
        functools.partial(_window_attn_kernel, n_steps),
        grid=(B, n_steps),
        in_specs=[smem, smem,
                  pl.BlockSpec((BLK, 3 * BLK), lambda b, i: (0, 0)),
                  pl.BlockSpec((1, 512), lambda b, i: (0, 0)),
                  cur(512), prv(LANES), cur(LANES), nxt(LANES), prv(LANES), cur(LANES), nxt(LANES)],
        out_specs=cur(512),
        out_shape=jax.ShapeDtypeStruct((B * S, 512), BF16),
        scratch_shapes=[pltpu.VMEM((3, A_HEADS, BLK, 3 * BLK), F32)],
        compiler_params=_cparams(2),
        name="window_attn",
    )(rel_bias, sink, bmap, g_out_a, qa, ka, ka, ka, va, va, va)


def _mla_attn_kernel(n_cast, g_ref, q_ref, k_ref, v_ref, *rest):
    o_ref = rest[n_cast]
    for src, dst in zip(rest[:n_cast], rest[n_cast + 1:]):
        dst[...] = src[...].astype(BF16)
    S = k_ref.shape[0]
    lane = lax.broadcasted_iota(jnp.int32, (1, LANES), 1)
    lo = lane < B_V_DIM
    ones_lo = jnp.broadcast_to(jnp.where(lo, 1.0, 0.0).astype(BF16), (S, LANES))
    ones_hi = jnp.broadcast_to(jnp.where(lo, 0.0, 1.0).astype(BF16), (S, LANES))
    v_pairs = []
    for j in range(B_HEADS // 2):
        vt = v_ref[:, j * LANES:(j + 1) * LANES]
        zero = jnp.zeros_like(vt)
        v_pairs.append((jnp.concatenate([jnp.where(lo, vt, zero), ones_lo], axis=1),
                        jnp.concatenate([jnp.where(lo, zero, vt), ones_hi], axis=1)))
    for r in range(q_ref.shape[0] // MLA_ROWS):
        rows = slice(r * MLA_ROWS, (r + 1) * MLA_ROWS)
        outs = []
        for j in range(B_HEADS // 2):
            ps = []
            for hd in (2 * j, 2 * j + 1):
                sl = slice(hd * LANES, (hd + 1) * LANES)
                s = _dot_nt(q_ref[rows, sl], k_ref[:, sl])
                m = jnp.max(s, axis=-1, keepdims=True)
                ps.append(jnp.exp2(s - m).astype(BF16))
            o = _dot(ps[0], v_pairs[j][0]) + _dot(ps[1], v_pairs[j][1])
            outs.append(o[:, :LANES] / o[:, LANES:])
        yb = jnp.concatenate(outs, axis=1)
        o_ref[rows, :] = _rms(yb, g_ref[...]).astype(BF16)


def _mla_attn(qm, km, vm, g_out_b, B, S, cast_weights):
    nq = S // TQ
    steps = B * nq
    qspec = lambda n: pl.BlockSpec((TQ, n), lambda b, i: (b * nq + i, 0))
    kspec = lambda n: pl.BlockSpec((S, n), lambda b, i: (b, 0))
    flat = [w.reshape(-1, w.shape[-1]) for w in cast_weights]
    slab = lambda w: pl.BlockSpec((w.shape[0] // steps, w.shape[1]), lambda b, i: (b * nq + i, 0))
    outs = pl.pallas_call(
        functools.partial(_mla_attn_kernel, len(flat)),
        grid=(B, nq),
        in_specs=[pl.BlockSpec((1, 512), lambda b, i: (0, 0)), qspec(1024), kspec(1024), kspec(512)]
                 + [slab(w) for w in flat],
        out_specs=[qspec(512)] + [slab(w) for w in flat],
        out_shape=[jax.ShapeDtypeStruct((B * S, 512), BF16)]
                  + [jax.ShapeDtypeStruct(w.shape, BF16) for w in flat],
        compiler_params=_cparams(2),
        name="mla_attn",
    )(g_out_b, qm, km, vm, *flat)
    return outs[0], [o.reshape(w.shape) for o, w in zip(outs[1:], cast_weights)]


def _out_proj_kernel(x_ref, ya_ref, yb_ref, wo_ref, g_ref, wr_ref, x1_ref, h2_ref, aff_ref):
    x1 = x_ref[...] + _dot(ya_ref[...], wo_ref[0:512, :]) + _dot(yb_ref[...], wo_ref[512:1024, :])
    x1_ref[...] = x1
    h2 = _rms(x1, g_ref[...])
    h2b = h2.astype(BF16)
    h2r = h2b.astype(F32)
    for j in range(SLAB_ROWS):
        h2_ref[pl.ds(j, h2r.shape[0], stride=SLAB_ROWS), :] = h2r[:, j * LANES:(j + 1) * LANES]
    logits = _dot_nt(wr_ref[...].astype(BF16), h2b)
    m = jnp.max(logits, axis=0, keepdims=True)
    e = jnp.exp(logits - m)
    aff_ref[0] = e / jnp.sum(e, axis=0, keepdims=True)


def _out_proj(xf, ya, yb, w_out, g_ffn, w_router_t, B, S):
    T = xf.shape[0]
    per_b = S // TM
    row = lambda n: pl.BlockSpec((TM, n), lambda i: (i, 0))
    full = lambda a: pl.BlockSpec(a.shape, lambda i: (0, 0))
    return pl.pallas_call(
        _out_proj_kernel,
        grid=(T // TM,),
        in_specs=[row(D_MODEL), row(512), row(512), full(w_out), full(g_ffn), full(w_router_t)],
        out_specs=[row(D_MODEL), pl.BlockSpec((TM * SLAB_ROWS, LANES), lambda i: (i, 0)),
                   pl.BlockSpec((1, N_EXPERTS, TM), lambda i: (i // per_b, 0, i % per_b))],
        out_shape=[jax.ShapeDtypeStruct((T, D_MODEL), F32), jax.ShapeDtypeStruct((T * SLAB_ROWS, LANES), F32),
                   jax.ShapeDtypeStruct((B, N_EXPERTS, S), F32)],
        compiler_params=_cparams(1),
        name="out_proj",
    )(xf, ya, yb, w_out, g_ffn, w_router_t)


CHUNK = 256
ROUTE_LOG_STEPS = 12
ROUTE_LIN_STEPS = 32


def _prefix_count(flags_f32, tri):
    S = flags_f32.shape[1]
    carry = jnp.zeros((flags_f32.shape[0], 1), F32)
    parts = []
    for c in range(S // CHUNK):
        blk = flags_f32[:, c * CHUNK:(c + 1) * CHUNK]
        parts.append(_dot(blk.astype(BF16), tri) + carry)
        carry = carry + jnp.sum(blk, axis=-1, keepdims=True)
    return jnp.concatenate(parts, axis=1)


def _route_kernel(cap, aff_ref, slot_ref):
    aff = aff_ref[...]
    rows = aff.shape[0]

    def enough(pivot):
        return jnp.sum(jnp.where(aff >= pivot, 1.0, 0.0), axis=-1, keepdims=True) >= cap

    def log_body(_, st):
        lo, hi, elo, ehi = st
        mid = 0.5 * (elo + ehi)
        pivot = jnp.exp2(mid)
        ok = enough(pivot)
        return (jnp.where(ok, pivot, lo), jnp.where(ok, hi, pivot), jnp.where(ok, mid, elo), jnp.where(ok, ehi, mid))

    def lin_body(_, st):
        lo, hi = st
        pivot = 0.5 * (lo + hi)
        ok = enough(pivot)
        return jnp.where(ok, pivot, lo), jnp.where(ok, hi, pivot)

    col = lambda v: jnp.full((rows, 1), v, F32)
    lo, hi, _, _ = lax.fori_loop(0, ROUTE_LOG_STEPS, log_body, (col(0.0), col(2.0), col(-152.0), col(1.0)))
    lo, hi = lax.fori_loop(0, ROUTE_LIN_STEPS, lin_body, (lo, hi))

    gt = jnp.where(aff >= hi, 1.0, 0.0)
    eq = jnp.where(aff >= lo, 1.0, 0.0) - gt
    need = cap - jnp.sum(gt, axis=-1, keepdims=True)
    r = lax.broadcasted_iota(jnp.int32, (CHUNK, CHUNK), 0)
    c = lax.broadcasted_iota(jnp.int32, (CHUNK, CHUNK), 1)
    tri = jnp.where(r < c, 1.0, 0.0).astype(BF16)
    sel = gt + eq * jnp.where(_prefix_count(eq, tri) < need, 1.0, 0.0)
    slot = _prefix_count(sel, tri)
    slot_ref[...] = jnp.where(sel > 0.5, slot, -1.0)


def _route(aff, cap):
    B, E, S = aff.shape
    spec = pl.BlockSpec((B * E, S), lambda i: (0, 0))
    return pl.pallas_call(
        functools.partial(_route_kernel, cap),
        grid=(1,),
        in_specs=[spec],
        out_specs=spec,
        out_shape=jax.ShapeDtypeStruct((B * E, S), F32),
        compiler_params=_cparams(1),
        name="route",
    )(aff.reshape(B * E, S))


def _moe_kernel(cap, slot_a_ref, aff_a_ref, slot_b_ref, aff_b_ref, next_a_ref, next_b_ref, h_ref,
                wg_ref, wu_ref, wd_ref, o_ref, xa_ref, xb_ref, ya_ref, yb_ref, tok_ref):
    S = h_ref.shape[0] // SLAB_ROWS
    stride = cap + SLAB_ROWS
    parity = pl.program_id(1) % 2
    slot_ids = lax.broadcasted_iota(jnp.int32, (cap, 1), 0).astype(F32)
    tok_ids = lax.broadcasted_iota(jnp.int32, (1, S), 1).astype(F32)

    def slot_affinities(slot_row, aff_row):
        return jnp.sum(jnp.where(slot_ids == slot_row, aff_row, 0.0), axis=-1, keepdims=True)

    def gather(slot_row, x_ref, which, par, done):
        sel = slot_ids == slot_row
        tok = jnp.sum(jnp.where(sel, tok_ids, 0.0), axis=-1, keepdims=True).astype(jnp.int32)
        tok = jnp.clip(tok, 0, S - 1)
        for grp in range(cap // GATHER_GROUP):
            ids = tok[grp * GATHER_GROUP:(grp + 1) * GATHER_GROUP, :]
            if len(done) >= 2:
                ids = jnp.minimum(ids, done[-2] + S)
            for j in range(GATHER_GROUP):
                c = grp * GATHER_GROUP + j
                t = ids[j, 0]
                row = pl.multiple_of(t * SLAB_ROWS, SLAB_ROWS)
                tok_ref[which, par, c] = row
                x_ref[pl.ds(c, SLAB_ROWS, stride=stride), :] = h_ref[pl.ds(row, SLAB_ROWS), :]
            done.append(t)

    def chunks_to_rows(tile_ref):
        return jnp.concatenate([tile_ref[j * stride:j * stride + cap, :] for j in range(SLAB_ROWS)], axis=1)

    def gate_up(x_ref, i):
        xg = chunks_to_rows(x_ref).astype(BF16)
        g = _dot(xg, wg_ref[i])
        u = _dot(xg, wu_ref[i])
        return (g * (1.0 / (1.0 + jnp.exp(-g))) * u).astype(BF16)

    def down(act, slot_row, aff_row, y_ref, i):
        y = _dot(act, wd_ref[i]) * slot_affinities(slot_row, aff_row)
        for j in range(SLAB_ROWS):
            y_ref[j * stride:j * stride + cap, :] = y[:, j * LANES:(j + 1) * LANES]

    def scatter_add(y_ref, which):
        for c0 in range(0, cap, SCATTER_GROUP):
            rows = [tok_ref[which, parity, c0 + j] for j in range(SCATTER_GROUP)]
            rows = [pl.multiple_of(r, SLAB_ROWS) for r in rows]
            new = [o_ref[pl.ds(rows[j], SLAB_ROWS), :] + y_ref[pl.ds(c0 + j, SLAB_ROWS, stride=stride), :]
                   for j in range(SCATTER_GROUP)]
            for j in range(SCATTER_GROUP):
                o_ref[pl.ds(rows[j], SLAB_ROWS), :] = new[j]

    @pl.when((pl.program_id(0) == 0) & (pl.program_id(1) == 0))
    def _first_step():
        moved = []
        gather(slot_a_ref[0], xa_ref, 0, 0, moved)
        gather(slot_b_ref[0], xb_ref, 1, 0, moved)

    @pl.when(pl.program_id(1) == 0)
    def _first_of_sequence():
        o_ref[...] = jnp.zeros_like(o_ref)

    moved = []
    act_a = gate_up(xa_ref, 0)
    gather(next_a_ref[0], xa_ref, 0, 1 - parity, moved)
    act_b = gate_up(xb_ref, 1)
    gather(next_b_ref[0], xb_ref, 1, 1 - parity, moved)
    down(act_a, slot_a_ref[0], aff_a_ref[0], ya_ref, 0)
    down(act_b, slot_b_ref[0], aff_b_ref[0], yb_ref, 1)
    scatter_add(ya_ref, 0)
    scatter_add(yb_ref, 1)


def _moe(slot, aff, h2, wg, wu, wd, cap):
    B, E, S = aff.shape
    assert E % 2 == 0 and (E // 2) % 2 == 0 and h2.shape == (B * S * SLAB_ROWS, LANES)
    n_steps = B * E // 2
    last = B * E - 1
    slot3 = slot.reshape(B * E, 1, S)
    aff3 = aff.reshape(B * E, 1, S)
    row = lambda d: pl.BlockSpec((1, 1, S), lambda b, k: (jnp.minimum(b * E + 2 * k + d, last), 0, 0))
    cur_a, cur_b, nxt_a, nxt_b = row(0), row(1), row(2), row(3)
    tok = pl.BlockSpec((S * SLAB_ROWS, LANES), lambda b, k: (b, 0))
    ahead = pl.BlockSpec((S * SLAB_ROWS, LANES),
                         lambda b, k: (jnp.minimum(b * (E // 2) + k + 1, n_steps - 1) // (E // 2), 0))
    tile = pltpu.VMEM((SLAB_ROWS * (cap + SLAB_ROWS), LANES), F32)
    wspec = lambda w: pl.BlockSpec((2,) + w.shape[1:], lambda b, k: (k, 0, 0))
    return pl.pallas_call(
        functools.partial(_moe_kernel, cap),
        grid=(B, E // 2),
        in_specs=[cur_a, cur_a, cur_b, cur_b, nxt_a, nxt_b, ahead, wspec(wg), wspec(wu), wspec(wd)],
        out_specs=tok,
        out_shape=jax.ShapeDtypeStruct(h2.shape, F32),
        scratch_shapes=[tile, tile, tile, tile, pltpu.SMEM((2, 2, cap), jnp.int32)],
        compiler_params=_cparams(2),
        name="moe",
    )(slot3, aff3, slot3, aff3, slot3, slot3, h2, wg, wu, wd)


def _ple_final_kernel(x1_ref, moe_ref, p_ref, gp_ref, wg_ref, wp_ref, gf_ref, o_ref):
    rows = x1_ref.shape[0]
    moe = jnp.concatenate([moe_ref[pl.ds(j, rows, stride=SLAB_ROWS), :] for j in range(SLAB_ROWS)], axis=1)
    x2 = x1_ref[...] + moe
    z = _dot(_rms(x2, gp_ref[...]).astype(BF16), wg_ref[...])
    gate = 1.0 / (1.0 + jnp.exp(-z))
    x3 = x2 + gate * _dot(p_ref[...].astype(BF16), wp_ref[...])
    o_ref[...] = _rms(x3, gf_ref[...])


def _ple_final(x1, moe, pf, g_ple, w_gate, w_proj, g_final):
    T = x1.shape[0]
    row = lambda n: pl.BlockSpec((TM, n), lambda i: (i, 0))
    full = lambda a: pl.BlockSpec(a.shape, lambda i: (0, 0))
    return pl.pallas_call(
        _ple_final_kernel,
        grid=(T // TM,),
        in_specs=[row(D_MODEL), pl.BlockSpec((TM * SLAB_ROWS, LANES), lambda i: (i, 0)), row(PLE_DIM),
                  full(g_ple), full(w_gate), full(w_proj), full(g_final)],
        out_specs=row(D_MODEL),
        out_shape=jax.ShapeDtypeStruct((T, D_MODEL), F32),
        compiler_params=_cparams(1),
        name="ple_final",
    )(x1, moe, pf, g_ple, w_gate, w_proj, g_final)


def _prep_in_proj_weight(w_in):
    D = w_in.shape[0]
    t1, t2 = w_in[:, 1152:1168], w_in[:, 1168:1184]
    cols = [w_in[:, 0:1152], t1, t2, t2, t1, jnp.zeros((D, 64), F32)]
    return jnp.concatenate(cols, axis=1).astype(BF16)


def _prep_mla_weights(w_uq, w_ukv):
    per_q = B_NOPE_DIM + B_ROPE_DIM
    w3 = w_uq.reshape(Q_RANK, B_HEADS, per_q)
    pad = LANES - per_q
    wq = jnp.pad(w3, ((0, 0), (0, 0), (0, pad))).reshape(Q_RANK, B_HEADS * LANES)
    t1 = w3[:, :, B_NOPE_DIM:B_NOPE_DIM + HALF_ROPE]
    t2 = w3[:, :, B_NOPE_DIM + HALF_ROPE:]
    wqr = jnp.concatenate([jnp.zeros((Q_RANK, B_HEADS, B_NOPE_DIM), F32), t2, t1,
                           jnp.zeros((Q_RANK, B_HEADS, pad), F32)], axis=2).reshape(Q_RANK, B_HEADS * LANES)
    w4 = w_ukv.reshape(KV_RANK, B_HEADS, B_NOPE_DIM + B_V_DIM)
    zk = jnp.zeros((KV_RANK, B_HEADS, LANES - B_NOPE_DIM), F32)
    wk = jnp.concatenate([w4[:, :, :B_NOPE_DIM], zk], axis=2).reshape(KV_RANK, B_HEADS * LANES)
    wv = w4[:, :, B_NOPE_DIM:].reshape(KV_RANK, B_HEADS * B_V_DIM)
    wkv = jnp.concatenate([wk, wv], axis=1)
    return wq.astype(BF16), wqr.astype(BF16), wkv.astype(BF16)


def kernel(x, p, positions, rel_bias, norm_mix_g, w_in, sink, g_cq, g_ckv, w_uq, w_ukv, g_out_a, g_out_b, w_out,
           norm_ffn_g, w_router, w_e_gate, w_e_up, w_e_down, norm_ple_g, w_ple_gate, w_ple_proj, final_norm_g):
    B, S, D = x.shape
    T = B * S
    cap = CAPACITY_FACTOR * S // N_EXPERTS
    xf = x.reshape(T, D)
    cos_d, sin_d = _rope_tables(positions)
    assert w_in.shape[0] == 1, "single-layer block: the final norm is fused into the last kernel"
    i = 0
    w1 = _prep_in_proj_weight(w_in[i])
    wq, wqr, wkv = _prep_mla_weights(w_uq[i], w_ukv[i])
    qa, ka, va, qm, km, vm = _in_proj(
        xf, norm_mix_g[i].reshape(1, D), w1, g_cq[i].reshape(1, -1), g_ckv[i].reshape(1, -1),
        wq, wqr, wkv, cos_d, sin_d)
    ya = _window_attn(qa, ka, va, rel_bias, sink[i], g_out_a[i].reshape(1, -1), B, S)
    yb, (wg, wu, wd) = _mla_attn(qm, km, vm, g_out_b[i].reshape(1, -1), B, S,
                                 (w_e_gate[i], w_e_up[i], w_e_down[i]))
    x1, h2, aff = _out_proj(xf, ya, yb, w_out[i].astype(BF16), norm_ffn_g[i].reshape(1, D),
                            w_router[i].T, B, S)
    slot = _route(aff, cap)
    moe = _moe(slot, aff, h2, wg, wu, wd, cap)
    out = _ple_final(x1, moe, p[i].reshape(T, -1), norm_ple_g[i].reshape(1, D),
                     w_ple_gate[i].astype(BF16), w_ple_proj[i].astype(BF16), final_norm_g.reshape(1, D))
    return out.reshape(B, S, D)
```

```python
import functools

import numpy as np
import jax
import jax.numpy as jnp
from jax import lax
from jax.experimental import pallas as pl
from jax.experimental.pallas import tpu as pltpu

F32 = jnp.float32
BF16 = jnp.bfloat16

D_MODEL = 1024
BLK = 128
WINDOW = 128
A_HEADS = 8
A_KV_HEADS = 2
A_HEAD_DIM = 64
B_HEADS = 8
B_NOPE_DIM = 64
B_ROPE_DIM = 32
B_V_DIM = 64
Q_RANK = 256
KV_RANK = 128
ROPE_THETA = 10000.0
NUM_BUCKETS = 32
MAX_DISTANCE = 128
N_EXPERTS = 16
EXPERT_FF = 512
CAPACITY_FACTOR = 2
PLE_DIM = 256
EPS = 1e-6

LANES = 128
HALF_ROPE = B_ROPE_DIM // 2
MASKED = -1e30
LOG2E = 1.4426950408889634
VMEM_LIMIT = 56 * 1024 * 1024
TM = 1024
TQ = 512
MLA_ROWS = 128
WIN_QBLOCKS = 8
GATHER_GROUP = 8
SCATTER_GROUP = 8
SLAB_ROWS = D_MODEL // LANES


def _cparams(n_axes):
    return pltpu.CompilerParams(dimension_semantics=("arbitrary",) * n_axes,
                                vmem_limit_bytes=VMEM_LIMIT)


def _rms(x, g):
    return x * lax.rsqrt(jnp.mean(x * x, axis=-1, keepdims=True) + EPS) * g


def _dot(a, b):
    return jnp.dot(a, b, preferred_element_type=F32)


def _dot_nt(a, b):
    return lax.dot_general(a, b, (((1,), (1,)), ((), ())), preferred_element_type=F32)


def _dot_tn(a, b):
    return lax.dot_general(a, b, (((0,), (0,)), ((), ())), preferred_element_type=F32)


ROPE_GROUPS = LANES // HALF_ROPE


def _rope_table_kernel(pos_ref, invf_ref, cos_ref, sin_ref):
    ang = pos_ref[...] * invf_ref[...]
    cos_ref[...] = jnp.cos(ang)
    sin_ref[...] = jnp.sin(ang)


def _rope_tables(positions):
    T = positions.size
    rows = T // ROPE_GROUPS
    inv_freq = 1.0 / (ROPE_THETA ** (jnp.arange(0, B_ROPE_DIM, 2, dtype=F32) / B_ROPE_DIM))
    pos = jnp.repeat(positions.astype(F32).reshape(ROPE_GROUPS, rows).T, HALF_ROPE, axis=1)
    invf = jnp.tile(inv_freq, ROPE_GROUPS).reshape(1, LANES)
    spec = pl.BlockSpec((rows, LANES), lambda i: (0, 0))
    return pl.pallas_call(
        _rope_table_kernel,
        grid=(1,),
        in_specs=[spec, pl.BlockSpec((1, LANES), lambda i: (0, 0))],
        out_specs=[spec] * 2,
        out_shape=[jax.ShapeDtypeStruct((rows, LANES), F32)] * 2,
        compiler_params=_cparams(1),
        name="rope_tables",
    )(pos, invf)


def _expand_rope(cd, sd, group):
    lane = lax.broadcasted_iota(jnp.int32, (1, LANES), 1)
    own = (lane >= group * HALF_ROPE) & (lane < (group + 1) * HALF_ROPE)
    xc = jnp.where(own, cd, 0.0)
    xs = jnp.where(own, sd, 0.0)
    shift = HALF_ROPE
    while shift < LANES:
        xc = xc + pltpu.roll(xc, shift, axis=1)
        xs = xs + pltpu.roll(xs, shift, axis=1)
        shift *= 2
    in_t1 = (lane >= B_NOPE_DIM) & (lane < B_NOPE_DIM + HALF_ROPE)
    in_t2 = (lane >= B_NOPE_DIM + HALF_ROPE) & (lane < B_NOPE_DIM + B_ROPE_DIM)
    c = jnp.where(lane < B_NOPE_DIM, 1.0, jnp.where(in_t1 | in_t2, xc, 0.0))
    s = jnp.where(in_t1, -xs, jnp.where(in_t2, xs, 0.0))
    return c, s


C_QA = 0
C_KA = C_QA + 512
C_VA = C_KA + LANES
C_CQ = C_VA + LANES
C_CKV = C_CQ + Q_RANK
C_KR = C_CKV + KV_RANK
C_END = C_KR + LANES
MLA_K_COLS = B_HEADS * LANES


def _in_proj_kernel(tiles_per_group, x_ref, g_ref, w1_ref, gcq_ref, gckv_ref, wq_ref, wqr_ref, wkv_ref,
                    cos_ref, sin_ref, qa_ref, ka_ref, va_ref, qm_ref, km_ref, vm_ref):
    h = _rms(x_ref[...], g_ref[...]).astype(BF16)
    z = _dot(h, w1_ref[...])
    qa_ref[...] = (z[:, C_QA:C_KA] * (A_HEAD_DIM ** -0.5 * LOG2E)).astype(BF16)
    ka_ref[...] = z[:, C_KA:C_VA].astype(BF16)
    va_ref[...] = z[:, C_VA:C_CQ].astype(BF16)
    cqn = _rms(z[:, C_CQ:C_CKV], gcq_ref[...]).astype(BF16)
    ckvn = _rms(z[:, C_CKV:C_KR], gckv_ref[...]).astype(BF16)
    krt = z[:, C_KR:C_END]

    lane = lax.broadcasted_iota(jnp.int32, (1, LANES), 1)
    c_full, s_rope = _expand_rope(cos_ref[...], sin_ref[...], pl.program_id(0) // tiles_per_group)
    c_rope = jnp.where(lane < B_NOPE_DIM, 0.0, c_full)

    q1 = _dot(cqn, wq_ref[...])
    q2 = _dot(cqn, wqr_ref[...])
    scale = (B_NOPE_DIM + B_ROPE_DIM) ** -0.5 * LOG2E
    for hd in range(B_HEADS):
        sl = slice(hd * LANES, (hd + 1) * LANES)
        qm_ref[:, sl] = ((q1[:, sl] * c_full + q2[:, sl] * s_rope) * scale).astype(BF16)

    kv = _dot(ckvn, wkv_ref[...])
    kr_part = pltpu.roll(krt, 64, axis=1) * c_rope + pltpu.roll(krt, 32, axis=1) * s_rope
    for hd in range(B_HEADS):
        sl = slice(hd * LANES, (hd + 1) * LANES)
        km_ref[:, sl] = (kv[:, sl] + kr_part).astype(BF16)
    vm_ref[...] = kv[:, MLA_K_COLS:].astype(BF16)


def _in_proj(xf, g_mix, w1, g_cq, g_ckv, wq, wqr, wkv, cos_d, sin_d):
    T = xf.shape[0]
    tiles_per_group = cos_d.shape[0] // TM
    assert tiles_per_group * TM == cos_d.shape[0]
    row = lambda n: pl.BlockSpec((TM, n), lambda i: (i, 0))
    full = lambda a: pl.BlockSpec(a.shape, lambda i: (0, 0))
    dense = pl.BlockSpec((TM, LANES), lambda i: (i % tiles_per_group, 0))
    outs = [512, LANES, LANES, MLA_K_COLS, MLA_K_COLS, B_HEADS * B_V_DIM]
    return pl.pallas_call(
        functools.partial(_in_proj_kernel, tiles_per_group),
        grid=(T // TM,),
        in_specs=[row(D_MODEL), full(g_mix), full(w1), full(g_cq), full(g_ckv), full(wq), full(wqr), full(wkv),
                  dense, dense],
        out_specs=[row(n) for n in outs],
        out_shape=[jax.ShapeDtypeStruct((T, n), BF16) for n in outs],
        compiler_params=_cparams(1),
        name="in_proj",
    )(xf, g_mix, w1, g_cq, g_ckv, wq, wqr, wkv, cos_d, sin_d)


def _bucket_map():
    qi = np.arange(BLK)[:, None]
    kj = np.arange(3 * BLK)[None, :]
    rel = kj - BLK - qi
    n = np.abs(rel)
    half = NUM_BUCKETS // 2
    max_exact = half // 2
    thresholds = [int(np.ceil(max_exact * 2 ** (k / 2) - 1e-9)) for k in range(1, half - max_exact)]
    large = max_exact + sum((n >= t).astype(np.int64) for t in thresholds)
    large = np.minimum(large, half - 1)
    bucket = np.where(rel > 0, half, 0) + np.where(n < max_exact, n, large)
    return np.where(n <= WINDOW, bucket, -1).astype(np.int32)


def _window_attn_kernel(n_steps, relb_ref, sink_ref, bmap_ref, g_ref, q_ref, kp_ref, kc_ref, kn_ref,
                        vp_ref, vc_ref, vn_ref, o_ref, bias_ref):
    b = pl.program_id(0)
    i = pl.program_id(1)

    @pl.when((b == 0) & (i == 0))
    def _build_bias():
        bmap = bmap_ref[...]
        col = lax.broadcasted_iota(jnp.int32, (1, 3 * BLK), 1)
        for hd in range(A_HEADS):
            bias_ref[1, hd] = jnp.full((BLK, 3 * BLK), MASKED, F32)
        for bk in range(NUM_BUCKETS):
            m = bmap == bk
            for hd in range(A_HEADS):
                bias_ref[1, hd] = jnp.where(m, relb_ref[bk, hd] * LOG2E, bias_ref[1, hd])
        for hd in range(A_HEADS):
            bias_ref[0, hd] = jnp.where(col < BLK, MASKED, bias_ref[1, hd])
            bias_ref[2, hd] = jnp.where(col >= 2 * BLK, MASKED, bias_ref[1, hd])

    lane = lax.broadcasted_iota(jnp.int32, (1, LANES), 1)
    lo = lane < A_HEAD_DIM
    ones_lo = jnp.broadcast_to(jnp.where(lo, 1.0, 0.0).astype(BF16), (3 * BLK, LANES))
    ones_hi = jnp.broadcast_to(jnp.where(lo, 0.0, 1.0).astype(BF16), (3 * BLK, LANES))

    def key_blocks(p_ref, c_ref, n_ref):
        return [p_ref[...]] + [c_ref[c * BLK:(c + 1) * BLK, :] for c in range(WIN_QBLOCKS)] + [n_ref[...]]

    def lane_swap(t):
        return pltpu.bitcast(pltpu.roll(pltpu.bitcast(t, jnp.int32), LANES // 2, axis=1), BF16)

    kdup = [[], []]
    v_even = [[], []]
    v_odd = [[], []]
    for kt, vt in zip(key_blocks(kp_ref, kc_ref, kn_ref), key_blocks(vp_ref, vc_ref, vn_ref)):
        ks, vs, zero = lane_swap(kt), lane_swap(vt), jnp.zeros_like(vt)
        kdup[0].append(jnp.where(lo, kt, ks))
        kdup[1].append(jnp.where(lo, ks, kt))
        v_even[0].append(jnp.where(lo, vt, zero))
        v_odd[0].append(jnp.where(lo, zero, vs))
        v_even[1].append(jnp.where(lo, vs, zero))
        v_odd[1].append(jnp.where(lo, zero, vt))

    for c in range(WIN_QBLOCKS):
        rows = slice(c * BLK, (c + 1) * BLK)
        if c == 0:
            variant = jnp.where(i == 0, 0, 1)
        elif c == WIN_QBLOCKS - 1:
            variant = jnp.where(i == n_steps - 1, 2, 1)
        else:
            variant = 1
        probs, sink_terms = [], []
        for hd in range(A_HEADS):
            g = hd // (A_HEADS // A_KV_HEADS)
            qt = q_ref[rows, (hd // 2) * LANES:(hd // 2 + 1) * LANES]
            qm = jnp.where(lo if hd % 2 == 0 else ~lo, qt, jnp.zeros_like(qt))
            kcat = jnp.concatenate(kdup[g][c:c + 3], axis=0)
            s = _dot_nt(qm, kcat) + bias_ref[variant, hd]
            sk = sink_ref[hd] * LOG2E
            m = jnp.maximum(jnp.max(s, axis=-1, keepdims=True), sk)
            probs.append(jnp.exp2(s - m).astype(BF16))
            sink_terms.append(jnp.exp2(sk - m))

        outs = []
        for j in range(A_HEADS // 2):
            g = (2 * j) // (A_HEADS // A_KV_HEADS)
            v_e = jnp.concatenate(v_even[g][c:c + 3], axis=0)
            v_o = jnp.concatenate(v_odd[g][c:c + 3], axis=0)
            o = (_dot(probs[2 * j], jnp.concatenate([v_e, ones_lo], axis=1))
                 + _dot(probs[2 * j + 1], jnp.concatenate([v_o, ones_hi], axis=1)))
            denom = o[:, LANES:] + jnp.where(lo, sink_terms[2 * j], sink_terms[2 * j + 1])
            outs.append(o[:, :LANES] / denom)
        ya = jnp.concatenate(outs, axis=1)
        o_ref[rows, :] = _rms(ya, g_ref[...]).astype(BF16)


def _window_attn(qa, ka, va, rel_bias, sink, g_out_a, B, S):
    nb = S // BLK
    n_steps = nb // WIN_QBLOCKS
    assert nb >= 2, "first and last query blocks use distinct edge masks"
    bmap = jnp.asarray(_bucket_map())
    smem = pl.BlockSpec(memory_space=pltpu.SMEM)
    cur = lambda n: pl.BlockSpec((WIN_QBLOCKS * BLK, n), lambda b, i: (b * n_steps + i, 0))
    prv = lambda n: pl.BlockSpec((BLK, n), lambda b, i: (b * nb + jnp.maximum(i * WIN_QBLOCKS - 1, 0), 0))
    nxt = lambda n: pl.BlockSpec((BLK, n), lambda b, i: (b * nb + jnp.minimum((i + 1) * WIN_QBLOCKS, nb - 1), 0))
    return pl.pallas_call(
        functools.partial(_window_attn_kernel, n_steps),
        grid=(B, n_steps),
        in_specs=[smem, smem,
                  pl.BlockSpec((BLK, 3 * BLK), lambda b, i: (0, 0)),
                  pl.BlockSpec((1, 512), lambda b, i: (0, 0)),
                  cur(512), prv(LANES), cur(LANES), nxt(LANES), prv(LANES), cur(LANES), nxt(LANES)],
        out_specs=cur(512),
        out_shape=jax.ShapeDtypeStruct((B * S, 512), BF16),
        scratch_shapes=[pltpu.VMEM((3, A_HEADS, BLK, 3 * BLK), F32)],
        compiler_params=_cparams(2),
        name="window_attn",
    )(rel_bias, sink, bmap, g_out_a, qa, ka, ka, ka, va, va, va)


def _mla_attn_kernel(n_cast, g_ref, q_ref, k_ref, v_ref, *rest):
    o_ref = rest[n_cast]
    for src, dst in zip(rest[:n_cast], rest[n_cast + 1:]):
        dst[...] = src[...].astype(BF16)
    S = k_ref.shape[0]
    lane = lax.broadcasted_iota(jnp.int32, (1, LANES), 1)
    lo = lane < B_V_DIM
    ones_lo = jnp.broadcast_to(jnp.where(lo, 1.0, 0.0).astype(BF16), (S, LANES))
    ones_hi = jnp.broadcast_to(jnp.where(lo, 0.0, 1.0).astype(BF16), (S, LANES))
    v_pairs = []
    for j in range(B_HEADS // 2):
        vt = v_ref[:, j * LANES:(j + 1) * LANES]
        zero = jnp.zeros_like(vt)
        v_pairs.append((jnp.concatenate([jnp.where(lo, vt, zero), ones_lo], axis=1),
                        jnp.concatenate([jnp.where(lo, zero, vt), ones_hi], axis=1)))
    for r in range(q_ref.shape[0] // MLA_ROWS):
        rows = slice(r * MLA_ROWS, (r + 1) * MLA_ROWS)
        outs = []
        for j in range(B_HEADS // 2):
            ps = []
            for hd in (2 * j, 2 * j + 1):
                sl = slice(hd * LANES, (hd + 1) * LANES)
                s = _dot_nt(q_ref[rows, sl], k_ref[:, sl])
                m = jnp.max(s, axis=-1, keepdims=True)
                ps.append(jnp.exp2(s - m).astype(BF16))
            o = _dot(ps[0], v_pairs[j][0]) + _dot(ps[1], v_pairs[j][1])
            outs.append(o[:, :LANES] / o[:, LANES:])
        yb = jnp.concatenate(outs, axis=1)
        o_ref[rows, :] = _rms(yb, g_ref[...]).astype(BF16)


def _mla_attn(qm, km, vm, g_out_b, B, S, cast_weights):
    nq = S // TQ
    steps = B * nq
    qspec = lambda n: pl.BlockSpec((TQ, n), lambda b, i: (b * nq + i, 0))
    kspec = lambda n: pl.BlockSpec((S, n), lambda b, i: (b, 0))
    flat = [w.reshape(-1, w.shape[-1]) for w in cast_weights]
    slab = lambda w: pl.BlockSpec((w.shape[0] // steps, w.shape[1]), lambda b, i: (b * nq + i, 0))
    outs = pl.pallas_call(
        functools.partial(_mla_attn_kernel, len(flat)),
        grid=(B, nq),
        in_specs=[pl.BlockSpec((1, 512), lambda b, i: (0, 0)), qspec(1024), kspec(1024), kspec(512)]
                 + [slab(w) for w in flat],
        out_specs=[qspec(512)] + [slab(w) for w in flat],
        out_shape=[jax.ShapeDtypeStruct((B * S, 512), BF16)]
                  + [jax.ShapeDtypeStruct(w.shape, BF16) for w in flat],
        compiler_params=_cparams(2),
        name="mla_attn",
    )(g_out_b, qm, km, vm, *flat)
    return outs[0], [o.reshape(w.shape) for o, w in zip(outs[1:], cast_weights)]


def _mixer_residual(x_ref, ya_ref, yb_ref, wo_ref):
    return x_ref[...] + _dot(ya_ref[...], wo_ref[0:512, :]) + _dot(yb_ref[...], wo_ref[512:1024, :])


def _out_proj_kernel(x_ref, ya_ref, yb_ref, wo_ref, g_ref, wr_ref, h2_ref, aff_ref):
    h2 = _rms(_mixer_residual(x_ref, ya_ref, yb_ref, wo_ref), g_ref[...])
    h2b = h2.astype(BF16)
    h2r = h2b.astype(F32)
    for j in range(SLAB_ROWS):
        h2_ref[pl.ds(j, h2r.shape[0], stride=SLAB_ROWS), :] = h2r[:, j * LANES:(j + 1) * LANES]
    logits = _dot_nt(wr_ref[...].astype(BF16), h2b)
    m = jnp.max(logits, axis=0, keepdims=True)
    e = jnp.exp(logits - m)
    aff_ref[0] = e / jnp.sum(e, axis=0, keepdims=True)


def _out_proj(xf, ya, yb, w_out, g_ffn, w_router_t, B, S):
    T = xf.shape[0]
    per_b = S // TM
    row = lambda n: pl.BlockSpec((TM, n), lambda i: (i, 0))
    full = lambda a: pl.BlockSpec(a.shape, lambda i: (0, 0))
    return pl.pallas_call(
        _out_proj_kernel,
        grid=(T // TM,),
        in_specs=[row(D_MODEL), row(512), row(512), full(w_out), full(g_ffn), full(w_router_t)],
        out_specs=[pl.BlockSpec((TM * SLAB_ROWS, LANES), lambda i: (i, 0)),
                   pl.BlockSpec((1, N_EXPERTS, TM), lambda i: (i // per_b, 0, i % per_b))],
        out_shape=[jax.ShapeDtypeStruct((T * SLAB_ROWS, LANES), F32),
                   jax.ShapeDtypeStruct((B, N_EXPERTS, S), F32)],
        compiler_params=_cparams(1),
        name="out_proj",
    )(xf, ya, yb, w_out, g_ffn, w_router_t)


CHUNK = 256
ROUTE_LOG_STEPS = 12
ROUTE_LIN_STEPS = 32


def _prefix_count(flags_f32, tri):
    S = flags_f32.shape[1]
    carry = jnp.zeros((flags_f32.shape[0], 1), F32)
    parts = []
    for c in range(S // CHUNK):
        blk = flags_f32[:, c * CHUNK:(c + 1) * CHUNK]
        parts.append(_dot(blk.astype(BF16), tri) + carry)
        carry = carry + jnp.sum(blk, axis=-1, keepdims=True)
    return jnp.concatenate(parts, axis=1)


def _route_kernel(cap, aff_ref, slot_ref):
    aff = aff_ref[...]
    rows = aff.shape[0]

    def enough(pivot):
        return jnp.sum(jnp.where(aff >= pivot, 1.0, 0.0), axis=-1, keepdims=True) >= cap

    def log_body(_, st):
        lo, hi, elo, ehi = st
        mid = 0.5 * (elo + ehi)
        pivot = jnp.exp2(mid)
        ok = enough(pivot)
        return (jnp.where(ok, pivot, lo), jnp.where(ok, hi, pivot), jnp.where(ok, mid, elo), jnp.where(ok, ehi, mid))

    def lin_body(_, st):
        lo, hi = st
        pivot = 0.5 * (lo + hi)
        ok = enough(pivot)
        return jnp.where(ok, pivot, lo), jnp.where(ok, hi, pivot)

    col = lambda v: jnp.full((rows, 1), v, F32)
    lo, hi, _, _ = lax.fori_loop(0, ROUTE_LOG_STEPS, log_body, (col(0.0), col(2.0), col(-152.0), col(1.0)))
    lo, hi = lax.fori_loop(0, ROUTE_LIN_STEPS, lin_body, (lo, hi))

    gt = jnp.where(aff >= hi, 1.0, 0.0)
    eq = jnp.where(aff >= lo, 1.0, 0.0) - gt
    need = cap - jnp.sum(gt, axis=-1, keepdims=True)
    r = lax.broadcasted_iota(jnp.int32, (CHUNK, CHUNK), 0)
    c = lax.broadcasted_iota(jnp.int32, (CHUNK, CHUNK), 1)
    tri = jnp.where(r < c, 1.0, 0.0).astype(BF16)
    sel = gt + eq * jnp.where(_prefix_count(eq, tri) < need, 1.0, 0.0)
    slot = _prefix_count(sel, tri)
    slot_ref[...] = jnp.where(sel > 0.5, slot, -1.0)


def _route(aff, cap):
    B, E, S = aff.shape
    spec = pl.BlockSpec((B * E, S), lambda i: (0, 0))
    return pl.pallas_call(
        functools.partial(_route_kernel, cap),
        grid=(1,),
        in_specs=[spec],
        out_specs=spec,
        out_shape=jax.ShapeDtypeStruct((B * E, S), F32),
        compiler_params=_cparams(1),
        name="route",
    )(aff.reshape(B * E, S))


def _moe_kernel(cap, slot_a_ref, aff_a_ref, slot_b_ref, aff_b_ref, next_a_ref, next_b_ref, h_ref,
                wg_ref, wu_ref, wd_ref, o_ref, xa_ref, xb_ref, ya_ref, yb_ref, tok_ref):
    S = h_ref.shape[0] // SLAB_ROWS
    stride = cap + SLAB_ROWS
    parity = pl.program_id(1) % 2
    slot_ids = lax.broadcasted_iota(jnp.int32, (cap, 1), 0).astype(F32)
    tok_ids = lax.broadcasted_iota(jnp.int32, (1, S), 1).astype(F32)

    def slot_affinities(slot_row, aff_row):
        return jnp.sum(jnp.where(slot_ids == slot_row, aff_row, 0.0), axis=-1, keepdims=True)

    def gather(slot_row, x_ref, which, par, done):
        sel = slot_ids == slot_row
        tok = jnp.sum(jnp.where(sel, tok_ids, 0.0), axis=-1, keepdims=True).astype(jnp.int32)
        tok = jnp.clip(tok, 0, S - 1)
        for grp in range(cap // GATHER_GROUP):
            ids = tok[grp * GATHER_GROUP:(grp + 1) * GATHER_GROUP, :]
            if len(done) >= 2:
                ids = jnp.minimum(ids, done[-2] + S)
            for j in range(GATHER_GROUP):
                c = grp * GATHER_GROUP + j
                t = ids[j, 0]
                row = pl.multiple_of(t * SLAB_ROWS, SLAB_ROWS)
                tok_ref[which, par, c] = row
                x_ref[pl.ds(c, SLAB_ROWS, stride=stride), :] = h_ref[pl.ds(row, SLAB_ROWS), :]
            done.append(t)

    def chunks_to_rows(tile_ref):
        return jnp.concatenate([tile_ref[j * stride:j * stride + cap, :] for j in range(SLAB_ROWS)], axis=1)

    def gate_up(x_ref, i):
        xg = chunks_to_rows(x_ref).astype(BF16)
        g = _dot(xg, wg_ref[i])
        u = _dot(xg, wu_ref[i])
        return (g * (1.0 / (1.0 + jnp.exp(-g))) * u).astype(BF16)

    def down(act, slot_row, aff_row, y_ref, i):
        y = _dot(act, wd_ref[i]) * slot_affinities(slot_row, aff_row)
        for j in range(SLAB_ROWS):
            y_ref[j * stride:j * stride + cap, :] = y[:, j * LANES:(j + 1) * LANES]

    def scatter_add(y_ref, which):
        for c0 in range(0, cap, SCATTER_GROUP):
            rows = [tok_ref[which, parity, c0 + j] for j in range(SCATTER_GROUP)]
            rows = [pl.multiple_of(r, SLAB_ROWS) for r in rows]
            new = [o_ref[pl.ds(rows[j], SLAB_ROWS), :] + y_ref[pl.ds(c0 + j, SLAB_ROWS, stride=stride), :]
                   for j in range(SCATTER_GROUP)]
            for j in range(SCATTER_GROUP):
                o_ref[pl.ds(rows[j], SLAB_ROWS), :] = new[j]

    @pl.when((pl.program_id(0) == 0) & (pl.program_id(1) == 0))
    def _first_step():
        moved = []
        gather(slot_a_ref[0], xa_ref, 0, 0, moved)
        gather(slot_b_ref[0], xb_ref, 1, 0, moved)

    @pl.when(pl.program_id(1) == 0)
    def _first_of_sequence():
        o_ref[...] = jnp.zeros_like(o_ref)

    moved = []
    act_a = gate_up(xa_ref, 0)
    gather(next_a_ref[0], xa_ref, 0, 1 - parity, moved)
    act_b = gate_up(xb_ref, 1)
    gather(next_b_ref[0], xb_ref, 1, 1 - parity, moved)
    down(act_a, slot_a_ref[0], aff_a_ref[0], ya_ref, 0)
    down(act_b, slot_b_ref[0], aff_b_ref[0], yb_ref, 1)
    scatter_add(ya_ref, 0)
    scatter_add(yb_ref, 1)


def _moe(slot, aff, h2, wg, wu, wd, cap):
    B, E, S = aff.shape
    assert E % 2 == 0 and (E // 2) % 2 == 0 and h2.shape == (B * S * SLAB_ROWS, LANES)
    n_steps = B * E // 2
    last = B * E - 1
    slot3 = slot.reshape(B * E, 1, S)
    aff3 = aff.reshape(B * E, 1, S)
    row = lambda d: pl.BlockSpec((1, 1, S), lambda b, k: (jnp.minimum(b * E + 2 * k + d, last), 0, 0))
    cur_a, cur_b, nxt_a, nxt_b = row(0), row(1), row(2), row(3)
    tok = pl.BlockSpec((S * SLAB_ROWS, LANES), lambda b, k: (b, 0))
    ahead = pl.BlockSpec((S * SLAB_ROWS, LANES),
                         lambda b, k: (jnp.minimum(b * (E // 2) + k + 1, n_steps - 1) // (E // 2), 0))
    tile = pltpu.VMEM((SLAB_ROWS * (cap + SLAB_ROWS), LANES), F32)
    wspec = lambda w: pl.BlockSpec((2,) + w.shape[1:], lambda b, k: (k, 0, 0))
    return pl.pallas_call(
        functools.partial(_moe_kernel, cap),
        grid=(B, E // 2),
        in_specs=[cur_a, cur_a, cur_b, cur_b, nxt_a, nxt_b, ahead, wspec(wg), wspec(wu), wspec(wd)],
        out_specs=tok,
        out_shape=jax.ShapeDtypeStruct(h2.shape, F32),
        scratch_shapes=[tile, tile, tile, tile, pltpu.SMEM((2, 2, cap), jnp.int32)],
        compiler_params=_cparams(2),
        name="moe",
    )(slot3, aff3, slot3, aff3, slot3, slot3, h2, wg, wu, wd)


def _ple_final_kernel(x_ref, ya_ref, yb_ref, wo_ref, moe_ref, p_ref, gp_ref, wg_ref, wp_ref, gf_ref, o_ref):
    rows = x_ref.shape[0]
    moe = jnp.concatenate([moe_ref[pl.ds(j, rows, stride=SLAB_ROWS), :] for j in range(SLAB_ROWS)], axis=1)
    x2 = _mixer_residual(x_ref, ya_ref, yb_ref, wo_ref) + moe
    z = _dot(_rms(x2, gp_ref[...]).astype(BF16), wg_ref[...])
    gate = 1.0 / (1.0 + jnp.exp(-z))
    x3 = x2 + gate * _dot(p_ref[...].astype(BF16), wp_ref[...])
    o_ref[...] = _rms(x3, gf_ref[...])


def _ple_final(xf, ya, yb, w_out, moe, pf, g_ple, w_gate, w_proj, g_final):
    T = xf.shape[0]
    row = lambda n: pl.BlockSpec((TM, n), lambda i: (i, 0))
    full = lambda a: pl.BlockSpec(a.shape, lambda i: (0, 0))
    return pl.pallas_call(
        _ple_final_kernel,
        grid=(T // TM,),
        in_specs=[row(D_MODEL), row(512), row(512), full(w_out),
                  pl.BlockSpec((TM * SLAB_ROWS, LANES), lambda i: (i, 0)), row(PLE_DIM),
                  full(g_ple), full(w_gate), full(w_proj), full(g_final)],
        out_specs=row(D_MODEL),
        out_shape=jax.ShapeDtypeStruct((T, D_MODEL), F32),
        compiler_params=_cparams(1),
        name="ple_final",
    )(xf, ya, yb, w_out, moe, pf, g_ple, w_gate, w_proj, g_final)


def _prep_in_proj_weight(w_in):
    D = w_in.shape[0]
    t1, t2 = w_in[:, 1152:1168], w_in[:, 1168:1184]
    cols = [w_in[:, 0:1152], t1, t2, t2, t1, jnp.zeros((D, 64), F32)]
    return jnp.concatenate(cols, axis=1).astype(BF16)


def _prep_mla_weights(w_uq, w_ukv):
    per_q = B_NOPE_DIM + B_ROPE_DIM
    w3 = w_uq.reshape(Q_RANK, B_HEADS, per_q)
    pad = LANES - per_q
    wq = jnp.pad(w3, ((0, 0), (0, 0), (0, pad))).reshape(Q_RANK, B_HEADS * LANES)
    t1 = w3[:, :, B_NOPE_DIM:B_NOPE_DIM + HALF_ROPE]
    t2 = w3[:, :, B_NOPE_DIM + HALF_ROPE:]
    wqr = jnp.concatenate([jnp.zeros((Q_RANK, B_HEADS, B_NOPE_DIM), F32), t2, t1,
                           jnp.zeros((Q_RANK, B_HEADS, pad), F32)], axis=2).reshape(Q_RANK, B_HEADS * LANES)
    w4 = w_ukv.reshape(KV_RANK, B_HEADS, B_NOPE_DIM + B_V_DIM)
    zk = jnp.zeros((KV_RANK, B_HEADS, LANES - B_NOPE_DIM), F32)
    wk = jnp.concatenate([w4[:, :, :B_NOPE_DIM], zk], axis=2).reshape(KV_RANK, B_HEADS * LANES)
    wv = w4[:, :, B_NOPE_DIM:].reshape(KV_RANK, B_HEADS * B_V_DIM)
    wkv = jnp.concatenate([wk, wv], axis=1)
    return wq.astype(BF16), wqr.astype(BF16), wkv.astype(BF16)


def kernel(x, p, positions, rel_bias, norm_mix_g, w_in, sink, g_cq, g_ckv, w_uq, w_ukv, g_out_a, g_out_b, w_out,
           norm_ffn_g, w_router, w_e_gate, w_e_up, w_e_down, norm_ple_g, w_ple_gate, w_ple_proj, final_norm_g):
    B, S, D = x.shape
    T = B * S
    cap = CAPACITY_FACTOR * S // N_EXPERTS
    xf = x.reshape(T, D)
    cos_d, sin_d = _rope_tables(positions)
    assert w_in.shape[0] == 1, "single-layer block: the final norm is fused into the last kernel"
    i = 0
    w1 = _prep_in_proj_weight(w_in[i])
    wq, wqr, wkv = _prep_mla_weights(w_uq[i], w_ukv[i])
    qa, ka, va, qm, km, vm = _in_proj(
        xf, norm_mix_g[i].reshape(1, D), w1, g_cq[i].reshape(1, -1), g_ckv[i].reshape(1, -1),
        wq, wqr, wkv, cos_d, sin_d)
    ya = _window_attn(qa, ka, va, rel_bias, sink[i], g_out_a[i].reshape(1, -1), B, S)
    yb, (wg, wu, wd) = _mla_attn(qm, km, vm, g_out_b[i].reshape(1, -1), B, S,
                                 (w_e_gate[i], w_e_up[i], w_e_down[i]))
    wo = w_out[i].astype(BF16)
    h2, aff = _out_proj(xf, ya, yb, wo, norm_ffn_g[i].reshape(1, D), w_router[i].T, B, S)
    slot = _route(aff, cap)
    moe = _moe(slot, aff, h2, wg, wu, wd, cap)
    out = _ple_final(xf, ya, yb, wo, moe, p[i].reshape(T, -1), norm_ple_g[i].reshape(1, D),
                     w_ple_gate[i].astype(BF16), w_ple_proj[i].astype(BF16), final_norm_g.reshape(1, D))
    return out.reshape(B, S, D)
```

```python
import functools

import numpy as np
import jax
import jax.numpy as jnp
from jax import lax
from jax.experimental import pallas as pl
from jax.experimental.pallas import tpu as pltpu

F32 = jnp.float32
BF16 = jnp.bfloat16

D_MODEL = 1024
BLK = 128
WINDOW = 128
A_HEADS = 8
A_KV_HEADS = 2
A_HEAD_DIM = 64
B_HEADS = 8
B_NOPE_DIM = 64
B_ROPE_DIM = 32
B_V_DIM = 64
Q_RANK = 256
KV_RANK = 128
ROPE_THETA = 10000.0
NUM_BUCKETS = 32
MAX_DISTANCE = 128
N_EXPERTS = 16
EXPERT_FF = 512
CAPACITY_FACTOR = 2
PLE_DIM = 256
EPS = 1e-6

LANES = 128
HALF_ROPE = B_ROPE_DIM // 2
MASKED = -1e30
LOG2E = 1.4426950408889634
VMEM_LIMIT = 56 * 1024 * 1024
TM = 1024
TQ = 512
MLA_ROWS = 128
WIN_QBLOCKS = 8
GATHER_GROUP = 8
SCATTER_GROUP = 8
SLAB_ROWS = D_MODEL // LANES
TOK_RADIX = 4096


def _cparams(n_axes):
    return pltpu.CompilerParams(dimension_semantics=("arbitrary",) * n_axes,
                                vmem_limit_bytes=VMEM_LIMIT)


def _rms(x, g):
    return x * lax.rsqrt(jnp.mean(x * x, axis=-1, keepdims=True) + EPS) * g


def _dot(a, b):
    return jnp.dot(a, b, preferred_element_type=F32)


def _dot_nt(a, b):
    return lax.dot_general(a, b, (((1,), (1,)), ((), ())), preferred_element_type=F32)


def _dot_tn(a, b):
    return lax.dot_general(a, b, (((0,), (0,)), ((), ())), preferred_element_type=F32)


ROPE_GROUPS = LANES // HALF_ROPE


def _rope_table_kernel(pos_ref, invf_ref, cos_ref, sin_ref):
    ang = pos_ref[...] * invf_ref[...]
    cos_ref[...] = jnp.cos(ang)
    sin_ref[...] = jnp.sin(ang)


def _rope_tables(positions):
    T = positions.size
    rows = T // ROPE_GROUPS
    inv_freq = 1.0 / (ROPE_THETA ** (jnp.arange(0, B_ROPE_DIM, 2, dtype=F32) / B_ROPE_DIM))
    pos = jnp.repeat(positions.astype(F32).reshape(ROPE_GROUPS, rows).T, HALF_ROPE, axis=1)
    invf = jnp.tile(inv_freq, ROPE_GROUPS).reshape(1, LANES)
    spec = pl.BlockSpec((rows, LANES), lambda i: (0, 0))
    return pl.pallas_call(
        _rope_table_kernel,
        grid=(1,),
        in_specs=[spec, pl.BlockSpec((1, LANES), lambda i: (0, 0))],
        out_specs=[spec] * 2,
        out_shape=[jax.ShapeDtypeStruct((rows, LANES), F32)] * 2,
        compiler_params=_cparams(1),
        name="rope_tables",
    )(pos, invf)


def _expand_rope(cd, sd, group):
    lane = lax.broadcasted_iota(jnp.int32, (1, LANES), 1)
    own = (lane >= group * HALF_ROPE) & (lane < (group + 1) * HALF_ROPE)
    xc = jnp.where(own, cd, 0.0)
    xs = jnp.where(own, sd, 0.0)
    shift = HALF_ROPE
    while shift < LANES:
        xc = xc + pltpu.roll(xc, shift, axis=1)
        xs = xs + pltpu.roll(xs, shift, axis=1)
        shift *= 2
    in_t1 = (lane >= B_NOPE_DIM) & (lane < B_NOPE_DIM + HALF_ROPE)
    in_t2 = (lane >= B_NOPE_DIM + HALF_ROPE) & (lane < B_NOPE_DIM + B_ROPE_DIM)
    c = jnp.where(lane < B_NOPE_DIM, 1.0, jnp.where(in_t1 | in_t2, xc, 0.0))
    s = jnp.where(in_t1, -xs, jnp.where(in_t2, xs, 0.0))
    return c, s


C_QA = 0
C_KA = C_QA + 512
C_VA = C_KA + LANES
C_CQ = C_VA + LANES
C_CKV = C_CQ + Q_RANK
C_KR = C_CKV + KV_RANK
C_END = C_KR + LANES
MLA_K_COLS = B_HEADS * LANES


def _in_proj_kernel(tiles_per_group, x_ref, g_ref, w1_ref, gcq_ref, gckv_ref, wq_ref, wqr_ref, wkv_ref,
                    cos_ref, sin_ref, qa_ref, ka_ref, va_ref, qm_ref, km_ref, vm_ref):
    h = _rms(x_ref[...], g_ref[...]).astype(BF16)
    z = _dot(h, w1_ref[...])
    qa_ref[...] = (z[:, C_QA:C_KA] * (A_HEAD_DIM ** -0.5 * LOG2E)).astype(BF16)
    ka_ref[...] = z[:, C_KA:C_VA].astype(BF16)
    va_ref[...] = z[:, C_VA:C_CQ].astype(BF16)
    cqn = _rms(z[:, C_CQ:C_CKV], gcq_ref[...]).astype(BF16)
    ckvn = _rms(z[:, C_CKV:C_KR], gckv_ref[...]).astype(BF16)
    krt = z[:, C_KR:C_END]

    lane = lax.broadcasted_iota(jnp.int32, (1, LANES), 1)
    c_full, s_rope = _expand_rope(cos_ref[...], sin_ref[...], pl.program_id(0) // tiles_per_group)
    c_rope = jnp.where(lane < B_NOPE_DIM, 0.0, c_full)

    q1 = _dot(cqn, wq_ref[...])
    q2 = _dot(cqn, wqr_ref[...])
    scale = (B_NOPE_DIM + B_ROPE_DIM) ** -0.5 * LOG2E
    for hd in range(B_HEADS):
        sl = slice(hd * LANES, (hd + 1) * LANES)
        qm_ref[:, sl] = ((q1[:, sl] * c_full + q2[:, sl] * s_rope) * scale).astype(BF16)

    kv = _dot(ckvn, wkv_ref[...])
    kr_part = pltpu.roll(krt, 64, axis=1) * c_rope + pltpu.roll(krt, 32, axis=1) * s_rope
    for hd in range(B_HEADS):
        sl = slice(hd * LANES, (hd + 1) * LANES)
        km_ref[:, sl] = (kv[:, sl] + kr_part).astype(BF16)
    vm_ref[...] = kv[:, MLA_K_COLS:].astype(BF16)


def _in_proj(xf, g_mix, w1, g_cq, g_ckv, wq, wqr, wkv, cos_d, sin_d):
    T = xf.shape[0]
    tiles_per_group = cos_d.shape[0] // TM
    assert tiles_per_group * TM == cos_d.shape[0]
    row = lambda n: pl.BlockSpec((TM, n), lambda i: (i, 0))
    full = lambda a: pl.BlockSpec(a.shape, lambda i: (0, 0))
    dense = pl.BlockSpec((TM, LANES), lambda i: (i % tiles_per_group, 0))
    outs = [512, LANES, LANES, MLA_K_COLS, MLA_K_COLS, B_HEADS * B_V_DIM]
    return pl.pallas_call(
        functools.partial(_in_proj_kernel, tiles_per_group),
        grid=(T // TM,),
        in_specs=[row(D_MODEL), full(g_mix), full(w1), full(g_cq), full(g_ckv), full(wq), full(wqr), full(wkv),
                  dense, dense],
        out_specs=[row(n) for n in outs],
        out_shape=[jax.ShapeDtypeStruct((T, n), BF16) for n in outs],
        compiler_params=_cparams(1),
        name="in_proj",
    )(xf, g_mix, w1, g_cq, g_ckv, wq, wqr, wkv, cos_d, sin_d)


def _bucket_map():
    qi = np.arange(BLK)[:, None]
    kj = np.arange(3 * BLK)[None, :]
    rel = kj - BLK - qi
    n = np.abs(rel)
    half = NUM_BUCKETS // 2
    max_exact = half // 2
    thresholds = [int(np.ceil(max_exact * 2 ** (k / 2) - 1e-9)) for k in range(1, half - max_exact)]
    large = max_exact + sum((n >= t).astype(np.int64) for t in thresholds)
    large = np.minimum(large, half - 1)
    bucket = np.where(rel > 0, half, 0) + np.where(n < max_exact, n, large)
    return np.where(n <= WINDOW, bucket, -1).astype(np.int32)


def _window_attn_kernel(n_steps, relb_ref, sink_ref, bmap_ref, g_ref, q_ref, kp_ref, kc_ref, kn_ref,
                        vp_ref, vc_ref, vn_ref, o_ref, bias_ref):
    b = pl.program_id(0)
    i = pl.program_id(1)

    @pl.when((b == 0) & (i == 0))
    def _build_bias():
        bmap = bmap_ref[...]
        col = lax.broadcasted_iota(jnp.int32, (1, 3 * BLK), 1)
        for hd in range(A_HEADS):
            bias_ref[1, hd] = jnp.full((BLK, 3 * BLK), MASKED, F32)
        for bk in range(NUM_BUCKETS):
            m = bmap == bk
            for hd in range(A_HEADS):
                bias_ref[1, hd] = jnp.where(m, relb_ref[bk, hd] * LOG2E, bias_ref[1, hd])
        for hd in range(A_HEADS):
            bias_ref[0, hd] = jnp.where(col < BLK, MASKED, bias_ref[1, hd])
            bias_ref[2, hd] = jnp.where(col >= 2 * BLK, MASKED, bias_ref[1, hd])

    lane = lax.broadcasted_iota(jnp.int32, (1, LANES), 1)
    lo = lane < A_HEAD_DIM
    ones_lo = jnp.broadcast_to(jnp.where(lo, 1.0, 0.0).astype(BF16), (3 * BLK, LANES))
    ones_hi = jnp.broadcast_to(jnp.where(lo, 0.0, 1.0).astype(BF16), (3 * BLK, LANES))

    def key_blocks(p_ref, c_ref, n_ref):
        return [p_ref[...]] + [c_ref[c * BLK:(c + 1) * BLK, :] for c in range(WIN_QBLOCKS)] + [n_ref[...]]

    def lane_swap(t):
        return pltpu.bitcast(pltpu.roll(pltpu.bitcast(t, jnp.int32), LANES // 2, axis=1), BF16)

    kdup = [[], []]
    v_even = [[], []]
    v_odd = [[], []]
    for kt, vt in zip(key_blocks(kp_ref, kc_ref, kn_ref), key_blocks(vp_ref, vc_ref, vn_ref)):
        ks, vs, zero = lane_swap(kt), lane_swap(vt), jnp.zeros_like(vt)
        kdup[0].append(jnp.where(lo, kt, ks))
        kdup[1].append(jnp.where(lo, ks, kt))
        v_even[0].append(jnp.where(lo, vt, zero))
        v_odd[0].append(jnp.where(lo, zero, vs))
        v_even[1].append(jnp.where(lo, vs, zero))
        v_odd[1].append(jnp.where(lo, zero, vt))

    for c in range(WIN_QBLOCKS):
        rows = slice(c * BLK, (c + 1) * BLK)
        if c == 0:
            variant = jnp.where(i == 0, 0, 1)
        elif c == WIN_QBLOCKS - 1:
            variant = jnp.where(i == n_steps - 1, 2, 1)
        else:
            variant = 1
        probs, sink_terms = [], []
        for hd in range(A_HEADS):
            g = hd // (A_HEADS // A_KV_HEADS)
            qt = q_ref[rows, (hd // 2) * LANES:(hd // 2 + 1) * LANES]
            qm = jnp.where(lo if hd % 2 == 0 else ~lo, qt, jnp.zeros_like(qt))
            kcat = jnp.concatenate(kdup[g][c:c + 3], axis=0)
            s = _dot_nt(qm, kcat) + bias_ref[variant, hd]
            sk = sink_ref[hd] * LOG2E
            m = jnp.maximum(jnp.max(s, axis=-1, keepdims=True), sk)
            probs.append(jnp.exp2(s - m).astype(BF16))
            sink_terms.append(jnp.exp2(sk - m))

        outs = []
        for j in range(A_HEADS // 2):
            g = (2 * j) // (A_HEADS // A_KV_HEADS)
            v_e = jnp.concatenate(v_even[g][c:c + 3], axis=0)
            v_o = jnp.concatenate(v_odd[g][c:c + 3], axis=0)
            o = (_dot(probs[2 * j], jnp.concatenate([v_e, ones_lo], axis=1))
                 + _dot(probs[2 * j + 1], jnp.concatenate([v_o, ones_hi], axis=1)))
            denom = o[:, LANES:] + jnp.where(lo, sink_terms[2 * j], sink_terms[2 * j + 1])
            outs.append(o[:, :LANES] / denom)
        ya = jnp.concatenate(outs, axis=1)
        o_ref[rows, :] = _rms(ya, g_ref[...]).astype(BF16)


def _window_attn(qa, ka, va, rel_bias, sink, g_out_a, B, S):
    nb = S // BLK
    n_steps = nb // WIN_QBLOCKS
    assert nb >= 2, "first and last query blocks use distinct edge masks"
    bmap = jnp.asarray(_bucket_map())
    smem = pl.BlockSpec(memory_space=pltpu.SMEM)
    cur = lambda n: pl.BlockSpec((WIN_QBLOCKS * BLK, n), lambda b, i: (b * n_steps + i, 0))
    prv = lambda n: pl.BlockSpec((BLK, n), lambda b, i: (b * nb + jnp.maximum(i * WIN_QBLOCKS - 1, 0), 0))
    nxt = lambda n: pl.BlockSpec((BLK, n), lambda b, i: (b * nb + jnp.minimum((i + 1) * WIN_QBLOCKS, nb - 1), 0))
    return pl.pallas_call(
        functools.partial(_window_attn_kernel, n_steps),
        grid=(B, n_steps),
        in_specs=[smem, smem,
                  pl.BlockSpec((BLK, 3 * BLK), lambda b, i: (0, 0)),
                  pl.BlockSpec((1, 512), lambda b, i: (0, 0)),
                  cur(512), prv(LANES), cur(LANES), nxt(LANES), prv(LANES), cur(LANES), nxt(LANES)],
        out_specs=cur(512),
        out_shape=jax.ShapeDtypeStruct((B * S, 512), BF16),
        scratch_shapes=[pltpu.VMEM((3, A_HEADS, BLK, 3 * BLK), F32)],
        compiler_params=_cparams(2),
        name="window_attn",
    )(rel_bias, sink, bmap, g_out_a, qa, ka, ka, ka, va, va, va)


def _mla_attn_kernel(n_cast, g_ref, q_ref, k_ref, v_ref, *rest):
    o_ref = rest[n_cast]
    for src, dst in zip(rest[:n_cast], rest[n_cast + 1:]):
        dst[...] = src[...].astype(BF16)
    S = k_ref.shape[0]
    lane = lax.broadcasted_iota(jnp.int32, (1, LANES), 1)
    lo = lane < B_V_DIM
    ones_lo = jnp.broadcast_to(jnp.where(lo, 1.0, 0.0).astype(BF16), (S, LANES))
    ones_hi = jnp.broadcast_to(jnp.where(lo, 0.0, 1.0).astype(BF16), (S, LANES))
    v_pairs = []
    for j in range(B_HEADS // 2):
        vt = v_ref[:, j * LANES:(j + 1) * LANES]
        zero = jnp.zeros_like(vt)
        v_pairs.append((jnp.concatenate([jnp.where(lo, vt, zero), ones_lo], axis=1),
                        jnp.concatenate([jnp.where(lo, zero, vt), ones_hi], axis=1)))
    for r in range(q_ref.shape[0] // MLA_ROWS):
        rows = slice(r * MLA_ROWS, (r + 1) * MLA_ROWS)
        outs = []
        for j in range(B_HEADS // 2):
            ps = []
            for hd in (2 * j, 2 * j + 1):
                sl = slice(hd * LANES, (hd + 1) * LANES)
                s = _dot_nt(q_ref[rows, sl], k_ref[:, sl])
                m = jnp.max(s, axis=-1, keepdims=True)
                ps.append(jnp.exp2(s - m).astype(BF16))
            o = _dot(ps[0], v_pairs[j][0]) + _dot(ps[1], v_pairs[j][1])
            outs.append(o[:, :LANES] / o[:, LANES:])
        yb = jnp.concatenate(outs, axis=1)
        o_ref[rows, :] = _rms(yb, g_ref[...]).astype(BF16)


def _mla_attn(qm, km, vm, g_out_b, B, S, cast_weights):
    nq = S // TQ
    steps = B * nq
    qspec = lambda n: pl.BlockSpec((TQ, n), lambda b, i: (b * nq + i, 0))
    kspec = lambda n: pl.BlockSpec((S, n), lambda b, i: (b, 0))
    flat = [w.reshape(-1, w.shape[-1]) for w in cast_weights]
    slab = lambda w: pl.BlockSpec((w.shape[0] // steps, w.shape[1]), lambda b, i: (b * nq + i, 0))
    outs = pl.pallas_call(
        functools.partial(_mla_attn_kernel, len(flat)),
        grid=(B, nq),
        in_specs=[pl.BlockSpec((1, 512), lambda b, i: (0, 0)), qspec(1024), kspec(1024), kspec(512)]
                 + [slab(w) for w in flat],
        out_specs=[qspec(512)] + [slab(w) for w in flat],
        out_shape=[jax.ShapeDtypeStruct((B * S, 512), BF16)]
                  + [jax.ShapeDtypeStruct(w.shape, BF16) for w in flat],
        compiler_params=_cparams(2),
        name="mla_attn",
    )(g_out_b, qm, km, vm, *flat)
    return outs[0], [o.reshape(w.shape) for o, w in zip(outs[1:], cast_weights)]


def _out_proj_kernel(x_ref, ya_ref, yb_ref, wo_ref, g_ref, wr_ref, x1_ref, h2_ref, aff_ref):
    x1 = x_ref[...] + _dot(ya_ref[...], wo_ref[0:512, :]) + _dot(yb_ref[...], wo_ref[512:1024, :])
    x1_ref[...] = x1
    h2 = _rms(x1, g_ref[...])
    h2b = h2.astype(BF16)
    h2r = h2b.astype(F32)
    for j in range(SLAB_ROWS):
        h2_ref[pl.ds(j, h2r.shape[0], stride=SLAB_ROWS), :] = h2r[:, j * LANES:(j + 1) * LANES]
    logits = _dot_nt(wr_ref[...].astype(BF16), h2b)
    m = jnp.max(logits, axis=0, keepdims=True)
    e = jnp.exp(logits - m)
    aff_ref[0] = e / jnp.sum(e, axis=0, keepdims=True)


def _out_proj(xf, ya, yb, w_out, g_ffn, w_router_t, B, S):
    T = xf.shape[0]
    per_b = S // TM
    row = lambda n: pl.BlockSpec((TM, n), lambda i: (i, 0))
    full = lambda a: pl.BlockSpec(a.shape, lambda i: (0, 0))
    return pl.pallas_call(
        _out_proj_kernel,
        grid=(T // TM,),
        in_specs=[row(D_MODEL), row(512), row(512), full(w_out), full(g_ffn), full(w_router_t)],
        out_specs=[row(D_MODEL), pl.BlockSpec((TM * SLAB_ROWS, LANES), lambda i: (i, 0)),
                   pl.BlockSpec((1, N_EXPERTS, TM), lambda i: (i // per_b, 0, i % per_b))],
        out_shape=[jax.ShapeDtypeStruct((T, D_MODEL), F32), jax.ShapeDtypeStruct((T * SLAB_ROWS, LANES), F32),
                   jax.ShapeDtypeStruct((B, N_EXPERTS, S), F32)],
        compiler_params=_cparams(1),
        name="out_proj",
    )(xf, ya, yb, w_out, g_ffn, w_router_t)


CHUNK = 256
ROUTE_LOG_STEPS = 12
ROUTE_LIN_STEPS = 32


def _prefix_count(flags_f32, tri):
    S = flags_f32.shape[1]
    carry = jnp.zeros((flags_f32.shape[0], 1), F32)
    parts = []
    for c in range(S // CHUNK):
        blk = flags_f32[:, c * CHUNK:(c + 1) * CHUNK]
        parts.append(_dot(blk.astype(BF16), tri) + carry)
        carry = carry + jnp.sum(blk, axis=-1, keepdims=True)
    return jnp.concatenate(parts, axis=1)


def _route_kernel(cap, aff_ref, slot_ref):
    aff = aff_ref[...]
    rows = aff.shape[0]

    def enough(pivot):
        return jnp.sum(jnp.where(aff >= pivot, 1.0, 0.0), axis=-1, keepdims=True) >= cap

    def log_body(_, st):
        lo, hi, elo, ehi = st
        mid = 0.5 * (elo + ehi)
        pivot = jnp.exp2(mid)
        ok = enough(pivot)
        return (jnp.where(ok, pivot, lo), jnp.where(ok, hi, pivot), jnp.where(ok, mid, elo), jnp.where(ok, ehi, mid))

    def lin_body(_, st):
        lo, hi = st
        pivot = 0.5 * (lo + hi)
        ok = enough(pivot)
        return jnp.where(ok, pivot, lo), jnp.where(ok, hi, pivot)

    col = lambda v: jnp.full((rows, 1), v, F32)
    lo, hi, _, _ = lax.fori_loop(0, ROUTE_LOG_STEPS, log_body, (col(0.0), col(2.0), col(-152.0), col(1.0)))
    lo, hi = lax.fori_loop(0, ROUTE_LIN_STEPS, lin_body, (lo, hi))

    gt = jnp.where(aff >= hi, 1.0, 0.0)
    eq = jnp.where(aff >= lo, 1.0, 0.0) - gt
    need = cap - jnp.sum(gt, axis=-1, keepdims=True)
    r = lax.broadcasted_iota(jnp.int32, (CHUNK, CHUNK), 0)
    c = lax.broadcasted_iota(jnp.int32, (CHUNK, CHUNK), 1)
    tri = jnp.where(r < c, 1.0, 0.0).astype(BF16)
    sel = gt + eq * jnp.where(_prefix_count(eq, tri) < need, 1.0, 0.0)
    slot = _prefix_count(sel, tri)
    slot_ref[...] = jnp.where(sel > 0.5, slot, -1.0)


def _route(aff, cap):
    B, E, S = aff.shape
    spec = pl.BlockSpec((B * E, S), lambda i: (0, 0))
    return pl.pallas_call(
        functools.partial(_route_kernel, cap),
        grid=(1,),
        in_specs=[spec],
        out_specs=spec,
        out_shape=jax.ShapeDtypeStruct((B * E, S), F32),
        compiler_params=_cparams(1),
        name="route",
    )(aff.reshape(B * E, S))


def _moe_kernel(cap, slot_a_ref, aff_a_ref, slot_b_ref, aff_b_ref, next_a_ref, next_b_ref, h_ref,
                wg_ref, wu_ref, wd_ref, o_ref, xa_ref, xb_ref, ya_ref, yb_ref, tok_ref):
    S = h_ref.shape[0] // SLAB_ROWS
    stride = cap + SLAB_ROWS
    parity = pl.program_id(1) % 2
    slot_ids = lax.broadcasted_iota(jnp.int32, (cap, 1), 0).astype(F32)
    tok_ids = lax.broadcasted_iota(jnp.int32, (1, S), 1).astype(F32)

    def slot_affinities(slot_row, aff_row):
        return jnp.sum(jnp.where(slot_ids == slot_row, aff_row, 0.0), axis=-1, keepdims=True)

    def gather(slot_row, x_ref, which, par, done):
        sel = slot_ids == slot_row
        tok = jnp.sum(jnp.where(sel, tok_ids, 0.0), axis=-1, keepdims=True).astype(jnp.int32)
        tok = jnp.clip(tok, 0, S - 1)
        half = cap // 2
        packed = tok[:half, :] + tok[half:, :] * TOK_RADIX
        for grp in range(half // GATHER_GROUP):
            words = packed[grp * GATHER_GROUP:(grp + 1) * GATHER_GROUP, :]
            if len(done) >= 2:
                words = jnp.minimum(words, done[-2] + TOK_RADIX * TOK_RADIX)
            for j in range(GATHER_GROUP):
                w = words[j, 0]
                lo_id = jnp.bitwise_and(w, TOK_RADIX - 1)
                hi_id = lax.shift_right_logical(w, TOK_RADIX.bit_length() - 1)
                for c, t in ((grp * GATHER_GROUP + j, lo_id), (half + grp * GATHER_GROUP + j, hi_id)):
                    row = pl.multiple_of(t * SLAB_ROWS, SLAB_ROWS)
                    tok_ref[which, par, c] = row
                    x_ref[pl.ds(c, SLAB_ROWS, stride=stride), :] = h_ref[pl.ds(row, SLAB_ROWS), :]
            done.append(w)

    def chunks_to_rows(tile_ref):
        return jnp.concatenate([tile_ref[j * stride:j * stride + cap, :] for j in range(SLAB_ROWS)], axis=1)

    def gate_up(x_ref, i):
        xg = chunks_to_rows(x_ref).astype(BF16)
        g = _dot(xg, wg_ref[i])
        u = _dot(xg, wu_ref[i])
        return (g * (1.0 / (1.0 + jnp.exp(-g))) * u).astype(BF16)

    def down(act, slot_row, aff_row, y_ref, i):
        y = _dot(act, wd_ref[i]) * slot_affinities(slot_row, aff_row)
        for j in range(SLAB_ROWS):
            y_ref[j * stride:j * stride + cap, :] = y[:, j * LANES:(j + 1) * LANES]

    def scatter_add(y_ref, which):
        for c0 in range(0, cap, SCATTER_GROUP):
            rows = [tok_ref[which, parity, c0 + j] for j in range(SCATTER_GROUP)]
            rows = [pl.multiple_of(r, SLAB_ROWS) for r in rows]
            new = [o_ref[pl.ds(rows[j], SLAB_ROWS), :] + y_ref[pl.ds(c0 + j, SLAB_ROWS, stride=stride), :]
                   for j in range(SCATTER_GROUP)]
            for j in range(SCATTER_GROUP):
                o_ref[pl.ds(rows[j], SLAB_ROWS), :] = new[j]

    @pl.when((pl.program_id(0) == 0) & (pl.program_id(1) == 0))
    def _first_step():
        moved = []
        gather(slot_a_ref[0], xa_ref, 0, 0, moved)
        gather(slot_b_ref[0], xb_ref, 1, 0, moved)

    @pl.when(pl.program_id(1) == 0)
    def _first_of_sequence():
        o_ref[...] = jnp.zeros_like(o_ref)

    moved = []
    act_a = gate_up(xa_ref, 0)
    gather(next_a_ref[0], xa_ref, 0, 1 - parity, moved)
    act_b = gate_up(xb_ref, 1)
    gather(next_b_ref[0], xb_ref, 1, 1 - parity, moved)
    down(act_a, slot_a_ref[0], aff_a_ref[0], ya_ref, 0)
    down(act_b, slot_b_ref[0], aff_b_ref[0], yb_ref, 1)
    scatter_add(ya_ref, 0)
    scatter_add(yb_ref, 1)


def _moe(slot, aff, h2, wg, wu, wd, cap):
    B, E, S = aff.shape
    assert E % 2 == 0 and (E // 2) % 2 == 0 and h2.shape == (B * S * SLAB_ROWS, LANES)
    assert S <= TOK_RADIX and cap % (2 * GATHER_GROUP) == 0
    n_steps = B * E // 2
    last = B * E - 1
    slot3 = slot.reshape(B * E, 1, S)
    aff3 = aff.reshape(B * E, 1, S)
    row = lambda d: pl.BlockSpec((1, 1, S), lambda b, k: (jnp.minimum(b * E + 2 * k + d, last), 0, 0))
    cur_a, cur_b, nxt_a, nxt_b = row(0), row(1), row(2), row(3)
    tok = pl.BlockSpec((S * SLAB_ROWS, LANES), lambda b, k: (b, 0))
    ahead = pl.BlockSpec((S * SLAB_ROWS, LANES),
                         lambda b, k: (jnp.minimum(b * (E // 2) + k + 1, n_steps - 1) // (E // 2), 0))
    tile = pltpu.VMEM((SLAB_ROWS * (cap + SLAB_ROWS), LANES), F32)
    wspec = lambda w: pl.BlockSpec((2,) + w.shape[1:], lambda b, k: (k, 0, 0))
    return pl.pallas_call(
        functools.partial(_moe_kernel, cap),
        grid=(B, E // 2),
        in_specs=[cur_a, cur_a, cur_b, cur_b, nxt_a, nxt_b, ahead, wspec(wg), wspec(wu), wspec(wd)],
        out_specs=tok,
        out_shape=jax.ShapeDtypeStruct(h2.shape, F32),
        scratch_shapes=[tile, tile, tile, tile, pltpu.SMEM((2, 2, cap), jnp.int32)],
        compiler_params=_cparams(2),
        name="moe",
    )(slot3, aff3, slot3, aff3, slot3, slot3, h2, wg, wu, wd)


def _ple_final_kernel(x1_ref, moe_ref, p_ref, gp_ref, wg_ref, wp_ref, gf_ref, o_ref):
    rows = x1_ref.shape[0]
    moe = jnp.concatenate([moe_ref[pl.ds(j, rows, stride=SLAB_ROWS), :] for j in range(SLAB_ROWS)], axis=1)
    x2 = x1_ref[...] + moe
    z = _dot(_rms(x2, gp_ref[...]).astype(BF16), wg_ref[...])
    gate = 1.0 / (1.0 + jnp.exp(-z))
    x3 = x2 + gate * _dot(p_ref[...].astype(BF16), wp_ref[...])
    o_ref[...] = _rms(x3, gf_ref[...])


def _ple_final(x1, moe, pf, g_ple, w_gate, w_proj, g_final):
    T = x1.shape[0]
    row = lambda n: pl.BlockSpec((TM, n), lambda i: (i, 0))
    full = lambda a: pl.BlockSpec(a.shape, lambda i: (0, 0))
    return pl.pallas_call(
        _ple_final_kernel,
        grid=(T // TM,),
        in_specs=[row(D_MODEL), pl.BlockSpec((TM * SLAB_ROWS, LANES), lambda i: (i, 0)), row(PLE_DIM),
                  full(g_ple), full(w_gate), full(w_proj), full(g_final)],
        out_specs=row(D_MODEL),
        out_shape=jax.ShapeDtypeStruct((T, D_MODEL), F32),
        compiler_params=_cparams(1),
        name="ple_final",
    )(x1, moe, pf, g_ple, w_gate, w_proj, g_final)


def _prep_in_proj_weight(w_in):
    D = w_in.shape[0]
    t1, t2 = w_in[:, 1152:1168], w_in[:, 1168:1184]
    cols = [w_in[:, 0:1152], t1, t2, t2, t1, jnp.zeros((D, 64), F32)]
    return jnp.concatenate(cols, axis=1).astype(BF16)


def _prep_mla_weights(w_uq, w_ukv):
    per_q = B_NOPE_DIM + B_ROPE_DIM
    w3 = w_uq.reshape(Q_RANK, B_HEADS, per_q)
    pad = LANES - per_q
    wq = jnp.pad(w3, ((0, 0), (0, 0), (0, pad))).reshape(Q_RANK, B_HEADS * LANES)
    t1 = w3[:, :, B_NOPE_DIM:B_NOPE_DIM + HALF_ROPE]
    t2 = w3[:, :, B_NOPE_DIM + HALF_ROPE:]
    wqr = jnp.concatenate([jnp.zeros((Q_RANK, B_HEADS, B_NOPE_DIM), F32), t2, t1,
                           jnp.zeros((Q_RANK, B_HEADS, pad), F32)], axis=2).reshape(Q_RANK, B_HEADS * LANES)
    w4 = w_ukv.reshape(KV_RANK, B_HEADS, B_NOPE_DIM + B_V_DIM)
    zk = jnp.zeros((KV_RANK, B_HEADS, LANES - B_NOPE_DIM), F32)
    wk = jnp.concatenate([w4[:, :, :B_NOPE_DIM], zk], axis=2).reshape(KV_RANK, B_HEADS * LANES)
    wv = w4[:, :, B_NOPE_DIM:].reshape(KV_RANK, B_HEADS * B_V_DIM)
    wkv = jnp.concatenate([wk, wv], axis=1)
    return wq.astype(BF16), wqr.astype(BF16), wkv.astype(BF16)


def kernel(x, p, positions, rel_bias, norm_mix_g, w_in, sink, g_cq, g_ckv, w_uq, w_ukv, g_out_a, g_out_b, w_out,
           norm_ffn_g, w_router, w_e_gate, w_e_up, w_e_down, norm_ple_g, w_ple_gate, w_ple_proj, final_norm_g):
    B, S, D = x.shape
    T = B * S
    cap = CAPACITY_FACTOR * S // N_EXPERTS
    xf = x.reshape(T, D)
    cos_d, sin_d = _rope_tables(positions)
    assert w_in.shape[0] == 1, "single-layer block: the final norm is fused into the last kernel"
    i = 0
    w1 = _prep_in_proj_weight(w_in[i])
    wq, wqr, wkv = _prep_mla_weights(w_uq[i], w_ukv[i])
    qa, ka, va, qm, km, vm = _in_proj(
        xf, norm_mix_g[i].reshape(1, D), w1, g_cq[i].reshape(1, -1), g_ckv[i].reshape(1, -1),
        wq, wqr, wkv, cos_d, sin_d)
    ya = _window_attn(qa, ka, va, rel_bias, sink[i], g_out_a[i].reshape(1, -1), B, S)
    yb, (wg, wu, wd) = _mla_attn(qm, km, vm, g_out_b[i].reshape(1, -1), B, S,
                                 (w_e_gate[i], w_e_up[i], w_e_down[i]))
    x1, h2, aff = _out_proj(xf, ya, yb, w_out[i].astype(BF16), norm_ffn_g[i].reshape(1, D),
                            w_router[i].T, B, S)
    slot = _route(aff, cap)
    moe = _moe(slot, aff, h2, wg, wu, wd, cap)
    out = _ple_final(x1, moe, p[i].reshape(T, -1), norm_ple_g[i].reshape(1, D),
                     w_ple_gate[i].astype(BF16), w_ple_proj[i].astype(BF16), final_norm_g.reshape(1, D))
    return out.reshape(B, S, D)
```

```python
import functools

import numpy as np
import jax
import jax.numpy as jnp
from jax import lax
from jax.experimental import pallas as pl
from jax.experimental.pallas import tpu as pltpu

F32 = jnp.float32
BF16 = jnp.bfloat16

D_MODEL = 1024
BLK = 128
WINDOW = 128
A_HEADS = 8
A_KV_HEADS = 2
A_HEAD_DIM = 64
B_HEADS = 8
B_NOPE_DIM = 64
B_ROPE_DIM = 32
B_V_DIM = 64
Q_RANK = 256
KV_RANK = 128
ROPE_THETA = 10000.0
NUM_BUCKETS = 32
MAX_DISTANCE = 128
N_EXPERTS = 16
EXPERT_FF = 512
CAPACITY_FACTOR = 2
PLE_DIM = 256
EPS = 1e-6

LANES = 128
HALF_ROPE = B_ROPE_DIM // 2
MASKED = -1e30
LOG2E = 1.4426950408889634
VMEM_LIMIT = 56 * 1024 * 1024
TM = 1024
TQ = 512
MLA_ROWS = 128
WIN_QBLOCKS = 8
GATHER_GROUP = 8
SCATTER_GROUP = 8
SLAB_ROWS = D_MODEL // LANES


def _cparams(n_axes):
    return pltpu.CompilerParams(dimension_semantics=("arbitrary",) * n_axes,
                                vmem_limit_bytes=VMEM_LIMIT)


def _rms(x, g):
    return x * lax.rsqrt(jnp.mean(x * x, axis=-1, keepdims=True) + EPS) * g


def _dot(a, b):
    return jnp.dot(a, b, preferred_element_type=F32)


def _dot_nt(a, b):
    return lax.dot_general(a, b, (((1,), (1,)), ((), ())), preferred_element_type=F32)


def _dot_tn(a, b):
    return lax.dot_general(a, b, (((0,), (0,)), ((), ())), preferred_element_type=F32)


ROPE_GROUPS = LANES // HALF_ROPE


def _rope_table_kernel(pos_ref, invf_ref, cos_ref, sin_ref):
    ang = pos_ref[...] * invf_ref[...]
    cos_ref[...] = jnp.cos(ang)
    sin_ref[...] = jnp.sin(ang)


def _rope_tables(positions):
    T = positions.size
    rows = T // ROPE_GROUPS
    inv_freq = 1.0 / (ROPE_THETA ** (jnp.arange(0, B_ROPE_DIM, 2, dtype=F32) / B_ROPE_DIM))
    pos = jnp.repeat(positions.astype(F32).reshape(ROPE_GROUPS, rows).T, HALF_ROPE, axis=1)
    invf = jnp.tile(inv_freq, ROPE_GROUPS).reshape(1, LANES)
    spec = pl.BlockSpec((rows, LANES), lambda i: (0, 0))
    return pl.pallas_call(
        _rope_table_kernel,
        grid=(1,),
        in_specs=[spec, pl.BlockSpec((1, LANES), lambda i: (0, 0))],
        out_specs=[spec] * 2,
        out_shape=[jax.ShapeDtypeStruct((rows, LANES), F32)] * 2,
        compiler_params=_cparams(1),
        name="rope_tables",
    )(pos, invf)


def _expand_rope(cd, sd, group):
    lane = lax.broadcasted_iota(jnp.int32, (1, LANES), 1)
    own = (lane >= group * HALF_ROPE) & (lane < (group + 1) * HALF_ROPE)
    xc = jnp.where(own, cd, 0.0)
    xs = jnp.where(own, sd, 0.0)
    shift = HALF_ROPE
    while shift < LANES:
        xc = xc + pltpu.roll(xc, shift, axis=1)
        xs = xs + pltpu.roll(xs, shift, axis=1)
        shift *= 2
    in_t1 = (lane >= B_NOPE_DIM) & (lane < B_NOPE_DIM + HALF_ROPE)
    in_t2 = (lane >= B_NOPE_DIM + HALF_ROPE) & (lane < B_NOPE_DIM + B_ROPE_DIM)
    c = jnp.where(lane < B_NOPE_DIM, 1.0, jnp.where(in_t1 | in_t2, xc, 0.0))
    s = jnp.where(in_t1, -xs, jnp.where(in_t2, xs, 0.0))
    return c, s


C_QA = 0
C_KA = C_QA + 512
C_VA = C_KA + LANES
C_CQ = C_VA + LANES
C_CKV = C_CQ + Q_RANK
C_KR = C_CKV + KV_RANK
C_END = C_KR + LANES
MLA_K_COLS = B_HEADS * LANES


def _in_proj_kernel(tiles_per_group, x_ref, g_ref, w1_ref, gcq_ref, gckv_ref, wq_ref, wqr_ref, wkv_ref,
                    cos_ref, sin_ref, qa_ref, ka_ref, va_ref, qm_ref, km_ref, vm_ref):
    h = _rms(x_ref[...], g_ref[...]).astype(BF16)
    z = _dot(h, w1_ref[...])
    qa_ref[...] = (z[:, C_QA:C_KA] * (A_HEAD_DIM ** -0.5 * LOG2E)).astype(BF16)
    ka_ref[...] = z[:, C_KA:C_VA].astype(BF16)
    va_ref[...] = z[:, C_VA:C_CQ].astype(BF16)
    cqn = _rms(z[:, C_CQ:C_CKV], gcq_ref[...]).astype(BF16)
    ckvn = _rms(z[:, C_CKV:C_KR], gckv_ref[...]).astype(BF16)
    krt = z[:, C_KR:C_END]

    lane = lax.broadcasted_iota(jnp.int32, (1, LANES), 1)
    c_full, s_rope = _expand_rope(cos_ref[...], sin_ref[...], pl.program_id(0) // tiles_per_group)
    c_rope = jnp.where(lane < B_NOPE_DIM, 0.0, c_full)

    q1 = _dot(cqn, wq_ref[...])
    q2 = _dot(cqn, wqr_ref[...])
    scale = (B_NOPE_DIM + B_ROPE_DIM) ** -0.5 * LOG2E
    for hd in range(B_HEADS):
        sl = slice(hd * LANES, (hd + 1) * LANES)
        qm_ref[:, sl] = ((q1[:, sl] * c_full + q2[:, sl] * s_rope) * scale).astype(BF16)

    kv = _dot(ckvn, wkv_ref[...])
    kr_part = pltpu.roll(krt, 64, axis=1) * c_rope + pltpu.roll(krt, 32, axis=1) * s_rope
    for hd in range(B_HEADS):
        sl = slice(hd * LANES, (hd + 1) * LANES)
        km_ref[:, sl] = (kv[:, sl] + kr_part).astype(BF16)
    vm_ref[...] = kv[:, MLA_K_COLS:].astype(BF16)


def _in_proj(xf, g_mix, w1, g_cq, g_ckv, wq, wqr, wkv, cos_d, sin_d):
    T = xf.shape[0]
    tiles_per_group = cos_d.shape[0] // TM
    assert tiles_per_group * TM == cos_d.shape[0]
    row = lambda n: pl.BlockSpec((TM, n), lambda i: (i, 0))
    full = lambda a: pl.BlockSpec(a.shape, lambda i: (0, 0))
    dense = pl.BlockSpec((TM, LANES), lambda i: (i % tiles_per_group, 0))
    outs = [512, LANES, LANES, MLA_K_COLS, MLA_K_COLS, B_HEADS * B_V_DIM]
    return pl.pallas_call(
        functools.partial(_in_proj_kernel, tiles_per_group),
        grid=(T // TM,),
        in_specs=[row(D_MODEL), full(g_mix), full(w1), full(g_cq), full(g_ckv), full(wq), full(wqr), full(wkv),
                  dense, dense],
        out_specs=[row(n) for n in outs],
        out_shape=[jax.ShapeDtypeStruct((T, n), BF16) for n in outs],
        compiler_params=_cparams(1),
        name="in_proj",
    )(xf, g_mix, w1, g_cq, g_ckv, wq, wqr, wkv, cos_d, sin_d)


def _bucket_map():
    qi = np.arange(BLK)[:, None]
    kj = np.arange(3 * BLK)[None, :]
    rel = kj - BLK - qi
    n = np.abs(rel)
    half = NUM_BUCKETS // 2
    max_exact = half // 2
    thresholds = [int(np.ceil(max_exact * 2 ** (k / 2) - 1e-9)) for k in range(1, half - max_exact)]
    large = max_exact + sum((n >= t).astype(np.int64) for t in thresholds)
    large = np.minimum(large, half - 1)
    bucket = np.where(rel > 0, half, 0) + np.where(n < max_exact, n, large)
    return np.where(n <= WINDOW, bucket, -1).astype(np.int32)


def _window_attn_kernel(n_steps, relb_ref, sink_ref, bmap_ref, g_ref, q_ref, kp_ref, kc_ref, kn_ref,
                        vp_ref, vc_ref, vn_ref, o_ref, bias_ref):
    b = pl.program_id(0)
    i = pl.program_id(1)

    @pl.when((b == 0) & (i == 0))
    def _build_bias():
        bmap = bmap_ref[...]
        col = lax.broadcasted_iota(jnp.int32, (1, 3 * BLK), 1)
        for hd in range(A_HEADS):
            bias_ref[1, hd] = jnp.full((BLK, 3 * BLK), MASKED, F32)
        for bk in range(NUM_BUCKETS):
            m = bmap == bk
            for hd in range(A_HEADS):
                bias_ref[1, hd] = jnp.where(m, relb_ref[bk, hd] * LOG2E, bias_ref[1, hd])
        for hd in range(A_HEADS):
            bias_ref[0, hd] = jnp.where(col < BLK, MASKED, bias_ref[1, hd])
            bias_ref[2, hd] = jnp.where(col >= 2 * BLK, MASKED, bias_ref[1, hd])

    lane = lax.broadcasted_iota(jnp.int32, (1, LANES), 1)
    lo = lane < A_HEAD_DIM
    ones_lo = jnp.broadcast_to(jnp.where(lo, 1.0, 0.0).astype(BF16), (3 * BLK, LANES))
    ones_hi = jnp.broadcast_to(jnp.where(lo, 0.0, 1.0).astype(BF16), (3 * BLK, LANES))

    def key_blocks(p_ref, c_ref, n_ref):
        return [p_ref[...]] + [c_ref[c * BLK:(c + 1) * BLK, :] for c in range(WIN_QBLOCKS)] + [n_ref[...]]

    def lane_swap(t):
        return pltpu.bitcast(pltpu.roll(pltpu.bitcast(t, jnp.int32), LANES // 2, axis=1), BF16)

    kdup = [[], []]
    v_even = [[], []]
    v_odd = [[], []]
    for kt, vt in zip(key_blocks(kp_ref, kc_ref, kn_ref), key_blocks(vp_ref, vc_ref, vn_ref)):
        ks, vs, zero = lane_swap(kt), lane_swap(vt), jnp.zeros_like(vt)
        kdup[0].append(jnp.where(lo, kt, ks))
        kdup[1].append(jnp.where(lo, ks, kt))
        v_even[0].append(jnp.where(lo, vt, zero))
        v_odd[0].append(jnp.where(lo, zero, vs))
        v_even[1].append(jnp.where(lo, vs, zero))
        v_odd[1].append(jnp.where(lo, zero, vt))

    for c in range(WIN_QBLOCKS):
        rows = slice(c * BLK, (c + 1) * BLK)
        if c == 0:
            variant = jnp.where(i == 0, 0, 1)
        elif c == WIN_QBLOCKS - 1:
            variant = jnp.where(i == n_steps - 1, 2, 1)
        else:
            variant = 1
        probs, sink_terms = [], []
        for hd in range(A_HEADS):
            g = hd // (A_HEADS // A_KV_HEADS)
            qt = q_ref[rows, (hd // 2) * LANES:(hd // 2 + 1) * LANES]
            qm = jnp.where(lo if hd % 2 == 0 else ~lo, qt, jnp.zeros_like(qt))
            kcat = jnp.concatenate(kdup[g][c:c + 3], axis=0)
            s = _dot_nt(qm, kcat) + bias_ref[variant, hd]
            sk = sink_ref[hd] * LOG2E
            m = jnp.maximum(jnp.max(s, axis=-1, keepdims=True), sk)
            probs.append(jnp.exp2(s - m).astype(BF16))
            sink_terms.append(jnp.exp2(sk - m))

        outs = []
        for j in range(A_HEADS // 2):
            g = (2 * j) // (A_HEADS // A_KV_HEADS)
            v_e = jnp.concatenate(v_even[g][c:c + 3], axis=0)
            v_o = jnp.concatenate(v_odd[g][c:c + 3], axis=0)
            o = (_dot(probs[2 * j], jnp.concatenate([v_e, ones_lo], axis=1))
                 + _dot(probs[2 * j + 1], jnp.concatenate([v_o, ones_hi], axis=1)))
            denom = o[:, LANES:] + jnp.where(lo, sink_terms[2 * j], sink_terms[2 * j + 1])
            outs.append(o[:, :LANES] / denom)
        ya = jnp.concatenate(outs, axis=1)
        o_ref[rows, :] = _rms(ya, g_ref[...]).astype(BF16)


def _window_attn(qa, ka, va, rel_bias, sink, g_out_a, B, S):
    nb = S // BLK
    n_steps = nb // WIN_QBLOCKS
    assert nb >= 2, "first and last query blocks use distinct edge masks"
    bmap = jnp.asarray(_bucket_map())
    smem = pl.BlockSpec(memory_space=pltpu.SMEM)
    cur = lambda n: pl.BlockSpec((WIN_QBLOCKS * BLK, n), lambda b, i: (b * n_steps + i, 0))
    prv = lambda n: pl.BlockSpec((BLK, n), lambda b, i: (b * nb + jnp.maximum(i * WIN_QBLOCKS - 1, 0), 0))
    nxt = lambda n: pl.BlockSpec((BLK, n), lambda b, i: (b * nb + jnp.minimum((i + 1) * WIN_QBLOCKS, nb - 1), 0))
    return pl.pallas_call(
        functools.partial(_window_attn_kernel, n_steps),
        grid=(B, n_steps),
        in_specs=[smem, smem,
                  pl.BlockSpec((BLK, 3 * BLK), lambda b, i: (0, 0)),
                  pl.BlockSpec((1, 512), lambda b, i: (0, 0)),
                  cur(512), prv(LANES), cur(LANES), nxt(LANES), prv(LANES), cur(LANES), nxt(LANES)],
        out_specs=cur(512),
        out_shape=jax.ShapeDtypeStruct((B * S, 512), BF16),
        scratch_shapes=[pltpu.VMEM((3, A_HEADS, BLK, 3 * BLK), F32)],
        compiler_params=_cparams(2),
        name="window_attn",
    )(rel_bias, sink, bmap, g_out_a, qa, ka, ka, ka, va, va, va)


def _mla_attn_kernel(n_cast, g_ref, q_ref, k_ref, v_ref, *rest):
    o_ref = rest[n_cast]
    for src, dst in zip(rest[:n_cast], rest[n_cast + 1:]):
        dst[...] = src[...].astype(BF16)
    S = k_ref.shape[0]
    lane = lax.broadcasted_iota(jnp.int32, (1, LANES), 1)
    lo = lane < B_V_DIM
    ones_lo = jnp.broadcast_to(jnp.where(lo, 1.0, 0.0).astype(BF16), (S, LANES))
    ones_hi = jnp.broadcast_to(jnp.where(lo, 0.0, 1.0).astype(BF16), (S, LANES))
    v_pairs = []
    for j in range(B_HEADS // 2):
        vt = v_ref[:, j * LANES:(j + 1) * LANES]
        zero = jnp.zeros_like(vt)
        v_pairs.append((jnp.concatenate([jnp.where(lo, vt, zero), ones_lo], axis=1),
                        jnp.concatenate([jnp.where(lo, zero, vt), ones_hi], axis=1)))
    for r in range(q_ref.shape[0] // MLA_ROWS):
        rows = slice(r * MLA_ROWS, (r + 1) * MLA_ROWS)
        outs = []
        for j in range(B_HEADS // 2):
            ps = []
            for hd in (2 * j, 2 * j + 1):
                sl = slice(hd * LANES, (hd + 1) * LANES)
                s = _dot_nt(q_ref[rows, sl], k_ref[:, sl])
                m = jnp.max(s, axis=-1, keepdims=True)
                ps.append(jnp.exp2(s - m).astype(BF16))
            o = _dot(ps[0], v_pairs[j][0]) + _dot(ps[1], v_pairs[j][1])
            outs.append(o[:, :LANES] / o[:, LANES:])
        yb = jnp.concatenate(outs, axis=1)
        o_ref[rows, :] = _rms(yb, g_ref[...]).astype(BF16)


def _mla_attn(qm, km, vm, g_out_b, B, S, cast_weights):
    nq = S // TQ
    steps = B * nq
    qspec = lambda n: pl.BlockSpec((TQ, n), lambda b, i: (b * nq + i, 0))
    kspec = lambda n: pl.BlockSpec((S, n), lambda b, i: (b, 0))
    flat = [w.reshape(-1, w.shape[-1]) for w in cast_weights]
    slab = lambda w: pl.BlockSpec((w.shape[0] // steps, w.shape[1]), lambda b, i: (b * nq + i, 0))
    outs = pl.pallas_call(
        functools.partial(_mla_attn_kernel, len(flat)),
        grid=(B, nq),
        in_specs=[pl.BlockSpec((1, 512), lambda b, i: (0, 0)), qspec(1024), kspec(1024), kspec(512)]
                 + [slab(w) for w in flat],
        out_specs=[qspec(512)] + [slab(w) for w in flat],
        out_shape=[jax.ShapeDtypeStruct((B * S, 512), BF16)]
                  + [jax.ShapeDtypeStruct(w.shape, BF16) for w in flat],
        compiler_params=_cparams(2),
        name="mla_attn",
    )(g_out_b, qm, km, vm, *flat)
    return outs[0], [o.reshape(w.shape) for o, w in zip(outs[1:], cast_weights)]


def _out_proj_kernel(x_ref, ya_ref, yb_ref, wo_ref, g_ref, wr_ref, x1_ref, h2_ref, aff_ref, afft_ref):
    x1 = x_ref[...] + _dot(ya_ref[...], wo_ref[0:512, :]) + _dot(yb_ref[...], wo_ref[512:1024, :])
    x1_ref[...] = x1
    h2 = _rms(x1, g_ref[...])
    h2b = h2.astype(BF16)
    h2r = h2b.astype(F32)
    for j in range(SLAB_ROWS):
        h2_ref[pl.ds(j, h2r.shape[0], stride=SLAB_ROWS), :] = h2r[:, j * LANES:(j + 1) * LANES]
    logits = _dot_nt(wr_ref[...].astype(BF16), h2b)
    m = jnp.max(logits, axis=0, keepdims=True)
    e = jnp.exp(logits - m)
    aff = e / jnp.sum(e, axis=0, keepdims=True)
    aff_ref[0] = aff
    pad = jnp.zeros((LANES - aff.shape[0], aff.shape[1]), F32)
    afft_ref[...] = jnp.concatenate([aff, pad], axis=0).T


def _out_proj(xf, ya, yb, w_out, g_ffn, w_router_t, B, S):
    T = xf.shape[0]
    per_b = S // TM
    row = lambda n: pl.BlockSpec((TM, n), lambda i: (i, 0))
    full = lambda a: pl.BlockSpec(a.shape, lambda i: (0, 0))
    return pl.pallas_call(
        _out_proj_kernel,
        grid=(T // TM,),
        in_specs=[row(D_MODEL), row(512), row(512), full(w_out), full(g_ffn), full(w_router_t)],
        out_specs=[row(D_MODEL), pl.BlockSpec((TM * SLAB_ROWS, LANES), lambda i: (i, 0)),
                   pl.BlockSpec((1, N_EXPERTS, TM), lambda i: (i // per_b, 0, i % per_b)), row(LANES)],
        out_shape=[jax.ShapeDtypeStruct((T, D_MODEL), F32), jax.ShapeDtypeStruct((T * SLAB_ROWS, LANES), F32),
                   jax.ShapeDtypeStruct((B, N_EXPERTS, S), F32), jax.ShapeDtypeStruct((T, LANES), F32)],
        compiler_params=_cparams(1),
        name="out_proj",
    )(xf, ya, yb, w_out, g_ffn, w_router_t)


CHUNK = 256
ROUTE_LOG_STEPS = 12
ROUTE_LIN_STEPS = 32


def _prefix_count(flags_f32, tri):
    S = flags_f32.shape[1]
    carry = jnp.zeros((flags_f32.shape[0], 1), F32)
    parts = []
    for c in range(S // CHUNK):
        blk = flags_f32[:, c * CHUNK:(c + 1) * CHUNK]
        parts.append(_dot(blk.astype(BF16), tri) + carry)
        carry = carry + jnp.sum(blk, axis=-1, keepdims=True)
    return jnp.concatenate(parts, axis=1)


def _route_kernel(cap, aff_ref, slot_ref):
    aff = aff_ref[...]
    rows = aff.shape[0]

    def enough(pivot):
        return jnp.sum(jnp.where(aff >= pivot, 1.0, 0.0), axis=-1, keepdims=True) >= cap

    def log_body(_, st):
        lo, hi, elo, ehi = st
        mid = 0.5 * (elo + ehi)
        pivot = jnp.exp2(mid)
        ok = enough(pivot)
        return (jnp.where(ok, pivot, lo), jnp.where(ok, hi, pivot), jnp.where(ok, mid, elo), jnp.where(ok, ehi, mid))

    def lin_body(_, st):
        lo, hi = st
        pivot = 0.5 * (lo + hi)
        ok = enough(pivot)
        return jnp.where(ok, pivot, lo), jnp.where(ok, hi, pivot)

    col = lambda v: jnp.full((rows, 1), v, F32)
    lo, hi, _, _ = lax.fori_loop(0, ROUTE_LOG_STEPS, log_body, (col(0.0), col(2.0), col(-152.0), col(1.0)))
    lo, hi = lax.fori_loop(0, ROUTE_LIN_STEPS, lin_body, (lo, hi))

    gt = jnp.where(aff >= hi, 1.0, 0.0)
    eq = jnp.where(aff >= lo, 1.0, 0.0) - gt
    need = cap - jnp.sum(gt, axis=-1, keepdims=True)
    r = lax.broadcasted_iota(jnp.int32, (CHUNK, CHUNK), 0)
    c = lax.broadcasted_iota(jnp.int32, (CHUNK, CHUNK), 1)
    tri = jnp.where(r < c, 1.0, 0.0).astype(BF16)
    sel = gt + eq * jnp.where(_prefix_count(eq, tri) < need, 1.0, 0.0)
    slot = _prefix_count(sel, tri)
    slot_ref[...] = jnp.where(sel > 0.5, slot, -1.0)


def _route(aff, cap):
    B, E, S = aff.shape
    spec = pl.BlockSpec((B * E, S), lambda i: (0, 0))
    return pl.pallas_call(
        functools.partial(_route_kernel, cap),
        grid=(1,),
        in_specs=[spec],
        out_specs=spec,
        out_shape=jax.ShapeDtypeStruct((B * E, S), F32),
        compiler_params=_cparams(1),
        name="route",
    )(aff.reshape(B * E, S))


def _moe_kernel(cap, slot_a_ref, slot_b_ref, next_a_ref, next_b_ref, h_ref, afft_ref,
                wg_ref, wu_ref, wd_ref, o_ref, xa_ref, xb_ref, ya_ref, yb_ref, aa_ref, ab_ref, tok_ref):
    S = h_ref.shape[0] // SLAB_ROWS
    stride = cap + SLAB_ROWS
    parity = pl.program_id(1) % 2
    slot_ids = lax.broadcasted_iota(jnp.int32, (cap, 1), 0).astype(F32)
    tok_ids = lax.broadcasted_iota(jnp.int32, (1, S), 1).astype(F32)

    def slot_affinities(a_ref, expert):
        lane = lax.broadcasted_iota(jnp.int32, (1, LANES), 1)
        return jnp.sum(jnp.where(lane == expert, a_ref[...], 0.0), axis=-1, keepdims=True)

    def gather(slot_row, x_ref, a_ref, which, par, done):
        sel = slot_ids == slot_row
        tok = jnp.sum(jnp.where(sel, tok_ids, 0.0), axis=-1, keepdims=True).astype(jnp.int32)
        tok = jnp.clip(tok, 0, S - 1)
        for grp in range(cap // GATHER_GROUP):
            ids = tok[grp * GATHER_GROUP:(grp + 1) * GATHER_GROUP, :]
            if len(done) >= 2:
                ids = jnp.minimum(ids, done[-2] + S)
            for j in range(GATHER_GROUP):
                c = grp * GATHER_GROUP + j
                t = ids[j, 0]
                row = pl.multiple_of(t * SLAB_ROWS, SLAB_ROWS)
                tok_ref[which, par, c] = row
                x_ref[pl.ds(c, SLAB_ROWS, stride=stride), :] = h_ref[pl.ds(row, SLAB_ROWS), :]
                a_ref[pl.ds(c, 1), :] = afft_ref[pl.ds(t, 1), :]
            done.append(t)

    def chunks_to_rows(tile_ref):
        return jnp.concatenate([tile_ref[j * stride:j * stride + cap, :] for j in range(SLAB_ROWS)], axis=1)

    def gate_up(x_ref, i):
        xg = chunks_to_rows(x_ref).astype(BF16)
        g = _dot(xg, wg_ref[i])
        u = _dot(xg, wu_ref[i])
        return (g * (1.0 / (1.0 + jnp.exp(-g))) * u).astype(BF16)

    def down(act, vals, y_ref, i):
        y = _dot(act, wd_ref[i]) * vals
        for j in range(SLAB_ROWS):
            y_ref[j * stride:j * stride + cap, :] = y[:, j * LANES:(j + 1) * LANES]

    def scatter_add(y_ref, which):
        for c0 in range(0, cap, SCATTER_GROUP):
            rows = [tok_ref[which, parity, c0 + j] for j in range(SCATTER_GROUP)]
            rows = [pl.multiple_of(r, SLAB_ROWS) for r in rows]
            new = [o_ref[pl.ds(rows[j], SLAB_ROWS), :] + y_ref[pl.ds(c0 + j, SLAB_ROWS, stride=stride), :]
                   for j in range(SCATTER_GROUP)]
            for j in range(SCATTER_GROUP):
                o_ref[pl.ds(rows[j], SLAB_ROWS), :] = new[j]

    @pl.when((pl.program_id(0) == 0) & (pl.program_id(1) == 0))
    def _first_step():
        moved = []
        gather(slot_a_ref[0], xa_ref, aa_ref, 0, 0, moved)
        gather(slot_b_ref[0], xb_ref, ab_ref, 1, 0, moved)

    @pl.when(pl.program_id(1) == 0)
    def _first_of_sequence():
        o_ref[...] = jnp.zeros_like(o_ref)

    expert_a = 2 * pl.program_id(1)
    vals_a = slot_affinities(aa_ref, expert_a)
    vals_b = slot_affinities(ab_ref, expert_a + 1)
    moved = []
    act_a = gate_up(xa_ref, 0)
    gather(next_a_ref[0], xa_ref, aa_ref, 0, 1 - parity, moved)
    act_b = gate_up(xb_ref, 1)
    gather(next_b_ref[0], xb_ref, ab_ref, 1, 1 - parity, moved)
    down(act_a, vals_a, ya_ref, 0)
    down(act_b, vals_b, yb_ref, 1)
    scatter_add(ya_ref, 0)
    scatter_add(yb_ref, 1)


def _moe(slot, afft, h2, wg, wu, wd, B, E, cap):
    S = slot.shape[1]
    assert E % 2 == 0 and (E // 2) % 2 == 0 and h2.shape == (B * S * SLAB_ROWS, LANES) and E <= LANES
    n_steps = B * E // 2
    last = B * E - 1
    slot3 = slot.reshape(B * E, 1, S)
    row = lambda d: pl.BlockSpec((1, 1, S), lambda b, k: (jnp.minimum(b * E + 2 * k + d, last), 0, 0))
    cur_a, cur_b, nxt_a, nxt_b = row(0), row(1), row(2), row(3)
    tok = pl.BlockSpec((S * SLAB_ROWS, LANES), lambda b, k: (b, 0))
    ahead_seq = lambda b, k: (jnp.minimum(b * (E // 2) + k + 1, n_steps - 1) // (E // 2), 0)
    ahead = pl.BlockSpec((S * SLAB_ROWS, LANES), ahead_seq)
    ahead_aff = pl.BlockSpec((S, LANES), ahead_seq)
    tile = pltpu.VMEM((SLAB_ROWS * (cap + SLAB_ROWS), LANES), F32)
    aff_rows = pltpu.VMEM((cap, LANES), F32)
    wspec = lambda w: pl.BlockSpec((2,) + w.shape[1:], lambda b, k: (k, 0, 0))
    return pl.pallas_call(
        functools.partial(_moe_kernel, cap),
        grid=(B, E // 2),
        in_specs=[cur_a, cur_b, nxt_a, nxt_b, ahead, ahead_aff, wspec(wg), wspec(wu), wspec(wd)],
        out_specs=tok,
        out_shape=jax.ShapeDtypeStruct(h2.shape, F32),
        scratch_shapes=[tile, tile, tile, tile, aff_rows, aff_rows, pltpu.SMEM((2, 2, cap), jnp.int32)],
        compiler_params=_cparams(2),
        name="moe",
    )(slot3, slot3, slot3, slot3, h2, afft, wg, wu, wd)


def _ple_final_kernel(x1_ref, moe_ref, p_ref, gp_ref, wg_ref, wp_ref, gf_ref, o_ref):
    rows = x1_ref.shape[0]
    moe = jnp.concatenate([moe_ref[pl.ds(j, rows, stride=SLAB_ROWS), :] for j in range(SLAB_ROWS)], axis=1)
    x2 = x1_ref[...] + moe
    z = _dot(_rms(x2, gp_ref[...]).astype(BF16), wg_ref[...])
    gate = 1.0 / (1.0 + jnp.exp(-z))
    x3 = x2 + gate * _dot(p_ref[...].astype(BF16), wp_ref[...])
    o_ref[...] = _rms(x3, gf_ref[...])


def _ple_final(x1, moe, pf, g_ple, w_gate, w_proj, g_final):
    T = x1.shape[0]
    row = lambda n: pl.BlockSpec((TM, n), lambda i: (i, 0))
    full = lambda a: pl.BlockSpec(a.shape, lambda i: (0, 0))
    return pl.pallas_call(
        _ple_final_kernel,
        grid=(T // TM,),
        in_specs=[row(D_MODEL), pl.BlockSpec((TM * SLAB_ROWS, LANES), lambda i: (i, 0)), row(PLE_DIM),
                  full(g_ple), full(w_gate), full(w_proj), full(g_final)],
        out_specs=row(D_MODEL),
        out_shape=jax.ShapeDtypeStruct((T, D_MODEL), F32),
        compiler_params=_cparams(1),
        name="ple_final",
    )(x1, moe, pf, g_ple, w_gate, w_proj, g_final)


def _prep_in_proj_weight(w_in):
    D = w_in.shape[0]
    t1, t2 = w_in[:, 1152:1168], w_in[:, 1168:1184]
    cols = [w_in[:, 0:1152], t1, t2, t2, t1, jnp.zeros((D, 64), F32)]
    return jnp.concatenate(cols, axis=1).astype(BF16)


def _prep_mla_weights(w_uq, w_ukv):
    per_q = B_NOPE_DIM + B_ROPE_DIM
    w3 = w_uq.reshape(Q_RANK, B_HEADS, per_q)
    pad = LANES - per_q
    wq = jnp.pad(w3, ((0, 0), (0, 0), (0, pad))).reshape(Q_RANK, B_HEADS * LANES)
    t1 = w3[:, :, B_NOPE_DIM:B_NOPE_DIM + HALF_ROPE]
    t2 = w3[:, :, B_NOPE_DIM + HALF_ROPE:]
    wqr = jnp.concatenate([jnp.zeros((Q_RANK, B_HEADS, B_NOPE_DIM), F32), t2, t1,
                           jnp.zeros((Q_RANK, B_HEADS, pad), F32)], axis=2).reshape(Q_RANK, B_HEADS * LANES)
    w4 = w_ukv.reshape(KV_RANK, B_HEADS, B_NOPE_DIM + B_V_DIM)
    zk = jnp.zeros((KV_RANK, B_HEADS, LANES - B_NOPE_DIM), F32)
    wk = jnp.concatenate([w4[:, :, :B_NOPE_DIM], zk], axis=2).reshape(KV_RANK, B_HEADS * LANES)
    wv = w4[:, :, B_NOPE_DIM:].reshape(KV_RANK, B_HEADS * B_V_DIM)
    wkv = jnp.concatenate([wk, wv], axis=1)
    return wq.astype(BF16), wqr.astype(BF16), wkv.astype(BF16)


def kernel(x, p, positions, rel_bias, norm_mix_g, w_in, sink, g_cq, g_ckv, w_uq, w_ukv, g_out_a, g_out_b, w_out,
           norm_ffn_g, w_router, w_e_gate, w_e_up, w_e_down, norm_ple_g, w_ple_gate, w_ple_proj, final_norm_g):
    B, S, D = x.shape
    T = B * S
    cap = CAPACITY_FACTOR * S // N_EXPERTS
    xf = x.reshape(T, D)
    cos_d, sin_d = _rope_tables(positions)
    assert w_in.shape[0] == 1, "single-layer block: the final norm is fused into the last kernel"
    i = 0
    w1 = _prep_in_proj_weight(w_in[i])
    wq, wqr, wkv = _prep_mla_weights(w_uq[i], w_ukv[i])
    qa, ka, va, qm, km, vm = _in_proj(
        xf, norm_mix_g[i].reshape(1, D), w1, g_cq[i].reshape(1, -1), g_ckv[i].reshape(1, -1),
        wq, wqr, wkv, cos_d, sin_d)
    ya = _window_attn(qa, ka, va, rel_bias, sink[i], g_out_a[i].reshape(1, -1), B, S)
    yb, (wg, wu, wd) = _mla_attn(qm, km, vm, g_out_b[i].reshape(1, -1), B, S,
                                 (w_e_gate[i], w_e_up[i], w_e_down[i]))
    x1, h2, aff, afft = _out_proj(xf, ya, yb, w_out[i].astype(BF16), norm_ffn_g[i].reshape(1, D),
                                  w_router[i].T, B, S)
    slot = _route(aff, cap)
    moe = _moe(slot, afft, h2, wg, wu, wd, B, N_EXPERTS, cap)
    out = _ple_final(x1, moe, p[i].reshape(T, -1), norm_ple_g[i].reshape(1, D),
                     w_ple_gate[i].astype(BF16), w_ple_proj[i].astype(BF16), final_norm_g.reshape(1, D))
    return out.reshape(B, S, D)
```

```python
import functools

import numpy as np
import jax
import jax.numpy as jnp
from jax import lax
from jax.experimental import pallas as pl
from jax.experimental.pallas import tpu as pltpu

F32 = jnp.float32
BF16 = jnp.bfloat16

D_MODEL = 1024
BLK = 128
WINDOW = 128
A_HEADS = 8
A_KV_HEADS = 2
A_HEAD_DIM = 64
B_HEADS = 8
B_NOPE_DIM = 64
B_ROPE_DIM = 32
B_V_DIM = 64
Q_RANK = 256
KV_RANK = 128
ROPE_THETA = 10000.0
NUM_BUCKETS = 32
MAX_DISTANCE = 128
N_EXPERTS = 16
EXPERT_FF = 512
CAPACITY_FACTOR = 2
PLE_DIM = 256
EPS = 1e-6

LANES = 128
HALF_ROPE = B_ROPE_DIM // 2
MASKED = -1e30
LOG2E = 1.4426950408889634
VMEM_LIMIT = 56 * 1024 * 1024
TM = 1024
TQ = 512
MLA_ROWS = 128
WIN_QBLOCKS = 16
GATHER_GROUP = 8
SCATTER_GROUP = 8
SLAB_ROWS = D_MODEL // LANES


def _cparams(n_axes):
    return pltpu.CompilerParams(dimension_semantics=("arbitrary",) * n_axes,
                                vmem_limit_bytes=VMEM_LIMIT)


def _rms(x, g):
    return x * lax.rsqrt(jnp.mean(x * x, axis=-1, keepdims=True) + EPS) * g


def _dot(a, b):
    return jnp.dot(a, b, preferred_element_type=F32)


def _dot_nt(a, b):
    return lax.dot_general(a, b, (((1,), (1,)), ((), ())), preferred_element_type=F32)


def _dot_tn(a, b):
    return lax.dot_general(a, b, (((0,), (0,)), ((), ())), preferred_element_type=F32)


ROPE_GROUPS = LANES // HALF_ROPE


def _rope_table_kernel(pos_ref, invf_ref, cos_ref, sin_ref):
    ang = pos_ref[...] * invf_ref[...]
    cos_ref[...] = jnp.cos(ang)
    sin_ref[...] = jnp.sin(ang)


def _rope_tables(positions):
    T = positions.size
    rows = T // ROPE_GROUPS
    inv_freq = 1.0 / (ROPE_THETA ** (jnp.arange(0, B_ROPE_DIM, 2, dtype=F32) / B_ROPE_DIM))
    pos = jnp.repeat(positions.astype(F32).reshape(ROPE_GROUPS, rows).T, HALF_ROPE, axis=1)
    invf = jnp.tile(inv_freq, ROPE_GROUPS).reshape(1, LANES)
    spec = pl.BlockSpec((rows, LANES), lambda i: (0, 0))
    return pl.pallas_call(
        _rope_table_kernel,
        grid=(1,),
        in_specs=[spec, pl.BlockSpec((1, LANES), lambda i: (0, 0))],
        out_specs=[spec] * 2,
        out_shape=[jax.ShapeDtypeStruct((rows, LANES), F32)] * 2,
        compiler_params=_cparams(1),
        name="rope_tables",
    )(pos, invf)


def _expand_rope(cd, sd, group):
    lane = lax.broadcasted_iota(jnp.int32, (1, LANES), 1)
    own = (lane >= group * HALF_ROPE) & (lane < (group + 1) * HALF_ROPE)
    xc = jnp.where(own, cd, 0.0)
    xs = jnp.where(own, sd, 0.0)
    shift = HALF_ROPE
    while shift < LANES:
        xc = xc + pltpu.roll(xc, shift, axis=1)
        xs = xs + pltpu.roll(xs, shift, axis=1)
        shift *= 2
    in_t1 = (lane >= B_NOPE_DIM) & (lane < B_NOPE_DIM + HALF_ROPE)
    in_t2 = (lane >= B_NOPE_DIM + HALF_ROPE) & (lane < B_NOPE_DIM + B_ROPE_DIM)
    c = jnp.where(lane < B_NOPE_DIM, 1.0, jnp.where(in_t1 | in_t2, xc, 0.0))
    s = jnp.where(in_t1, -xs, jnp.where(in_t2, xs, 0.0))
    return c, s


C_QA = 0
C_KA = C_QA + 512
C_VA = C_KA + LANES
C_CQ = C_VA + LANES
C_CKV = C_CQ + Q_RANK
C_KR = C_CKV + KV_RANK
C_END = C_KR + LANES
MLA_K_COLS = B_HEADS * LANES


def _in_proj_kernel(tiles_per_group, x_ref, g_ref, w1_ref, gcq_ref, gckv_ref, wq_ref, wqr_ref, wkv_ref,
                    cos_ref, sin_ref, qa_ref, ka_ref, va_ref, qm_ref, km_ref, vm_ref):
    h = _rms(x_ref[...], g_ref[...]).astype(BF16)
    z = _dot(h, w1_ref[...])
    qa_ref[...] = (z[:, C_QA:C_KA] * (A_HEAD_DIM ** -0.5 * LOG2E)).astype(BF16)
    ka_ref[...] = z[:, C_KA:C_VA].astype(BF16)
    va_ref[...] = z[:, C_VA:C_CQ].astype(BF16)
    cqn = _rms(z[:, C_CQ:C_CKV], gcq_ref[...]).astype(BF16)
    ckvn = _rms(z[:, C_CKV:C_KR], gckv_ref[...]).astype(BF16)
    krt = z[:, C_KR:C_END]

    lane = lax.broadcasted_iota(jnp.int32, (1, LANES), 1)
    c_full, s_rope = _expand_rope(cos_ref[...], sin_ref[...], pl.program_id(0) // tiles_per_group)
    c_rope = jnp.where(lane < B_NOPE_DIM, 0.0, c_full)

    q1 = _dot(cqn, wq_ref[...])
    q2 = _dot(cqn, wqr_ref[...])
    scale = (B_NOPE_DIM + B_ROPE_DIM) ** -0.5 * LOG2E
    for hd in range(B_HEADS):
        sl = slice(hd * LANES, (hd + 1) * LANES)
        qm_ref[:, sl] = ((q1[:, sl] * c_full + q2[:, sl] * s_rope) * scale).astype(BF16)

    kv = _dot(ckvn, wkv_ref[...])
    kr_part = pltpu.roll(krt, 64, axis=1) * c_rope + pltpu.roll(krt, 32, axis=1) * s_rope
    for hd in range(B_HEADS):
        sl = slice(hd * LANES, (hd + 1) * LANES)
        km_ref[:, sl] = (kv[:, sl] + kr_part).astype(BF16)
    vm_ref[...] = kv[:, MLA_K_COLS:].astype(BF16)


def _in_proj(xf, g_mix, w1, g_cq, g_ckv, wq, wqr, wkv, cos_d, sin_d):
    T = xf.shape[0]
    tiles_per_group = cos_d.shape[0] // TM
    assert tiles_per_group * TM == cos_d.shape[0]
    row = lambda n: pl.BlockSpec((TM, n), lambda i: (i, 0))
    full = lambda a: pl.BlockSpec(a.shape, lambda i: (0, 0))
    dense = pl.BlockSpec((TM, LANES), lambda i: (i % tiles_per_group, 0))
    outs = [512, LANES, LANES, MLA_K_COLS, MLA_K_COLS, B_HEADS * B_V_DIM]
    return pl.pallas_call(
        functools.partial(_in_proj_kernel, tiles_per_group),
        grid=(T // TM,),
        in_specs=[row(D_MODEL), full(g_mix), full(w1), full(g_cq), full(g_ckv), full(wq), full(wqr), full(wkv),
                  dense, dense],
        out_specs=[row(n) for n in outs],
        out_shape=[jax.ShapeDtypeStruct((T, n), BF16) for n in outs],
        compiler_params=_cparams(1),
        name="in_proj",
    )(xf, g_mix, w1, g_cq, g_ckv, wq, wqr, wkv, cos_d, sin_d)


def _bucket_map():
    qi = np.arange(BLK)[:, None]
    kj = np.arange(3 * BLK)[None, :]
    rel = kj - BLK - qi
    n = np.abs(rel)
    half = NUM_BUCKETS // 2
    max_exact = half // 2
    thresholds = [int(np.ceil(max_exact * 2 ** (k / 2) - 1e-9)) for k in range(1, half - max_exact)]
    large = max_exact + sum((n >= t).astype(np.int64) for t in thresholds)
    large = np.minimum(large, half - 1)
    bucket = np.where(rel > 0, half, 0) + np.where(n < max_exact, n, large)
    return np.where(n <= WINDOW, bucket, -1).astype(np.int32)


def _window_attn_kernel(n_steps, relb_ref, sink_ref, bmap_ref, g_ref, q_ref, kp_ref, kc_ref, kn_ref,
                        vp_ref, vc_ref, vn_ref, o_ref, bias_ref):
    b = pl.program_id(0)
    i = pl.program_id(1)

    @pl.when((b == 0) & (i == 0))
    def _build_bias():
        bmap = bmap_ref[...]
        col = lax.broadcasted_iota(jnp.int32, (1, 3 * BLK), 1)
        for hd in range(A_HEADS):
            bias_ref[1, hd] = jnp.full((BLK, 3 * BLK), MASKED, F32)
        for bk in range(NUM_BUCKETS):
            m = bmap == bk
            for hd in range(A_HEADS):
                bias_ref[1, hd] = jnp.where(m, relb_ref[bk, hd] * LOG2E, bias_ref[1, hd])
        for hd in range(A_HEADS):
            bias_ref[0, hd] = jnp.where(col < BLK, MASKED, bias_ref[1, hd])
            bias_ref[2, hd] = jnp.where(col >= 2 * BLK, MASKED, bias_ref[1, hd])

    lane = lax.broadcasted_iota(jnp.int32, (1, LANES), 1)
    lo = lane < A_HEAD_DIM
    ones_lo = jnp.broadcast_to(jnp.where(lo, 1.0, 0.0).astype(BF16), (3 * BLK, LANES))
    ones_hi = jnp.broadcast_to(jnp.where(lo, 0.0, 1.0).astype(BF16), (3 * BLK, LANES))

    def key_blocks(p_ref, c_ref, n_ref):
        return [p_ref[...]] + [c_ref[c * BLK:(c + 1) * BLK, :] for c in range(WIN_QBLOCKS)] + [n_ref[...]]

    def lane_swap(t):
        return pltpu.bitcast(pltpu.roll(pltpu.bitcast(t, jnp.int32), LANES // 2, axis=1), BF16)

    kdup = [[], []]
    v_even = [[], []]
    v_odd = [[], []]
    for kt, vt in zip(key_blocks(kp_ref, kc_ref, kn_ref), key_blocks(vp_ref, vc_ref, vn_ref)):
        ks, vs, zero = lane_swap(kt), lane_swap(vt), jnp.zeros_like(vt)
        kdup[0].append(jnp.where(lo, kt, ks))
        kdup[1].append(jnp.where(lo, ks, kt))
        v_even[0].append(jnp.where(lo, vt, zero))
        v_odd[0].append(jnp.where(lo, zero, vs))
        v_even[1].append(jnp.where(lo, vs, zero))
        v_odd[1].append(jnp.where(lo, zero, vt))

    for c in range(WIN_QBLOCKS):
        rows = slice(c * BLK, (c + 1) * BLK)
        if c == 0:
            variant = jnp.where(i == 0, 0, 1)
        elif c == WIN_QBLOCKS - 1:
            variant = jnp.where(i == n_steps - 1, 2, 1)
        else:
            variant = 1
        probs, sink_terms = [], []
        for hd in range(A_HEADS):
            g = hd // (A_HEADS // A_KV_HEADS)
            qt = q_ref[rows, (hd // 2) * LANES:(hd // 2 + 1) * LANES]
            qm = jnp.where(lo if hd % 2 == 0 else ~lo, qt, jnp.zeros_like(qt))
            kcat = jnp.concatenate(kdup[g][c:c + 3], axis=0)
            s = _dot_nt(qm, kcat) + bias_ref[variant, hd]
            sk = sink_ref[hd] * LOG2E
            m = jnp.maximum(jnp.max(s, axis=-1, keepdims=True), sk)
            probs.append(jnp.exp2(s - m).astype(BF16))
            sink_terms.append(jnp.exp2(sk - m))

        outs = []
        for j in range(A_HEADS // 2):
            g = (2 * j) // (A_HEADS // A_KV_HEADS)
            v_e = jnp.concatenate(v_even[g][c:c + 3], axis=0)
            v_o = jnp.concatenate(v_odd[g][c:c + 3], axis=0)
            o = (_dot(probs[2 * j], jnp.concatenate([v_e, ones_lo], axis=1))
                 + _dot(probs[2 * j + 1], jnp.concatenate([v_o, ones_hi], axis=1)))
            denom = o[:, LANES:] + jnp.where(lo, sink_terms[2 * j], sink_terms[2 * j + 1])
            outs.append(o[:, :LANES] / denom)
        ya = jnp.concatenate(outs, axis=1)
        o_ref[rows, :] = _rms(ya, g_ref[...]).astype(BF16)


def _window_attn(qa, ka, va, rel_bias, sink, g_out_a, B, S):
    nb = S // BLK
    n_steps = nb // WIN_QBLOCKS
    assert nb >= 2, "first and last query blocks use distinct edge masks"
    bmap = jnp.asarray(_bucket_map())
    smem = pl.BlockSpec(memory_space=pltpu.SMEM)
    cur = lambda n: pl.BlockSpec((WIN_QBLOCKS * BLK, n), lambda b, i: (b * n_steps + i, 0))
    prv = lambda n: pl.BlockSpec((BLK, n), lambda b, i: (b * nb + jnp.maximum(i * WIN_QBLOCKS - 1, 0), 0))
    nxt = lambda n: pl.BlockSpec((BLK, n), lambda b, i: (b * nb + jnp.minimum((i + 1) * WIN_QBLOCKS, nb - 1), 0))
    return pl.pallas_call(
        functools.partial(_window_attn_kernel, n_steps),
        grid=(B, n_steps),
        in_specs=[smem, smem,
                  pl.BlockSpec((BLK, 3 * BLK), lambda b, i: (0, 0)),
                  pl.BlockSpec((1, 512), lambda b, i: (0, 0)),
                  cur(512), prv(LANES), cur(LANES), nxt(LANES), prv(LANES), cur(LANES), nxt(LANES)],
        out_specs=cur(512),
        out_shape=jax.ShapeDtypeStruct((B * S, 512), BF16),
        scratch_shapes=[pltpu.VMEM((3, A_HEADS, BLK, 3 * BLK), F32)],
        compiler_params=_cparams(2),
        name="window_attn",
    )(rel_bias, sink, bmap, g_out_a, qa, ka, ka, ka, va, va, va)


def _mla_attn_kernel(n_cast, g_ref, q_ref, k_ref, v_ref, *rest):
    o_ref = rest[n_cast]
    for src, dst in zip(rest[:n_cast], rest[n_cast + 1:]):
        dst[...] = src[...].astype(BF16)
    S = k_ref.shape[0]
    lane = lax.broadcasted_iota(jnp.int32, (1, LANES), 1)
    lo = lane < B_V_DIM
    ones_lo = jnp.broadcast_to(jnp.where(lo, 1.0, 0.0).astype(BF16), (S, LANES))
    ones_hi = jnp.broadcast_to(jnp.where(lo, 0.0, 1.0).astype(BF16), (S, LANES))
    v_pairs = []
    for j in range(B_HEADS // 2):
        vt = v_ref[:, j * LANES:(j + 1) * LANES]
        zero = jnp.zeros_like(vt)
        v_pairs.append((jnp.concatenate([jnp.where(lo, vt, zero), ones_lo], axis=1),
                        jnp.concatenate([jnp.where(lo, zero, vt), ones_hi], axis=1)))
    for r in range(q_ref.shape[0] // MLA_ROWS):
        rows = slice(r * MLA_ROWS, (r + 1) * MLA_ROWS)
        outs = []
        for j in range(B_HEADS // 2):
            ps = []
            for hd in (2 * j, 2 * j + 1):
                sl = slice(hd * LANES, (hd + 1) * LANES)
                s = _dot_nt(q_ref[rows, sl], k_ref[:, sl])
                m = jnp.max(s, axis=-1, keepdims=True)
                ps.append(jnp.exp2(s - m).astype(BF16))
            o = _dot(ps[0], v_pairs[j][0]) + _dot(ps[1], v_pairs[j][1])
            outs.append(o[:, :LANES] / o[:, LANES:])
        yb = jnp.concatenate(outs, axis=1)
        o_ref[rows, :] = _rms(yb, g_ref[...]).astype(BF16)


def _mla_attn(qm, km, vm, g_out_b, B, S, cast_weights):
    nq = S // TQ
    steps = B * nq
    qspec = lambda n: pl.BlockSpec((TQ, n), lambda b, i: (b * nq + i, 0))
    kspec = lambda n: pl.BlockSpec((S, n), lambda b, i: (b, 0))
    flat = [w.reshape(-1, w.shape[-1]) for w in cast_weights]
    slab = lambda w: pl.BlockSpec((w.shape[0] // steps, w.shape[1]), lambda b, i: (b * nq + i, 0))
    outs = pl.pallas_call(
        functools.partial(_mla_attn_kernel, len(flat)),
        grid=(B, nq),
        in_specs=[pl.BlockSpec((1, 512), lambda b, i: (0, 0)), qspec(1024), kspec(1024), kspec(512)]
                 + [slab(w) for w in flat],
        out_specs=[qspec(512)] + [slab(w) for w in flat],
        out_shape=[jax.ShapeDtypeStruct((B * S, 512), BF16)]
                  + [jax.ShapeDtypeStruct(w.shape, BF16) for w in flat],
        compiler_params=_cparams(2),
        name="mla_attn",
    )(g_out_b, qm, km, vm, *flat)
    return outs[0], [o.reshape(w.shape) for o, w in zip(outs[1:], cast_weights)]


def _out_proj_kernel(x_ref, ya_ref, yb_ref, wo_ref, g_ref, wr_ref, x1_ref, h2_ref, aff_ref, afft_ref):
    x1 = x_ref[...] + _dot(ya_ref[...], wo_ref[0:512, :]) + _dot(yb_ref[...], wo_ref[512:1024, :])
    x1_ref[...] = x1
    h2 = _rms(x1, g_ref[...])
    h2b = h2.astype(BF16)
    h2r = h2b.astype(F32)
    for j in range(SLAB_ROWS):
        h2_ref[pl.ds(j, h2r.shape[0], stride=SLAB_ROWS), :] = h2r[:, j * LANES:(j + 1) * LANES]
    logits = _dot_nt(wr_ref[...].astype(BF16), h2b)
    m = jnp.max(logits, axis=0, keepdims=True)
    e = jnp.exp(logits - m)
    aff = e / jnp.sum(e, axis=0, keepdims=True)
    aff_ref[0] = aff
    pad = jnp.zeros((LANES - aff.shape[0], aff.shape[1]), F32)
    afft_ref[...] = jnp.concatenate([aff, pad], axis=0).T


def _out_proj(xf, ya, yb, w_out, g_ffn, w_router_t, B, S):
    T = xf.shape[0]
    per_b = S // TM
    row = lambda n: pl.BlockSpec((TM, n), lambda i: (i, 0))
    full = lambda a: pl.BlockSpec(a.shape, lambda i: (0, 0))
    return pl.pallas_call(
        _out_proj_kernel,
        grid=(T // TM,),
        in_specs=[row(D_MODEL), row(512), row(512), full(w_out), full(g_ffn), full(w_router_t)],
        out_specs=[row(D_MODEL), pl.BlockSpec((TM * SLAB_ROWS, LANES), lambda i: (i, 0)),
                   pl.BlockSpec((1, N_EXPERTS, TM), lambda i: (i // per_b, 0, i % per_b)), row(LANES)],
        out_shape=[jax.ShapeDtypeStruct((T, D_MODEL), F32), jax.ShapeDtypeStruct((T * SLAB_ROWS, LANES), F32),
                   jax.ShapeDtypeStruct((B, N_EXPERTS, S), F32), jax.ShapeDtypeStruct((T, LANES), F32)],
        compiler_params=_cparams(1),
        name="out_proj",
    )(xf, ya, yb, w_out, g_ffn, w_router_t)


CHUNK = 256
ROUTE_LOG_STEPS = 12
ROUTE_LIN_STEPS = 32


def _prefix_count(flags_f32, tri):
    S = flags_f32.shape[1]
    carry = jnp.zeros((flags_f32.shape[0], 1), F32)
    parts = []
    for c in range(S // CHUNK):
        blk = flags_f32[:, c * CHUNK:(c + 1) * CHUNK]
        parts.append(_dot(blk.astype(BF16), tri) + carry)
        carry = carry + jnp.sum(blk, axis=-1, keepdims=True)
    return jnp.concatenate(parts, axis=1)


def _route_kernel(cap, aff_ref, slot_ref):
    aff = aff_ref[...]
    rows = aff.shape[0]

    def enough(pivot):
        return jnp.sum(jnp.where(aff >= pivot, 1.0, 0.0), axis=-1, keepdims=True) >= cap

    def log_body(_, st):
        lo, hi, elo, ehi = st
        mid = 0.5 * (elo + ehi)
        pivot = jnp.exp2(mid)
        ok = enough(pivot)
        return (jnp.where(ok, pivot, lo), jnp.where(ok, hi, pivot), jnp.where(ok, mid, elo), jnp.where(ok, ehi, mid))

    def lin_body(_, st):
        lo, hi = st
        pivot = 0.5 * (lo + hi)
        ok = enough(pivot)
        return jnp.where(ok, pivot, lo), jnp.where(ok, hi, pivot)

    col = lambda v: jnp.full((rows, 1), v, F32)
    lo, hi, _, _ = lax.fori_loop(0, ROUTE_LOG_STEPS, log_body, (col(0.0), col(2.0), col(-152.0), col(1.0)))
    lo, hi = lax.fori_loop(0, ROUTE_LIN_STEPS, lin_body, (lo, hi))

    gt = jnp.where(aff >= hi, 1.0, 0.0)
    eq = jnp.where(aff >= lo, 1.0, 0.0) - gt
    need = cap - jnp.sum(gt, axis=-1, keepdims=True)
    r = lax.broadcasted_iota(jnp.int32, (CHUNK, CHUNK), 0)
    c = lax.broadcasted_iota(jnp.int32, (CHUNK, CHUNK), 1)
    tri = jnp.where(r < c, 1.0, 0.0).astype(BF16)
    sel = gt + eq * jnp.where(_prefix_count(eq, tri) < need, 1.0, 0.0)
    slot = _prefix_count(sel, tri)
    slot_ref[...] = jnp.where(sel > 0.5, slot, -1.0)


def _route(aff, cap):
    B, E, S = aff.shape
    spec = pl.BlockSpec((B * E, S), lambda i: (0, 0))
    return pl.pallas_call(
        functools.partial(_route_kernel, cap),
        grid=(1,),
        in_specs=[spec],
        out_specs=spec,
        out_shape=jax.ShapeDtypeStruct((B * E, S), F32),
        compiler_params=_cparams(1),
        name="route",
    )(aff.reshape(B * E, S))


def _moe_kernel(cap, slot_a_ref, slot_b_ref, next_a_ref, next_b_ref, h_ref, afft_ref,
                wg_ref, wu_ref, wd_ref, o_ref, xa_ref, xb_ref, ya_ref, yb_ref, aa_ref, ab_ref, tok_ref):
    S = h_ref.shape[0] // SLAB_ROWS
    stride = cap + SLAB_ROWS
    parity = pl.program_id(1) % 2
    slot_ids = lax.broadcasted_iota(jnp.int32, (cap, 1), 0).astype(F32)
    tok_ids = lax.broadcasted_iota(jnp.int32, (1, S), 1).astype(F32)

    def slot_affinities(a_ref, expert):
        lane = lax.broadcasted_iota(jnp.int32, (1, LANES), 1)
        return jnp.sum(jnp.where(lane == expert, a_ref[...], 0.0), axis=-1, keepdims=True)

    def gather(slot_row, x_ref, a_ref, which, par, done):
        sel = slot_ids == slot_row
        tok = jnp.sum(jnp.where(sel, tok_ids, 0.0), axis=-1, keepdims=True).astype(jnp.int32)
        tok = jnp.clip(tok, 0, S - 1)
        for grp in range(cap // GATHER_GROUP):
            ids = tok[grp * GATHER_GROUP:(grp + 1) * GATHER_GROUP, :]
            if len(done) >= 2:
                ids = jnp.minimum(ids, done[-2] + S)
            for j in range(GATHER_GROUP):
                c = grp * GATHER_GROUP + j
                t = ids[j, 0]
                row = pl.multiple_of(t * SLAB_ROWS, SLAB_ROWS)
                tok_ref[which, par, c] = row
                x_ref[pl.ds(c, SLAB_ROWS, stride=stride), :] = h_ref[pl.ds(row, SLAB_ROWS), :]
                a_ref[pl.ds(c, 1), :] = afft_ref[pl.ds(t, 1), :]
            done.append(t)

    def chunks_to_rows(tile_ref):
        return jnp.concatenate([tile_ref[j * stride:j * stride + cap, :] for j in range(SLAB_ROWS)], axis=1)

    def gate_up(x_ref, i):
        xg = chunks_to_rows(x_ref).astype(BF16)
        g = _dot(xg, wg_ref[i])
        u = _dot(xg, wu_ref[i])
        return (g * (1.0 / (1.0 + jnp.exp(-g))) * u).astype(BF16)

    def down(act, vals, y_ref, i):
        y = _dot(act, wd_ref[i]) * vals
        for j in range(SLAB_ROWS):
            y_ref[j * stride:j * stride + cap, :] = y[:, j * LANES:(j + 1) * LANES]

    def scatter_add(y_ref, which):
        for c0 in range(0, cap, SCATTER_GROUP):
            rows = [tok_ref[which, parity, c0 + j] for j in range(SCATTER_GROUP)]
            rows = [pl.multiple_of(r, SLAB_ROWS) for r in rows]
            new = [o_ref[pl.ds(rows[j], SLAB_ROWS), :] + y_ref[pl.ds(c0 + j, SLAB_ROWS, stride=stride), :]
                   for j in range(SCATTER_GROUP)]
            for j in range(SCATTER_GROUP):
                o_ref[pl.ds(rows[j], SLAB_ROWS), :] = new[j]

    @pl.when((pl.program_id(0) == 0) & (pl.program_id(1) == 0))
    def _first_step():
        moved = []
        gather(slot_a_ref[0], xa_ref, aa_ref, 0, 0, moved)
        gather(slot_b_ref[0], xb_ref, ab_ref, 1, 0, moved)

    @pl.when(pl.program_id(1) == 0)
    def _first_of_sequence():
        o_ref[...] = jnp.zeros_like(o_ref)

    expert_a = 2 * pl.program_id(1)
    vals_a = slot_affinities(aa_ref, expert_a)
    vals_b = slot_affinities(ab_ref, expert_a + 1)
    moved = []
    act_a = gate_up(xa_ref, 0)
    gather(next_a_ref[0], xa_ref, aa_ref, 0, 1 - parity, moved)
    act_b = gate_up(xb_ref, 1)
    gather(next_b_ref[0], xb_ref, ab_ref, 1, 1 - parity, moved)
    down(act_a, vals_a, ya_ref, 0)
    down(act_b, vals_b, yb_ref, 1)
    scatter_add(ya_ref, 0)
    scatter_add(yb_ref, 1)


def _moe(slot, afft, h2, wg, wu, wd, B, E, cap):
    S = slot.shape[1]
    assert E % 2 == 0 and (E // 2) % 2 == 0 and h2.shape == (B * S * SLAB_ROWS, LANES) and E <= LANES
    n_steps = B * E // 2
    last = B * E - 1
    slot3 = slot.reshape(B * E, 1, S)
    row = lambda d: pl.BlockSpec((1, 1, S), lambda b, k: (jnp.minimum(b * E + 2 * k + d, last), 0, 0))
    cur_a, cur_b, nxt_a, nxt_b = row(0), row(1), row(2), row(3)
    tok = pl.BlockSpec((S * SLAB_ROWS, LANES), lambda b, k: (b, 0))
    ahead_seq = lambda b, k: (jnp.minimum(b * (E // 2) + k + 1, n_steps - 1) // (E // 2), 0)
    ahead = pl.BlockSpec((S * SLAB_ROWS, LANES), ahead_seq)
    ahead_aff = pl.BlockSpec((S, LANES), ahead_seq)
    tile = pltpu.VMEM((SLAB_ROWS * (cap + SLAB_ROWS), LANES), F32)
    aff_rows = pltpu.VMEM((cap, LANES), F32)
    wspec = lambda w: pl.BlockSpec((2,) + w.shape[1:], lambda b, k: (k, 0, 0))
    return pl.pallas_call(
        functools.partial(_moe_kernel, cap),
        grid=(B, E // 2),
        in_specs=[cur_a, cur_b, nxt_a, nxt_b, ahead, ahead_aff, wspec(wg), wspec(wu), wspec(wd)],
        out_specs=tok,
        out_shape=jax.ShapeDtypeStruct(h2.shape, F32),
        scratch_shapes=[tile, tile, tile, tile, aff_rows, aff_rows, pltpu.SMEM((2, 2, cap), jnp.int32)],
        compiler_params=_cparams(2),
        name="moe",
    )(slot3, slot3, slot3, slot3, h2, afft, wg, wu, wd)


def _ple_final_kernel(x1_ref, moe_ref, p_ref, gp_ref, wg_ref, wp_ref, gf_ref, o_ref):
    rows = x1_ref.shape[0]
    moe = jnp.concatenate([moe_ref[pl.ds(j, rows, stride=SLAB_ROWS), :] for j in range(SLAB_ROWS)], axis=1)
    x2 = x1_ref[...] + moe
    z = _dot(_rms(x2, gp_ref[...]).astype(BF16), wg_ref[...])
    gate = 1.0 / (1.0 + jnp.exp(-z))
    x3 = x2 + gate * _dot(p_ref[...].astype(BF16), wp_ref[...])
    o_ref[...] = _rms(x3, gf_ref[...])


def _ple_final(x1, moe, pf, g_ple, w_gate, w_proj, g_final):
    T = x1.shape[0]
    row = lambda n: pl.BlockSpec((TM, n), lambda i: (i, 0))
    full = lambda a: pl.BlockSpec(a.shape, lambda i: (0, 0))
    return pl.pallas_call(
        _ple_final_kernel,
        grid=(T // TM,),
        in_specs=[row(D_MODEL), pl.BlockSpec((TM * SLAB_ROWS, LANES), lambda i: (i, 0)), row(PLE_DIM),
                  full(g_ple), full(w_gate), full(w_proj), full(g_final)],
        out_specs=row(D_MODEL),
        out_shape=jax.ShapeDtypeStruct((T, D_MODEL), F32),
        compiler_params=_cparams(1),
        name="ple_final",
    )(x1, moe, pf, g_ple, w_gate, w_proj, g_final)


def _prep_in_proj_weight(w_in):
    D = w_in.shape[0]
    t1, t2 = w_in[:, 1152:1168], w_in[:, 1168:1184]
    cols = [w_in[:, 0:1152], t1, t2, t2, t1, jnp.zeros((D, 64), F32)]
    return jnp.concatenate(cols, axis=1).astype(BF16)


def _prep_mla_weights(w_uq, w_ukv):
    per_q = B_NOPE_DIM + B_ROPE_DIM
    w3 = w_uq.reshape(Q_RANK, B_HEADS, per_q)
    pad = LANES - per_q
    wq = jnp.pad(w3, ((0, 0), (0, 0), (0, pad))).reshape(Q_RANK, B_HEADS * LANES)
    t1 = w3[:, :, B_NOPE_DIM:B_NOPE_DIM + HALF_ROPE]
    t2 = w3[:, :, B_NOPE_DIM + HALF_ROPE:]
    wqr = jnp.concatenate([jnp.zeros((Q_RANK, B_HEADS, B_NOPE_DIM), F32), t2, t1,
                           jnp.zeros((Q_RANK, B_HEADS, pad), F32)], axis=2).reshape(Q_RANK, B_HEADS * LANES)
    w4 = w_ukv.reshape(KV_RANK, B_HEADS, B_NOPE_DIM + B_V_DIM)
    zk = jnp.zeros((KV_RANK, B_HEADS, LANES - B_NOPE_DIM), F32)
    wk = jnp.concatenate([w4[:, :, :B_NOPE_DIM], zk], axis=2).reshape(KV_RANK, B_HEADS * LANES)
    wv = w4[:, :, B_NOPE_DIM:].reshape(KV_RANK, B_HEADS * B_V_DIM)
    wkv = jnp.concatenate([wk, wv], axis=1)
    return wq.astype(BF16), wqr.astype(BF16), wkv.astype(BF16)


def kernel(x, p, positions, rel_bias, norm_mix_g, w_in, sink, g_cq, g_ckv, w_uq, w_ukv, g_out_a, g_out_b, w_out,
           norm_ffn_g, w_router, w_e_gate, w_e_up, w_e_down, norm_ple_g, w_ple_gate, w_ple_proj, final_norm_g):
    B, S, D = x.shape
    T = B * S
    cap = CAPACITY_FACTOR * S // N_EXPERTS
    xf = x.reshape(T, D)
    cos_d, sin_d = _rope_tables(positions)
    assert w_in.shape[0] == 1, "single-layer block: the final norm is fused into the last kernel"
    i = 0
    w1 = _prep_in_proj_weight(w_in[i])
    wq, wqr, wkv = _prep_mla_weights(w_uq[i], w_ukv[i])
    qa, ka, va, qm, km, vm = _in_proj(
        xf, norm_mix_g[i].reshape(1, D), w1, g_cq[i].reshape(1, -1), g_ckv[i].reshape(1, -1),
        wq, wqr, wkv, cos_d, sin_d)
    ya = _window_attn(qa, ka, va, rel_bias, sink[i], g_out_a[i].reshape(1, -1), B, S)
    yb, (wg, wu, wd) = _mla_attn(qm, km, vm, g_out_b[i].reshape(1, -1), B, S,
                                 (w_e_gate[i], w_e_up[i], w_e_down[i]))
    x1, h2, aff, afft = _out_proj(xf, ya, yb, w_out[i].astype(BF16), norm_ffn_g[i].reshape(1, D),
                                  w_router[i].T, B, S)
    slot = _route(aff, cap)
    moe = _moe(slot, afft, h2, wg, wu, wd, B, N_EXPERTS, cap)
    out = _ple_final(x1, moe, p[i].reshape(T, -1), norm_ple_g[i].reshape(1, D),
                     w_ple_gate[i].astype(BF16), w_ple_proj[i].astype(BF16), final_norm_g.reshape(1, D))
    return out.reshape(B, S, D)
```

```python
import functools

import numpy as np
import jax
import jax.numpy as jnp
from jax import lax
from jax.experimental import pallas as pl
from jax.experimental.pallas import tpu as pltpu

F32 = jnp.float32
BF16 = jnp.bfloat16

D_MODEL = 1024
BLK = 128
WINDOW = 128
A_HEADS = 8
A_KV_HEADS = 2
A_HEAD_DIM = 64
B_HEADS = 8
B_NOPE_DIM = 64
B_ROPE_DIM = 32
B_V_DIM = 64
Q_RANK = 256
KV_RANK = 128
ROPE_THETA = 10000.0
NUM_BUCKETS = 32
MAX_DISTANCE = 128
N_EXPERTS = 16
EXPERT_FF = 512
CAPACITY_FACTOR = 2
PLE_DIM = 256
EPS = 1e-6

LANES = 128
HALF_ROPE = B_ROPE_DIM // 2
MASKED = -1e30
LOG2E = 1.4426950408889634
VMEM_LIMIT = 56 * 1024 * 1024
TM = 1024
TQ = 512
MLA_ROWS = 128
WIN_QBLOCKS = 8
GATHER_GROUP = 8
SCATTER_GROUP = 8
SLAB_ROWS = D_MODEL // LANES
W_SLOTS = 3
MOE_VMEM_LIMIT = 58 * 1024 * 1024


def _cparams(n_axes):
    return pltpu.CompilerParams(dimension_semantics=("arbitrary",) * n_axes,
                                vmem_limit_bytes=VMEM_LIMIT)


def _rms(x, g):
    return x * lax.rsqrt(jnp.mean(x * x, axis=-1, keepdims=True) + EPS) * g


def _dot(a, b):
    return jnp.dot(a, b, preferred_element_type=F32)


def _dot_nt(a, b):
    return lax.dot_general(a, b, (((1,), (1,)), ((), ())), preferred_element_type=F32)


def _dot_tn(a, b):
    return lax.dot_general(a, b, (((0,), (0,)), ((), ())), preferred_element_type=F32)


ROPE_GROUPS = LANES // HALF_ROPE


def _rope_table_kernel(pos_ref, invf_ref, cos_ref, sin_ref):
    ang = pos_ref[...] * invf_ref[...]
    cos_ref[...] = jnp.cos(ang)
    sin_ref[...] = jnp.sin(ang)


def _rope_tables(positions):
    T = positions.size
    rows = T // ROPE_GROUPS
    inv_freq = 1.0 / (ROPE_THETA ** (jnp.arange(0, B_ROPE_DIM, 2, dtype=F32) / B_ROPE_DIM))
    pos = jnp.repeat(positions.astype(F32).reshape(ROPE_GROUPS, rows).T, HALF_ROPE, axis=1)
    invf = jnp.tile(inv_freq, ROPE_GROUPS).reshape(1, LANES)
    spec = pl.BlockSpec((rows, LANES), lambda i: (0, 0))
    return pl.pallas_call(
        _rope_table_kernel,
        grid=(1,),
        in_specs=[spec, pl.BlockSpec((1, LANES), lambda i: (0, 0))],
        out_specs=[spec] * 2,
        out_shape=[jax.ShapeDtypeStruct((rows, LANES), F32)] * 2,
        compiler_params=_cparams(1),
        name="rope_tables",
    )(pos, invf)


def _expand_rope(cd, sd, group):
    lane = lax.broadcasted_iota(jnp.int32, (1, LANES), 1)
    own = (lane >= group * HALF_ROPE) & (lane < (group + 1) * HALF_ROPE)
    xc = jnp.where(own, cd, 0.0)
    xs = jnp.where(own, sd, 0.0)
    shift = HALF_ROPE
    while shift < LANES:
        xc = xc + pltpu.roll(xc, shift, axis=1)
        xs = xs + pltpu.roll(xs, shift, axis=1)
        shift *= 2
    in_t1 = (lane >= B_NOPE_DIM) & (lane < B_NOPE_DIM + HALF_ROPE)
    in_t2 = (lane >= B_NOPE_DIM + HALF_ROPE) & (lane < B_NOPE_DIM + B_ROPE_DIM)
    c = jnp.where(lane < B_NOPE_DIM, 1.0, jnp.where(in_t1 | in_t2, xc, 0.0))
    s = jnp.where(in_t1, -xs, jnp.where(in_t2, xs, 0.0))
    return c, s


C_QA = 0
C_KA = C_QA + 512
C_VA = C_KA + LANES
C_CQ = C_VA + LANES
C_CKV = C_CQ + Q_RANK
C_KR = C_CKV + KV_RANK
C_END = C_KR + LANES
MLA_K_COLS = B_HEADS * LANES


def _in_proj_kernel(tiles_per_group, x_ref, g_ref, w1_ref, gcq_ref, gckv_ref, wq_ref, wqr_ref, wkv_ref,
                    cos_ref, sin_ref, qa_ref, ka_ref, va_ref, qm_ref, km_ref, vm_ref):
    h = _rms(x_ref[...], g_ref[...]).astype(BF16)
    z = _dot(h, w1_ref[...])
    qa_ref[...] = (z[:, C_QA:C_KA] * (A_HEAD_DIM ** -0.5 * LOG2E)).astype(BF16)
    ka_ref[...] = z[:, C_KA:C_VA].astype(BF16)
    va_ref[...] = z[:, C_VA:C_CQ].astype(BF16)
    cqn = _rms(z[:, C_CQ:C_CKV], gcq_ref[...]).astype(BF16)
    ckvn = _rms(z[:, C_CKV:C_KR], gckv_ref[...]).astype(BF16)
    krt = z[:, C_KR:C_END]

    lane = lax.broadcasted_iota(jnp.int32, (1, LANES), 1)
    c_full, s_rope = _expand_rope(cos_ref[...], sin_ref[...], pl.program_id(0) // tiles_per_group)
    c_rope = jnp.where(lane < B_NOPE_DIM, 0.0, c_full)

    q1 = _dot(cqn, wq_ref[...])
    q2 = _dot(cqn, wqr_ref[...])
    scale = (B_NOPE_DIM + B_ROPE_DIM) ** -0.5 * LOG2E
    for hd in range(B_HEADS):
        sl = slice(hd * LANES, (hd + 1) * LANES)
        qm_ref[:, sl] = ((q1[:, sl] * c_full + q2[:, sl] * s_rope) * scale).astype(BF16)

    kv = _dot(ckvn, wkv_ref[...])
    kr_part = pltpu.roll(krt, 64, axis=1) * c_rope + pltpu.roll(krt, 32, axis=1) * s_rope
    for hd in range(B_HEADS):
        sl = slice(hd * LANES, (hd + 1) * LANES)
        km_ref[:, sl] = (kv[:, sl] + kr_part).astype(BF16)
    vm_ref[...] = kv[:, MLA_K_COLS:].astype(BF16)


def _in_proj(xf, g_mix, w1, g_cq, g_ckv, wq, wqr, wkv, cos_d, sin_d):
    T = xf.shape[0]
    tiles_per_group = cos_d.shape[0] // TM
    assert tiles_per_group * TM == cos_d.shape[0]
    row = lambda n: pl.BlockSpec((TM, n), lambda i: (i, 0))
    full = lambda a: pl.BlockSpec(a.shape, lambda i: (0, 0))
    dense = pl.BlockSpec((TM, LANES), lambda i: (i % tiles_per_group, 0))
    outs = [512, LANES, LANES, MLA_K_COLS, MLA_K_COLS, B_HEADS * B_V_DIM]
    return pl.pallas_call(
        functools.partial(_in_proj_kernel, tiles_per_group),
        grid=(T // TM,),
        in_specs=[row(D_MODEL), full(g_mix), full(w1), full(g_cq), full(g_ckv), full(wq), full(wqr), full(wkv),
                  dense, dense],
        out_specs=[row(n) for n in outs],
        out_shape=[jax.ShapeDtypeStruct((T, n), BF16) for n in outs],
        compiler_params=_cparams(1),
        name="in_proj",
    )(xf, g_mix, w1, g_cq, g_ckv, wq, wqr, wkv, cos_d, sin_d)


def _bucket_map():
    qi = np.arange(BLK)[:, None]
    kj = np.arange(3 * BLK)[None, :]
    rel = kj - BLK - qi
    n = np.abs(rel)
    half = NUM_BUCKETS // 2
    max_exact = half // 2
    thresholds = [int(np.ceil(max_exact * 2 ** (k / 2) - 1e-9)) for k in range(1, half - max_exact)]
    large = max_exact + sum((n >= t).astype(np.int64) for t in thresholds)
    large = np.minimum(large, half - 1)
    bucket = np.where(rel > 0, half, 0) + np.where(n < max_exact, n, large)
    return np.where(n <= WINDOW, bucket, -1).astype(np.int32)


def _window_attn_kernel(n_steps, relb_ref, sink_ref, bmap_ref, g_ref, q_ref, kp_ref, kc_ref, kn_ref,
                        vp_ref, vc_ref, vn_ref, o_ref, bias_ref):
    b = pl.program_id(0)
    i = pl.program_id(1)

    @pl.when((b == 0) & (i == 0))
    def _build_bias():
        bmap = bmap_ref[...]
        col = lax.broadcasted_iota(jnp.int32, (1, 3 * BLK), 1)
        for hd in range(A_HEADS):
            bias_ref[1, hd] = jnp.full((BLK, 3 * BLK), MASKED, F32)
        for bk in range(NUM_BUCKETS):
            m = bmap == bk
            for hd in range(A_HEADS):
                bias_ref[1, hd] = jnp.where(m, relb_ref[bk, hd] * LOG2E, bias_ref[1, hd])
        for hd in range(A_HEADS):
            bias_ref[0, hd] = jnp.where(col < BLK, MASKED, bias_ref[1, hd])
            bias_ref[2, hd] = jnp.where(col >= 2 * BLK, MASKED, bias_ref[1, hd])

    lane = lax.broadcasted_iota(jnp.int32, (1, LANES), 1)
    lo = lane < A_HEAD_DIM
    ones_lo = jnp.broadcast_to(jnp.where(lo, 1.0, 0.0).astype(BF16), (3 * BLK, LANES))
    ones_hi = jnp.broadcast_to(jnp.where(lo, 0.0, 1.0).astype(BF16), (3 * BLK, LANES))

    def key_blocks(p_ref, c_ref, n_ref):
        return [p_ref[...]] + [c_ref[c * BLK:(c + 1) * BLK, :] for c in range(WIN_QBLOCKS)] + [n_ref[...]]

    def lane_swap(t):
        return pltpu.bitcast(pltpu.roll(pltpu.bitcast(t, jnp.int32), LANES // 2, axis=1), BF16)

    kdup = [[], []]
    v_even = [[], []]
    v_odd = [[], []]
    for kt, vt in zip(key_blocks(kp_ref, kc_ref, kn_ref), key_blocks(vp_ref, vc_ref, vn_ref)):
        ks, vs, zero = lane_swap(kt), lane_swap(vt), jnp.zeros_like(vt)
        kdup[0].append(jnp.where(lo, kt, ks))
        kdup[1].append(jnp.where(lo, ks, kt))
        v_even[0].append(jnp.where(lo, vt, zero))
        v_odd[0].append(jnp.where(lo, zero, vs))
        v_even[1].append(jnp.where(lo, vs, zero))
        v_odd[1].append(jnp.where(lo, zero, vt))

    for c in range(WIN_QBLOCKS):
        rows = slice(c * BLK, (c + 1) * BLK)
        if c == 0:
            variant = jnp.where(i == 0, 0, 1)
        elif c == WIN_QBLOCKS - 1:
            variant = jnp.where(i == n_steps - 1, 2, 1)
        else:
            variant = 1
        probs, sink_terms = [], []
        for hd in range(A_HEADS):
            g = hd // (A_HEADS // A_KV_HEADS)
            qt = q_ref[rows, (hd // 2) * LANES:(hd // 2 + 1) * LANES]
            qm = jnp.where(lo if hd % 2 == 0 else ~lo, qt, jnp.zeros_like(qt))
            kcat = jnp.concatenate(kdup[g][c:c + 3], axis=0)
            s = _dot_nt(qm, kcat) + bias_ref[variant, hd]
            sk = sink_ref[hd] * LOG2E
            m = jnp.maximum(jnp.max(s, axis=-1, keepdims=True), sk)
            probs.append(jnp.exp2(s - m).astype(BF16))
            sink_terms.append(jnp.exp2(sk - m))

        outs = []
        for j in range(A_HEADS // 2):
            g = (2 * j) // (A_HEADS // A_KV_HEADS)
            v_e = jnp.concatenate(v_even[g][c:c + 3], axis=0)
            v_o = jnp.concatenate(v_odd[g][c:c + 3], axis=0)
            o = (_dot(probs[2 * j], jnp.concatenate([v_e, ones_lo], axis=1))
                 + _dot(probs[2 * j + 1], jnp.concatenate([v_o, ones_hi], axis=1)))
            denom = o[:, LANES:] + jnp.where(lo, sink_terms[2 * j], sink_terms[2 * j + 1])
            outs.append(o[:, :LANES] / denom)
        ya = jnp.concatenate(outs, axis=1)
        o_ref[rows, :] = _rms(ya, g_ref[...]).astype(BF16)


def _window_attn(qa, ka, va, rel_bias, sink, g_out_a, B, S):
    nb = S // BLK
    n_steps = nb // WIN_QBLOCKS
    assert nb >= 2, "first and last query blocks use distinct edge masks"
    bmap = jnp.asarray(_bucket_map())
    smem = pl.BlockSpec(memory_space=pltpu.SMEM)
    cur = lambda n: pl.BlockSpec((WIN_QBLOCKS * BLK, n), lambda b, i: (b * n_steps + i, 0))
    prv = lambda n: pl.BlockSpec((BLK, n), lambda b, i: (b * nb + jnp.maximum(i * WIN_QBLOCKS - 1, 0), 0))
    nxt = lambda n: pl.BlockSpec((BLK, n), lambda b, i: (b * nb + jnp.minimum((i + 1) * WIN_QBLOCKS, nb - 1), 0))
    return pl.pallas_call(
        functools.partial(_window_attn_kernel, n_steps),
        grid=(B, n_steps),
        in_specs=[smem, smem,
                  pl.BlockSpec((BLK, 3 * BLK), lambda b, i: (0, 0)),
                  pl.BlockSpec((1, 512), lambda b, i: (0, 0)),
                  cur(512), prv(LANES), cur(LANES), nxt(LANES), prv(LANES), cur(LANES), nxt(LANES)],
        out_specs=cur(512),
        out_shape=jax.ShapeDtypeStruct((B * S, 512), BF16),
        scratch_shapes=[pltpu.VMEM((3, A_HEADS, BLK, 3 * BLK), F32)],
        compiler_params=_cparams(2),
        name="window_attn",
    )(rel_bias, sink, bmap, g_out_a, qa, ka, ka, ka, va, va, va)


def _mla_attn_kernel(n_cast, g_ref, q_ref, k_ref, v_ref, *rest):
    o_ref = rest[n_cast]
    for src, dst in zip(rest[:n_cast], rest[n_cast + 1:]):
        dst[...] = src[...].astype(BF16)
    S = k_ref.shape[0]
    lane = lax.broadcasted_iota(jnp.int32, (1, LANES), 1)
    lo = lane < B_V_DIM
    ones_lo = jnp.broadcast_to(jnp.where(lo, 1.0, 0.0).astype(BF16), (S, LANES))
    ones_hi = jnp.broadcast_to(jnp.where(lo, 0.0, 1.0).astype(BF16), (S, LANES))
    v_pairs = []
    for j in range(B_HEADS // 2):
        vt = v_ref[:, j * LANES:(j + 1) * LANES]
        zero = jnp.zeros_like(vt)
        v_pairs.append((jnp.concatenate([jnp.where(lo, vt, zero), ones_lo], axis=1),
                        jnp.concatenate([jnp.where(lo, zero, vt), ones_hi], axis=1)))
    for r in range(q_ref.shape[0] // MLA_ROWS):
        rows = slice(r * MLA_ROWS, (r + 1) * MLA_ROWS)
        outs = []
        for j in range(B_HEADS // 2):
            ps = []
            for hd in (2 * j, 2 * j + 1):
                sl = slice(hd * LANES, (hd + 1) * LANES)
                s = _dot_nt(q_ref[rows, sl], k_ref[:, sl])
                m = jnp.max(s, axis=-1, keepdims=True)
                ps.append(jnp.exp2(s - m).astype(BF16))
            o = _dot(ps[0], v_pairs[j][0]) + _dot(ps[1], v_pairs[j][1])
            outs.append(o[:, :LANES] / o[:, LANES:])
        yb = jnp.concatenate(outs, axis=1)
        o_ref[rows, :] = _rms(yb, g_ref[...]).astype(BF16)


def _mla_attn(qm, km, vm, g_out_b, B, S, cast_weights):
    nq = S // TQ
    steps = B * nq
    qspec = lambda n: pl.BlockSpec((TQ, n), lambda b, i: (b * nq + i, 0))
    kspec = lambda n: pl.BlockSpec((S, n), lambda b, i: (b, 0))
    flat = [w.reshape(-1, w.shape[-1]) for w in cast_weights]
    slab = lambda w: pl.BlockSpec((w.shape[0] // steps, w.shape[1]), lambda b, i: (b * nq + i, 0))
    outs = pl.pallas_call(
        functools.partial(_mla_attn_kernel, len(flat)),
        grid=(B, nq),
        in_specs=[pl.BlockSpec((1, 512), lambda b, i: (0, 0)), qspec(1024), kspec(1024), kspec(512)]
                 + [slab(w) for w in flat],
        out_specs=[qspec(512)] + [slab(w) for w in flat],
        out_shape=[jax.ShapeDtypeStruct((B * S, 512), BF16)]
                  + [jax.ShapeDtypeStruct(w.shape, BF16) for w in flat],
        compiler_params=_cparams(2),
        name="mla_attn",
    )(g_out_b, qm, km, vm, *flat)
    return outs[0], [o.reshape(w.shape) for o, w in zip(outs[1:], cast_weights)]


def _out_proj_kernel(x_ref, ya_ref, yb_ref, wo_ref, g_ref, wr_ref, x1_ref, h2_ref, aff_ref, afft_ref):
    x1 = x_ref[...] + _dot(ya_ref[...], wo_ref[0:512, :]) + _dot(yb_ref[...], wo_ref[512:1024, :])
    x1_ref[...] = x1
    h2 = _rms(x1, g_ref[...])
    h2b = h2.astype(BF16)
    h2r = h2b.astype(F32)
    for j in range(SLAB_ROWS):
        h2_ref[pl.ds(j, h2r.shape[0], stride=SLAB_ROWS), :] = h2r[:, j * LANES:(j + 1) * LANES]
    logits = _dot_nt(wr_ref[...].astype(BF16), h2b)
    m = jnp.max(logits, axis=0, keepdims=True)
    e = jnp.exp(logits - m)
    aff = e / jnp.sum(e, axis=0, keepdims=True)
    aff_ref[0] = aff
    pad = jnp.zeros((LANES - aff.shape[0], aff.shape[1]), F32)
    afft_ref[...] = jnp.concatenate([aff, pad], axis=0).T


def _out_proj(xf, ya, yb, w_out, g_ffn, w_router_t, B, S):
    T = xf.shape[0]
    per_b = S // TM
    row = lambda n: pl.BlockSpec((TM, n), lambda i: (i, 0))
    full = lambda a: pl.BlockSpec(a.shape, lambda i: (0, 0))
    return pl.pallas_call(
        _out_proj_kernel,
        grid=(T // TM,),
        in_specs=[row(D_MODEL), row(512), row(512), full(w_out), full(g_ffn), full(w_router_t)],
        out_specs=[row(D_MODEL), pl.BlockSpec((TM * SLAB_ROWS, LANES), lambda i: (i, 0)),
                   pl.BlockSpec((1, N_EXPERTS, TM), lambda i: (i // per_b, 0, i % per_b)), row(LANES)],
        out_shape=[jax.ShapeDtypeStruct((T, D_MODEL), F32), jax.ShapeDtypeStruct((T * SLAB_ROWS, LANES), F32),
                   jax.ShapeDtypeStruct((B, N_EXPERTS, S), F32), jax.ShapeDtypeStruct((T, LANES), F32)],
        compiler_params=_cparams(1),
        name="out_proj",
    )(xf, ya, yb, w_out, g_ffn, w_router_t)


CHUNK = 256
ROUTE_LOG_STEPS = 12
ROUTE_LIN_STEPS = 32


def _prefix_count(flags_f32, tri):
    S = flags_f32.shape[1]
    carry = jnp.zeros((flags_f32.shape[0], 1), F32)
    parts = []
    for c in range(S // CHUNK):
        blk = flags_f32[:, c * CHUNK:(c + 1) * CHUNK]
        parts.append(_dot(blk.astype(BF16), tri) + carry)
        carry = carry + jnp.sum(blk, axis=-1, keepdims=True)
    return jnp.concatenate(parts, axis=1)


def _route_kernel(cap, aff_ref, slot_ref):
    aff = aff_ref[...]
    rows = aff.shape[0]

    def enough(pivot):
        return jnp.sum(jnp.where(aff >= pivot, 1.0, 0.0), axis=-1, keepdims=True) >= cap

    def log_body(_, st):
        lo, hi, elo, ehi = st
        mid = 0.5 * (elo + ehi)
        pivot = jnp.exp2(mid)
        ok = enough(pivot)
        return (jnp.where(ok, pivot, lo), jnp.where(ok, hi, pivot), jnp.where(ok, mid, elo), jnp.where(ok, ehi, mid))

    def lin_body(_, st):
        lo, hi = st
        pivot = 0.5 * (lo + hi)
        ok = enough(pivot)
        return jnp.where(ok, pivot, lo), jnp.where(ok, hi, pivot)

    col = lambda v: jnp.full((rows, 1), v, F32)
    lo, hi, _, _ = lax.fori_loop(0, ROUTE_LOG_STEPS, log_body, (col(0.0), col(2.0), col(-152.0), col(1.0)))
    lo, hi = lax.fori_loop(0, ROUTE_LIN_STEPS, lin_body, (lo, hi))

    gt = jnp.where(aff >= hi, 1.0, 0.0)
    eq = jnp.where(aff >= lo, 1.0, 0.0) - gt
    need = cap - jnp.sum(gt, axis=-1, keepdims=True)
    r = lax.broadcasted_iota(jnp.int32, (CHUNK, CHUNK), 0)
    c = lax.broadcasted_iota(jnp.int32, (CHUNK, CHUNK), 1)
    tri = jnp.where(r < c, 1.0, 0.0).astype(BF16)
    sel = gt + eq * jnp.where(_prefix_count(eq, tri) < need, 1.0, 0.0)
    slot = _prefix_count(sel, tri)
    slot_ref[...] = jnp.where(sel > 0.5, slot, -1.0)


def _route(aff, cap):
    B, E, S = aff.shape
    spec = pl.BlockSpec((B * E, S), lambda i: (0, 0))
    return pl.pallas_call(
        functools.partial(_route_kernel, cap),
        grid=(1,),
        in_specs=[spec],
        out_specs=spec,
        out_shape=jax.ShapeDtypeStruct((B * E, S), F32),
        compiler_params=_cparams(1),
        name="route",
    )(aff.reshape(B * E, S))


def _moe_kernel(cap, slot_a_ref, slot_b_ref, next_a_ref, next_b_ref, h_ref, afft_ref,
                wg_hbm, wu_hbm, wd_hbm, o_ref, xa_ref, xb_ref, ya_ref, yb_ref, aa_ref, ab_ref, tok_ref,
                wg_ref, wu_ref, wd_ref, w_sem):
    S = h_ref.shape[0] // SLAB_ROWS
    stride = cap + SLAB_ROWS
    parity = pl.program_id(1) % 2
    slot_ids = lax.broadcasted_iota(jnp.int32, (cap, 1), 0).astype(F32)
    tok_ids = lax.broadcasted_iota(jnp.int32, (1, S), 1).astype(F32)

    def slot_affinities(a_ref, expert):
        lane = lax.broadcasted_iota(jnp.int32, (1, LANES), 1)
        return jnp.sum(jnp.where(lane == expert, a_ref[...], 0.0), axis=-1, keepdims=True)

    def gather(slot_row, x_ref, a_ref, which, par, done):
        sel = slot_ids == slot_row
        tok = jnp.sum(jnp.where(sel, tok_ids, 0.0), axis=-1, keepdims=True).astype(jnp.int32)
        tok = jnp.clip(tok, 0, S - 1)
        for grp in range(cap // GATHER_GROUP):
            ids = tok[grp * GATHER_GROUP:(grp + 1) * GATHER_GROUP, :]
            if len(done) >= 2:
                ids = jnp.minimum(ids, done[-2] + S)
            for j in range(GATHER_GROUP):
                c = grp * GATHER_GROUP + j
                t = ids[j, 0]
                row = pl.multiple_of(t * SLAB_ROWS, SLAB_ROWS)
                tok_ref[which, par, c] = row
                x_ref[pl.ds(c, SLAB_ROWS, stride=stride), :] = h_ref[pl.ds(row, SLAB_ROWS), :]
                a_ref[pl.ds(c, 1), :] = afft_ref[pl.ds(t, 1), :]
            done.append(t)

    def chunks_to_rows(tile_ref):
        return jnp.concatenate([tile_ref[j * stride:j * stride + cap, :] for j in range(SLAB_ROWS)], axis=1)

    def gate_up(x_ref, i):
        xg = chunks_to_rows(x_ref).astype(BF16)
        g = _dot(xg, wg_ref[w_slot, i])
        u = _dot(xg, wu_ref[w_slot, i])
        return (g * (1.0 / (1.0 + jnp.exp(-g))) * u).astype(BF16)

    def down(act, vals, y_ref, i):
        y = _dot(act, wd_ref[w_slot, i]) * vals
        for j in range(SLAB_ROWS):
            y_ref[j * stride:j * stride + cap, :] = y[:, j * LANES:(j + 1) * LANES]

    def scatter_add(y_ref, which):
        for c0 in range(0, cap, SCATTER_GROUP):
            rows = [tok_ref[which, parity, c0 + j] for j in range(SCATTER_GROUP)]
            rows = [pl.multiple_of(r, SLAB_ROWS) for r in rows]
            new = [o_ref[pl.ds(rows[j], SLAB_ROWS), :] + y_ref[pl.ds(c0 + j, SLAB_ROWS, stride=stride), :]
                   for j in range(SCATTER_GROUP)]
            for j in range(SCATTER_GROUP):
                o_ref[pl.ds(rows[j], SLAB_ROWS), :] = new[j]

    n_pairs = pl.num_programs(1)
    step = pl.program_id(0) * n_pairs + pl.program_id(1)
    n_all = pl.num_programs(0) * n_pairs
    w_slot = step % W_SLOTS

    def weight_copies(for_step):
        pair = for_step % n_pairs
        slot = for_step % W_SLOTS
        return [pltpu.make_async_copy(src.at[pl.ds(2 * pair, 2)], dst.at[slot], w_sem.at[j, slot])
                for j, (src, dst) in enumerate(((wg_hbm, wg_ref), (wu_hbm, wu_ref), (wd_hbm, wd_ref)))]

    @pl.when(step == 0)
    def _prime_weights():
        for first in (0, 1):
            for cp in weight_copies(first):
                cp.start()

    for cp in weight_copies(step):
        cp.wait()

    @pl.when(step + 2 < n_all)
    def _prefetch_weights():
        for cp in weight_copies(step + 2):
            cp.start()

    @pl.when((pl.program_id(0) == 0) & (pl.program_id(1) == 0))
    def _first_step():
        moved = []
        gather(slot_a_ref[0], xa_ref, aa_ref, 0, 0, moved)
        gather(slot_b_ref[0], xb_ref, ab_ref, 1, 0, moved)

    @pl.when(pl.program_id(1) == 0)
    def _first_of_sequence():
        o_ref[...] = jnp.zeros_like(o_ref)

    expert_a = 2 * pl.program_id(1)
    vals_a = slot_affinities(aa_ref, expert_a)
    vals_b = slot_affinities(ab_ref, expert_a + 1)
    moved = []
    act_a = gate_up(xa_ref, 0)
    gather(next_a_ref[0], xa_ref, aa_ref, 0, 1 - parity, moved)
    act_b = gate_up(xb_ref, 1)
    gather(next_b_ref[0], xb_ref, ab_ref, 1, 1 - parity, moved)
    down(act_a, vals_a, ya_ref, 0)
    down(act_b, vals_b, yb_ref, 1)
    scatter_add(ya_ref, 0)
    scatter_add(yb_ref, 1)


def _moe(slot, afft, h2, wg, wu, wd, B, E, cap):
    S = slot.shape[1]
    assert E % 2 == 0 and (E // 2) % 2 == 0 and h2.shape == (B * S * SLAB_ROWS, LANES) and E <= LANES
    n_steps = B * E // 2
    last = B * E - 1
    slot3 = slot.reshape(B * E, 1, S)
    row = lambda d: pl.BlockSpec((1, 1, S), lambda b, k: (jnp.minimum(b * E + 2 * k + d, last), 0, 0))
    cur_a, cur_b, nxt_a, nxt_b = row(0), row(1), row(2), row(3)
    tok = pl.BlockSpec((S * SLAB_ROWS, LANES), lambda b, k: (b, 0))
    ahead_seq = lambda b, k: (jnp.minimum(b * (E // 2) + k + 1, n_steps - 1) // (E // 2), 0)
    ahead = pl.BlockSpec((S * SLAB_ROWS, LANES), ahead_seq)
    ahead_aff = pl.BlockSpec((S, LANES), ahead_seq)
    tile = pltpu.VMEM((SLAB_ROWS * (cap + SLAB_ROWS), LANES), F32)
    aff_rows = pltpu.VMEM((cap, LANES), F32)
    hbm = pl.BlockSpec(memory_space=pl.ANY)
    ring = lambda w: pltpu.VMEM((W_SLOTS, 2) + w.shape[1:], w.dtype)
    return pl.pallas_call(
        functools.partial(_moe_kernel, cap),
        grid=(B, E // 2),
        in_specs=[cur_a, cur_b, nxt_a, nxt_b, ahead, ahead_aff, hbm, hbm, hbm],
        out_specs=tok,
        out_shape=jax.ShapeDtypeStruct(h2.shape, F32),
        scratch_shapes=[tile, tile, tile, tile, aff_rows, aff_rows, pltpu.SMEM((2, 2, cap), jnp.int32),
                        ring(wg), ring(wu), ring(wd), pltpu.SemaphoreType.DMA((3, W_SLOTS))],
        compiler_params=pltpu.CompilerParams(dimension_semantics=("arbitrary", "arbitrary"),
                                             vmem_limit_bytes=MOE_VMEM_LIMIT),
        name="moe",
    )(slot3, slot3, slot3, slot3, h2, afft, wg, wu, wd)


def _ple_final_kernel(x1_ref, moe_ref, p_ref, gp_ref, wg_ref, wp_ref, gf_ref, o_ref):
    rows = x1_ref.shape[0]
    moe = jnp.concatenate([moe_ref[pl.ds(j, rows, stride=SLAB_ROWS), :] for j in range(SLAB_ROWS)], axis=1)
    x2 = x1_ref[...] + moe
    z = _dot(_rms(x2, gp_ref[...]).astype(BF16), wg_ref[...])
    gate = 1.0 / (1.0 + jnp.exp(-z))
    x3 = x2 + gate * _dot(p_ref[...].astype(BF16), wp_ref[...])
    o_ref[...] = _rms(x3, gf_ref[...])


def _ple_final(x1, moe, pf, g_ple, w_gate, w_proj, g_final):
    T = x1.shape[0]
    row = lambda n: pl.BlockSpec((TM, n), lambda i: (i, 0))
    full = lambda a: pl.BlockSpec(a.shape, lambda i: (0, 0))
    return pl.pallas_call(
        _ple_final_kernel,
        grid=(T // TM,),
        in_specs=[row(D_MODEL), pl.BlockSpec((TM * SLAB_ROWS, LANES), lambda i: (i, 0)), row(PLE_DIM),
                  full(g_ple), full(w_gate), full(w_proj), full(g_final)],
        out_specs=row(D_MODEL),
        out_shape=jax.ShapeDtypeStruct((T, D_MODEL), F32),
        compiler_params=_cparams(1),
        name="ple_final",
    )(x1, moe, pf, g_ple, w_gate, w_proj, g_final)


def _prep_in_proj_weight(w_in):
    D = w_in.shape[0]
    t1, t2 = w_in[:, 1152:1168], w_in[:, 1168:1184]
    cols = [w_in[:, 0:1152], t1, t2, t2, t1, jnp.zeros((D, 64), F32)]
    return jnp.concatenate(cols, axis=1).astype(BF16)


def _prep_mla_weights(w_uq, w_ukv):
    per_q = B_NOPE_DIM + B_ROPE_DIM
    w3 = w_uq.reshape(Q_RANK, B_HEADS, per_q)
    pad = LANES - per_q
    wq = jnp.pad(w3, ((0, 0), (0, 0), (0, pad))).reshape(Q_RANK, B_HEADS * LANES)
    t1 = w3[:, :, B_NOPE_DIM:B_NOPE_DIM + HALF_ROPE]
    t2 = w3[:, :, B_NOPE_DIM + HALF_ROPE:]
    wqr = jnp.concatenate([jnp.zeros((Q_RANK, B_HEADS, B_NOPE_DIM), F32), t2, t1,
                           jnp.zeros((Q_RANK, B_HEADS, pad), F32)], axis=2).reshape(Q_RANK, B_HEADS * LANES)
    w4 = w_ukv.reshape(KV_RANK, B_HEADS, B_NOPE_DIM + B_V_DIM)
    zk = jnp.zeros((KV_RANK, B_HEADS, LANES - B_NOPE_DIM), F32)
    wk = jnp.concatenate([w4[:, :, :B_NOPE_DIM], zk], axis=2).reshape(KV_RANK, B_HEADS * LANES)
    wv = w4[:, :, B_NOPE_DIM:].reshape(KV_RANK, B_HEADS * B_V_DIM)
    wkv = jnp.concatenate([wk, wv], axis=1)
    return wq.astype(BF16), wqr.astype(BF16), wkv.astype(BF16)


def kernel(x, p, positions, rel_bias, norm_mix_g, w_in, sink, g_cq, g_ckv, w_uq, w_ukv, g_out_a, g_out_b, w_out,
           norm_ffn_g, w_router, w_e_gate, w_e_up, w_e_down, norm_ple_g, w_ple_gate, w_ple_proj, final_norm_g):
    B, S, D = x.shape
    T = B * S
    cap = CAPACITY_FACTOR * S // N_EXPERTS
    xf = x.reshape(T, D)
    cos_d, sin_d = _rope_tables(positions)
    assert w_in.shape[0] == 1, "single-layer block: the final norm is fused into the last kernel"
    i = 0
    w1 = _prep_in_proj_weight(w_in[i])
    wq, wqr, wkv = _prep_mla_weights(w_uq[i], w_ukv[i])
    qa, ka, va, qm, km, vm = _in_proj(
        xf, norm_mix_g[i].reshape(1, D), w1, g_cq[i].reshape(1, -1), g_ckv[i].reshape(1, -1),
        wq, wqr, wkv, cos_d, sin_d)
    ya = _window_attn(qa, ka, va, rel_bias, sink[i], g_out_a[i].reshape(1, -1), B, S)
    yb, (wg, wu, wd) = _mla_attn(qm, km, vm, g_out_b[i].reshape(1, -1), B, S,
                                 (w_e_gate[i], w_e_up[i], w_e_down[i]))
    x1, h2, aff, afft = _out_proj(xf, ya, yb, w_out[i].astype(BF16), norm_ffn_g[i].reshape(1, D),
                                  w_router[i].T, B, S)
    slot = _route(aff, cap)
    moe = _moe(slot, afft, h2, wg, wu, wd, B, N_EXPERTS, cap)
    out = _ple_final(x1, moe, p[i].reshape(T, -1), norm_ple_g[i].reshape(1, D),
                     w_ple_gate[i].astype(BF16), w_ple_proj[i].astype(BF16), final_norm_g.reshape(1, D))
    return out.reshape(B, S, D)
```

```python
import functools

import numpy as np
import jax
import jax.numpy as jnp
from jax import lax
from jax.experimental import pallas as pl
from jax.experimental.pallas import tpu as pltpu

F32 = jnp.float32
BF16 = jnp.bfloat16

D_MODEL = 1024
BLK = 128
WINDOW = 128
A_HEADS = 8
A_KV_HEADS = 2
A_HEAD_DIM = 64
B_HEADS = 8
B_NOPE_DIM = 64
B_ROPE_DIM = 32
B_V_DIM = 64
Q_RANK = 256
KV_RANK = 128
ROPE_THETA = 10000.0
NUM_BUCKETS = 32
MAX_DISTANCE = 128
N_EXPERTS = 16
EXPERT_FF = 512
CAPACITY_FACTOR = 2
PLE_DIM = 256
EPS = 1e-6

LANES = 128
HALF_ROPE = B_ROPE_DIM // 2
MASKED = -1e30
LOG2E = 1.4426950408889634
VMEM_LIMIT = 56 * 1024 * 1024
TM = 1024
TQ = 512
MLA_ROWS = 128
WIN_QBLOCKS = 8
GATHER_GROUP = 8
SCATTER_GROUP = 8
SLAB_ROWS = D_MODEL // LANES
W_SLOTS = 3
MOE_VMEM_LIMIT = 58 * 1024 * 1024


def _cparams(n_axes):
    return pltpu.CompilerParams(dimension_semantics=("arbitrary",) * n_axes,
                                vmem_limit_bytes=VMEM_LIMIT)


def _rms(x, g):
    return x * lax.rsqrt(jnp.mean(x * x, axis=-1, keepdims=True) + EPS) * g


def _dot(a, b):
    return jnp.dot(a, b, preferred_element_type=F32)


def _dot_nt(a, b):
    return lax.dot_general(a, b, (((1,), (1,)), ((), ())), preferred_element_type=F32)


def _dot_tn(a, b):
    return lax.dot_general(a, b, (((0,), (0,)), ((), ())), preferred_element_type=F32)


ROPE_GROUPS = LANES // HALF_ROPE


def _rope_angles(positions):
    T = positions.size
    rows = T // ROPE_GROUPS
    inv_freq = 1.0 / (ROPE_THETA ** (jnp.arange(0, B_ROPE_DIM, 2, dtype=F32) / B_ROPE_DIM))
    pos = jnp.repeat(positions.astype(F32).reshape(ROPE_GROUPS, rows).T, HALF_ROPE, axis=1)
    invf = jnp.tile(inv_freq, ROPE_GROUPS).reshape(1, LANES)
    return pos, invf


def _expand_rope(cd, sd, group):
    lane = lax.broadcasted_iota(jnp.int32, (1, LANES), 1)
    own = (lane >= group * HALF_ROPE) & (lane < (group + 1) * HALF_ROPE)
    xc = jnp.where(own, cd, 0.0)
    xs = jnp.where(own, sd, 0.0)
    shift = HALF_ROPE
    while shift < LANES:
        xc = xc + pltpu.roll(xc, shift, axis=1)
        xs = xs + pltpu.roll(xs, shift, axis=1)
        shift *= 2
    in_t1 = (lane >= B_NOPE_DIM) & (lane < B_NOPE_DIM + HALF_ROPE)
    in_t2 = (lane >= B_NOPE_DIM + HALF_ROPE) & (lane < B_NOPE_DIM + B_ROPE_DIM)
    c = jnp.where(lane < B_NOPE_DIM, 1.0, jnp.where(in_t1 | in_t2, xc, 0.0))
    s = jnp.where(in_t1, -xs, jnp.where(in_t2, xs, 0.0))
    return c, s


C_QA = 0
C_KA = C_QA + 512
C_VA = C_KA + LANES
C_CQ = C_VA + LANES
C_CKV = C_CQ + Q_RANK
C_KR = C_CKV + KV_RANK
C_END = C_KR + LANES
MLA_K_COLS = B_HEADS * LANES


def _in_proj_kernel(tiles_per_group, x_ref, g_ref, w1_ref, gcq_ref, gckv_ref, wq_ref, wqr_ref, wkv_ref,
                    pos_ref, invf_ref, qa_ref, ka_ref, va_ref, qm_ref, km_ref, vm_ref, cos_ref, sin_ref):
    @pl.when(pl.program_id(0) < tiles_per_group)
    def _fill_rope_table():
        ang = pos_ref[...] * invf_ref[...]
        cos_ref[pl.program_id(0)] = jnp.cos(ang)
        sin_ref[pl.program_id(0)] = jnp.sin(ang)

    dense_tile = pl.program_id(0) % tiles_per_group
    h = _rms(x_ref[...], g_ref[...]).astype(BF16)
    z = _dot(h, w1_ref[...])
    qa_ref[...] = (z[:, C_QA:C_KA] * (A_HEAD_DIM ** -0.5 * LOG2E)).astype(BF16)
    ka_ref[...] = z[:, C_KA:C_VA].astype(BF16)
    va_ref[...] = z[:, C_VA:C_CQ].astype(BF16)
    cqn = _rms(z[:, C_CQ:C_CKV], gcq_ref[...]).astype(BF16)
    ckvn = _rms(z[:, C_CKV:C_KR], gckv_ref[...]).astype(BF16)
    krt = z[:, C_KR:C_END]

    lane = lax.broadcasted_iota(jnp.int32, (1, LANES), 1)
    c_full, s_rope = _expand_rope(cos_ref[dense_tile], sin_ref[dense_tile], pl.program_id(0) // tiles_per_group)
    c_rope = jnp.where(lane < B_NOPE_DIM, 0.0, c_full)

    q1 = _dot(cqn, wq_ref[...])
    q2 = _dot(cqn, wqr_ref[...])
    scale = (B_NOPE_DIM + B_ROPE_DIM) ** -0.5 * LOG2E
    for hd in range(B_HEADS):
        sl = slice(hd * LANES, (hd + 1) * LANES)
        qm_ref[:, sl] = ((q1[:, sl] * c_full + q2[:, sl] * s_rope) * scale).astype(BF16)

    kv = _dot(ckvn, wkv_ref[...])
    kr_part = pltpu.roll(krt, 64, axis=1) * c_rope + pltpu.roll(krt, 32, axis=1) * s_rope
    for hd in range(B_HEADS):
        sl = slice(hd * LANES, (hd + 1) * LANES)
        km_ref[:, sl] = (kv[:, sl] + kr_part).astype(BF16)
    vm_ref[...] = kv[:, MLA_K_COLS:].astype(BF16)


def _in_proj(xf, g_mix, w1, g_cq, g_ckv, wq, wqr, wkv, pos_d, invf):
    T = xf.shape[0]
    tiles_per_group = pos_d.shape[0] // TM
    assert tiles_per_group * TM == pos_d.shape[0]
    row = lambda n: pl.BlockSpec((TM, n), lambda i: (i, 0))
    full = lambda a: pl.BlockSpec(a.shape, lambda i: (0, 0))
    dense = pl.BlockSpec((TM, LANES), lambda i: (jnp.minimum(i, tiles_per_group - 1), 0))
    outs = [512, LANES, LANES, MLA_K_COLS, MLA_K_COLS, B_HEADS * B_V_DIM]
    return pl.pallas_call(
        functools.partial(_in_proj_kernel, tiles_per_group),
        grid=(T // TM,),
        in_specs=[row(D_MODEL), full(g_mix), full(w1), full(g_cq), full(g_ckv), full(wq), full(wqr), full(wkv),
                  dense, full(invf)],
        out_specs=[row(n) for n in outs],
        out_shape=[jax.ShapeDtypeStruct((T, n), BF16) for n in outs],
        scratch_shapes=[pltpu.VMEM((tiles_per_group, TM, LANES), F32)] * 2,
        compiler_params=_cparams(1),
        name="in_proj",
    )(xf, g_mix, w1, g_cq, g_ckv, wq, wqr, wkv, pos_d, invf)


def _bucket_map():
    qi = np.arange(BLK)[:, None]
    kj = np.arange(3 * BLK)[None, :]
    rel = kj - BLK - qi
    n = np.abs(rel)
    half = NUM_BUCKETS // 2
    max_exact = half // 2
    thresholds = [int(np.ceil(max_exact * 2 ** (k / 2) - 1e-9)) for k in range(1, half - max_exact)]
    large = max_exact + sum((n >= t).astype(np.int64) for t in thresholds)
    large = np.minimum(large, half - 1)
    bucket = np.where(rel > 0, half, 0) + np.where(n < max_exact, n, large)
    return np.where(n <= WINDOW, bucket, -1).astype(np.int32)


def _window_attn_kernel(n_steps, relb_ref, sink_ref, bmap_ref, g_ref, q_ref, kp_ref, kc_ref, kn_ref,
                        vp_ref, vc_ref, vn_ref, o_ref, bias_ref):
    b = pl.program_id(0)
    i = pl.program_id(1)

    @pl.when((b == 0) & (i == 0))
    def _build_bias():
        bmap = bmap_ref[...]
        col = lax.broadcasted_iota(jnp.int32, (1, 3 * BLK), 1)
        for hd in range(A_HEADS):
            bias_ref[1, hd] = jnp.full((BLK, 3 * BLK), MASKED, F32)
        for bk in range(NUM_BUCKETS):
            m = bmap == bk
            for hd in range(A_HEADS):
                bias_ref[1, hd] = jnp.where(m, relb_ref[bk, hd] * LOG2E, bias_ref[1, hd])
        for hd in range(A_HEADS):
            bias_ref[0, hd] = jnp.where(col < BLK, MASKED, bias_ref[1, hd])
            bias_ref[2, hd] = jnp.where(col >= 2 * BLK, MASKED, bias_ref[1, hd])

    lane = lax.broadcasted_iota(jnp.int32, (1, LANES), 1)
    lo = lane < A_HEAD_DIM
    ones_lo = jnp.broadcast_to(jnp.where(lo, 1.0, 0.0).astype(BF16), (3 * BLK, LANES))
    ones_hi = jnp.broadcast_to(jnp.where(lo, 0.0, 1.0).astype(BF16), (3 * BLK, LANES))

    def key_blocks(p_ref, c_ref, n_ref):
        return [p_ref[...]] + [c_ref[c * BLK:(c + 1) * BLK, :] for c in range(WIN_QBLOCKS)] + [n_ref[...]]

    def lane_swap(t):
        return pltpu.bitcast(pltpu.roll(pltpu.bitcast(t, jnp.int32), LANES // 2, axis=1), BF16)

    kdup = [[], []]
    v_even = [[], []]
    v_odd = [[], []]
    for kt, vt in zip(key_blocks(kp_ref, kc_ref, kn_ref), key_blocks(vp_ref, vc_ref, vn_ref)):
        ks, vs, zero = lane_swap(kt), lane_swap(vt), jnp.zeros_like(vt)
        kdup[0].append(jnp.where(lo, kt, ks))
        kdup[1].append(jnp.where(lo, ks, kt))
        v_even[0].append(jnp.where(lo, vt, zero))
        v_odd[0].append(jnp.where(lo, zero, vs))
        v_even[1].append(jnp.where(lo, vs, zero))
        v_odd[1].append(jnp.where(lo, zero, vt))

    for c in range(WIN_QBLOCKS):
        rows = slice(c * BLK, (c + 1) * BLK)
        if c == 0:
            variant = jnp.where(i == 0, 0, 1)
        elif c == WIN_QBLOCKS - 1:
            variant = jnp.where(i == n_steps - 1, 2, 1)
        else:
            variant = 1
        probs, sink_terms = [], []
        for hd in range(A_HEADS):
            g = hd // (A_HEADS // A_KV_HEADS)
            qt = q_ref[rows, (hd // 2) * LANES:(hd // 2 + 1) * LANES]
            qm = jnp.where(lo if hd % 2 == 0 else ~lo, qt, jnp.zeros_like(qt))
            kcat = jnp.concatenate(kdup[g][c:c + 3], axis=0)
            s = _dot_nt(qm, kcat) + bias_ref[variant, hd]
            sk = sink_ref[hd] * LOG2E
            m = jnp.maximum(jnp.max(s, axis=-1, keepdims=True), sk)
            probs.append(jnp.exp2(s - m).astype(BF16))
            sink_terms.append(jnp.exp2(sk - m))

        outs = []
        for j in range(A_HEADS // 2):
            g = (2 * j) // (A_HEADS // A_KV_HEADS)
            v_e = jnp.concatenate(v_even[g][c:c + 3], axis=0)
            v_o = jnp.concatenate(v_odd[g][c:c + 3], axis=0)
            o = (_dot(probs[2 * j], jnp.concatenate([v_e, ones_lo], axis=1))
                 + _dot(probs[2 * j + 1], jnp.concatenate([v_o, ones_hi], axis=1)))
            denom = o[:, LANES:] + jnp.where(lo, sink_terms[2 * j], sink_terms[2 * j + 1])
            outs.append(o[:, :LANES] / denom)
        ya = jnp.concatenate(outs, axis=1)
        o_ref[rows, :] = _rms(ya, g_ref[...]).astype(BF16)


def _window_attn(qa, ka, va, rel_bias, sink, g_out_a, B, S):
    nb = S // BLK
    n_steps = nb // WIN_QBLOCKS
    assert nb >= 2, "first and last query blocks use distinct edge masks"
    bmap = jnp.asarray(_bucket_map())
    smem = pl.BlockSpec(memory_space=pltpu.SMEM)
    cur = lambda n: pl.BlockSpec((WIN_QBLOCKS * BLK, n), lambda b, i: (b * n_steps + i, 0))
    prv = lambda n: pl.BlockSpec((BLK, n), lambda b, i: (b * nb + jnp.maximum(i * WIN_QBLOCKS - 1, 0), 0))
    nxt = lambda n: pl.BlockSpec((BLK, n), lambda b, i: (b * nb + jnp.minimum((i + 1) * WIN_QBLOCKS, nb - 1), 0))
    return pl.pallas_call(
        functools.partial(_window_attn_kernel, n_steps),
        grid=(B, n_steps),
        in_specs=[smem, smem,
                  pl.BlockSpec((BLK, 3 * BLK), lambda b, i: (0, 0)),
                  pl.BlockSpec((1, 512), lambda b, i: (0, 0)),
                  cur(512), prv(LANES), cur(LANES), nxt(LANES), prv(LANES), cur(LANES), nxt(LANES)],
        out_specs=cur(512),
        out_shape=jax.ShapeDtypeStruct((B * S, 512), BF16),
        scratch_shapes=[pltpu.VMEM((3, A_HEADS, BLK, 3 * BLK), F32)],
        compiler_params=_cparams(2),
        name="window_attn",
    )(rel_bias, sink, bmap, g_out_a, qa, ka, ka, ka, va, va, va)


def _mla_attn_kernel(n_cast, g_ref, q_ref, k_ref, v_ref, *rest):
    o_ref = rest[n_cast]
    for src, dst in zip(rest[:n_cast], rest[n_cast + 1:]):
        dst[...] = src[...].astype(BF16)
    S = k_ref.shape[0]
    lane = lax.broadcasted_iota(jnp.int32, (1, LANES), 1)
    lo = lane < B_V_DIM
    ones_lo = jnp.broadcast_to(jnp.where(lo, 1.0, 0.0).astype(BF16), (S, LANES))
    ones_hi = jnp.broadcast_to(jnp.where(lo, 0.0, 1.0).astype(BF16), (S, LANES))
    v_pairs = []
    for j in range(B_HEADS // 2):
        vt = v_ref[:, j * LANES:(j + 1) * LANES]
        zero = jnp.zeros_like(vt)
        v_pairs.append((jnp.concatenate([jnp.where(lo, vt, zero), ones_lo], axis=1),
                        jnp.concatenate([jnp.where(lo, zero, vt), ones_hi], axis=1)))
    for r in range(q_ref.shape[0] // MLA_ROWS):
        rows = slice(r * MLA_ROWS, (r + 1) * MLA_ROWS)
        outs = []
        for j in range(B_HEADS // 2):
            ps = []
            for hd in (2 * j, 2 * j + 1):
                sl = slice(hd * LANES, (hd + 1) * LANES)
                s = _dot_nt(q_ref[rows, sl], k_ref[:, sl])
                m = jnp.max(s, axis=-1, keepdims=True)
                ps.append(jnp.exp2(s - m).astype(BF16))
            o = _dot(ps[0], v_pairs[j][0]) + _dot(ps[1], v_pairs[j][1])
            outs.append(o[:, :LANES] / o[:, LANES:])
        yb = jnp.concatenate(outs, axis=1)
        o_ref[rows, :] = _rms(yb, g_ref[...]).astype(BF16)


def _mla_attn(qm, km, vm, g_out_b, B, S, cast_weights):
    nq = S // TQ
    steps = B * nq
    qspec = lambda n: pl.BlockSpec((TQ, n), lambda b, i: (b * nq + i, 0))
    kspec = lambda n: pl.BlockSpec((S, n), lambda b, i: (b, 0))
    flat = [w.reshape(-1, w.shape[-1]) for w in cast_weights]
    slab = lambda w: pl.BlockSpec((w.shape[0] // steps, w.shape[1]), lambda b, i: (b * nq + i, 0))
    outs = pl.pallas_call(
        functools.partial(_mla_attn_kernel, len(flat)),
        grid=(B, nq),
        in_specs=[pl.BlockSpec((1, 512), lambda b, i: (0, 0)), qspec(1024), kspec(1024), kspec(512)]
                 + [slab(w) for w in flat],
        out_specs=[qspec(512)] + [slab(w) for w in flat],
        out_shape=[jax.ShapeDtypeStruct((B * S, 512), BF16)]
                  + [jax.ShapeDtypeStruct(w.shape, BF16) for w in flat],
        compiler_params=_cparams(2),
        name="mla_attn",
    )(g_out_b, qm, km, vm, *flat)
    return outs[0], [o.reshape(w.shape) for o, w in zip(outs[1:], cast_weights)]


def _out_proj_kernel(x_ref, ya_ref, yb_ref, wo_ref, g_ref, wr_ref, x1_ref, h2_ref, aff_ref, afft_ref):
    x1 = x_ref[...] + _dot(ya_ref[...], wo_ref[0:512, :]) + _dot(yb_ref[...], wo_ref[512:1024, :])
    x1_ref[...] = x1
    h2 = _rms(x1, g_ref[...])
    h2b = h2.astype(BF16)
    h2r = h2b.astype(F32)
    for j in range(SLAB_ROWS):
        h2_ref[pl.ds(j, h2r.shape[0], stride=SLAB_ROWS), :] = h2r[:, j * LANES:(j + 1) * LANES]
    logits = _dot_nt(wr_ref[...].astype(BF16), h2b)
    m = jnp.max(logits, axis=0, keepdims=True)
    e = jnp.exp(logits - m)
    aff = e / jnp.sum(e, axis=0, keepdims=True)
    aff_ref[0] = aff
    pad = jnp.zeros((LANES - aff.shape[0], aff.shape[1]), F32)
    afft_ref[...] = jnp.concatenate([aff, pad], axis=0).T


def _out_proj(xf, ya, yb, w_out, g_ffn, w_router_t, B, S):
    T = xf.shape[0]
    per_b = S // TM
    row = lambda n: pl.BlockSpec((TM, n), lambda i: (i, 0))
    full = lambda a: pl.BlockSpec(a.shape, lambda i: (0, 0))
    return pl.pallas_call(
        _out_proj_kernel,
        grid=(T // TM,),
        in_specs=[row(D_MODEL), row(512), row(512), full(w_out), full(g_ffn), full(w_router_t)],
        out_specs=[row(D_MODEL), pl.BlockSpec((TM * SLAB_ROWS, LANES), lambda i: (i, 0)),
                   pl.BlockSpec((1, N_EXPERTS, TM), lambda i: (i // per_b, 0, i % per_b)), row(LANES)],
        out_shape=[jax.ShapeDtypeStruct((T, D_MODEL), F32), jax.ShapeDtypeStruct((T * SLAB_ROWS, LANES), F32),
                   jax.ShapeDtypeStruct((B, N_EXPERTS, S), F32), jax.ShapeDtypeStruct((T, LANES), F32)],
        compiler_params=_cparams(1),
        name="out_proj",
    )(xf, ya, yb, w_out, g_ffn, w_router_t)


CHUNK = 256
ROUTE_LOG_STEPS = 12
ROUTE_LIN_STEPS = 32


def _prefix_count(flags_f32, tri):
    S = flags_f32.shape[1]
    carry = jnp.zeros((flags_f32.shape[0], 1), F32)
    parts = []
    for c in range(S // CHUNK):
        blk = flags_f32[:, c * CHUNK:(c + 1) * CHUNK]
        parts.append(_dot(blk.astype(BF16), tri) + carry)
        carry = carry + jnp.sum(blk, axis=-1, keepdims=True)
    return jnp.concatenate(parts, axis=1)


def _route_kernel(cap, aff_ref, slot_ref):
    aff = aff_ref[...]
    rows = aff.shape[0]

    def enough(pivot):
        return jnp.sum(jnp.where(aff >= pivot, 1.0, 0.0), axis=-1, keepdims=True) >= cap

    def log_body(_, st):
        lo, hi, elo, ehi = st
        mid = 0.5 * (elo + ehi)
        pivot = jnp.exp2(mid)
        ok = enough(pivot)
        return (jnp.where(ok, pivot, lo), jnp.where(ok, hi, pivot), jnp.where(ok, mid, elo), jnp.where(ok, ehi, mid))

    def lin_body(_, st):
        lo, hi = st
        pivot = 0.5 * (lo + hi)
        ok = enough(pivot)
        return jnp.where(ok, pivot, lo), jnp.where(ok, hi, pivot)

    col = lambda v: jnp.full((rows, 1), v, F32)
    lo, hi, _, _ = lax.fori_loop(0, ROUTE_LOG_STEPS, log_body, (col(0.0), col(2.0), col(-152.0), col(1.0)))
    lo, hi = lax.fori_loop(0, ROUTE_LIN_STEPS, lin_body, (lo, hi))

    gt = jnp.where(aff >= hi, 1.0, 0.0)
    eq = jnp.where(aff >= lo, 1.0, 0.0) - gt
    need = cap - jnp.sum(gt, axis=-1, keepdims=True)
    r = lax.broadcasted_iota(jnp.int32, (CHUNK, CHUNK), 0)
    c = lax.broadcasted_iota(jnp.int32, (CHUNK, CHUNK), 1)
    tri = jnp.where(r < c, 1.0, 0.0).astype(BF16)
    sel = gt + eq * jnp.where(_prefix_count(eq, tri) < need, 1.0, 0.0)
    slot = _prefix_count(sel, tri)
    slot_ref[...] = jnp.where(sel > 0.5, slot, -1.0)


def _route(aff, cap):
    B, E, S = aff.shape
    spec = pl.BlockSpec((B * E, S), lambda i: (0, 0))
    return pl.pallas_call(
        functools.partial(_route_kernel, cap),
        grid=(1,),
        in_specs=[spec],
        out_specs=spec,
        out_shape=jax.ShapeDtypeStruct((B * E, S), F32),
        compiler_params=_cparams(1),
        name="route",
    )(aff.reshape(B * E, S))


def _moe_kernel(cap, slot_a_ref, slot_b_ref, next_a_ref, next_b_ref, h_ref, afft_ref,
                wg_hbm, wu_hbm, wd_hbm, o_ref, xa_ref, xb_ref, ya_ref, yb_ref, aa_ref, ab_ref, tok_ref,
                wg_ref, wu_ref, wd_ref, w_sem):
    S = h_ref.shape[0] // SLAB_ROWS
    stride = cap + SLAB_ROWS
    parity = pl.program_id(1) % 2
    slot_ids = lax.broadcasted_iota(jnp.int32, (cap, 1), 0).astype(F32)
    tok_ids = lax.broadcasted_iota(jnp.int32, (1, S), 1).astype(F32)

    def slot_affinities(a_ref, expert):
        lane = lax.broadcasted_iota(jnp.int32, (1, LANES), 1)
        return jnp.sum(jnp.where(lane == expert, a_ref[...], 0.0), axis=-1, keepdims=True)

    def gather(slot_row, x_ref, a_ref, which, par, done):
        sel = slot_ids == slot_row
        tok = jnp.sum(jnp.where(sel, tok_ids, 0.0), axis=-1, keepdims=True).astype(jnp.int32)
        tok = jnp.clip(tok, 0, S - 1)
        for grp in range(cap // GATHER_GROUP):
            ids = tok[grp * GATHER_GROUP:(grp + 1) * GATHER_GROUP, :]
            if len(done) >= 2:
                ids = jnp.minimum(ids, done[-2] + S)
            for j in range(GATHER_GROUP):
                c = grp * GATHER_GROUP + j
                t = ids[j, 0]
                row = pl.multiple_of(t * SLAB_ROWS, SLAB_ROWS)
                tok_ref[which, par, c] = row
                x_ref[pl.ds(c, SLAB_ROWS, stride=stride), :] = h_ref[pl.ds(row, SLAB_ROWS), :]
                a_ref[pl.ds(c, 1), :] = afft_ref[pl.ds(t, 1), :]
            done.append(t)

    def chunks_to_rows(tile_ref):
        return jnp.concatenate([tile_ref[j * stride:j * stride + cap, :] for j in range(SLAB_ROWS)], axis=1)

    def gate_up(x_ref, i):
        xg = chunks_to_rows(x_ref).astype(BF16)
        g = _dot(xg, wg_ref[w_slot, i])
        u = _dot(xg, wu_ref[w_slot, i])
        return (g * (1.0 / (1.0 + jnp.exp(-g))) * u).astype(BF16)

    def down(act, vals, y_ref, i):
        y = _dot(act, wd_ref[w_slot, i]) * vals
        for j in range(SLAB_ROWS):
            y_ref[j * stride:j * stride + cap, :] = y[:, j * LANES:(j + 1) * LANES]

    def scatter_add(y_ref, which):
        for c0 in range(0, cap, SCATTER_GROUP):
            rows = [tok_ref[which, parity, c0 + j] for j in range(SCATTER_GROUP)]
            rows = [pl.multiple_of(r, SLAB_ROWS) for r in rows]
            new = [o_ref[pl.ds(rows[j], SLAB_ROWS), :] + y_ref[pl.ds(c0 + j, SLAB_ROWS, stride=stride), :]
                   for j in range(SCATTER_GROUP)]
            for j in range(SCATTER_GROUP):
                o_ref[pl.ds(rows[j], SLAB_ROWS), :] = new[j]

    n_pairs = pl.num_programs(1)
    step = pl.program_id(0) * n_pairs + pl.program_id(1)
    n_all = pl.num_programs(0) * n_pairs
    w_slot = step % W_SLOTS

    def weight_copies(for_step):
        pair = for_step % n_pairs
        slot = for_step % W_SLOTS
        return [pltpu.make_async_copy(src.at[pl.ds(2 * pair, 2)], dst.at[slot], w_sem.at[j, slot])
                for j, (src, dst) in enumerate(((wg_hbm, wg_ref), (wu_hbm, wu_ref), (wd_hbm, wd_ref)))]

    @pl.when(step == 0)
    def _prime_weights():
        for first in (0, 1):
            for cp in weight_copies(first):
                cp.start()

    for cp in weight_copies(step):
        cp.wait()

    @pl.when(step + 2 < n_all)
    def _prefetch_weights():
        for cp in weight_copies(step + 2):
            cp.start()

    @pl.when((pl.program_id(0) == 0) & (pl.program_id(1) == 0))
    def _first_step():
        moved = []
        gather(slot_a_ref[0], xa_ref, aa_ref, 0, 0, moved)
        gather(slot_b_ref[0], xb_ref, ab_ref, 1, 0, moved)

    @pl.when(pl.program_id(1) == 0)
    def _first_of_sequence():
        o_ref[...] = jnp.zeros_like(o_ref)

    expert_a = 2 * pl.program_id(1)
    vals_a = slot_affinities(aa_ref, expert_a)
    vals_b = slot_affinities(ab_ref, expert_a + 1)
    moved = []
    act_a = gate_up(xa_ref, 0)
    gather(next_a_ref[0], xa_ref, aa_ref, 0, 1 - parity, moved)
    act_b = gate_up(xb_ref, 1)
    gather(next_b_ref[0], xb_ref, ab_ref, 1, 1 - parity, moved)
    down(act_a, vals_a, ya_ref, 0)
    down(act_b, vals_b, yb_ref, 1)
    scatter_add(ya_ref, 0)
    scatter_add(yb_ref, 1)


def _moe(slot, afft, h2, wg, wu, wd, B, E, cap):
    S = slot.shape[1]
    assert E % 2 == 0 and (E // 2) % 2 == 0 and h2.shape == (B * S * SLAB_ROWS, LANES) and E <= LANES
    n_steps = B * E // 2
    last = B * E - 1
    slot3 = slot.reshape(B * E, 1, S)
    row = lambda d: pl.BlockSpec((1, 1, S), lambda b, k: (jnp.minimum(b * E + 2 * k + d, last), 0, 0))
    cur_a, cur_b, nxt_a, nxt_b = row(0), row(1), row(2), row(3)
    tok = pl.BlockSpec((S * SLAB_ROWS, LANES), lambda b, k: (b, 0))
    ahead_seq = lambda b, k: (jnp.minimum(b * (E // 2) + k + 1, n_steps - 1) // (E // 2), 0)
    ahead = pl.BlockSpec((S * SLAB_ROWS, LANES), ahead_seq)
    ahead_aff = pl.BlockSpec((S, LANES), ahead_seq)
    tile = pltpu.VMEM((SLAB_ROWS * (cap + SLAB_ROWS), LANES), F32)
    aff_rows = pltpu.VMEM((cap, LANES), F32)
    hbm = pl.BlockSpec(memory_space=pl.ANY)
    ring = lambda w: pltpu.VMEM((W_SLOTS, 2) + w.shape[1:], w.dtype)
    return pl.pallas_call(
        functools.partial(_moe_kernel, cap),
        grid=(B, E // 2),
        in_specs=[cur_a, cur_b, nxt_a, nxt_b, ahead, ahead_aff, hbm, hbm, hbm],
        out_specs=tok,
        out_shape=jax.ShapeDtypeStruct(h2.shape, F32),
        scratch_shapes=[tile, tile, tile, tile, aff_rows, aff_rows, pltpu.SMEM((2, 2, cap), jnp.int32),
                        ring(wg), ring(wu), ring(wd), pltpu.SemaphoreType.DMA((3, W_SLOTS))],
        compiler_params=pltpu.CompilerParams(dimension_semantics=("arbitrary", "arbitrary"),
                                             vmem_limit_bytes=MOE_VMEM_LIMIT),
        name="moe",
    )(slot3, slot3, slot3, slot3, h2, afft, wg, wu, wd)


def _ple_final_kernel(x1_ref, moe_ref, p_ref, gp_ref, wg_ref, wp_ref, gf_ref, o_ref):
    rows = x1_ref.shape[0]
    moe = jnp.concatenate([moe_ref[pl.ds(j, rows, stride=SLAB_ROWS), :] for j in range(SLAB_ROWS)], axis=1)
    x2 = x1_ref[...] + moe
    z = _dot(_rms(x2, gp_ref[...]).astype(BF16), wg_ref[...])
    gate = 1.0 / (1.0 + jnp.exp(-z))
    x3 = x2 + gate * _dot(p_ref[...].astype(BF16), wp_ref[...])
    o_ref[...] = _rms(x3, gf_ref[...])


def _ple_final(x1, moe, pf, g_ple, w_gate, w_proj, g_final):
    T = x1.shape[0]
    row = lambda n: pl.BlockSpec((TM, n), lambda i: (i, 0))
    full = lambda a: pl.BlockSpec(a.shape, lambda i: (0, 0))
    return pl.pallas_call(
        _ple_final_kernel,
        grid=(T // TM,),
        in_specs=[row(D_MODEL), pl.BlockSpec((TM * SLAB_ROWS, LANES), lambda i: (i, 0)), row(PLE_DIM),
                  full(g_ple), full(w_gate), full(w_proj), full(g_final)],
        out_specs=row(D_MODEL),
        out_shape=jax.ShapeDtypeStruct((T, D_MODEL), F32),
        compiler_params=_cparams(1),
        name="ple_final",
    )(x1, moe, pf, g_ple, w_gate, w_proj, g_final)


def _prep_in_proj_weight(w_in):
    D = w_in.shape[0]
    t1, t2 = w_in[:, 1152:1168], w_in[:, 1168:1184]
    cols = [w_in[:, 0:1152], t1, t2, t2, t1, jnp.zeros((D, 64), F32)]
    return jnp.concatenate(cols, axis=1).astype(BF16)


def _prep_mla_weights(w_uq, w_ukv):
    per_q = B_NOPE_DIM + B_ROPE_DIM
    w3 = w_uq.reshape(Q_RANK, B_HEADS, per_q)
    pad = LANES - per_q
    wq = jnp.pad(w3, ((0, 0), (0, 0), (0, pad))).reshape(Q_RANK, B_HEADS * LANES)
    t1 = w3[:, :, B_NOPE_DIM:B_NOPE_DIM + HALF_ROPE]
    t2 = w3[:, :, B_NOPE_DIM + HALF_ROPE:]
    wqr = jnp.concatenate([jnp.zeros((Q_RANK, B_HEADS, B_NOPE_DIM), F32), t2, t1,
                           jnp.zeros((Q_RANK, B_HEADS, pad), F32)], axis=2).reshape(Q_RANK, B_HEADS * LANES)
    w4 = w_ukv.reshape(KV_RANK, B_HEADS, B_NOPE_DIM + B_V_DIM)
    zk = jnp.zeros((KV_RANK, B_HEADS, LANES - B_NOPE_DIM), F32)
    wk = jnp.concatenate([w4[:, :, :B_NOPE_DIM], zk], axis=2).reshape(KV_RANK, B_HEADS * LANES)
    wv = w4[:, :, B_NOPE_DIM:].reshape(KV_RANK, B_HEADS * B_V_DIM)
    wkv = jnp.concatenate([wk, wv], axis=1)
    return wq.astype(BF16), wqr.astype(BF16), wkv.astype(BF16)


def kernel(x, p, positions, rel_bias, norm_mix_g, w_in, sink, g_cq, g_ckv, w_uq, w_ukv, g_out_a, g_out_b, w_out,
           norm_ffn_g, w_router, w_e_gate, w_e_up, w_e_down, norm_ple_g, w_ple_gate, w_ple_proj, final_norm_g):
    B, S, D = x.shape
    T = B * S
    cap = CAPACITY_FACTOR * S // N_EXPERTS
    xf = x.reshape(T, D)
    pos_d, invf = _rope_angles(positions)
    assert w_in.shape[0] == 1, "single-layer block: the final norm is fused into the last kernel"
    i = 0
    w1 = _prep_in_proj_weight(w_in[i])
    wq, wqr, wkv = _prep_mla_weights(w_uq[i], w_ukv[i])
    qa, ka, va, qm, km, vm = _in_proj(
        xf, norm_mix_g[i].reshape(1, D), w1, g_cq[i].reshape(1, -1), g_ckv[i].reshape(1, -1),
        wq, wqr, wkv, pos_d, invf)
    ya = _window_attn(qa, ka, va, rel_bias, sink[i], g_out_a[i].reshape(1, -1), B, S)
    yb, (wg, wu, wd) = _mla_attn(qm, km, vm, g_out_b[i].reshape(1, -1), B, S,
                                 (w_e_gate[i], w_e_up[i], w_e_down[i]))
    x1, h2, aff, afft = _out_proj(xf, ya, yb, w_out[i].astype(BF16), norm_ffn_g[i].reshape(1, D),
                                  w_router[i].T, B, S)
    slot = _route(aff, cap)
    moe = _moe(slot, afft, h2, wg, wu, wd, B, N_EXPERTS, cap)
    out = _ple_final(x1, moe, p[i].reshape(T, -1), norm_ple_g[i].reshape(1, D),
                     w_ple_gate[i].astype(BF16), w_ple_proj[i].astype(BF16), final_norm_g.reshape(1, D))
    return out.reshape(B, S, D)
```
